```python
import math
import jax, jax.numpy as jnp
from jax import lax
import numpy as np

D_MODEL = 1024
BATCH = 4
SEQ = 4096
DEPTH = 1
DEC_BATCH = 8
DEC_SEQ = 32
PAST_LEN = 4096

CHUNK = 64
N_META = 16
N_DIFF_HEADS = 8
DIFF_HEAD_DIM = 64
DIFF_V_DIM = 2 * DIFF_HEAD_DIM
DIFF_WIDTH = N_DIFF_HEADS * DIFF_V_DIM
N_SB_HEADS = 8
SB_HEAD_DIM = 64
SB_WIDTH = N_SB_HEADS * SB_HEAD_DIM
N_BRANCH = 2
D_FF = 4 * D_MODEL
ROT_DIM = DIFF_HEAD_DIM // 4
ROPE_THETA = 500000.0
Q_BLOCK = 128
EPS = 1e-6
NEG_INF = -1e30
IN_COLS = 3 * DIFF_WIDTH + 3 * SB_WIDTH + N_BRANCH * D_MODEL
IN_SPLITS = [DIFF_WIDTH, 2 * DIFF_WIDTH, 3 * DIFF_WIDTH,
             3 * DIFF_WIDTH + SB_WIDTH, 3 * DIFF_WIDTH + 2 * SB_WIDTH, 3 * DIFF_WIDTH + 3 * SB_WIDTH]

kernel_name = "hybrid_diff_stickbreak_stream_step"


def _rmsnorm(x, g):
    x32 = x.astype(jnp.float32)
    y = x32 * lax.rsqrt(jnp.mean(x32 * x32, axis=-1, keepdims=True) + EPS)
    return (y * g.astype(jnp.float32)).astype(x.dtype)


def _rope(x, pos):
    half = ROT_DIM // 2
    inv = ROPE_THETA ** (-jnp.arange(0, ROT_DIM, 2, dtype=jnp.float32) / ROT_DIM)
    ang = pos.astype(jnp.float32)[:, None] * inv[None, :]
    shp = (pos.shape[0],) + (1,) * (x.ndim - 3) + (half,)
    cos = jnp.cos(ang).reshape(shp)
    sin = jnp.sin(ang).reshape(shp)
    xr = x[..., :ROT_DIM].astype(jnp.float32)
    x1, x2 = xr[..., :half], xr[..., half:]
    rot = jnp.concatenate([x1 * cos - x2 * sin, x2 * cos + x1 * sin], axis=-1)
    return jnp.concatenate([rot.astype(x.dtype), x[..., ROT_DIM:]], axis=-1)


def _query_blocks(tq):
    blk = min(Q_BLOCK, tq)
    return blk, -(-tq // blk)


def _split_blocks(a, blk, nb, axis):
    pad = nb * blk - a.shape[axis]
    widths = [(0, 0)] * a.ndim
    widths[axis] = (0, pad)
    a = jnp.pad(a, widths, mode="edge")
    a = a.reshape(a.shape[:axis] + (nb, blk) + a.shape[axis + 1:])
    return jnp.moveaxis(a, axis, 0)


def _merge_blocks(o, tq):
    o = jnp.moveaxis(o, 0, 1)
    o = o.reshape((o.shape[0], -1) + o.shape[3:])
    return o[:, :tq]


def _diff_attention(q, k, v, q_chunk, k_chunk, lam):
    tq = q.shape[1]
    blk, nb = _query_blocks(tq)
    scale = DIFF_HEAD_DIM ** -0.5

    def block(args):
        qb, cb = args
        s = jnp.einsum("bqhcd,bkhcd->bchqk", qb, k).astype(jnp.float32) * scale
        mask = k_chunk[None, :] <= cb[:, None]
        p = jax.nn.softmax(jnp.where(mask, s, NEG_INF), axis=-1)
        a = p[:, 0] - lam * p[:, 1]
        return jnp.einsum("bhqk,bkhe->bqhe", a.astype(v.dtype), v)

    o = lax.map(block, (_split_blocks(q, blk, nb, 1), _split_blocks(q_chunk, blk, nb, 0)))
    return _merge_blocks(o, tq)


def _stick_breaking(q, k, v, q_idx, k_idx):
    tq = q.shape[1]
    blk, nb = _query_blocks(tq)
    scale = SB_HEAD_DIM ** -0.5

    def block(args):
        qb, ib = args
        z = jnp.einsum("bqhd,bkhd->bhqk", qb, k).astype(jnp.float32) * scale
        mask = k_idx[None, :] < ib[:, None]
        log_1m = jnp.where(mask, jax.nn.log_sigmoid(-z), 0.0)
        suffix = lax.cumsum(log_1m, axis=3, reverse=True) - log_1m
        w = jnp.where(mask, jnp.exp(jax.nn.log_sigmoid(z) + suffix), 0.0)
        return jnp.einsum("bhqk,bkhd->bqhd", w.astype(v.dtype), v)

    o = lax.map(block, (_split_blocks(q, blk, nb, 1), _split_blocks(q_idx, blk, nb, 0)))
    return _merge_blocks(o, tq)


def _layer(x, q_pos, q_chunk, k_pos, k_chunk, past_dk, past_dv, past_sk, past_sv, lam_init,
           g_mix, w_in, q_norm_g, k_norm_g, lam_q1, lam_k1, lam_q2, lam_k2, sub_g,
           w_diff_out, w_sb_out, w_out, g_ffn, w_ff1, w_ff2):
    b, t, _ = x.shape
    h = _rmsnorm(x, g_mix)
    dq, dk, dv, sq, sk, sv, gate = jnp.split(h @ w_in, IN_SPLITS, axis=-1)
    dq = _rope(_rmsnorm(dq.reshape(b, t, N_DIFF_HEADS, 2, DIFF_HEAD_DIM), q_norm_g), q_pos)
    dk = _rope(_rmsnorm(dk.reshape(b, t, N_DIFF_HEADS, 2, DIFF_HEAD_DIM), k_norm_g), q_pos)
    dk = dk.reshape(b, t, N_DIFF_HEADS, 2 * DIFF_HEAD_DIM)
    dv = dv.reshape(b, t, N_DIFF_HEADS, DIFF_V_DIM)
    sq = sq.reshape(b, t, N_SB_HEADS, SB_HEAD_DIM)
    sk = sk.reshape(b, t, N_SB_HEADS, SB_HEAD_DIM)
    sv = sv.reshape(b, t, N_SB_HEADS, SB_HEAD_DIM)
    if past_dk is None:
        dk_all, dv_all, sk_all, sv_all = dk, dv, sk, sv
    else:
        dk_all = jnp.concatenate([past_dk, dk], axis=1)
        dv_all = jnp.concatenate([past_dv, dv], axis=1)
        sk_all = jnp.concatenate([past_sk, sk], axis=1)
        sv_all = jnp.concatenate([past_sv, sv], axis=1)
    f32 = jnp.float32
    lam = (jnp.exp(jnp.sum(lam_q1.astype(f32) * lam_k1.astype(f32)))
           - jnp.exp(jnp.sum(lam_q2.astype(f32) * lam_k2.astype(f32))) + lam_init)
    o_d = _diff_attention(dq, dk_all.reshape(b, -1, N_DIFF_HEADS, 2, DIFF_HEAD_DIM), dv_all,
                          q_chunk, k_chunk, lam)
    o_d = _rmsnorm(o_d, sub_g) * (1.0 - lam_init)
    o_s = _stick_breaking(sq, sk_all, sv_all, q_pos, k_pos)
    g_d, g_s = jnp.split(jax.nn.sigmoid(gate), N_BRANCH, axis=-1)
    merged = (g_d * (o_d.reshape(b, t, DIFF_WIDTH) @ w_diff_out)
              + g_s * (o_s.reshape(b, t, SB_WIDTH) @ w_sb_out))
    x = x + merged @ w_out
    x = x + jnp.square(jax.nn.relu(_rmsnorm(x, g_ffn) @ w_ff1)) @ w_ff2
    return x, dk, dv, sk, sv


def setup_inputs(seed: int = 0) -> dict:
    key = jax.random.key(seed)
    ks = jax.random.split(key, 24)

    def n(k, shape, s):
        return jax.random.normal(k, shape, jnp.float32) * s

    return {
        "x_prompt": n(ks[0], (BATCH, SEQ, D_MODEL), 1.0),
        "x_sample": n(ks[1], (DEC_BATCH, DEC_SEQ, D_MODEL), 1.0),
        "cache_diff_k": n(ks[2], (DEPTH, DEC_BATCH, PAST_LEN, N_DIFF_HEADS, 2 * DIFF_HEAD_DIM), 1.0),
        "cache_diff_v": n(ks[3], (DEPTH, DEC_BATCH, PAST_LEN, N_DIFF_HEADS, DIFF_V_DIM), 1.0),
        "cache_sb_k": n(ks[4], (DEPTH, DEC_BATCH, PAST_LEN, N_SB_HEADS, SB_HEAD_DIM), 1.0),
        "cache_sb_v": n(ks[5], (DEPTH, DEC_BATCH, PAST_LEN, N_SB_HEADS, SB_HEAD_DIM), 1.0),
        "meta_tokens": n(ks[6], (N_META, D_MODEL), 1.0),
        "g_mix": 1.0 + n(ks[7], (DEPTH, D_MODEL), 0.02),
        "w_in": n(ks[8], (DEPTH, D_MODEL, IN_COLS), D_MODEL ** -0.5),
        "q_norm_g": 1.0 + n(ks[9], (DEPTH, DIFF_HEAD_DIM), 0.02),
        "k_norm_g": 1.0 + n(ks[10], (DEPTH, DIFF_HEAD_DIM), 0.02),
        "lam_q1": n(ks[11], (DEPTH, DIFF_HEAD_DIM), 0.1),
        "lam_k1": n(ks[12], (DEPTH, DIFF_HEAD_DIM), 0.1),
        "lam_q2": n(ks[13], (DEPTH, DIFF_HEAD_DIM), 0.1),
        "lam_k2": n(ks[14], (DEPTH, DIFF_HEAD_DIM), 0.1),
        "sub_g": 1.0 + n(ks[15], (DEPTH, DIFF_V_DIM), 0.02),
        "w_diff_out": n(ks[16], (DEPTH, DIFF_WIDTH, D_MODEL), DIFF_WIDTH ** -0.5),
        "w_sb_out": n(ks[17], (DEPTH, SB_WIDTH, D_MODEL), SB_WIDTH ** -0.5),
        "w_out": n(ks[18], (DEPTH, D_MODEL, D_MODEL), D_MODEL ** -0.5),
        "g_ffn": 1.0 + n(ks[19], (DEPTH, D_MODEL), 0.02),
        "w_ff1": n(ks[20], (DEPTH, D_MODEL, D_FF), D_MODEL ** -0.5),
        "w_ff2": n(ks[21], (DEPTH, D_FF, D_MODEL), D_FF ** -0.5),
    }


def reference(x_prompt, x_sample, cache_diff_k, cache_diff_v, cache_sb_k, cache_sb_v, meta_tokens,
              g_mix, w_in, q_norm_g, k_norm_g, lam_q1, lam_k1, lam_q2, lam_k2, sub_g,
              w_diff_out, w_sb_out, w_out, g_ffn, w_ff1, w_ff2):
    b, s, _ = x_prompt.shape
    t = s + N_META
    meta = jnp.broadcast_to(meta_tokens[None].astype(x_prompt.dtype), (b, N_META, D_MODEL))
    xp = jnp.concatenate([meta, x_prompt], axis=1)
    p_pos = jnp.arange(t, dtype=jnp.int32)
    p_chunk = jnp.where(p_pos < N_META, 0, (p_pos - N_META) // CHUNK + 1)
    past_len = cache_diff_k.shape[2]
    ds = x_sample.shape[1]
    k_pos_s = jnp.arange(past_len + ds, dtype=jnp.int32)
    k_chunk_s = k_pos_s // CHUNK + 1
    q_pos_s = k_pos_s[past_len:]
    q_chunk_s = k_chunk_s[past_len:]
    xs = x_sample
    pdk, pdv, psk, psv, sdk, sdv, ssk, ssv = [], [], [], [], [], [], [], []
    for l in range(DEPTH):
        lam_init = 0.8 - 0.6 * math.exp(-0.3 * l)
        xp, a1, a2, a3, a4 = _layer(
            xp, p_pos, p_chunk, p_pos, p_chunk, None, None, None, None, lam_init,
            g_mix[l], w_in[l], q_norm_g[l], k_norm_g[l], lam_q1[l], lam_k1[l], lam_q2[l], lam_k2[l],
            sub_g[l], w_diff_out[l], w_sb_out[l], w_out[l], g_ffn[l], w_ff1[l], w_ff2[l])
        xs, c1, c2, c3, c4 = _layer(
            xs, q_pos_s, q_chunk_s, k_pos_s, k_chunk_s,
            cache_diff_k[l], cache_diff_v[l], cache_sb_k[l], cache_sb_v[l], lam_init,
            g_mix[l], w_in[l], q_norm_g[l], k_norm_g[l], lam_q1[l], lam_k1[l], lam_q2[l], lam_k2[l],
            sub_g[l], w_diff_out[l], w_sb_out[l], w_out[l], g_ffn[l], w_ff1[l], w_ff2[l])
        pdk.append(a1); pdv.append(a2); psk.append(a3); psv.append(a4)
        sdk.append(c1); sdv.append(c2); ssk.append(c3); ssv.append(c4)
    y_prompt = xp[:, N_META:]
    y_sample = xs
    return (y_prompt, y_sample,
            jnp.stack(pdk), jnp.stack(pdv), jnp.stack(psk), jnp.stack(psv),
            jnp.stack(sdk), jnp.stack(sdv), jnp.stack(ssk), jnp.stack(ssv))
```

```python
import functools
import math

import jax
import jax.numpy as jnp
from jax import lax
from jax.experimental import pallas as pl
from jax.experimental.pallas import tpu as pltpu

F32 = jnp.float32
BF16 = jnp.bfloat16

D_MODEL = 1024
N_META = 16
CHUNK = 64
N_HEADS = 8
HEAD_DIM = 64
DIFF_WIDTH = N_HEADS * 2 * HEAD_DIM
SB_WIDTH = N_HEADS * HEAD_DIM
QKV_COLS = 3 * DIFF_WIDTH + 3 * SB_WIDTH
D_FF = 4 * D_MODEL
ROT_DIM = HEAD_DIM // 4
ROPE_THETA = 500000.0
EPS = 1e-6
NEG = -1e30
QK_SCALE = HEAD_DIM ** -0.5
LAM_INIT = 0.8 - 0.6 * math.exp(-0.3 * 0)

LANES = 128
COL_BLOCK = 512
VMEM_LIMIT = 56 * 1024 * 1024

_NT = (((1,), (1,)), ((), ()))


def _rms(x):
    return x * lax.rsqrt(jnp.mean(x * x, axis=-1, keepdims=True) + EPS)


def _const_spec(shape):
    return pl.BlockSpec(shape, lambda *_: (0,) * len(shape), pipeline_mode=pl.Buffered(1))


def _proj_kernel(x_ref, gmix_ref, w_ref, qg_ref, kg_ref, gmat_ref, cos_ref, sa_ref, sb_ref,
                 qd_ref, kd_ref, kdb_ref, vd_ref, vdb_ref, qs_ref, ks_ref, ksb_ref, vs_ref, vsb_ref):
    h = (_rms(x_ref[...]) * gmix_ref[...]).astype(BF16)
    cos = cos_ref[...]
    sa = sa_ref[...]
    sb = sb_ref[...]

    def col(j):
        return jnp.dot(h, w_ref[:, j * COL_BLOCK:(j + 1) * COL_BLOCK], preferred_element_type=F32)

    def normed_rot(y, g):
        msq = jnp.dot((y * y).astype(BF16), gmat_ref[...], preferred_element_type=F32)
        yn = y * lax.rsqrt(msq + EPS) * g
        parts = []
        for c in range(COL_BLOCK // LANES):
            t = yn[:, c * LANES:(c + 1) * LANES]
            parts.append(t * cos + pltpu.roll(t, LANES - ROT_DIM // 2, 1) * sa
                         + pltpu.roll(t, ROT_DIM // 2, 1) * sb)
        return jnp.concatenate(parts, axis=1)

    for j in range(2):
        sl = slice(j * COL_BLOCK, (j + 1) * COL_BLOCK)
        q = normed_rot(col(j), qg_ref[...])
        qd_ref[:, sl] = (q * QK_SCALE).astype(BF16)
        k = normed_rot(col(2 + j), kg_ref[...])
        kd_ref[:, sl] = k
        kdb_ref[:, sl] = k.astype(BF16)
        v = col(4 + j)
        vd_ref[:, sl] = v
        vdb_ref[:, sl] = v.astype(BF16)
    qs_ref[...] = (col(6) * QK_SCALE).astype(BF16)
    k = col(7)
    ks_ref[...] = k
    ksb_ref[...] = k.astype(BF16)
    v = col(8)
    vs_ref[...] = v
    vsb_ref[...] = v.astype(BF16)


def _proj(x, gmix, w_qkv, qg, kg, gmat, cos, sa, sb, *, tm):
    rows = x.shape[0]
    n_pos_tiles = cos.shape[0] // tm
    row = lambda w: pl.BlockSpec((tm, w), lambda i: (i, 0))
    tab = pl.BlockSpec((tm, LANES), lambda i: (i % n_pos_tiles, 0))
    wide = lambda dt: jax.ShapeDtypeStruct((rows, DIFF_WIDTH), dt)
    narrow = lambda dt: jax.ShapeDtypeStruct((rows, SB_WIDTH), dt)
    return pl.pallas_call(
        _proj_kernel,
        grid=(rows // tm,),
        in_specs=[row(D_MODEL), _const_spec((1, D_MODEL)), _const_spec((D_MODEL, QKV_COLS)),
                  _const_spec((1, COL_BLOCK)), _const_spec((1, COL_BLOCK)),
                  _const_spec((COL_BLOCK, COL_BLOCK)), tab, tab, tab],
        out_specs=[row(DIFF_WIDTH)] * 5 + [row(SB_WIDTH)] * 5,
        out_shape=[wide(BF16), wide(F32), wide(BF16), wide(F32), wide(BF16),
                   narrow(BF16), narrow(F32), narrow(BF16), narrow(F32), narrow(BF16)],
        compiler_params=pltpu.CompilerParams(dimension_semantics=("arbitrary",),
                                             vmem_limit_bytes=VMEM_LIMIT),
        name="proj",
    )(x, gmix, w_qkv, qg, kg, gmat, cos, sa, sb)


def _diff_kernel(lq1_ref, lk1_ref, lq2_ref, lk2_ref, subg_ref, q_ref, kp_ref, vp_ref, km_ref, vm_ref,
                 o_ref, m_ref, l_ref, acc_ref, *, tq, tk, tkp, n_pref, pref_valid):
    i = pl.program_id(2)
    q = q_ref[0]
    lane = lax.broadcasted_iota(jnp.int32, q.shape, 1)
    zero = jnp.zeros_like(q)
    qs = (jnp.where(lane < HEAD_DIM, q, zero), jnp.where(lane >= HEAD_DIM, q, zero))
    m_ref[...] = jnp.full(m_ref.shape, NEG, F32)
    l_ref[...] = jnp.zeros(l_ref.shape, F32)
    acc_ref[...] = jnp.zeros(acc_ref.shape, F32)

    def block(kblk, vblk, mask):
        for a in range(2):
            s = lax.dot_general(qs[a], kblk, _NT, preferred_element_type=F32)
            if mask is not None:
                s = jnp.where(mask, s, NEG)
            m_prev = m_ref[a]
            m_new = jnp.maximum(m_prev, jnp.max(s, axis=1, keepdims=True))
            alpha = jnp.exp(m_prev - m_new)
            p = jnp.exp(s - m_new)
            l_ref[a] = alpha * l_ref[a] + jnp.sum(p, axis=1, keepdims=True)
            acc_ref[a] = alpha * acc_ref[a] + jnp.dot(p.astype(BF16), vblk, preferred_element_type=F32)
            m_ref[a] = m_new

    if n_pref == 1:
        pmask = None
        if pref_valid < tkp:
            pmask = lax.broadcasted_iota(jnp.int32, (tq, tkp), 1) < pref_valid
        block(kp_ref[0], vp_ref[0], pmask)
    else:
        def pref_body(t, carry):
            off = pl.multiple_of(t * tkp, tkp)
            block(kp_ref[0, pl.ds(off, tkp), :], vp_ref[0, pl.ds(off, tkp), :], None)
            return carry
        lax.fori_loop(0, n_pref, pref_body, 0)

    n_diag = tq // tk

    def full_body(kb, carry):
        off = pl.multiple_of(kb * tk, tk)
        block(km_ref[0, pl.ds(off, tk), :], vm_ref[0, pl.ds(off, tk), :], None)
        return carry
    lax.fori_loop(0, i * n_diag, full_body, 0)

    row_chunk = lax.broadcasted_iota(jnp.int32, (tq, tk), 0) // CHUNK
    col = lax.broadcasted_iota(jnp.int32, (tq, tk), 1)
    for d in range(n_diag):
        off = pl.multiple_of((i * n_diag + d) * tk, tk)
        mask = (col + d * tk) // CHUNK <= row_chunk
        block(km_ref[0, pl.ds(off, tk), :], vm_ref[0, pl.ds(off, tk), :], mask)

    lam = (jnp.exp(jnp.sum(lq1_ref[...] * lk1_ref[...], axis=1, keepdims=True))
           - jnp.exp(jnp.sum(lq2_ref[...] * lk2_ref[...], axis=1, keepdims=True)) + LAM_INIT)
    o = acc_ref[0] / l_ref[0] - lam * (acc_ref[1] / l_ref[1])
    o = _rms(o) * subg_ref[...] * (1.0 - LAM_INIT)
    o_ref[0] = o.astype(BF16)


def _diff_attn(lams, subg, q, kp, vp, km, vm, *, tq, tk, tkp, pref_valid, pref_shared):
    nb, tq_total, _ = q.shape
    p_len = kp.shape[1]
    t_main = km.shape[1]
    n_pref = p_len // tkp
    kernel = functools.partial(_diff_kernel, tq=tq, tk=tk, tkp=tkp, n_pref=n_pref, pref_valid=pref_valid)
    pref_idx = (lambda b, h, i: (0, 0, h)) if pref_shared else (lambda b, h, i: (b, 0, h))
    small = _const_spec((1, HEAD_DIM))
    return pl.pallas_call(
        kernel,
        grid=(nb, N_HEADS, tq_total // tq),
        in_specs=[small, small, small, small, _const_spec((1, LANES)),
                  pl.BlockSpec((1, tq, LANES), lambda b, h, i: (b, i, h)),
                  pl.BlockSpec((1, p_len, LANES), pref_idx),
                  pl.BlockSpec((1, p_len, LANES), pref_idx),
                  pl.BlockSpec((1, t_main, LANES), lambda b, h, i: (b, 0, h)),
                  pl.BlockSpec((1, t_main, LANES), lambda b, h, i: (b, 0, h))],
        out_specs=pl.BlockSpec((1, tq, LANES), lambda b, h, i: (b, i, h)),
        out_shape=jax.ShapeDtypeStruct((nb, tq_total, DIFF_WIDTH), BF16),
        scratch_shapes=[pltpu.VMEM((2, tq, 1), F32), pltpu.VMEM((2, tq, 1), F32),
                        pltpu.VMEM((2, tq, LANES), F32)],
        compiler_params=pltpu.CompilerParams(dimension_semantics=("arbitrary",) * 3,
                                             vmem_limit_bytes=VMEM_LIMIT),
        name="diff_attn",
    )(*lams, subg, q, kp, vp, km, vm)


def _strict_lower(n):
    r = lax.broadcasted_iota(jnp.int32, (n, n), 0)
    c = lax.broadcasted_iota(jnp.int32, (n, n), 1)
    return jnp.where(r > c, 1.0, 0.0).astype(BF16)


def _sb_kernel(q_ref, kp_ref, vp_ref, km_ref, vm_ref, o_ref, c_ref, acc_ref,
               *, tq, tk, tkp, n_pref, pref_valid):
    i = pl.program_id(2)
    q = q_ref[0]
    lane = lax.broadcasted_iota(jnp.int32, q.shape, 1)
    zero = jnp.zeros_like(q)
    qs = (jnp.where(lane < HEAD_DIM, q, zero), jnp.where(lane >= HEAD_DIM, q, zero))
    c_ref[...] = jnp.zeros(c_ref.shape, F32)
    acc_ref[...] = jnp.zeros(acc_ref.shape, F32)

    def block(kblk, vblk, mask, tri):
        vlane = lax.broadcasted_iota(jnp.int32, vblk.shape, 1)
        vzero = jnp.zeros_like(vblk)
        vs = (jnp.where(vlane < HEAD_DIM, vblk, vzero), jnp.where(vlane >= HEAD_DIM, vblk, vzero))
        for a in range(2):
            z = lax.dot_general(qs[a], kblk, _NT, preferred_element_type=F32)
            log1m = -(jnp.maximum(z, 0.0) + jnp.log(1.0 + jnp.exp(-jnp.abs(z))))
            if mask is not None:
                log1m = jnp.where(mask, log1m, 0.0)
            hi = log1m.astype(BF16)
            lo = (log1m - hi.astype(F32)).astype(BF16)
            suffix = (jnp.dot(hi, tri, preferred_element_type=F32)
                      + jnp.dot(lo, tri, preferred_element_type=F32))
            t = z + log1m + suffix + c_ref[a]
            if mask is not None:
                t = jnp.where(mask, t, NEG)
            w = jnp.exp(t)
            acc_ref[...] += jnp.dot(w.astype(BF16), vs[a], preferred_element_type=F32)
            c_ref[a] += jnp.sum(log1m, axis=1, keepdims=True)

    n_diag = tq // tk
    tri_main = _strict_lower(tk)
    row = lax.broadcasted_iota(jnp.int32, (tq, tk), 0)
    col = lax.broadcasted_iota(jnp.int32, (tq, tk), 1)
    for d in reversed(range(n_diag)):
        off = pl.multiple_of((i * n_diag + d) * tk, tk)
        block(km_ref[0, pl.ds(off, tk), :], vm_ref[0, pl.ds(off, tk), :], col + d * tk < row, tri_main)

    def full_body(t, carry):
        off = pl.multiple_of((i * n_diag - 1 - t) * tk, tk)
        block(km_ref[0, pl.ds(off, tk), :], vm_ref[0, pl.ds(off, tk), :], None, tri_main)
        return carry
    lax.fori_loop(0, i * n_diag, full_body, 0)

    tri_pref = tri_main if tkp == tk else _strict_lower(tkp)
    if n_pref == 1:
        pmask = None
        if pref_valid < tkp:
            pmask = lax.broadcasted_iota(jnp.int32, (tq, tkp), 1) < pref_valid
        block(kp_ref[0], vp_ref[0], pmask, tri_pref)
    else:
        def pref_body(t, carry):
            off = pl.multiple_of((n_pref - 1 - t) * tkp, tkp)
            block(kp_ref[0, pl.ds(off, tkp), :], vp_ref[0, pl.ds(off, tkp), :], None, tri_pref)
            return carry
        lax.fori_loop(0, n_pref, pref_body, 0)

    o_ref[0] = acc_ref[...].astype(BF16)


def _sb_attn(q, kp, vp, km, vm, *, tq, tk, tkp, pref_valid, pref_shared):
    nb, tq_total, _ = q.shape
    p_len = kp.shape[1]
    t_main = km.shape[1]
    n_pref = p_len // tkp
    kernel = functools.partial(_sb_kernel, tq=tq, tk=tk, tkp=tkp, n_pref=n_pref, pref_valid=pref_valid)
    pref_idx = (lambda b, h, i: (0, 0, h)) if pref_shared else (lambda b, h, i: (b, 0, h))
    return pl.pallas_call(
        kernel,
        grid=(nb, SB_WIDTH // LANES, tq_total // tq),
        in_specs=[pl.BlockSpec((1, tq, LANES), lambda b, h, i: (b, i, h)),
                  pl.BlockSpec((1, p_len, LANES), pref_idx),
                  pl.BlockSpec((1, p_len, LANES), pref_idx),
                  pl.BlockSpec((1, t_main, LANES), lambda b, h, i: (b, 0, h)),
                  pl.BlockSpec((1, t_main, LANES), lambda b, h, i: (b, 0, h))],
        out_specs=pl.BlockSpec((1, tq, LANES), lambda b, h, i: (b, i, h)),
        out_shape=jax.ShapeDtypeStruct((nb, tq_total, SB_WIDTH), BF16),
        scratch_shapes=[pltpu.VMEM((2, tq, 1), F32), pltpu.VMEM((tq, LANES), F32)],
        compiler_params=pltpu.CompilerParams(dimension_semantics=("arbitrary",) * 3,
                                             vmem_limit_bytes=VMEM_LIMIT),
        name="sb_attn",
    )(q, kp, vp, km, vm)


def _out_kernel(x_ref, od_ref, os_ref, gmix_ref, wg_ref, wdo_ref, wso_ref, wo_ref, gffn_ref,
                w1_ref, w2_ref, y_ref):
    x = x_ref[...]
    h = (_rms(x) * gmix_ref[...]).astype(BF16)
    gate = jax.nn.sigmoid(jnp.dot(h, wg_ref[...], preferred_element_type=F32))
    a = jnp.dot(od_ref[...], wdo_ref[...], preferred_element_type=F32)
    b = jnp.dot(os_ref[...], wso_ref[...], preferred_element_type=F32)
    merged = (gate[:, :D_MODEL] * a + gate[:, D_MODEL:] * b).astype(BF16)
    x1 = x + jnp.dot(merged, wo_ref[...], preferred_element_type=F32)
    h2 = (_rms(x1) * gffn_ref[...]).astype(BF16)
    y = x1
    for c in range(D_FF // D_MODEL):
        sl = slice(c * D_MODEL, (c + 1) * D_MODEL)
        f = jnp.maximum(jnp.dot(h2, w1_ref[:, sl], preferred_element_type=F32), 0.0)
        y = y + jnp.dot((f * f).astype(BF16), w2_ref[sl, :], preferred_element_type=F32)
    y_ref[...] = y


def _out(x, od, osb, gmix, wg, wdo, wso, wo, gffn, w1, w2, *, tm):
    rows = x.shape[0]
    row = lambda w: pl.BlockSpec((tm, w), lambda i: (i, 0))
    return pl.pallas_call(
        _out_kernel,
        grid=(rows // tm,),
        in_specs=[row(D_MODEL), row(DIFF_WIDTH), row(SB_WIDTH), _const_spec((1, D_MODEL)),
                  _const_spec((D_MODEL, 2 * D_MODEL)), _const_spec((DIFF_WIDTH, D_MODEL)),
                  _const_spec((SB_WIDTH, D_MODEL)), _const_spec((D_MODEL, D_MODEL)),
                  _const_spec((1, D_MODEL)), _const_spec((D_MODEL, D_FF)), _const_spec((D_FF, D_MODEL))],
        out_specs=row(D_MODEL),
        out_shape=jax.ShapeDtypeStruct((rows, D_MODEL), F32),
        compiler_params=pltpu.CompilerParams(dimension_semantics=("arbitrary",),
                                             vmem_limit_bytes=VMEM_LIMIT),
        name="out",
    )(x, od, osb, gmix, wg, wdo, wso, wo, gffn, w1, w2)


def _rope_tables(pos):
    half = ROT_DIM // 2
    inv = ROPE_THETA ** (-jnp.arange(0, ROT_DIM, 2, dtype=F32) / ROT_DIM)
    ang = pos.astype(F32)[:, None] * inv[None, :]
    cos, sin = jnp.cos(ang), jnp.sin(ang)
    n = pos.shape[0]
    pad = jnp.zeros((n, HEAD_DIM - ROT_DIM), F32)
    z8 = jnp.zeros((n, half), F32)
    cos64 = jnp.concatenate([cos, cos, pad + 1.0], axis=1)
    sa64 = jnp.concatenate([-sin, z8, pad], axis=1)
    sb64 = jnp.concatenate([z8, sin, pad], axis=1)
    rep = lambda t: jnp.concatenate([t, t], axis=1)
    return rep(cos64), rep(sa64), rep(sb64)


def kernel(x_prompt, x_sample, cache_diff_k, cache_diff_v, cache_sb_k, cache_sb_v, meta_tokens,
           g_mix, w_in, q_norm_g, k_norm_g, lam_q1, lam_k1, lam_q2, lam_k2, sub_g,
           w_diff_out, w_sb_out, w_out, g_ffn, w_ff1, w_ff2):
    nb, seq, _ = x_prompt.shape
    db, dseq, _ = x_sample.shape
    past = cache_diff_k.shape[2]
    lyr = 0

    w_in_b = w_in[lyr].astype(BF16)
    w_qkv, w_gate = w_in_b[:, :QKV_COLS], w_in_b[:, QKV_COLS:]
    wdo, wso, wo = (w_diff_out[lyr].astype(BF16), w_sb_out[lyr].astype(BF16), w_out[lyr].astype(BF16))
    w1, w2 = w_ff1[lyr].astype(BF16), w_ff2[lyr].astype(BF16)
    gmix = g_mix[lyr].reshape(1, D_MODEL)
    gffn = g_ffn[lyr].reshape(1, D_MODEL)
    qg = jnp.tile(q_norm_g[lyr], COL_BLOCK // HEAD_DIM).reshape(1, COL_BLOCK)
    kg = jnp.tile(k_norm_g[lyr], COL_BLOCK // HEAD_DIM).reshape(1, COL_BLOCK)
    subg = sub_g[lyr].reshape(1, LANES)
    lams = [t[lyr].reshape(1, HEAD_DIM) for t in (lam_q1, lam_k1, lam_q2, lam_k2)]
    grp = jnp.arange(COL_BLOCK, dtype=jnp.int32) // HEAD_DIM
    gmat = jnp.where(grp[:, None] == grp[None, :], 1.0 / HEAD_DIM, 0.0).astype(BF16)

    main_pos = N_META + jnp.arange(seq, dtype=jnp.int32)
    small_pos = jnp.concatenate([jnp.arange(N_META, dtype=jnp.int32),
                                 jnp.tile(past + jnp.arange(dseq, dtype=jnp.int32), db)])
    x_main = x_prompt.reshape(nb * seq, D_MODEL)
    x_small = jnp.concatenate([meta_tokens.astype(F32), x_sample.reshape(db * dseq, D_MODEL)], axis=0)
    pm = _proj(x_main, gmix, w_qkv, qg, kg, gmat, *_rope_tables(main_pos), tm=512)
    ps = _proj(x_small, gmix, w_qkv, qg, kg, gmat, *_rope_tables(small_pos), tm=x_small.shape[0])
    qd_m, kd_m, kdb_m, vd_m, vdb_m, qs_m, ks_m, ksb_m, vs_m, vsb_m = pm
    qd_s, kd_s, kdb_s, vd_s, vdb_s, qs_s, ks_s, ksb_s, vs_s, vsb_s = ps

    def bt(a, n, t):
        return a.reshape(n, t, a.shape[-1])

    def meta_prefix(a):
        return jnp.pad(a[:N_META], ((0, LANES - N_META), (0, 0)))[None]

    od_p = _diff_attn(lams, subg, bt(qd_m, nb, seq), meta_prefix(kdb_s), meta_prefix(vdb_s),
                      bt(kdb_m, nb, seq), bt(vdb_m, nb, seq),
                      tq=512, tk=512, tkp=LANES, pref_valid=N_META, pref_shared=True)
    os_p = _sb_attn(bt(qs_m, nb, seq), meta_prefix(ksb_s), meta_prefix(vsb_s),
                    bt(ksb_m, nb, seq), bt(vsb_m, nb, seq),
                    tq=512, tk=256, tkp=LANES, pref_valid=N_META, pref_shared=True)

    smp = lambda a: bt(a[N_META:], db, dseq)
    cdk = cache_diff_k[lyr].reshape(db, past, DIFF_WIDTH).astype(BF16)
    cdv = cache_diff_v[lyr].reshape(db, past, DIFF_WIDTH).astype(BF16)
    csk = cache_sb_k[lyr].reshape(db, past, SB_WIDTH).astype(BF16)
    csv = cache_sb_v[lyr].reshape(db, past, SB_WIDTH).astype(BF16)
    od_s = _diff_attn(lams, subg, smp(qd_s), cdk, cdv, smp(kdb_s), smp(vdb_s),
                      tq=dseq, tk=dseq, tkp=512, pref_valid=512, pref_shared=False)
    os_s = _sb_attn(smp(qs_s), csk, csv, smp(ksb_s), smp(vsb_s),
                    tq=dseq, tk=dseq, tkp=256, pref_valid=256, pref_shared=False)

    y_p = _out(x_main, od_p.reshape(nb * seq, DIFF_WIDTH), os_p.reshape(nb * seq, SB_WIDTH),
               gmix, w_gate, wdo, wso, wo, gffn, w1, w2, tm=512)
    y_s = _out(x_sample.reshape(db * dseq, D_MODEL), od_s.reshape(db * dseq, DIFF_WIDTH),
               os_s.reshape(db * dseq, SB_WIDTH), gmix, w_gate, wdo, wso, wo, gffn, w1, w2, tm=db * dseq)

    def prompt_cache(main, small, dim):
        meta = jnp.broadcast_to(small[:N_META][None], (nb, N_META, small.shape[-1]))
        full = jnp.concatenate([meta, bt(main, nb, seq)], axis=1)
        return full.reshape(1, nb, seq + N_META, N_HEADS, dim)

    def sample_cache(small, dim):
        return small[N_META:].reshape(1, db, dseq, N_HEADS, dim)

    return (y_p.reshape(nb, seq, D_MODEL), y_s.reshape(db, dseq, D_MODEL),
            prompt_cache(kd_m, kd_s, 2 * HEAD_DIM), prompt_cache(vd_m, vd_s, 2 * HEAD_DIM),
            prompt_cache(ks_m, ks_s, HEAD_DIM), prompt_cache(vs_m, vs_s, HEAD_DIM),
            sample_cache(kd_s, 2 * HEAD_DIM), sample_cache(vd_s, 2 * HEAD_DIM),
            sample_cache(ks_s, HEAD_DIM), sample_cache(vs_s, HEAD_DIM))
```

```python
import functools
import math

import jax
import jax.numpy as jnp
from jax import lax
from jax.experimental import pallas as pl
from jax.experimental.pallas import tpu as pltpu

F32 = jnp.float32
BF16 = jnp.bfloat16

D_MODEL = 1024
N_META = 16
CHUNK = 64
N_HEADS = 8
HEAD_DIM = 64
DIFF_WIDTH = N_HEADS * 2 * HEAD_DIM
SB_WIDTH = N_HEADS * HEAD_DIM
QKV_COLS = 3 * DIFF_WIDTH + 3 * SB_WIDTH
D_FF = 4 * D_MODEL
ROT_DIM = HEAD_DIM // 4
ROPE_THETA = 500000.0
EPS = 1e-6
NEG = -1e30
LOG2E = math.log2(math.e)
Q_SCALE = HEAD_DIM ** -0.5 * LOG2E
ONES_ROWS = 16
LAM_INIT = 0.8 - 0.6 * math.exp(-0.3 * 0)

LANES = 128
COL_BLOCK = 512
VMEM_LIMIT = 56 * 1024 * 1024

_NT = (((1,), (1,)), ((), ()))


def _rms(x):
    return x * lax.rsqrt(jnp.mean(x * x, axis=-1, keepdims=True) + EPS)


def _const_spec(shape):
    return pl.BlockSpec(shape, lambda *_: (0,) * len(shape), pipeline_mode=pl.Buffered(1))


def _proj_kernel(x_ref, gmix_ref, w_ref, qg_ref, kg_ref, gmat_ref, cos_ref, sa_ref, sb_ref,
                 qd_ref, kd_ref, kdb_ref, vd_ref, vdb_ref, qs_ref, ks_ref, ksb_ref, vs_ref, vsb_ref,
                 *, transpose_v):
    h = (_rms(x_ref[...]) * gmix_ref[...]).astype(BF16)
    cos = cos_ref[...]
    sa = sa_ref[...]
    sb = sb_ref[...]

    def col(j):
        return jnp.dot(h, w_ref[:, j * COL_BLOCK:(j + 1) * COL_BLOCK], preferred_element_type=F32)

    def normed_rot(y, g):
        msq = jnp.dot((y * y).astype(BF16), gmat_ref[...], preferred_element_type=F32)
        yn = y * lax.rsqrt(msq + EPS) * g
        parts = []
        for c in range(COL_BLOCK // LANES):
            t = yn[:, c * LANES:(c + 1) * LANES]
            parts.append(t * cos + pltpu.roll(t, LANES - ROT_DIM // 2, 1) * sa
                         + pltpu.roll(t, ROT_DIM // 2, 1) * sb)
        return jnp.concatenate(parts, axis=1)

    for j in range(2):
        sl = slice(j * COL_BLOCK, (j + 1) * COL_BLOCK)
        q = normed_rot(col(j), qg_ref[...])
        qd_ref[:, sl] = (q * Q_SCALE).astype(BF16)
        k = normed_rot(col(2 + j), kg_ref[...])
        kd_ref[:, sl] = k
        kdb_ref[:, sl] = k.astype(BF16)
        v = col(4 + j)
        vd_ref[:, sl] = v
        if transpose_v:
            vdb_ref[sl, :] = v.T.astype(BF16)
        else:
            vdb_ref[:, sl] = v.astype(BF16)
    qs_ref[...] = (col(6) * Q_SCALE).astype(BF16)
    k = col(7)
    ks_ref[...] = k
    ksb_ref[...] = k.astype(BF16)
    v = col(8)
    vs_ref[...] = v
    vsb_ref[...] = v.T.astype(BF16) if transpose_v else v.astype(BF16)


def _proj(x, gmix, w_qkv, qg, kg, gmat, cos, sa, sb, *, tm, transpose_v):
    rows = x.shape[0]
    n_pos_tiles = cos.shape[0] // tm
    row = lambda w: pl.BlockSpec((tm, w), lambda i: (i, 0))
    tab = pl.BlockSpec((tm, LANES), lambda i: (i % n_pos_tiles, 0))
    wide = lambda dt: jax.ShapeDtypeStruct((rows, DIFF_WIDTH), dt)
    narrow = lambda dt: jax.ShapeDtypeStruct((rows, SB_WIDTH), dt)
    if transpose_v:
        nb = rows // cos.shape[0]
        vt_spec = lambda w: pl.BlockSpec((None, w, tm), lambda i: (i // n_pos_tiles, 0, i % n_pos_tiles))
        vt_shape = lambda w: jax.ShapeDtypeStruct((nb, w, cos.shape[0]), BF16)
    else:
        vt_spec, vt_shape = row, lambda w: jax.ShapeDtypeStruct((rows, w), BF16)
    return pl.pallas_call(
        functools.partial(_proj_kernel, transpose_v=transpose_v),
        grid=(rows // tm,),
        in_specs=[row(D_MODEL), _const_spec((1, D_MODEL)), _const_spec((D_MODEL, QKV_COLS)),
                  _const_spec((1, COL_BLOCK)), _const_spec((1, COL_BLOCK)),
                  _const_spec((COL_BLOCK, COL_BLOCK)), tab, tab, tab],
        out_specs=[row(DIFF_WIDTH)] * 4 + [vt_spec(DIFF_WIDTH)] + [row(SB_WIDTH)] * 4 + [vt_spec(SB_WIDTH)],
        out_shape=[wide(BF16), wide(F32), wide(BF16), wide(F32), vt_shape(DIFF_WIDTH),
                   narrow(BF16), narrow(F32), narrow(BF16), narrow(F32), vt_shape(SB_WIDTH)],
        compiler_params=pltpu.CompilerParams(dimension_semantics=("arbitrary",),
                                             vmem_limit_bytes=VMEM_LIMIT),
        name="proj",
    )(x, gmix, w_qkv, qg, kg, gmat, cos, sa, sb)


def _aligned(off, m):
    return off if isinstance(off, int) else pl.multiple_of(off, m)


def _split_halves(q):
    lane = lax.broadcasted_iota(jnp.int32, q.shape, 1)
    zero = jnp.zeros_like(q)
    return jnp.where(lane < HEAD_DIM, q, zero), jnp.where(lane >= HEAD_DIM, q, zero)


def _diff_kernel(lq1_ref, lk1_ref, lq2_ref, lk2_ref, subg_ref, q_ref, kp_ref, vp_ref, km_ref, vm_ref,
                 o_ref, m_ref, acc_ref, s_ref,
                 *, tq, tk, tkp, n_pref, pref_valid, n_diag, one_tile):
    i = 0 if one_tile else pl.program_id(2)
    n_full = i * n_diag
    qs = _split_halves(q_ref[0])
    m_ref[...] = jnp.full(m_ref.shape, NEG, F32)
    acc_ref[...] = jnp.zeros(acc_ref.shape, F32)

    def issue(kblk, slot):
        n = kblk.shape[0]
        for a in range(2):
            s_ref[slot, a, :n, :] = lax.dot_general(kblk, qs[a], _NT, preferred_element_type=F32)

    def consume(slot, vtblk, mask):
        n = vtblk.shape[1]
        vt_ones = jnp.concatenate([vtblk, jnp.ones((ONES_ROWS, n), BF16)], axis=0)
        for a in range(2):
            s = s_ref[slot, a, :n, :]
            if mask is not None:
                s = jnp.where(mask, s, NEG)
            m_prev = m_ref[a]
            m_new = jnp.maximum(m_prev, jnp.max(s, axis=0, keepdims=True))
            alpha = jnp.exp2(m_prev - m_new)
            p = jnp.exp2(s - m_new)
            acc_ref[a] = alpha * acc_ref[a] + jnp.dot(vt_ones, p.astype(BF16), preferred_element_type=F32)
            m_ref[a] = m_new

    def pref_k(t):
        return kp_ref[0, pl.ds(_aligned(t * tkp, tkp), tkp), :]

    def pref_vt(t):
        return vp_ref[0, :, pl.ds(_aligned(t * tkp, tkp), tkp)]

    def main_k(g):
        return km_ref[0, pl.ds(_aligned(g * tk, tk), tk), :]

    def main_vt(g):
        return vm_ref[0, :, pl.ds(_aligned(g * tk, tk), tk)]

    if n_pref == 1:
        pmask = None
        if pref_valid < tkp:
            pmask = lax.broadcasted_iota(jnp.int32, (tkp, tq), 0) < pref_valid
        issue(pref_k(0), 1)
        issue(main_k(0), 0)
        consume(1, pref_vt(0), pmask)
    else:
        issue(pref_k(0), 0)

        def pref_body(t, carry):
            issue(pref_k(2 * t + 1), 1)
            consume(0, pref_vt(2 * t), None)
            issue(pref_k(2 * t + 2), 0)
            consume(1, pref_vt(2 * t + 1), None)
            return carry
        lax.fori_loop(0, n_pref // 2 - 1, pref_body, 0)
        issue(pref_k(n_pref - 1), 1)
        consume(0, pref_vt(n_pref - 2), None)
        issue(main_k(0), 0)
        consume(1, pref_vt(n_pref - 1), None)

    def full_body(t, carry):
        issue(main_k(2 * t + 1), 1)
        consume(0, main_vt(2 * t), None)
        issue(main_k(2 * t + 2), 0)
        consume(1, main_vt(2 * t + 1), None)
        return carry
    if not one_tile:
        lax.fori_loop(0, i * (n_diag // 2), full_body, 0)

    key_idx = lax.broadcasted_iota(jnp.int32, (tk, tq), 0)
    q_chunk = lax.broadcasted_iota(jnp.int32, (tk, tq), 1) // CHUNK
    for d in range(n_diag):
        g = n_full + d
        if d + 1 < n_diag:
            issue(main_k(g + 1), (d + 1) % 2)
        consume(d % 2, main_vt(g), (key_idx + d * tk) // CHUNK <= q_chunk)

    lam = (jnp.exp(jnp.sum(lq1_ref[...] * lk1_ref[...], axis=1, keepdims=True))
           - jnp.exp(jnp.sum(lq2_ref[...] * lk2_ref[...], axis=1, keepdims=True)) + LAM_INIT)
    o = (acc_ref[0, :LANES] / acc_ref[0, LANES:LANES + 1]
         - lam * (acc_ref[1, :LANES] / acc_ref[1, LANES:LANES + 1]))
    o = o * lax.rsqrt(jnp.mean(o * o, axis=0, keepdims=True) + EPS) * subg_ref[...] * (1.0 - LAM_INIT)
    o_ref[0] = o.T.astype(BF16)


def _diff_attn(lams, subg_col, q, kp, vtp, km, vtm, *, tq, tk, tkp, pref_valid, pref_shared):
    nb, tq_total, _ = q.shape
    p_len = kp.shape[1]
    t_main = km.shape[1]
    n_pref = p_len // tkp
    n_diag, one_tile = min(tq, t_main) // tk, tq_total == tq
    assert n_pref == 1 or n_pref % 2 == 0, "prefix blocks are consumed in pairs"
    assert one_tile or n_diag % 2 == 0, "main blocks are consumed in pairs"
    kernel = functools.partial(_diff_kernel, tq=tq, tk=tk, tkp=tkp, n_pref=n_pref, pref_valid=pref_valid,
                               n_diag=n_diag, one_tile=one_tile)
    pk_idx = (lambda b, h, i: (0, 0, h)) if pref_shared else (lambda b, h, i: (b, 0, h))
    pv_idx = (lambda b, h, i: (0, h, 0)) if pref_shared else (lambda b, h, i: (b, h, 0))
    small = _const_spec((1, HEAD_DIM))
    return pl.pallas_call(
        kernel,
        grid=(nb, N_HEADS, tq_total // tq),
        in_specs=[small, small, small, small, _const_spec((LANES, 1)),
                  pl.BlockSpec((1, tq, LANES), lambda b, h, i: (b, i, h)),
                  pl.BlockSpec((1, p_len, LANES), pk_idx),
                  pl.BlockSpec((1, LANES, p_len), pv_idx),
                  pl.BlockSpec((1, t_main, LANES), lambda b, h, i: (b, 0, h)),
                  pl.BlockSpec((1, LANES, t_main), lambda b, h, i: (b, h, 0))],
        out_specs=pl.BlockSpec((1, tq, LANES), lambda b, h, i: (b, i, h)),
        out_shape=jax.ShapeDtypeStruct((nb, tq_total, DIFF_WIDTH), BF16),
        scratch_shapes=[pltpu.VMEM((2, 1, tq), F32), pltpu.VMEM((2, LANES + ONES_ROWS, tq), F32),
                        pltpu.VMEM((2, 2, max(tk, tkp), tq), F32)],
        compiler_params=pltpu.CompilerParams(dimension_semantics=("arbitrary",) * 3,
                                             vmem_limit_bytes=VMEM_LIMIT),
        name="diff_attn",
    )(*lams, subg_col, q, kp, vtp, km, vtm)


def _suffix_matrix(n):
    r = lax.broadcasted_iota(jnp.int32, (n + ONES_ROWS, n), 0)
    c = lax.broadcasted_iota(jnp.int32, (n + ONES_ROWS, n), 1)
    return jnp.where((c > r) | (r == n), 1.0, 0.0).astype(BF16)


def _sb_kernel(q_ref, kp_ref, vp_ref, km_ref, vm_ref, o_ref, c_ref, acc_ref, z_ref,
               *, tq, tk, tkp, n_pref, pref_valid, n_diag, one_tile):
    i = 0 if one_tile else pl.program_id(2)
    n_full = i * n_diag
    qs = _split_halves(q_ref[0])
    c_ref[...] = jnp.zeros(c_ref.shape, F32)
    acc_ref[...] = jnp.zeros(acc_ref.shape, F32)

    def issue(kblk, slot):
        n = kblk.shape[0]
        for a in range(2):
            z_ref[slot, a, :n, :] = lax.dot_general(kblk, qs[a], _NT, preferred_element_type=F32)

    def consume(slot, vtblk, mask, sfx_mat):
        n = vtblk.shape[1]
        for a in range(2):
            rows = slice(a * HEAD_DIM, (a + 1) * HEAD_DIM)
            u = z_ref[slot, a, :n, :]
            nu = -u
            log1m = jnp.minimum(nu, 0.0) - jnp.log(1.0 + jnp.exp2(jnp.minimum(u, nu))) * LOG2E
            if mask is not None:
                log1m = jnp.where(mask, log1m, 0.0)
            sfx = jnp.dot(sfx_mat, log1m.astype(BF16), preferred_element_type=F32)
            t = u + log1m + sfx[:n]
            if mask is not None:
                t = jnp.where(mask, t, NEG)
            pv = jnp.dot(vtblk[rows, :], jnp.exp2(t).astype(BF16), preferred_element_type=F32)
            acc_ref[rows, :] += pv * jnp.exp2(c_ref[a])
            c_ref[a] += sfx[n:n + 1]

    def pref_k(t):
        return kp_ref[0, pl.ds(_aligned(t * tkp, tkp), tkp), :]

    def pref_vt(t):
        return vp_ref[0, :, pl.ds(_aligned(t * tkp, tkp), tkp)]

    def main_k(g):
        return km_ref[0, pl.ds(_aligned(g * tk, tk), tk), :]

    def main_vt(g):
        return vm_ref[0, :, pl.ds(_aligned(g * tk, tk), tk)]

    sfx_main = _suffix_matrix(tk)
    sfx_pref = sfx_main if tkp == tk else _suffix_matrix(tkp)
    key = lax.broadcasted_iota(jnp.int32, (tk, tq), 0)
    qry = lax.broadcasted_iota(jnp.int32, (tk, tq), 1)

    if n_pref == 1:
        issue(pref_k(0), 2)
    issue(main_k(n_full + n_diag - 1), 0)
    for d in reversed(range(n_diag)):
        s = (n_diag - 1 - d) % 2
        if d > 0:
            issue(main_k(n_full + d - 1), 1 - s)
        elif not one_tile:
            issue(main_k(jnp.maximum(n_full - 1, 0)), 1 - s)
        elif n_pref > 1:
            issue(pref_k(n_pref - 1), 1 - s)
        consume(s, main_vt(n_full + d), key + d * tk < qry, sfx_main)

    if not one_tile:
        assert n_diag % 2 == 0
        def full_body(t, carry):
            g = n_full - 1 - 2 * t
            issue(main_k(g - 1), 1)
            consume(0, main_vt(g), None, sfx_main)
            issue(main_k(jnp.maximum(g - 2, 0)), 0)
            consume(1, main_vt(g - 1), None, sfx_main)
            return carry
        lax.fori_loop(0, i * (n_diag // 2), full_body, 0)

    if n_pref == 1:
        pmask = None
        if pref_valid < tkp:
            pmask = lax.broadcasted_iota(jnp.int32, (tkp, tq), 0) < pref_valid
        consume(2, pref_vt(0), pmask, sfx_pref)
    else:
        assert one_tile and n_diag == 1 and n_pref % 2 == 0
        def pref_body(t, carry):
            g = n_pref - 1 - 2 * t
            issue(pref_k(g - 1), 0)
            consume(1, pref_vt(g), None, sfx_pref)
            issue(pref_k(jnp.maximum(g - 2, 0)), 1)
            consume(0, pref_vt(g - 1), None, sfx_pref)
            return carry
        lax.fori_loop(0, n_pref // 2, pref_body, 0)

    o_ref[0] = acc_ref[...].T.astype(BF16)


def _sb_attn(q, kp, vtp, km, vtm, *, tq, tk, tkp, pref_valid, pref_shared):
    nb, tq_total, _ = q.shape
    p_len = kp.shape[1]
    t_main = km.shape[1]
    n_pref = p_len // tkp
    kernel = functools.partial(_sb_kernel, tq=tq, tk=tk, tkp=tkp, n_pref=n_pref, pref_valid=pref_valid,
                               n_diag=min(tq, t_main) // tk, one_tile=tq_total == tq)
    pk_idx = (lambda b, h, i: (0, 0, h)) if pref_shared else (lambda b, h, i: (b, 0, h))
    pv_idx = (lambda b, h, i: (0, h, 0)) if pref_shared else (lambda b, h, i: (b, h, 0))
    return pl.pallas_call(
        kernel,
        grid=(nb, SB_WIDTH // LANES, tq_total // tq),
        in_specs=[pl.BlockSpec((1, tq, LANES), lambda b, h, i: (b, i, h)),
                  pl.BlockSpec((1, p_len, LANES), pk_idx),
                  pl.BlockSpec((1, LANES, p_len), pv_idx),
                  pl.BlockSpec((1, t_main, LANES), lambda b, h, i: (b, 0, h)),
                  pl.BlockSpec((1, LANES, t_main), lambda b, h, i: (b, h, 0))],
        out_specs=pl.BlockSpec((1, tq, LANES), lambda b, h, i: (b, i, h)),
        out_shape=jax.ShapeDtypeStruct((nb, tq_total, SB_WIDTH), BF16),
        scratch_shapes=[pltpu.VMEM((2, 1, tq), F32), pltpu.VMEM((LANES, tq), F32),
                        pltpu.VMEM((3 if n_pref == 1 else 2, 2, max(tk, tkp), tq), F32)],
        compiler_params=pltpu.CompilerParams(dimension_semantics=("arbitrary",) * 3,
                                             vmem_limit_bytes=VMEM_LIMIT),
        name="sb_attn",
    )(q, kp, vtp, km, vtm)


def _out_kernel(x_ref, od_ref, os_ref, gmix_ref, wg_ref, wdo_ref, wso_ref, wo_ref, gffn_ref,
                w1_ref, w2_ref, y_ref):
    x = x_ref[...]
    h = (_rms(x) * gmix_ref[...]).astype(BF16)
    gate = jax.nn.sigmoid(jnp.dot(h, wg_ref[...], preferred_element_type=F32))
    a = jnp.dot(od_ref[...], wdo_ref[...], preferred_element_type=F32)
    b = jnp.dot(os_ref[...], wso_ref[...], preferred_element_type=F32)
    merged = (gate[:, :D_MODEL] * a + gate[:, D_MODEL:] * b).astype(BF16)
    x1 = x + jnp.dot(merged, wo_ref[...], preferred_element_type=F32)
    h2 = (_rms(x1) * gffn_ref[...]).astype(BF16)
    y = x1
    for c in range(D_FF // D_MODEL):
        sl = slice(c * D_MODEL, (c + 1) * D_MODEL)
        f = jnp.maximum(jnp.dot(h2, w1_ref[:, sl], preferred_element_type=F32), 0.0)
        y = y + jnp.dot((f * f).astype(BF16), w2_ref[sl, :], preferred_element_type=F32)
    y_ref[...] = y


def _out(x, od, osb, gmix, wg, wdo, wso, wo, gffn, w1, w2, *, tm):
    rows = x.shape[0]
    row = lambda w: pl.BlockSpec((tm, w), lambda i: (i, 0))
    return pl.pallas_call(
        _out_kernel,
        grid=(rows // tm,),
        in_specs=[row(D_MODEL), row(DIFF_WIDTH), row(SB_WIDTH), _const_spec((1, D_MODEL)),
                  _const_spec((D_MODEL, 2 * D_MODEL)), _const_spec((DIFF_WIDTH, D_MODEL)),
                  _const_spec((SB_WIDTH, D_MODEL)), _const_spec((D_MODEL, D_MODEL)),
                  _const_spec((1, D_MODEL)), _const_spec((D_MODEL, D_FF)), _const_spec((D_FF, D_MODEL))],
        out_specs=row(D_MODEL),
        out_shape=jax.ShapeDtypeStruct((rows, D_MODEL), F32),
        compiler_params=pltpu.CompilerParams(dimension_semantics=("arbitrary",),
                                             vmem_limit_bytes=VMEM_LIMIT),
        name="out",
    )(x, od, osb, gmix, wg, wdo, wso, wo, gffn, w1, w2)


def _rope_tables(pos):
    half = ROT_DIM // 2
    inv = ROPE_THETA ** (-jnp.arange(0, ROT_DIM, 2, dtype=F32) / ROT_DIM)
    ang = pos.astype(F32)[:, None] * inv[None, :]
    cos, sin = jnp.cos(ang), jnp.sin(ang)
    n = pos.shape[0]
    pad = jnp.zeros((n, HEAD_DIM - ROT_DIM), F32)
    z8 = jnp.zeros((n, half), F32)
    cos64 = jnp.concatenate([cos, cos, pad + 1.0], axis=1)
    sa64 = jnp.concatenate([-sin, z8, pad], axis=1)
    sb64 = jnp.concatenate([z8, sin, pad], axis=1)
    rep = lambda t: jnp.concatenate([t, t], axis=1)
    return rep(cos64), rep(sa64), rep(sb64)


def kernel(x_prompt, x_sample, cache_diff_k, cache_diff_v, cache_sb_k, cache_sb_v, meta_tokens,
           g_mix, w_in, q_norm_g, k_norm_g, lam_q1, lam_k1, lam_q2, lam_k2, sub_g,
           w_diff_out, w_sb_out, w_out, g_ffn, w_ff1, w_ff2):
    nb, seq, _ = x_prompt.shape
    db, dseq, _ = x_sample.shape
    past = cache_diff_k.shape[2]
    lyr = 0

    w_in_b = w_in[lyr].astype(BF16)
    w_qkv, w_gate = w_in_b[:, :QKV_COLS], w_in_b[:, QKV_COLS:]
    wdo, wso, wo = (w_diff_out[lyr].astype(BF16), w_sb_out[lyr].astype(BF16), w_out[lyr].astype(BF16))
    w1, w2 = w_ff1[lyr].astype(BF16), w_ff2[lyr].astype(BF16)
    gmix = g_mix[lyr].reshape(1, D_MODEL)
    gffn = g_ffn[lyr].reshape(1, D_MODEL)
    qg = jnp.tile(q_norm_g[lyr], COL_BLOCK // HEAD_DIM).reshape(1, COL_BLOCK)
    kg = jnp.tile(k_norm_g[lyr], COL_BLOCK // HEAD_DIM).reshape(1, COL_BLOCK)
    subg_col = sub_g[lyr].reshape(LANES, 1)
    lams = [t[lyr].reshape(1, HEAD_DIM) for t in (lam_q1, lam_k1, lam_q2, lam_k2)]
    grp = jnp.arange(COL_BLOCK, dtype=jnp.int32) // HEAD_DIM
    gmat = jnp.where(grp[:, None] == grp[None, :], 1.0 / HEAD_DIM, 0.0).astype(BF16)

    main_pos = N_META + jnp.arange(seq, dtype=jnp.int32)
    small_pos = jnp.concatenate([jnp.arange(N_META, dtype=jnp.int32),
                                 jnp.tile(past + jnp.arange(dseq, dtype=jnp.int32), db)])
    x_main = x_prompt.reshape(nb * seq, D_MODEL)
    x_small = jnp.concatenate([meta_tokens.astype(F32), x_sample.reshape(db * dseq, D_MODEL)], axis=0)
    pm = _proj(x_main, gmix, w_qkv, qg, kg, gmat, *_rope_tables(main_pos), tm=512, transpose_v=True)
    ps = _proj(x_small, gmix, w_qkv, qg, kg, gmat, *_rope_tables(small_pos), tm=x_small.shape[0],
               transpose_v=False)
    qd_m, kd_m, kdb_m, vd_m, vdt_m, qs_m, ks_m, ksb_m, vs_m, vst_m = pm
    qd_s, kd_s, kdb_s, vd_s, vdb_s, qs_s, ks_s, ksb_s, vs_s, vsb_s = ps

    def bt(a, n, t):
        return a.reshape(n, t, a.shape[-1])

    def meta_keys(a):
        return jnp.pad(a[:N_META], ((0, LANES - N_META), (0, 0)))[None]

    def meta_vals_t(a):
        return jnp.swapaxes(meta_keys(a), 1, 2)

    od_p = _diff_attn(lams, subg_col, bt(qd_m, nb, seq), meta_keys(kdb_s), meta_vals_t(vdb_s),
                      bt(kdb_m, nb, seq), vdt_m,
                      tq=512, tk=256, tkp=LANES, pref_valid=N_META, pref_shared=True)
    os_p = _sb_attn(bt(qs_m, nb, seq), meta_keys(ksb_s), meta_vals_t(vsb_s),
                    bt(ksb_m, nb, seq), vst_m,
                    tq=512, tk=256, tkp=LANES, pref_valid=N_META, pref_shared=True)

    smp = lambda a: bt(a[N_META:], db, dseq)
    smp_q = lambda a: jnp.pad(smp(a), ((0, 0), (0, LANES - dseq), (0, 0)))
    smp_vt = lambda a: jnp.swapaxes(smp(a), 1, 2)
    cdk = cache_diff_k[lyr].reshape(db, past, DIFF_WIDTH).astype(BF16)
    cdv = jnp.swapaxes(cache_diff_v[lyr].reshape(db, past, DIFF_WIDTH), 1, 2).astype(BF16)
    csk = cache_sb_k[lyr].reshape(db, past, SB_WIDTH).astype(BF16)
    csv = jnp.swapaxes(cache_sb_v[lyr].reshape(db, past, SB_WIDTH), 1, 2).astype(BF16)
    od_s = _diff_attn(lams, subg_col, smp_q(qd_s), cdk, cdv, smp(kdb_s), smp_vt(vdb_s),
                      tq=LANES, tk=dseq, tkp=512, pref_valid=512, pref_shared=False)[:, :dseq]
    os_s = _sb_attn(smp_q(qs_s), csk, csv, smp(ksb_s), smp_vt(vsb_s),
                    tq=LANES, tk=dseq, tkp=256, pref_valid=256, pref_shared=False)[:, :dseq]

    y_p = _out(x_main, od_p.reshape(nb * seq, DIFF_WIDTH), os_p.reshape(nb * seq, SB_WIDTH),
               gmix, w_gate, wdo, wso, wo, gffn, w1, w2, tm=512)
    y_s = _out(x_sample.reshape(db * dseq, D_MODEL), od_s.reshape(db * dseq, DIFF_WIDTH),
               os_s.reshape(db * dseq, SB_WIDTH), gmix, w_gate, wdo, wso, wo, gffn, w1, w2, tm=db * dseq)

    def prompt_cache(main, small, dim):
        meta = jnp.broadcast_to(small[:N_META][None], (nb, N_META, small.shape[-1]))
        full = jnp.concatenate([meta, bt(main, nb, seq)], axis=1)
        return full.reshape(1, nb, seq + N_META, N_HEADS, dim)

    def sample_cache(small, dim):
        return small[N_META:].reshape(1, db, dseq, N_HEADS, dim)

    return (y_p.reshape(nb, seq, D_MODEL), y_s.reshape(db, dseq, D_MODEL),
            prompt_cache(kd_m, kd_s, 2 * HEAD_DIM), prompt_cache(vd_m, vd_s, 2 * HEAD_DIM),
            prompt_cache(ks_m, ks_s, HEAD_DIM), prompt_cache(vs_m, vs_s, HEAD_DIM),
            sample_cache(kd_s, 2 * HEAD_DIM), sample_cache(vd_s, 2 * HEAD_DIM),
            sample_cache(ks_s, HEAD_DIM), sample_cache(vs_s, HEAD_DIM))
```

```python
import functools
import math

import jax
import jax.numpy as jnp
from jax import lax
from jax.experimental import pallas as pl
from jax.experimental.pallas import tpu as pltpu

F32 = jnp.float32
BF16 = jnp.bfloat16

D_MODEL = 1024
N_META = 16
CHUNK = 64
N_HEADS = 8
HEAD_DIM = 64
DIFF_WIDTH = N_HEADS * 2 * HEAD_DIM
SB_WIDTH = N_HEADS * HEAD_DIM
QKV_COLS = 3 * DIFF_WIDTH + 3 * SB_WIDTH
D_FF = 4 * D_MODEL
ROT_DIM = HEAD_DIM // 4
ROPE_THETA = 500000.0
EPS = 1e-6
NEG = -1e30
LOG2E = math.log2(math.e)
Q_SCALE = HEAD_DIM ** -0.5 * LOG2E
ONES_ROWS = 16
LAM_INIT = 0.8 - 0.6 * math.exp(-0.3 * 0)

LANES = 128
COL_BLOCK = 512
VMEM_LIMIT = 56 * 1024 * 1024

_NT = (((1,), (1,)), ((), ()))


def _rms(x):
    return x * lax.rsqrt(jnp.mean(x * x, axis=-1, keepdims=True) + EPS)


def _const_spec(shape):
    return pl.BlockSpec(shape, lambda *_: (0,) * len(shape), pipeline_mode=pl.Buffered(1))


def _proj_kernel(x_ref, gmix_ref, w_ref, qg_ref, kg_ref, gmat_ref, cos_ref, sa_ref, sb_ref,
                 qd_ref, kd_ref, kdb_ref, vd_ref, vdb_ref, qs_ref, ks_ref, ksb_ref, vs_ref, vsb_ref,
                 *, prompt_layout):
    tm = x_ref.shape[0]
    heads_per_block = COL_BLOCK // LANES

    def store_diff(ref, j, y):
        if not prompt_layout:
            ref[:, j * COL_BLOCK:(j + 1) * COL_BLOCK] = y
            return
        for hh in range(heads_per_block):
            ref[pl.ds(j * heads_per_block + hh, tm, stride=N_HEADS), :] = y[:, hh * LANES:(hh + 1) * LANES]

    h = (_rms(x_ref[...]) * gmix_ref[...]).astype(BF16)
    cos = cos_ref[...]
    sa = sa_ref[...]
    sb = sb_ref[...]

    def col(j):
        return jnp.dot(h, w_ref[:, j * COL_BLOCK:(j + 1) * COL_BLOCK], preferred_element_type=F32)

    def normed_rot(y, g):
        msq = jnp.dot((y * y).astype(BF16), gmat_ref[...], preferred_element_type=F32)
        yn = y * lax.rsqrt(msq + EPS) * g
        parts = []
        for c in range(COL_BLOCK // LANES):
            t = yn[:, c * LANES:(c + 1) * LANES]
            parts.append(t * cos + pltpu.roll(t, LANES - ROT_DIM // 2, 1) * sa
                         + pltpu.roll(t, ROT_DIM // 2, 1) * sb)
        return jnp.concatenate(parts, axis=1)

    for j in range(2):
        sl = slice(j * COL_BLOCK, (j + 1) * COL_BLOCK)
        q = normed_rot(col(j), qg_ref[...])
        qd_ref[:, sl] = (q * Q_SCALE).astype(BF16)
        k = normed_rot(col(2 + j), kg_ref[...])
        store_diff(kd_ref, j, k)
        kdb_ref[:, sl] = k.astype(BF16)
        v = col(4 + j)
        store_diff(vd_ref, j, v)
        if prompt_layout:
            vdb_ref[sl, :] = v.T.astype(BF16)
        else:
            vdb_ref[:, sl] = v.astype(BF16)
    qs_ref[...] = (col(6) * Q_SCALE).astype(BF16)
    k = col(7)
    ksb_ref[...] = k.astype(BF16)
    v = col(8)
    if prompt_layout:
        ks_ref[...] = k.T
        vt = v.T
        vs_ref[...] = vt
        vsb_ref[...] = vt.astype(BF16)
    else:
        ks_ref[...] = k
        vs_ref[...] = v
        vsb_ref[...] = v.astype(BF16)


def _proj(x, gmix, w_qkv, qg, kg, gmat, cos, sa, sb, *, tm, prompt_layout, lead=0):
    rows = x.shape[0]
    seq = cos.shape[0]
    n_pos_tiles = seq // tm
    row = lambda w: pl.BlockSpec((tm, w), lambda i: (i, 0))
    tab = pl.BlockSpec((tm, LANES), lambda i: (i % n_pos_tiles, 0))
    wide = lambda dt: jax.ShapeDtypeStruct((rows, DIFF_WIDTH), dt)
    narrow = lambda dt: jax.ShapeDtypeStruct((rows, SB_WIDTH), dt)
    if prompt_layout:
        nb = rows // seq
        t_spec = lambda w: pl.BlockSpec((None, w, tm), lambda i: (i // n_pos_tiles, 0, i % n_pos_tiles))
        t_shape = lambda w, dt: jax.ShapeDtypeStruct((nb, w, seq), dt)
        hm_spec = pl.BlockSpec(
            (pl.Element(tm * N_HEADS), pl.Element(LANES)),
            lambda i: (((i // n_pos_tiles) * (seq + lead) + lead + (i % n_pos_tiles) * tm) * N_HEADS, 0))
        hm_shape = jax.ShapeDtypeStruct((nb * (seq + lead) * N_HEADS, LANES), F32)
        out_specs = [row(DIFF_WIDTH), hm_spec, row(DIFF_WIDTH), hm_spec, t_spec(DIFF_WIDTH),
                     row(SB_WIDTH), t_spec(SB_WIDTH), row(SB_WIDTH), t_spec(SB_WIDTH), t_spec(SB_WIDTH)]
        out_shape = [wide(BF16), hm_shape, wide(BF16), hm_shape, t_shape(DIFF_WIDTH, BF16),
                     narrow(BF16), t_shape(SB_WIDTH, F32), narrow(BF16), t_shape(SB_WIDTH, F32),
                     t_shape(SB_WIDTH, BF16)]
    else:
        out_specs = [row(DIFF_WIDTH)] * 5 + [row(SB_WIDTH)] * 5
        out_shape = [wide(BF16), wide(F32), wide(BF16), wide(F32), wide(BF16),
                     narrow(BF16), narrow(F32), narrow(BF16), narrow(F32), narrow(BF16)]
    return pl.pallas_call(
        functools.partial(_proj_kernel, prompt_layout=prompt_layout),
        grid=(rows // tm,),
        in_specs=[row(D_MODEL), _const_spec((1, D_MODEL)), _const_spec((D_MODEL, QKV_COLS)),
                  _const_spec((1, COL_BLOCK)), _const_spec((1, COL_BLOCK)),
                  _const_spec((COL_BLOCK, COL_BLOCK)), tab, tab, tab],
        out_specs=out_specs,
        out_shape=out_shape,
        compiler_params=pltpu.CompilerParams(dimension_semantics=("arbitrary",),
                                             vmem_limit_bytes=VMEM_LIMIT),
        name="proj",
    )(x, gmix, w_qkv, qg, kg, gmat, cos, sa, sb)


def _fill_lead_kernel(lead_ref, big_ref, out_ref):
    del big_ref
    out_ref[...] = lead_ref[...]


def _fill_lead_tokens(big, lead_rows, *, nb):
    n = lead_rows.shape[0]
    per_batch = big.shape[0] // nb
    return pl.pallas_call(
        _fill_lead_kernel,
        grid=(nb,),
        in_specs=[_const_spec((n, LANES)), pl.BlockSpec(memory_space=pl.ANY)],
        out_specs=pl.BlockSpec((pl.Element(n), pl.Element(LANES)), lambda b: (b * per_batch, 0)),
        out_shape=jax.ShapeDtypeStruct(big.shape, big.dtype),
        input_output_aliases={1: 0},
        compiler_params=pltpu.CompilerParams(dimension_semantics=("arbitrary",)),
        name="fill_lead_tokens",
    )(lead_rows, big)


def _aligned(off, m):
    return off if isinstance(off, int) else pl.multiple_of(off, m)


def _split_halves(q):
    lane = lax.broadcasted_iota(jnp.int32, q.shape, 1)
    zero = jnp.zeros_like(q)
    return jnp.where(lane < HEAD_DIM, q, zero), jnp.where(lane >= HEAD_DIM, q, zero)


def _diff_kernel(lq1_ref, lk1_ref, lq2_ref, lk2_ref, subg_ref, q_ref, kp_ref, vp_ref, km_ref, vm_ref,
                 o_ref, m_ref, acc_ref, s_ref,
                 *, tq, tk, tkp, n_pref, pref_valid, n_diag, one_tile):
    i = 0 if one_tile else pl.program_id(2)
    n_full = i * n_diag
    qs = _split_halves(q_ref[0])
    m_ref[...] = jnp.full(m_ref.shape, NEG, F32)
    acc_ref[...] = jnp.zeros(acc_ref.shape, F32)

    def issue(kblk, slot):
        n = kblk.shape[0]
        for a in range(2):
            s_ref[slot, a, :n, :] = lax.dot_general(kblk, qs[a], _NT, preferred_element_type=F32)

    def consume(slot, vtblk, mask):
        n = vtblk.shape[1]
        vt_ones = jnp.concatenate([vtblk, jnp.ones((ONES_ROWS, n), BF16)], axis=0)
        for a in range(2):
            s = s_ref[slot, a, :n, :]
            if mask is not None:
                s = jnp.where(mask, s, NEG)
            m_prev = m_ref[a]
            m_new = jnp.maximum(m_prev, jnp.max(s, axis=0, keepdims=True))
            alpha = jnp.exp2(m_prev - m_new)
            p = jnp.exp2(s - m_new)
            acc_ref[a] = alpha * acc_ref[a] + jnp.dot(vt_ones, p.astype(BF16), preferred_element_type=F32)
            m_ref[a] = m_new

    def pref_k(t):
        return kp_ref[0, pl.ds(_aligned(t * tkp, tkp), tkp), :]

    def pref_vt(t):
        return vp_ref[0, :, pl.ds(_aligned(t * tkp, tkp), tkp)]

    def main_k(g):
        return km_ref[0, pl.ds(_aligned(g * tk, tk), tk), :]

    def main_vt(g):
        return vm_ref[0, :, pl.ds(_aligned(g * tk, tk), tk)]

    if n_pref == 1:
        pmask = None
        if pref_valid < tkp:
            pmask = lax.broadcasted_iota(jnp.int32, (tkp, tq), 0) < pref_valid
        issue(pref_k(0), 1)
        issue(main_k(0), 0)
        consume(1, pref_vt(0), pmask)
    else:
        issue(pref_k(0), 0)

        def pref_body(t, carry):
            issue(pref_k(2 * t + 1), 1)
            consume(0, pref_vt(2 * t), None)
            issue(pref_k(2 * t + 2), 0)
            consume(1, pref_vt(2 * t + 1), None)
            return carry
        lax.fori_loop(0, n_pref // 2 - 1, pref_body, 0)
        issue(pref_k(n_pref - 1), 1)
        consume(0, pref_vt(n_pref - 2), None)
        issue(main_k(0), 0)
        consume(1, pref_vt(n_pref - 1), None)

    def full_body(t, carry):
        issue(main_k(2 * t + 1), 1)
        consume(0, main_vt(2 * t), None)
        issue(main_k(2 * t + 2), 0)
        consume(1, main_vt(2 * t + 1), None)
        return carry
    if not one_tile:
        lax.fori_loop(0, i * (n_diag // 2), full_body, 0)

    key_idx = lax.broadcasted_iota(jnp.int32, (tk, tq), 0)
    q_chunk = lax.broadcasted_iota(jnp.int32, (tk, tq), 1) // CHUNK
    for d in range(n_diag):
        g = n_full + d
        if d + 1 < n_diag:
            issue(main_k(g + 1), (d + 1) % 2)
        consume(d % 2, main_vt(g), (key_idx + d * tk) // CHUNK <= q_chunk)

    lam = (jnp.exp(jnp.sum(lq1_ref[...] * lk1_ref[...], axis=1, keepdims=True))
           - jnp.exp(jnp.sum(lq2_ref[...] * lk2_ref[...], axis=1, keepdims=True)) + LAM_INIT)
    o = (acc_ref[0, :LANES] / acc_ref[0, LANES:LANES + 1]
         - lam * (acc_ref[1, :LANES] / acc_ref[1, LANES:LANES + 1]))
    o = o * lax.rsqrt(jnp.mean(o * o, axis=0, keepdims=True) + EPS) * subg_ref[...] * (1.0 - LAM_INIT)
    o_ref[0] = o.T.astype(BF16)


def _diff_attn(lams, subg_col, q, kp, vtp, km, vtm, *, tq, tk, tkp, pref_valid, pref_shared):
    nb, tq_total, _ = q.shape
    p_len = kp.shape[1]
    t_main = km.shape[1]
    n_pref = p_len // tkp
    n_diag, one_tile = min(tq, t_main) // tk, tq_total == tq
    assert n_pref == 1 or n_pref % 2 == 0, "prefix blocks are consumed in pairs"
    assert one_tile or n_diag % 2 == 0, "main blocks are consumed in pairs"
    kernel = functools.partial(_diff_kernel, tq=tq, tk=tk, tkp=tkp, n_pref=n_pref, pref_valid=pref_valid,
                               n_diag=n_diag, one_tile=one_tile)
    pk_idx = (lambda b, h, i: (0, 0, h)) if pref_shared else (lambda b, h, i: (b, 0, h))
    pv_idx = (lambda b, h, i: (0, h, 0)) if pref_shared else (lambda b, h, i: (b, h, 0))
    small = _const_spec((1, HEAD_DIM))
    return pl.pallas_call(
        kernel,
        grid=(nb, N_HEADS, tq_total // tq),
        in_specs=[small, small, small, small, _const_spec((LANES, 1)),
                  pl.BlockSpec((1, tq, LANES), lambda b, h, i: (b, i, h)),
                  pl.BlockSpec((1, p_len, LANES), pk_idx),
                  pl.BlockSpec((1, LANES, p_len), pv_idx),
                  pl.BlockSpec((1, t_main, LANES), lambda b, h, i: (b, 0, h)),
                  pl.BlockSpec((1, LANES, t_main), lambda b, h, i: (b, h, 0))],
        out_specs=pl.BlockSpec((1, tq, LANES), lambda b, h, i: (b, i, h)),
        out_shape=jax.ShapeDtypeStruct((nb, tq_total, DIFF_WIDTH), BF16),
        scratch_shapes=[pltpu.VMEM((2, 1, tq), F32), pltpu.VMEM((2, LANES + ONES_ROWS, tq), F32),
                        pltpu.VMEM((2, 2, max(tk, tkp), tq), F32)],
        compiler_params=pltpu.CompilerParams(dimension_semantics=("arbitrary",) * 3,
                                             vmem_limit_bytes=VMEM_LIMIT),
        name="diff_attn",
    )(*lams, subg_col, q, kp, vtp, km, vtm)


def _suffix_matrix(n):
    r = lax.broadcasted_iota(jnp.int32, (n + ONES_ROWS, n), 0)
    c = lax.broadcasted_iota(jnp.int32, (n + ONES_ROWS, n), 1)
    return jnp.where((c > r) | (r == n), 1.0, 0.0).astype(BF16)


def _sb_kernel(q_ref, kp_ref, vp_ref, km_ref, vm_ref, o_ref, c_ref, acc_ref, z_ref,
               *, tq, tk, tkp, n_pref, pref_valid, n_diag, one_tile):
    i = 0 if one_tile else pl.program_id(2)
    n_full = i * n_diag
    qs = _split_halves(q_ref[0])
    c_ref[...] = jnp.zeros(c_ref.shape, F32)
    acc_ref[...] = jnp.zeros(acc_ref.shape, F32)

    def issue(kblk, slot):
        n = kblk.shape[0]
        for a in range(2):
            z_ref[slot, a, :n, :] = lax.dot_general(kblk, qs[a], _NT, preferred_element_type=F32)

    def consume(slot, vtblk, mask, sfx_mat):
        n = vtblk.shape[1]
        for a in range(2):
            rows = slice(a * HEAD_DIM, (a + 1) * HEAD_DIM)
            u = z_ref[slot, a, :n, :]
            nu = -u
            log1m = jnp.minimum(nu, 0.0) - jnp.log(1.0 + jnp.exp2(jnp.minimum(u, nu))) * LOG2E
            if mask is not None:
                log1m = jnp.where(mask, log1m, 0.0)
            sfx = jnp.dot(sfx_mat, log1m.astype(BF16), preferred_element_type=F32)
            t = u + log1m + sfx[:n]
            if mask is not None:
                t = jnp.where(mask, t, NEG)
            pv = jnp.dot(vtblk[rows, :], jnp.exp2(t).astype(BF16), preferred_element_type=F32)
            acc_ref[rows, :] += pv * jnp.exp2(c_ref[a])
            c_ref[a] += sfx[n:n + 1]

    def pref_k(t):
        return kp_ref[0, pl.ds(_aligned(t * tkp, tkp), tkp), :]

    def pref_vt(t):
        return vp_ref[0, :, pl.ds(_aligned(t * tkp, tkp), tkp)]

    def main_k(g):
        return km_ref[0, pl.ds(_aligned(g * tk, tk), tk), :]

    def main_vt(g):
        return vm_ref[0, :, pl.ds(_aligned(g * tk, tk), tk)]

    sfx_main = _suffix_matrix(tk)
    sfx_pref = sfx_main if tkp == tk else _suffix_matrix(tkp)
    key = lax.broadcasted_iota(jnp.int32, (tk, tq), 0)
    qry = lax.broadcasted_iota(jnp.int32, (tk, tq), 1)

    if n_pref == 1:
        issue(pref_k(0), 2)
    issue(main_k(n_full + n_diag - 1), 0)
    for d in reversed(range(n_diag)):
        s = (n_diag - 1 - d) % 2
        if d > 0:
            issue(main_k(n_full + d - 1), 1 - s)
        elif not one_tile:
            issue(main_k(jnp.maximum(n_full - 1, 0)), 1 - s)
        elif n_pref > 1:
            issue(pref_k(n_pref - 1), 1 - s)
        consume(s, main_vt(n_full + d), key + d * tk < qry, sfx_main)

    if not one_tile:
        assert n_diag % 2 == 0
        def full_body(t, carry):
            g = n_full - 1 - 2 * t
            issue(main_k(g - 1), 1)
            consume(0, main_vt(g), None, sfx_main)
            issue(main_k(jnp.maximum(g - 2, 0)), 0)
            consume(1, main_vt(g - 1), None, sfx_main)
            return carry
        lax.fori_loop(0, i * (n_diag // 2), full_body, 0)

    if n_pref == 1:
        pmask = None
        if pref_valid < tkp:
            pmask = lax.broadcasted_iota(jnp.int32, (tkp, tq), 0) < pref_valid
        consume(2, pref_vt(0), pmask, sfx_pref)
    else:
        assert one_tile and n_diag == 1 and n_pref % 2 == 0
        def pref_body(t, carry):
            g = n_pref - 1 - 2 * t
            issue(pref_k(g - 1), 0)
            consume(1, pref_vt(g), None, sfx_pref)
            issue(pref_k(jnp.maximum(g - 2, 0)), 1)
            consume(0, pref_vt(g - 1), None, sfx_pref)
            return carry
        lax.fori_loop(0, n_pref // 2, pref_body, 0)

    o_ref[0] = acc_ref[...].T.astype(BF16)


def _sb_attn(q, kp, vtp, km, vtm, *, tq, tk, tkp, pref_valid, pref_shared):
    nb, tq_total, _ = q.shape
    p_len = kp.shape[1]
    t_main = km.shape[1]
    n_pref = p_len // tkp
    kernel = functools.partial(_sb_kernel, tq=tq, tk=tk, tkp=tkp, n_pref=n_pref, pref_valid=pref_valid,
                               n_diag=min(tq, t_main) // tk, one_tile=tq_total == tq)
    pk_idx = (lambda b, h, i: (0, 0, h)) if pref_shared else (lambda b, h, i: (b, 0, h))
    pv_idx = (lambda b, h, i: (0, h, 0)) if pref_shared else (lambda b, h, i: (b, h, 0))
    return pl.pallas_call(
        kernel,
        grid=(nb, SB_WIDTH // LANES, tq_total // tq),
        in_specs=[pl.BlockSpec((1, tq, LANES), lambda b, h, i: (b, i, h)),
                  pl.BlockSpec((1, p_len, LANES), pk_idx),
                  pl.BlockSpec((1, LANES, p_len), pv_idx),
                  pl.BlockSpec((1, t_main, LANES), lambda b, h, i: (b, 0, h)),
                  pl.BlockSpec((1, LANES, t_main), lambda b, h, i: (b, h, 0))],
        out_specs=pl.BlockSpec((1, tq, LANES), lambda b, h, i: (b, i, h)),
        out_shape=jax.ShapeDtypeStruct((nb, tq_total, SB_WIDTH), BF16),
        scratch_shapes=[pltpu.VMEM((2, 1, tq), F32), pltpu.VMEM((LANES, tq), F32),
                        pltpu.VMEM((3 if n_pref == 1 else 2, 2, max(tk, tkp), tq), F32)],
        compiler_params=pltpu.CompilerParams(dimension_semantics=("arbitrary",) * 3,
                                             vmem_limit_bytes=VMEM_LIMIT),
        name="sb_attn",
    )(q, kp, vtp, km, vtm)


def _out_kernel(x_ref, od_ref, os_ref, gmix_ref, wg_ref, wdo_ref, wso_ref, wo_ref, gffn_ref,
                w1_ref, w2_ref, y_ref):
    x = x_ref[...]
    h = (_rms(x) * gmix_ref[...]).astype(BF16)
    gate = jax.nn.sigmoid(jnp.dot(h, wg_ref[...], preferred_element_type=F32))
    a = jnp.dot(od_ref[...], wdo_ref[...], preferred_element_type=F32)
    b = jnp.dot(os_ref[...], wso_ref[...], preferred_element_type=F32)
    merged = (gate[:, :D_MODEL] * a + gate[:, D_MODEL:] * b).astype(BF16)
    x1 = x + jnp.dot(merged, wo_ref[...], preferred_element_type=F32)
    h2 = (_rms(x1) * gffn_ref[...]).astype(BF16)
    y = x1
    for c in range(D_FF // D_MODEL):
        sl = slice(c * D_MODEL, (c + 1) * D_MODEL)
        f = jnp.maximum(jnp.dot(h2, w1_ref[:, sl], preferred_element_type=F32), 0.0)
        y = y + jnp.dot((f * f).astype(BF16), w2_ref[sl, :], preferred_element_type=F32)
    y_ref[...] = y


def _out(x, od, osb, gmix, wg, wdo, wso, wo, gffn, w1, w2, *, tm):
    rows = x.shape[0]
    row = lambda w: pl.BlockSpec((tm, w), lambda i: (i, 0))
    return pl.pallas_call(
        _out_kernel,
        grid=(rows // tm,),
        in_specs=[row(D_MODEL), row(DIFF_WIDTH), row(SB_WIDTH), _const_spec((1, D_MODEL)),
                  _const_spec((D_MODEL, 2 * D_MODEL)), _const_spec((DIFF_WIDTH, D_MODEL)),
                  _const_spec((SB_WIDTH, D_MODEL)), _const_spec((D_MODEL, D_MODEL)),
                  _const_spec((1, D_MODEL)), _const_spec((D_MODEL, D_FF)), _const_spec((D_FF, D_MODEL))],
        out_specs=row(D_MODEL),
        out_shape=jax.ShapeDtypeStruct((rows, D_MODEL), F32),
        compiler_params=pltpu.CompilerParams(dimension_semantics=("arbitrary",),
                                             vmem_limit_bytes=VMEM_LIMIT),
        name="out",
    )(x, od, osb, gmix, wg, wdo, wso, wo, gffn, w1, w2)


def _rope_tables(pos):
    half = ROT_DIM // 2
    inv = ROPE_THETA ** (-jnp.arange(0, ROT_DIM, 2, dtype=F32) / ROT_DIM)
    ang = pos.astype(F32)[:, None] * inv[None, :]
    cos, sin = jnp.cos(ang), jnp.sin(ang)
    n = pos.shape[0]
    pad = jnp.zeros((n, HEAD_DIM - ROT_DIM), F32)
    z8 = jnp.zeros((n, half), F32)
    cos64 = jnp.concatenate([cos, cos, pad + 1.0], axis=1)
    sa64 = jnp.concatenate([-sin, z8, pad], axis=1)
    sb64 = jnp.concatenate([z8, sin, pad], axis=1)
    rep = lambda t: jnp.concatenate([t, t], axis=1)
    return rep(cos64), rep(sa64), rep(sb64)


def kernel(x_prompt, x_sample, cache_diff_k, cache_diff_v, cache_sb_k, cache_sb_v, meta_tokens,
           g_mix, w_in, q_norm_g, k_norm_g, lam_q1, lam_k1, lam_q2, lam_k2, sub_g,
           w_diff_out, w_sb_out, w_out, g_ffn, w_ff1, w_ff2):
    nb, seq, _ = x_prompt.shape
    db, dseq, _ = x_sample.shape
    past = cache_diff_k.shape[2]
    lyr = 0

    w_in_b = w_in[lyr].astype(BF16)
    w_qkv, w_gate = w_in_b[:, :QKV_COLS], w_in_b[:, QKV_COLS:]
    wdo, wso, wo = (w_diff_out[lyr].astype(BF16), w_sb_out[lyr].astype(BF16), w_out[lyr].astype(BF16))
    w1, w2 = w_ff1[lyr].astype(BF16), w_ff2[lyr].astype(BF16)
    gmix = g_mix[lyr].reshape(1, D_MODEL)
    gffn = g_ffn[lyr].reshape(1, D_MODEL)
    qg = jnp.tile(q_norm_g[lyr], COL_BLOCK // HEAD_DIM).reshape(1, COL_BLOCK)
    kg = jnp.tile(k_norm_g[lyr], COL_BLOCK // HEAD_DIM).reshape(1, COL_BLOCK)
    subg_col = sub_g[lyr].reshape(LANES, 1)
    lams = [t[lyr].reshape(1, HEAD_DIM) for t in (lam_q1, lam_k1, lam_q2, lam_k2)]
    grp = jnp.arange(COL_BLOCK, dtype=jnp.int32) // HEAD_DIM
    gmat = jnp.where(grp[:, None] == grp[None, :], 1.0 / HEAD_DIM, 0.0).astype(BF16)

    main_pos = N_META + jnp.arange(seq, dtype=jnp.int32)
    small_pos = jnp.concatenate([jnp.arange(N_META, dtype=jnp.int32),
                                 jnp.tile(past + jnp.arange(dseq, dtype=jnp.int32), db)])
    x_main = x_prompt.reshape(nb * seq, D_MODEL)
    x_small = jnp.concatenate([meta_tokens.astype(F32), x_sample.reshape(db * dseq, D_MODEL)], axis=0)
    pm = _proj(x_main, gmix, w_qkv, qg, kg, gmat, *_rope_tables(main_pos), tm=512, prompt_layout=True,
               lead=N_META)
    ps = _proj(x_small, gmix, w_qkv, qg, kg, gmat, *_rope_tables(small_pos), tm=x_small.shape[0],
               prompt_layout=False)
    qd_m, kd_hm, kdb_m, vd_hm, vdt_m, qs_m, kst_m, ksb_m, vstf_m, vst_m = pm
    qd_s, kd_s, kdb_s, vd_s, vdb_s, qs_s, ks_s, ksb_s, vs_s, vsb_s = ps

    def bt(a, n, t):
        return a.reshape(n, t, a.shape[-1])

    def meta_keys(a):
        return jnp.pad(a[:N_META], ((0, LANES - N_META), (0, 0)))[None]

    def meta_vals_t(a):
        return jnp.swapaxes(meta_keys(a), 1, 2)

    od_p = _diff_attn(lams, subg_col, bt(qd_m, nb, seq), meta_keys(kdb_s), meta_vals_t(vdb_s),
                      bt(kdb_m, nb, seq), vdt_m,
                      tq=512, tk=256, tkp=LANES, pref_valid=N_META, pref_shared=True)
    os_p = _sb_attn(bt(qs_m, nb, seq), meta_keys(ksb_s), meta_vals_t(vsb_s),
                    bt(ksb_m, nb, seq), vst_m,
                    tq=512, tk=256, tkp=LANES, pref_valid=N_META, pref_shared=True)

    smp = lambda a: bt(a[N_META:], db, dseq)
    smp_q = lambda a: jnp.pad(smp(a), ((0, 0), (0, LANES - dseq), (0, 0)))
    smp_vt = lambda a: jnp.swapaxes(smp(a), 1, 2)
    cdk = cache_diff_k[lyr].reshape(db, past, DIFF_WIDTH).astype(BF16)
    cdv = jnp.swapaxes(cache_diff_v[lyr].reshape(db, past, DIFF_WIDTH), 1, 2).astype(BF16)
    csk = cache_sb_k[lyr].reshape(db, past, SB_WIDTH).astype(BF16)
    csv = jnp.swapaxes(cache_sb_v[lyr].reshape(db, past, SB_WIDTH), 1, 2).astype(BF16)
    od_s = _diff_attn(lams, subg_col, smp_q(qd_s), cdk, cdv, smp(kdb_s), smp_vt(vdb_s),
                      tq=LANES, tk=dseq, tkp=512, pref_valid=512, pref_shared=False)[:, :dseq]
    os_s = _sb_attn(smp_q(qs_s), csk, csv, smp(ksb_s), smp_vt(vsb_s),
                    tq=LANES, tk=dseq, tkp=256, pref_valid=256, pref_shared=False)[:, :dseq]

    y_p = _out(x_main, od_p.reshape(nb * seq, DIFF_WIDTH), os_p.reshape(nb * seq, SB_WIDTH),
               gmix, w_gate, wdo, wso, wo, gffn, w1, w2, tm=512)
    y_s = _out(x_sample.reshape(db * dseq, D_MODEL), od_s.reshape(db * dseq, DIFF_WIDTH),
               os_s.reshape(db * dseq, SB_WIDTH), gmix, w_gate, wdo, wso, wo, gffn, w1, w2, tm=db * dseq)

    def diff_prompt_cache(head_major, small):
        meta = small[:N_META].reshape(N_META * N_HEADS, 2 * HEAD_DIM)
        full = _fill_lead_tokens(head_major, meta, nb=nb)
        return full.reshape(1, nb, seq + N_META, N_HEADS, 2 * HEAD_DIM)

    def sb_prompt_cache(main_t, small):
        meta_t = jnp.broadcast_to(small[:N_META].T[None], (nb, SB_WIDTH, N_META))
        full = jnp.concatenate([meta_t, main_t], axis=2).reshape(nb, N_HEADS, HEAD_DIM, seq + N_META)
        return jnp.transpose(full, (0, 3, 1, 2))[None]

    def sample_cache(small, dim):
        return small[N_META:].reshape(1, db, dseq, N_HEADS, dim)

    return (y_p.reshape(nb, seq, D_MODEL), y_s.reshape(db, dseq, D_MODEL),
            diff_prompt_cache(kd_hm, kd_s), diff_prompt_cache(vd_hm, vd_s),
            sb_prompt_cache(kst_m, ks_s), sb_prompt_cache(vstf_m, vs_s),
            sample_cache(kd_s, 2 * HEAD_DIM), sample_cache(vd_s, 2 * HEAD_DIM),
            sample_cache(ks_s, HEAD_DIM), sample_cache(vs_s, HEAD_DIM))
```

```python
import functools
import math

import jax
import jax.numpy as jnp
from jax import lax
from jax.experimental import pallas as pl
from jax.experimental.pallas import tpu as pltpu

F32 = jnp.float32
BF16 = jnp.bfloat16

D_MODEL = 1024
N_META = 16
CHUNK = 64
N_HEADS = 8
HEAD_DIM = 64
DIFF_WIDTH = N_HEADS * 2 * HEAD_DIM
SB_WIDTH = N_HEADS * HEAD_DIM
QKV_COLS = 3 * DIFF_WIDTH + 3 * SB_WIDTH
D_FF = 4 * D_MODEL
ROT_DIM = HEAD_DIM // 4
ROPE_THETA = 500000.0
EPS = 1e-6
NEG = -1e30
LOG2E = math.log2(math.e)
Q_SCALE = HEAD_DIM ** -0.5 * LOG2E
ONES_ROWS = 16
LAM_INIT = 0.8 - 0.6 * math.exp(-0.3 * 0)

LANES = 128
COL_BLOCK = 512
VMEM_LIMIT = 56 * 1024 * 1024

_NT = (((1,), (1,)), ((), ()))


def _rms(x):
    return x * lax.rsqrt(jnp.mean(x * x, axis=-1, keepdims=True) + EPS)


def _const_spec(shape):
    return pl.BlockSpec(shape, lambda *_: (0,) * len(shape), pipeline_mode=pl.Buffered(1))


def _proj_kernel(x_ref, gmix_ref, w_ref, qg_ref, kg_ref, gmat_ref, cos_ref, sa_ref, sb_ref,
                 qd_ref, kd_ref, kdb_ref, vd_ref, vdb_ref, qs_ref, ks_ref, ksb_ref, vs_ref, vsb_ref,
                 *, prompt_layout):
    tm = x_ref.shape[0]
    heads_per_block = COL_BLOCK // LANES

    def store_diff(ref, j, y):
        if not prompt_layout:
            ref[:, j * COL_BLOCK:(j + 1) * COL_BLOCK] = y
            return
        for hh in range(heads_per_block):
            ref[pl.ds(j * heads_per_block + hh, tm, stride=N_HEADS), :] = y[:, hh * LANES:(hh + 1) * LANES]

    h = (_rms(x_ref[...]) * gmix_ref[...]).astype(BF16)
    cos = cos_ref[...]
    sa = sa_ref[...]
    sb = sb_ref[...]

    def col(j):
        return jnp.dot(h, w_ref[:, j * COL_BLOCK:(j + 1) * COL_BLOCK], preferred_element_type=F32)

    def normed_rot(y, g):
        msq = jnp.dot((y * y).astype(BF16), gmat_ref[...], preferred_element_type=F32)
        yn = y * lax.rsqrt(msq + EPS) * g
        parts = []
        for c in range(COL_BLOCK // LANES):
            t = yn[:, c * LANES:(c + 1) * LANES]
            parts.append(t * cos + pltpu.roll(t, LANES - ROT_DIM // 2, 1) * sa
                         + pltpu.roll(t, ROT_DIM // 2, 1) * sb)
        return jnp.concatenate(parts, axis=1)

    for j in range(2):
        sl = slice(j * COL_BLOCK, (j + 1) * COL_BLOCK)
        q = normed_rot(col(j), qg_ref[...])
        qd_ref[:, sl] = (q * Q_SCALE).astype(BF16)
        k = normed_rot(col(2 + j), kg_ref[...])
        store_diff(kd_ref, j, k)
        kdb_ref[:, sl] = k.astype(BF16)
        v = col(4 + j)
        store_diff(vd_ref, j, v)
        if prompt_layout:
            vdb_ref[sl, :] = v.T.astype(BF16)
        else:
            vdb_ref[:, sl] = v.astype(BF16)
    qs_ref[...] = (col(6) * Q_SCALE).astype(BF16)
    k = col(7)
    ksb_ref[...] = k.astype(BF16)
    v = col(8)
    if prompt_layout:
        ks_ref[...] = k.T
        vt = v.T
        vs_ref[...] = vt
        vsb_ref[...] = vt.astype(BF16)
    else:
        ks_ref[...] = k
        vs_ref[...] = v
        vsb_ref[...] = v.astype(BF16)


def _proj(x, gmix, w_qkv, qg, kg, gmat, cos, sa, sb, *, tm, prompt_layout, lead=0):
    rows = x.shape[0]
    seq = cos.shape[0]
    n_pos_tiles = seq // tm
    row = lambda w: pl.BlockSpec((tm, w), lambda i: (i, 0))
    tab = pl.BlockSpec((tm, LANES), lambda i: (i % n_pos_tiles, 0))
    wide = lambda dt: jax.ShapeDtypeStruct((rows, DIFF_WIDTH), dt)
    narrow = lambda dt: jax.ShapeDtypeStruct((rows, SB_WIDTH), dt)
    if prompt_layout:
        nb = rows // seq
        t_spec = lambda w: pl.BlockSpec((None, w, tm), lambda i: (i // n_pos_tiles, 0, i % n_pos_tiles))
        t_shape = lambda w, dt: jax.ShapeDtypeStruct((nb, w, seq), dt)
        hm_spec = pl.BlockSpec(
            (pl.Element(tm * N_HEADS), pl.Element(LANES)),
            lambda i: (((i // n_pos_tiles) * (seq + lead) + lead + (i % n_pos_tiles) * tm) * N_HEADS, 0))
        hm_shape = jax.ShapeDtypeStruct((nb * (seq + lead) * N_HEADS, LANES), F32)
        out_specs = [row(DIFF_WIDTH), hm_spec, row(DIFF_WIDTH), hm_spec, t_spec(DIFF_WIDTH),
                     row(SB_WIDTH), t_spec(SB_WIDTH), row(SB_WIDTH), t_spec(SB_WIDTH), t_spec(SB_WIDTH)]
        out_shape = [wide(BF16), hm_shape, wide(BF16), hm_shape, t_shape(DIFF_WIDTH, BF16),
                     narrow(BF16), t_shape(SB_WIDTH, F32), narrow(BF16), t_shape(SB_WIDTH, F32),
                     t_shape(SB_WIDTH, BF16)]
    else:
        out_specs = [row(DIFF_WIDTH)] * 5 + [row(SB_WIDTH)] * 5
        out_shape = [wide(BF16), wide(F32), wide(BF16), wide(F32), wide(BF16),
                     narrow(BF16), narrow(F32), narrow(BF16), narrow(F32), narrow(BF16)]
    return pl.pallas_call(
        functools.partial(_proj_kernel, prompt_layout=prompt_layout),
        grid=(rows // tm,),
        in_specs=[row(D_MODEL), _const_spec((1, D_MODEL)), _const_spec((D_MODEL, QKV_COLS)),
                  _const_spec((1, COL_BLOCK)), _const_spec((1, COL_BLOCK)),
                  _const_spec((COL_BLOCK, COL_BLOCK)), tab, tab, tab],
        out_specs=out_specs,
        out_shape=out_shape,
        compiler_params=pltpu.CompilerParams(dimension_semantics=("arbitrary",),
                                             vmem_limit_bytes=VMEM_LIMIT),
        name="proj",
    )(x, gmix, w_qkv, qg, kg, gmat, cos, sa, sb)


def _fill_lead_kernel(lead_ref, big_ref, out_ref):
    del big_ref
    out_ref[...] = lead_ref[...]


def _fill_lead_tokens(big, lead_rows, *, nb):
    n = lead_rows.shape[0]
    per_batch = big.shape[0] // nb
    return pl.pallas_call(
        _fill_lead_kernel,
        grid=(nb,),
        in_specs=[_const_spec((n, LANES)), pl.BlockSpec(memory_space=pl.ANY)],
        out_specs=pl.BlockSpec((pl.Element(n), pl.Element(LANES)), lambda b: (b * per_batch, 0)),
        out_shape=jax.ShapeDtypeStruct(big.shape, big.dtype),
        input_output_aliases={1: 0},
        compiler_params=pltpu.CompilerParams(dimension_semantics=("arbitrary",)),
        name="fill_lead_tokens",
    )(lead_rows, big)


def _split_halves(q):
    lane = lax.broadcasted_iota(jnp.int32, q.shape, 1)
    zero = jnp.zeros_like(q)
    return jnp.where(lane < HEAD_DIM, q, zero), jnp.where(lane >= HEAD_DIM, q, zero)


def _diff_kernel(lq1_ref, lk1_ref, lq2_ref, lk2_ref, subg_ref, q_ref, kp_ref, vp_ref, km_ref, vm_ref,
                 o_ref, m_ref, acc_ref, s_ref, *, tq, tk, tkp, pref_valid, n_diag):
    i = pl.program_id(2)
    n_full = i * n_diag
    qs = _split_halves(q_ref[0])
    m_ref[...] = jnp.full(m_ref.shape, NEG, F32)
    acc_ref[...] = jnp.zeros(acc_ref.shape, F32)

    def issue(kblk, slot):
        n = kblk.shape[0]
        for a in range(2):
            s_ref[slot, a, :n, :] = lax.dot_general(kblk, qs[a], _NT, preferred_element_type=F32)

    def consume(slot, vtblk, mask):
        n = vtblk.shape[1]
        vt_ones = jnp.concatenate([vtblk, jnp.ones((ONES_ROWS, n), BF16)], axis=0)
        for a in range(2):
            s = s_ref[slot, a, :n, :]
            if mask is not None:
                s = jnp.where(mask, s, NEG)
            m_prev = m_ref[a]
            m_new = jnp.maximum(m_prev, jnp.max(s, axis=0, keepdims=True))
            alpha = jnp.exp2(m_prev - m_new)
            p = jnp.exp2(s - m_new)
            acc_ref[a] = alpha * acc_ref[a] + jnp.dot(vt_ones, p.astype(BF16), preferred_element_type=F32)
            m_ref[a] = m_new

    def main_k(g):
        return km_ref[0, pl.ds(pl.multiple_of(g * tk, tk), tk), :]

    def main_vt(g):
        return vm_ref[0, :, pl.ds(pl.multiple_of(g * tk, tk), tk)]

    issue(kp_ref[0], 1)
    issue(main_k(0), 0)
    consume(1, vp_ref[0], lax.broadcasted_iota(jnp.int32, (tkp, tq), 0) < pref_valid)

    def full_body(t, carry):
        issue(main_k(2 * t + 1), 1)
        consume(0, main_vt(2 * t), None)
        issue(main_k(2 * t + 2), 0)
        consume(1, main_vt(2 * t + 1), None)
        return carry
    lax.fori_loop(0, i * (n_diag // 2), full_body, 0)

    key_idx = lax.broadcasted_iota(jnp.int32, (tk, tq), 0)
    q_chunk = lax.broadcasted_iota(jnp.int32, (tk, tq), 1) // CHUNK
    for d in range(n_diag):
        g = n_full + d
        if d + 1 < n_diag:
            issue(main_k(g + 1), (d + 1) % 2)
        consume(d % 2, main_vt(g), (key_idx + d * tk) // CHUNK <= q_chunk)

    lam = (jnp.exp(jnp.sum(lq1_ref[...] * lk1_ref[...], axis=1, keepdims=True))
           - jnp.exp(jnp.sum(lq2_ref[...] * lk2_ref[...], axis=1, keepdims=True)) + LAM_INIT)
    o = (acc_ref[0, :LANES] / acc_ref[0, LANES:LANES + 1]
         - lam * (acc_ref[1, :LANES] / acc_ref[1, LANES:LANES + 1]))
    o = o * lax.rsqrt(jnp.mean(o * o, axis=0, keepdims=True) + EPS) * subg_ref[...] * (1.0 - LAM_INIT)
    o_ref[0] = o.T.astype(BF16)


def _diff_attn(lams, subg_col, q, kp, vtp, km, vtm, *, tq, tk, pref_valid):
    nb, tq_total, _ = q.shape
    tkp = kp.shape[1]
    t_main = km.shape[1]
    n_diag = tq // tk
    assert n_diag % 2 == 0, "main blocks are consumed in pairs"
    kernel = functools.partial(_diff_kernel, tq=tq, tk=tk, tkp=tkp, pref_valid=pref_valid, n_diag=n_diag)
    small = _const_spec((1, HEAD_DIM))
    return pl.pallas_call(
        kernel,
        grid=(nb, N_HEADS, tq_total // tq),
        in_specs=[small, small, small, small, _const_spec((LANES, 1)),
                  pl.BlockSpec((1, tq, LANES), lambda b, h, i: (b, i, h)),
                  pl.BlockSpec((1, tkp, LANES), lambda b, h, i: (0, 0, h)),
                  pl.BlockSpec((1, LANES, tkp), lambda b, h, i: (0, h, 0)),
                  pl.BlockSpec((1, t_main, LANES), lambda b, h, i: (b, 0, h)),
                  pl.BlockSpec((1, LANES, t_main), lambda b, h, i: (b, h, 0))],
        out_specs=pl.BlockSpec((1, tq, LANES), lambda b, h, i: (b, i, h)),
        out_shape=jax.ShapeDtypeStruct((nb, tq_total, DIFF_WIDTH), BF16),
        scratch_shapes=[pltpu.VMEM((2, 1, tq), F32), pltpu.VMEM((2, LANES + ONES_ROWS, tq), F32),
                        pltpu.VMEM((2, 2, max(tk, tkp), tq), F32)],
        compiler_params=pltpu.CompilerParams(dimension_semantics=("arbitrary",) * 3,
                                             vmem_limit_bytes=VMEM_LIMIT),
        name="diff_attn",
    )(*lams, subg_col, q, kp, vtp, km, vtm)


def _suffix_matrix(n):
    r = lax.broadcasted_iota(jnp.int32, (n + ONES_ROWS, n), 0)
    c = lax.broadcasted_iota(jnp.int32, (n + ONES_ROWS, n), 1)
    return jnp.where((c > r) | (r == n), 1.0, 0.0).astype(BF16)


def _sb_kernel(q_ref, kp_ref, vp_ref, km_ref, vm_ref, o_ref, c_ref, acc_ref, z_ref,
               *, tq, tk, tkp, pref_valid, n_diag):
    i = pl.program_id(2)
    n_full = i * n_diag
    qs = _split_halves(q_ref[0])
    c_ref[...] = jnp.zeros(c_ref.shape, F32)
    acc_ref[...] = jnp.zeros(acc_ref.shape, F32)

    def issue(kblk, slot):
        n = kblk.shape[0]
        for a in range(2):
            z_ref[slot, a, :n, :] = lax.dot_general(kblk, qs[a], _NT, preferred_element_type=F32)

    def consume(slot, vtblk, mask, sfx_mat):
        n = vtblk.shape[1]
        for a in range(2):
            rows = slice(a * HEAD_DIM, (a + 1) * HEAD_DIM)
            u = z_ref[slot, a, :n, :]
            nu = -u
            log1m = jnp.minimum(nu, 0.0) - jnp.log(1.0 + jnp.exp2(jnp.minimum(u, nu))) * LOG2E
            if mask is not None:
                log1m = jnp.where(mask, log1m, 0.0)
            sfx = jnp.dot(sfx_mat, log1m.astype(BF16), preferred_element_type=F32)
            t = u + log1m + sfx[:n]
            if mask is not None:
                t = jnp.where(mask, t, NEG)
            pv = jnp.dot(vtblk[rows, :], jnp.exp2(t).astype(BF16), preferred_element_type=F32)
            acc_ref[rows, :] += pv * jnp.exp2(c_ref[a])
            c_ref[a] += sfx[n:n + 1]

    def main_k(g):
        return km_ref[0, pl.ds(pl.multiple_of(g * tk, tk), tk), :]

    def main_vt(g):
        return vm_ref[0, :, pl.ds(pl.multiple_of(g * tk, tk), tk)]

    sfx_main = _suffix_matrix(tk)
    sfx_pref = sfx_main if tkp == tk else _suffix_matrix(tkp)
    key = lax.broadcasted_iota(jnp.int32, (tk, tq), 0)
    qry = lax.broadcasted_iota(jnp.int32, (tk, tq), 1)

    issue(kp_ref[0], 2)
    issue(main_k(n_full + n_diag - 1), 0)
    for d in reversed(range(n_diag)):
        s = (n_diag - 1 - d) % 2
        issue(main_k(n_full + d - 1 if d > 0 else jnp.maximum(n_full - 1, 0)), 1 - s)
        consume(s, main_vt(n_full + d), key + d * tk < qry, sfx_main)

    def full_body(t, carry):
        g = n_full - 1 - 2 * t
        issue(main_k(g - 1), 1)
        consume(0, main_vt(g), None, sfx_main)
        issue(main_k(jnp.maximum(g - 2, 0)), 0)
        consume(1, main_vt(g - 1), None, sfx_main)
        return carry
    lax.fori_loop(0, i * (n_diag // 2), full_body, 0)

    consume(2, vp_ref[0], lax.broadcasted_iota(jnp.int32, (tkp, tq), 0) < pref_valid, sfx_pref)
    o_ref[0] = acc_ref[...].T.astype(BF16)


def _sb_attn(q, kp, vtp, km, vtm, *, tq, tk, pref_valid):
    nb, tq_total, _ = q.shape
    tkp = kp.shape[1]
    t_main = km.shape[1]
    n_diag = tq // tk
    assert n_diag % 2 == 0, "main blocks are consumed in pairs"
    kernel = functools.partial(_sb_kernel, tq=tq, tk=tk, tkp=tkp, pref_valid=pref_valid, n_diag=n_diag)
    return pl.pallas_call(
        kernel,
        grid=(nb, SB_WIDTH // LANES, tq_total // tq),
        in_specs=[pl.BlockSpec((1, tq, LANES), lambda b, h, i: (b, i, h)),
                  pl.BlockSpec((1, tkp, LANES), lambda b, h, i: (0, 0, h)),
                  pl.BlockSpec((1, LANES, tkp), lambda b, h, i: (0, h, 0)),
                  pl.BlockSpec((1, t_main, LANES), lambda b, h, i: (b, 0, h)),
                  pl.BlockSpec((1, LANES, t_main), lambda b, h, i: (b, h, 0))],
        out_specs=pl.BlockSpec((1, tq, LANES), lambda b, h, i: (b, i, h)),
        out_shape=jax.ShapeDtypeStruct((nb, tq_total, SB_WIDTH), BF16),
        scratch_shapes=[pltpu.VMEM((2, 1, tq), F32), pltpu.VMEM((LANES, tq), F32),
                        pltpu.VMEM((3, 2, max(tk, tkp), tq), F32)],
        compiler_params=pltpu.CompilerParams(dimension_semantics=("arbitrary",) * 3,
                                             vmem_limit_bytes=VMEM_LIMIT),
        name="sb_attn",
    )(q, kp, vtp, km, vtm)


def _diff_decode_kernel(lq1_ref, lk1_ref, lq2_ref, lk2_ref, subg_ref, q_ref, ck_ref, cv_ref, kn_ref, vn_ref,
                        o_ref, m_ref, l_ref, acc_ref, *, chunk):
    kc = pl.program_id(1)
    nq = q_ref.shape[0]

    @pl.when(kc == 0)
    def _():
        m_ref[...] = jnp.full(m_ref.shape, NEG, F32)
        l_ref[...] = jnp.zeros(l_ref.shape, F32)
        acc_ref[...] = jnp.zeros(acc_ref.shape, F32)

    def update(h, k, v):
        n = k.shape[0]
        qs = _split_halves(q_ref[:, h * LANES:(h + 1) * LANES])
        v_ones = jnp.concatenate([v, jnp.ones((n, LANES), BF16)], axis=1)
        ps, alphas = [], []
        for a in range(2):
            s = lax.dot_general(qs[a], k, _NT, preferred_element_type=F32)
            m_prev = m_ref[2 * h + a]
            m_new = jnp.maximum(m_prev, jnp.max(s, axis=1, keepdims=True))
            alphas.append(jnp.exp2(m_prev - m_new))
            ps.append(jnp.exp2(s - m_new).astype(BF16))
            m_ref[2 * h + a] = m_new
        pv = jnp.dot(jnp.concatenate(ps, axis=0), v_ones, preferred_element_type=F32)
        for a in range(2):
            part = pv[a * nq:(a + 1) * nq]
            acc_ref[2 * h + a] = alphas[a] * acc_ref[2 * h + a] + part[:, :LANES]
            l_ref[2 * h + a] = alphas[a] * l_ref[2 * h + a] + part[:, LANES:]

    for h in range(N_HEADS):
        update(h, ck_ref[pl.ds(h, chunk, stride=N_HEADS), :].astype(BF16),
               cv_ref[pl.ds(h, chunk, stride=N_HEADS), :].astype(BF16))

    @pl.when(kc == pl.num_programs(1) - 1)
    def _():
        lam = (jnp.exp(jnp.sum(lq1_ref[...] * lk1_ref[...], axis=1, keepdims=True))
               - jnp.exp(jnp.sum(lq2_ref[...] * lk2_ref[...], axis=1, keepdims=True)) + LAM_INIT)
        for h in range(N_HEADS):
            cols = slice(h * LANES, (h + 1) * LANES)
            update(h, kn_ref[:, cols], vn_ref[:, cols])
            o = acc_ref[2 * h] / l_ref[2 * h] - lam * (acc_ref[2 * h + 1] / l_ref[2 * h + 1])
            o_ref[:, cols] = (_rms(o) * subg_ref[...] * (1.0 - LAM_INIT)).astype(BF16)


def _diff_decode(lams, subg_row, q, ck, cv, kn, vn, *, chunk):
    nb, nq, _ = q.shape
    n_chunks = ck.shape[1] // (chunk * N_HEADS)
    small = _const_spec((1, HEAD_DIM))
    tok = pl.BlockSpec((None, nq, DIFF_WIDTH), lambda b, c: (b, 0, 0))
    cache = pl.BlockSpec((None, chunk * N_HEADS, LANES), lambda b, c: (b, c, 0))
    return pl.pallas_call(
        functools.partial(_diff_decode_kernel, chunk=chunk),
        grid=(nb, n_chunks),
        in_specs=[small, small, small, small, _const_spec((1, LANES)), tok, cache, cache, tok, tok],
        out_specs=tok,
        out_shape=jax.ShapeDtypeStruct((nb, nq, DIFF_WIDTH), BF16),
        scratch_shapes=[pltpu.VMEM((2 * N_HEADS, nq, 1), F32), pltpu.VMEM((2 * N_HEADS, nq, LANES), F32),
                        pltpu.VMEM((2 * N_HEADS, nq, LANES), F32)],
        compiler_params=pltpu.CompilerParams(dimension_semantics=("arbitrary",) * 2,
                                             vmem_limit_bytes=VMEM_LIMIT),
        name="diff_decode",
    )(*lams, subg_row, q, ck, cv, kn, vn)


SB_BLOCK = 256


def _sb_decode_kernel(q_ref, ckt_ref, cvt_ref, knt_ref, vnt_ref, o_ref):
    nq = q_ref.shape[0]
    past = ckt_ref.shape[1]
    n_blk = past // SB_BLOCK
    def suffix_cols(n):
        r = lax.broadcasted_iota(jnp.int32, (n, n + LANES), 0)
        c = lax.broadcasted_iota(jnp.int32, (n, n + LANES), 1)
        return jnp.where((r > c) | (c >= n), 1.0, 0.0).astype(BF16)
    sfx_blk = suffix_cols(SB_BLOCK)
    sfx_new = suffix_cols(LANES)
    row = lax.broadcasted_iota(jnp.int32, (nq, LANES), 0)
    lane = lax.broadcasted_iota(jnp.int32, (nq, LANES), 1)
    new_mask = lane < row

    def log1m_of(u):
        nu = -u
        return jnp.minimum(nu, 0.0) - jnp.log(1.0 + jnp.exp2(jnp.minimum(u, nu))) * LOG2E

    for j in range(ckt_ref.shape[0] // LANES):
        rows = slice(j * LANES, (j + 1) * LANES)
        kt = ckt_ref[rows, :].astype(BF16)
        vt = cvt_ref[rows, :].astype(BF16)
        qs = _split_halves(q_ref[:, rows])
        ws, wns = [], []
        for a in range(2):
            un = jnp.dot(qs[a], knt_ref[rows, :], preferred_element_type=F32)
            ln = jnp.where(new_mask, log1m_of(un), 0.0)
            sn = jnp.dot(ln.astype(BF16), sfx_new, preferred_element_type=F32)
            wns.append(jnp.exp2(jnp.where(new_mask, un + ln + sn[:, :LANES], NEG)).astype(BF16))
            carry = sn[:, LANES:]
            u = jnp.dot(qs[a], kt, preferred_element_type=F32)
            l1m = log1m_of(u)
            stacked = jnp.concatenate(
                [l1m[:, b * SB_BLOCK:(b + 1) * SB_BLOCK] for b in range(n_blk)], axis=0).astype(BF16)
            sfx = jnp.dot(stacked, sfx_blk, preferred_element_type=F32)
            ts = [None] * n_blk
            for b in reversed(range(n_blk)):
                cols = slice(b * SB_BLOCK, (b + 1) * SB_BLOCK)
                part = sfx[b * nq:(b + 1) * nq]
                c2 = jnp.concatenate([carry] * (SB_BLOCK // LANES), axis=1)
                ts[b] = u[:, cols] + l1m[:, cols] + part[:, :SB_BLOCK] + c2
                carry = carry + part[:, SB_BLOCK:]
            ws.append(jnp.exp2(jnp.concatenate(ts, axis=1)).astype(BF16))
        o = (lax.dot_general(jnp.concatenate(ws, axis=0), vt, _NT, preferred_element_type=F32)
             + lax.dot_general(jnp.concatenate(wns, axis=0), vnt_ref[rows, :], _NT,
                               preferred_element_type=F32))
        o_ref[:, rows] = jnp.where(lane < HEAD_DIM, o[:nq], o[nq:]).astype(BF16)


def _sb_decode(q, ckt, cvt, knt, vnt):
    nb, nq, _ = q.shape
    assert nq <= LANES
    past = ckt.shape[2]
    width = 2 * LANES
    tok = pl.BlockSpec((None, nq, width), lambda b, g: (b, 0, g))
    cache = pl.BlockSpec((None, width, past), lambda b, g: (b, g, 0))
    new = pl.BlockSpec((None, width, LANES), lambda b, g: (b, g, 0))
    return pl.pallas_call(
        _sb_decode_kernel,
        grid=(nb, SB_WIDTH // width),
        in_specs=[tok, cache, cache, new, new],
        out_specs=tok,
        out_shape=jax.ShapeDtypeStruct((nb, nq, SB_WIDTH), BF16),
        compiler_params=pltpu.CompilerParams(dimension_semantics=("arbitrary",) * 2,
                                             vmem_limit_bytes=VMEM_LIMIT),
        name="sb_decode",
    )(q, ckt, cvt, knt, vnt)


def _out_kernel(x_ref, od_ref, os_ref, gmix_ref, wg_ref, wdo_ref, wso_ref, wo_ref, gffn_ref,
                w1_ref, w2_ref, y_ref):
    x = x_ref[...]
    h = (_rms(x) * gmix_ref[...]).astype(BF16)
    gate = jax.nn.sigmoid(jnp.dot(h, wg_ref[...], preferred_element_type=F32))
    a = jnp.dot(od_ref[...], wdo_ref[...], preferred_element_type=F32)
    b = jnp.dot(os_ref[...], wso_ref[...], preferred_element_type=F32)
    merged = (gate[:, :D_MODEL] * a + gate[:, D_MODEL:] * b).astype(BF16)
    x1 = x + jnp.dot(merged, wo_ref[...], preferred_element_type=F32)
    h2 = (_rms(x1) * gffn_ref[...]).astype(BF16)
    y = x1
    for c in range(D_FF // D_MODEL):
        sl = slice(c * D_MODEL, (c + 1) * D_MODEL)
        f = jnp.maximum(jnp.dot(h2, w1_ref[:, sl], preferred_element_type=F32), 0.0)
        y = y + jnp.dot((f * f).astype(BF16), w2_ref[sl, :], preferred_element_type=F32)
    y_ref[...] = y


def _out(x, od, osb, gmix, wg, wdo, wso, wo, gffn, w1, w2, *, tm):
    rows = x.shape[0]
    row = lambda w: pl.BlockSpec((tm, w), lambda i: (i, 0))
    return pl.pallas_call(
        _out_kernel,
        grid=(rows // tm,),
        in_specs=[row(D_MODEL), row(DIFF_WIDTH), row(SB_WIDTH), _const_spec((1, D_MODEL)),
                  _const_spec((D_MODEL, 2 * D_MODEL)), _const_spec((DIFF_WIDTH, D_MODEL)),
                  _const_spec((SB_WIDTH, D_MODEL)), _const_spec((D_MODEL, D_MODEL)),
                  _const_spec((1, D_MODEL)), _const_spec((D_MODEL, D_FF)), _const_spec((D_FF, D_MODEL))],
        out_specs=row(D_MODEL),
        out_shape=jax.ShapeDtypeStruct((rows, D_MODEL), F32),
        compiler_params=pltpu.CompilerParams(dimension_semantics=("arbitrary",),
                                             vmem_limit_bytes=VMEM_LIMIT),
        name="out",
    )(x, od, osb, gmix, wg, wdo, wso, wo, gffn, w1, w2)


def _rope_tables(pos):
    half = ROT_DIM // 2
    inv = ROPE_THETA ** (-jnp.arange(0, ROT_DIM, 2, dtype=F32) / ROT_DIM)
    ang = pos.astype(F32)[:, None] * inv[None, :]
    cos, sin = jnp.cos(ang), jnp.sin(ang)
    n = pos.shape[0]
    pad = jnp.zeros((n, HEAD_DIM - ROT_DIM), F32)
    z8 = jnp.zeros((n, half), F32)
    cos64 = jnp.concatenate([cos, cos, pad + 1.0], axis=1)
    sa64 = jnp.concatenate([-sin, z8, pad], axis=1)
    sb64 = jnp.concatenate([z8, sin, pad], axis=1)
    rep = lambda t: jnp.concatenate([t, t], axis=1)
    return rep(cos64), rep(sa64), rep(sb64)


def kernel(x_prompt, x_sample, cache_diff_k, cache_diff_v, cache_sb_k, cache_sb_v, meta_tokens,
           g_mix, w_in, q_norm_g, k_norm_g, lam_q1, lam_k1, lam_q2, lam_k2, sub_g,
           w_diff_out, w_sb_out, w_out, g_ffn, w_ff1, w_ff2):
    nb, seq, _ = x_prompt.shape
    db, dseq, _ = x_sample.shape
    past = cache_diff_k.shape[2]
    lyr = 0

    w_in_b = w_in[lyr].astype(BF16)
    w_qkv, w_gate = w_in_b[:, :QKV_COLS], w_in_b[:, QKV_COLS:]
    wdo, wso, wo = (w_diff_out[lyr].astype(BF16), w_sb_out[lyr].astype(BF16), w_out[lyr].astype(BF16))
    w1, w2 = w_ff1[lyr].astype(BF16), w_ff2[lyr].astype(BF16)
    gmix = g_mix[lyr].reshape(1, D_MODEL)
    gffn = g_ffn[lyr].reshape(1, D_MODEL)
    qg = jnp.tile(q_norm_g[lyr], COL_BLOCK // HEAD_DIM).reshape(1, COL_BLOCK)
    kg = jnp.tile(k_norm_g[lyr], COL_BLOCK // HEAD_DIM).reshape(1, COL_BLOCK)
    subg_col = sub_g[lyr].reshape(LANES, 1)
    lams = [t[lyr].reshape(1, HEAD_DIM) for t in (lam_q1, lam_k1, lam_q2, lam_k2)]
    grp = jnp.arange(COL_BLOCK, dtype=jnp.int32) // HEAD_DIM
    gmat = jnp.where(grp[:, None] == grp[None, :], 1.0 / HEAD_DIM, 0.0).astype(BF16)

    main_pos = N_META + jnp.arange(seq, dtype=jnp.int32)
    small_pos = jnp.concatenate([jnp.arange(N_META, dtype=jnp.int32),
                                 jnp.tile(past + jnp.arange(dseq, dtype=jnp.int32), db)])
    x_main = x_prompt.reshape(nb * seq, D_MODEL)
    x_small = jnp.concatenate([meta_tokens.astype(F32), x_sample.reshape(db * dseq, D_MODEL)], axis=0)
    pm = _proj(x_main, gmix, w_qkv, qg, kg, gmat, *_rope_tables(main_pos), tm=512, prompt_layout=True,
               lead=N_META)
    ps = _proj(x_small, gmix, w_qkv, qg, kg, gmat, *_rope_tables(small_pos), tm=x_small.shape[0],
               prompt_layout=False)
    qd_m, kd_hm, kdb_m, vd_hm, vdt_m, qs_m, kst_m, ksb_m, vstf_m, vst_m = pm
    qd_s, kd_s, kdb_s, vd_s, vdb_s, qs_s, ks_s, ksb_s, vs_s, vsb_s = ps

    def bt(a, n, t):
        return a.reshape(n, t, a.shape[-1])

    def meta_keys(a):
        return jnp.pad(a[:N_META], ((0, LANES - N_META), (0, 0)))[None]

    def meta_vals_t(a):
        return jnp.swapaxes(meta_keys(a), 1, 2)

    od_p = _diff_attn(lams, subg_col, bt(qd_m, nb, seq), meta_keys(kdb_s), meta_vals_t(vdb_s),
                      bt(kdb_m, nb, seq), vdt_m,
                      tq=512, tk=256, pref_valid=N_META)
    os_p = _sb_attn(bt(qs_m, nb, seq), meta_keys(ksb_s), meta_vals_t(vsb_s),
                    bt(ksb_m, nb, seq), vst_m,
                    tq=512, tk=256, pref_valid=N_META)

    smp = lambda a: bt(a[N_META:], db, dseq)
    cdk = cache_diff_k[lyr].reshape(db, past * N_HEADS, LANES)
    cdv = cache_diff_v[lyr].reshape(db, past * N_HEADS, LANES)
    od_s = _diff_decode(lams, sub_g[lyr].reshape(1, LANES), smp(qd_s), cdk, cdv, smp(kdb_s), smp(vdb_s),
                        chunk=1024)
    sb_t = lambda c: jnp.transpose(c[lyr], (0, 2, 3, 1)).reshape(db, SB_WIDTH, past)
    new_t = lambda a: jnp.pad(jnp.swapaxes(smp(a), 1, 2), ((0, 0), (0, 0), (0, LANES - dseq)))
    os_s = _sb_decode(smp(qs_s), sb_t(cache_sb_k), sb_t(cache_sb_v), new_t(ksb_s), new_t(vsb_s))

    y_p = _out(x_main, od_p.reshape(nb * seq, DIFF_WIDTH), os_p.reshape(nb * seq, SB_WIDTH),
               gmix, w_gate, wdo, wso, wo, gffn, w1, w2, tm=512)
    y_s = _out(x_sample.reshape(db * dseq, D_MODEL), od_s.reshape(db * dseq, DIFF_WIDTH),
               os_s.reshape(db * dseq, SB_WIDTH), gmix, w_gate, wdo, wso, wo, gffn, w1, w2, tm=db * dseq)

    def diff_prompt_cache(head_major, small):
        meta = small[:N_META].reshape(N_META * N_HEADS, 2 * HEAD_DIM)
        full = _fill_lead_tokens(head_major, meta, nb=nb)
        return full.reshape(1, nb, seq + N_META, N_HEADS, 2 * HEAD_DIM)

    def sb_prompt_cache(main_t, small):
        meta_t = jnp.broadcast_to(small[:N_META].T[None], (nb, SB_WIDTH, N_META))
        full = jnp.concatenate([meta_t, main_t], axis=2).reshape(nb, N_HEADS, HEAD_DIM, seq + N_META)
        return jnp.transpose(full, (0, 3, 1, 2))[None]

    def sample_cache(small, dim):
        return small[N_META:].reshape(1, db, dseq, N_HEADS, dim)

    return (y_p.reshape(nb, seq, D_MODEL), y_s.reshape(db, dseq, D_MODEL),
            diff_prompt_cache(kd_hm, kd_s), diff_prompt_cache(vd_hm, vd_s),
            sb_prompt_cache(kst_m, ks_s), sb_prompt_cache(vstf_m, vs_s),
            sample_cache(kd_s, 2 * HEAD_DIM), sample_cache(vd_s, 2 * HEAD_DIM),
            sample_cache(ks_s, HEAD_DIM), sample_cache(vs_s, HEAD_DIM))
```

```python
import functools
import math

import jax
import jax.numpy as jnp
from jax import lax
from jax.experimental import pallas as pl
from jax.experimental.pallas import tpu as pltpu

F32 = jnp.float32
BF16 = jnp.bfloat16

D_MODEL = 1024
N_META = 16
CHUNK = 64
N_HEADS = 8
HEAD_DIM = 64
DIFF_WIDTH = N_HEADS * 2 * HEAD_DIM
SB_WIDTH = N_HEADS * HEAD_DIM
QKV_COLS = 3 * DIFF_WIDTH + 3 * SB_WIDTH
D_FF = 4 * D_MODEL
ROT_DIM = HEAD_DIM // 4
ROPE_THETA = 500000.0
EPS = 1e-6
NEG = -1e30
LOG2E = math.log2(math.e)
Q_SCALE = HEAD_DIM ** -0.5 * LOG2E
ONES_ROWS = 16
LAM_INIT = 0.8 - 0.6 * math.exp(-0.3 * 0)

LANES = 128
COL_BLOCK = 512
VMEM_LIMIT = 56 * 1024 * 1024

_NT = (((1,), (1,)), ((), ()))


def _rms(x):
    return x * lax.rsqrt(jnp.mean(x * x, axis=-1, keepdims=True) + EPS)


def _const_spec(shape):
    return pl.BlockSpec(shape, lambda *_: (0,) * len(shape), pipeline_mode=pl.Buffered(1))


def _proj_kernel(x_ref, gmix_ref, w_ref, qg_ref, kg_ref, gmat_ref, cos_ref, sa_ref, sb_ref,
                 qd_ref, kd_ref, kdb_ref, vd_ref, vdb_ref, qs_ref, ks_ref, ksb_ref, vs_ref, vsb_ref,
                 *, prompt_layout):
    tm = x_ref.shape[0]
    heads_per_block = COL_BLOCK // LANES

    def store_diff(ref, j, y):
        if not prompt_layout:
            ref[:, j * COL_BLOCK:(j + 1) * COL_BLOCK] = y
            return
        for hh in range(heads_per_block):
            ref[pl.ds(j * heads_per_block + hh, tm, stride=N_HEADS), :] = y[:, hh * LANES:(hh + 1) * LANES]

    h = (_rms(x_ref[...]) * gmix_ref[...]).astype(BF16)
    cos = cos_ref[...]
    sa = sa_ref[...]
    sb = sb_ref[...]

    def col(j):
        return jnp.dot(h, w_ref[:, j * COL_BLOCK:(j + 1) * COL_BLOCK], preferred_element_type=F32)

    def normed_rot(y, g):
        msq = jnp.dot((y * y).astype(BF16), gmat_ref[...], preferred_element_type=F32)
        yn = y * lax.rsqrt(msq + EPS) * g
        parts = []
        for c in range(COL_BLOCK // LANES):
            t = yn[:, c * LANES:(c + 1) * LANES]
            parts.append(t * cos + pltpu.roll(t, LANES - ROT_DIM // 2, 1) * sa
                         + pltpu.roll(t, ROT_DIM // 2, 1) * sb)
        return jnp.concatenate(parts, axis=1)

    for j in range(2):
        sl = slice(j * COL_BLOCK, (j + 1) * COL_BLOCK)
        q = normed_rot(col(j), qg_ref[...])
        qd_ref[:, sl] = (q * Q_SCALE).astype(BF16)
        k = normed_rot(col(2 + j), kg_ref[...])
        store_diff(kd_ref, j, k)
        kdb_ref[:, sl] = k.astype(BF16)
        v = col(4 + j)
        store_diff(vd_ref, j, v)
        if prompt_layout:
            vdb_ref[sl, :] = v.T.astype(BF16)
        else:
            vdb_ref[:, sl] = v.astype(BF16)
    qs_ref[...] = (col(6) * Q_SCALE).astype(BF16)
    k = col(7)
    ksb_ref[...] = k.astype(BF16)
    v = col(8)
    if prompt_layout:
        ks_ref[...] = k.T
        vt = v.T
        vs_ref[...] = vt
        vsb_ref[...] = vt.astype(BF16)
    else:
        ks_ref[...] = k
        vs_ref[...] = v
        vsb_ref[...] = v.astype(BF16)


def _proj(x, gmix, w_qkv, qg, kg, gmat, cos, sa, sb, *, tm, prompt_layout, lead=0):
    rows = x.shape[0]
    seq = cos.shape[0]
    n_pos_tiles = seq // tm
    row = lambda w: pl.BlockSpec((tm, w), lambda i: (i, 0))
    tab = pl.BlockSpec((tm, LANES), lambda i: (i % n_pos_tiles, 0))
    wide = lambda dt: jax.ShapeDtypeStruct((rows, DIFF_WIDTH), dt)
    narrow = lambda dt: jax.ShapeDtypeStruct((rows, SB_WIDTH), dt)
    if prompt_layout:
        nb = rows // seq
        t_spec = lambda w: pl.BlockSpec((None, w, tm), lambda i: (i // n_pos_tiles, 0, i % n_pos_tiles))
        t_shape = lambda w, dt: jax.ShapeDtypeStruct((nb, w, seq), dt)
        hm_spec = pl.BlockSpec(
            (pl.Element(tm * N_HEADS), pl.Element(LANES)),
            lambda i: (((i // n_pos_tiles) * (seq + lead) + lead + (i % n_pos_tiles) * tm) * N_HEADS, 0))
        hm_shape = jax.ShapeDtypeStruct((nb * (seq + lead) * N_HEADS, LANES), F32)
        out_specs = [row(DIFF_WIDTH), hm_spec, row(DIFF_WIDTH), hm_spec, t_spec(DIFF_WIDTH),
                     row(SB_WIDTH), t_spec(SB_WIDTH), row(SB_WIDTH), t_spec(SB_WIDTH), t_spec(SB_WIDTH)]
        out_shape = [wide(BF16), hm_shape, wide(BF16), hm_shape, t_shape(DIFF_WIDTH, BF16),
                     narrow(BF16), t_shape(SB_WIDTH, F32), narrow(BF16), t_shape(SB_WIDTH, F32),
                     t_shape(SB_WIDTH, BF16)]
    else:
        out_specs = [row(DIFF_WIDTH)] * 5 + [row(SB_WIDTH)] * 5
        out_shape = [wide(BF16), wide(F32), wide(BF16), wide(F32), wide(BF16),
                     narrow(BF16), narrow(F32), narrow(BF16), narrow(F32), narrow(BF16)]
    return pl.pallas_call(
        functools.partial(_proj_kernel, prompt_layout=prompt_layout),
        grid=(rows // tm,),
        in_specs=[row(D_MODEL), _const_spec((1, D_MODEL)), _const_spec((D_MODEL, QKV_COLS)),
                  _const_spec((1, COL_BLOCK)), _const_spec((1, COL_BLOCK)),
                  _const_spec((COL_BLOCK, COL_BLOCK)), tab, tab, tab],
        out_specs=out_specs,
        out_shape=out_shape,
        compiler_params=pltpu.CompilerParams(dimension_semantics=("arbitrary",),
                                             vmem_limit_bytes=VMEM_LIMIT),
        name="proj",
    )(x, gmix, w_qkv, qg, kg, gmat, cos, sa, sb)


def _fill_lead_kernel(lead_ref, big_ref, out_ref):
    del big_ref
    out_ref[...] = lead_ref[...]


def _fill_lead_tokens(big, lead_rows, *, nb):
    n = lead_rows.shape[0]
    per_batch = big.shape[0] // nb
    return pl.pallas_call(
        _fill_lead_kernel,
        grid=(nb,),
        in_specs=[_const_spec((n, LANES)), pl.BlockSpec(memory_space=pl.ANY)],
        out_specs=pl.BlockSpec((pl.Element(n), pl.Element(LANES)), lambda b: (b * per_batch, 0)),
        out_shape=jax.ShapeDtypeStruct(big.shape, big.dtype),
        input_output_aliases={1: 0},
        compiler_params=pltpu.CompilerParams(dimension_semantics=("arbitrary",)),
        name="fill_lead_tokens",
    )(lead_rows, big)


def _split_halves(q):
    lane = lax.broadcasted_iota(jnp.int32, q.shape, 1)
    zero = jnp.zeros_like(q)
    return jnp.where(lane < HEAD_DIM, q, zero), jnp.where(lane >= HEAD_DIM, q, zero)


def _diff_kernel(lq1_ref, lk1_ref, lq2_ref, lk2_ref, subg_ref, q_ref, kp_ref, vp_ref, km_ref, vm_ref,
                 o_ref, m_ref, acc_ref, s_ref, *, tq, tk, tkp, pref_valid, n_diag):
    i = pl.program_id(2)
    n_full = i * n_diag
    qs = _split_halves(q_ref[0])
    m_ref[...] = jnp.full(m_ref.shape, NEG, F32)
    acc_ref[...] = jnp.zeros(acc_ref.shape, F32)

    def issue(kblk, slot):
        n = kblk.shape[0]
        for a in range(2):
            s_ref[slot, a, :n, :] = lax.dot_general(kblk, qs[a], _NT, preferred_element_type=F32)

    def consume(slot, vtblk, mask):
        n = vtblk.shape[1]
        vt_ones = jnp.concatenate([vtblk, jnp.ones((ONES_ROWS, n), BF16)], axis=0)
        for a in range(2):
            s = s_ref[slot, a, :n, :]
            if mask is not None:
                s = jnp.where(mask, s, NEG)
            m_prev = m_ref[a]
            m_new = jnp.maximum(m_prev, jnp.max(s, axis=0, keepdims=True))
            alpha = jnp.exp2(m_prev - m_new)
            p = jnp.exp2(s - m_new)
            acc_ref[a] = alpha * acc_ref[a] + jnp.dot(vt_ones, p.astype(BF16), preferred_element_type=F32)
            m_ref[a] = m_new

    def main_k(g):
        return km_ref[0, pl.ds(pl.multiple_of(g * tk, tk), tk), :]

    def main_vt(g):
        return vm_ref[0, :, pl.ds(pl.multiple_of(g * tk, tk), tk)]

    issue(kp_ref[0], 1)
    issue(main_k(0), 0)
    consume(1, vp_ref[0], lax.broadcasted_iota(jnp.int32, (tkp, tq), 0) < pref_valid)

    def full_body(t, carry):
        issue(main_k(2 * t + 1), 1)
        consume(0, main_vt(2 * t), None)
        issue(main_k(2 * t + 2), 0)
        consume(1, main_vt(2 * t + 1), None)
        return carry
    lax.fori_loop(0, i * (n_diag // 2), full_body, 0)

    key_idx = lax.broadcasted_iota(jnp.int32, (tk, tq), 0)
    q_chunk = lax.broadcasted_iota(jnp.int32, (tk, tq), 1) // CHUNK
    for d in range(n_diag):
        g = n_full + d
        if d + 1 < n_diag:
            issue(main_k(g + 1), (d + 1) % 2)
        consume(d % 2, main_vt(g), (key_idx + d * tk) // CHUNK <= q_chunk)

    lam = (jnp.exp(jnp.sum(lq1_ref[...] * lk1_ref[...], axis=1, keepdims=True))
           - jnp.exp(jnp.sum(lq2_ref[...] * lk2_ref[...], axis=1, keepdims=True)) + LAM_INIT)
    o = (acc_ref[0, :LANES] / acc_ref[0, LANES:LANES + 1]
         - lam * (acc_ref[1, :LANES] / acc_ref[1, LANES:LANES + 1]))
    o = o * lax.rsqrt(jnp.mean(o * o, axis=0, keepdims=True) + EPS) * subg_ref[...] * (1.0 - LAM_INIT)
    o_ref[0] = o.T.astype(BF16)


def _diff_attn(lams, subg_col, q, kp, vtp, km, vtm, *, tq, tk, pref_valid):
    nb, tq_total, _ = q.shape
    tkp = kp.shape[1]
    t_main = km.shape[1]
    n_diag = tq // tk
    assert n_diag % 2 == 0, "main blocks are consumed in pairs"
    kernel = functools.partial(_diff_kernel, tq=tq, tk=tk, tkp=tkp, pref_valid=pref_valid, n_diag=n_diag)
    small = _const_spec((1, HEAD_DIM))
    return pl.pallas_call(
        kernel,
        grid=(nb, N_HEADS, tq_total // tq),
        in_specs=[small, small, small, small, _const_spec((LANES, 1)),
                  pl.BlockSpec((1, tq, LANES), lambda b, h, i: (b, i, h)),
                  pl.BlockSpec((1, tkp, LANES), lambda b, h, i: (0, 0, h)),
                  pl.BlockSpec((1, LANES, tkp), lambda b, h, i: (0, h, 0)),
                  pl.BlockSpec((1, t_main, LANES), lambda b, h, i: (b, 0, h)),
                  pl.BlockSpec((1, LANES, t_main), lambda b, h, i: (b, h, 0))],
        out_specs=pl.BlockSpec((1, tq, LANES), lambda b, h, i: (b, i, h)),
        out_shape=jax.ShapeDtypeStruct((nb, tq_total, DIFF_WIDTH), BF16),
        scratch_shapes=[pltpu.VMEM((2, 1, tq), F32), pltpu.VMEM((2, LANES + ONES_ROWS, tq), F32),
                        pltpu.VMEM((2, 2, max(tk, tkp), tq), F32)],
        compiler_params=pltpu.CompilerParams(dimension_semantics=("arbitrary",) * 3,
                                             vmem_limit_bytes=VMEM_LIMIT),
        name="diff_attn",
    )(*lams, subg_col, q, kp, vtp, km, vtm)


def _suffix_matrix(n):
    r = lax.broadcasted_iota(jnp.int32, (n + ONES_ROWS, n), 0)
    c = lax.broadcasted_iota(jnp.int32, (n + ONES_ROWS, n), 1)
    return jnp.where((c > r) | (r == n), 1.0, 0.0).astype(BF16)


def _sb_kernel(q_ref, kp_ref, vp_ref, km_ref, vm_ref, o_ref, c_ref, acc_ref, z_ref,
               *, tq, tk, tkp, pref_valid, n_diag):
    i = pl.program_id(2)
    n_full = i * n_diag
    qs = _split_halves(q_ref[0])
    c_ref[...] = jnp.zeros(c_ref.shape, F32)
    acc_ref[...] = jnp.zeros(acc_ref.shape, F32)

    def issue(kblk, slot):
        n = kblk.shape[0]
        for a in range(2):
            z_ref[slot, a, :n, :] = lax.dot_general(kblk, qs[a], _NT, preferred_element_type=F32)

    def consume(slot, vtblk, mask, sfx_mat):
        n = vtblk.shape[1]
        for a in range(2):
            rows = slice(a * HEAD_DIM, (a + 1) * HEAD_DIM)
            u = z_ref[slot, a, :n, :]
            nu = -u
            log1m = jnp.minimum(nu, 0.0) - jnp.log(1.0 + jnp.exp2(jnp.minimum(u, nu))) * LOG2E
            if mask is not None:
                log1m = jnp.where(mask, log1m, 0.0)
            sfx = jnp.dot(sfx_mat, log1m.astype(BF16), preferred_element_type=F32)
            t = u + log1m + sfx[:n]
            if mask is not None:
                t = jnp.where(mask, t, NEG)
            pv = jnp.dot(vtblk[rows, :], jnp.exp2(t).astype(BF16), preferred_element_type=F32)
            acc_ref[rows, :] += pv * jnp.exp2(c_ref[a])
            c_ref[a] += sfx[n:n + 1]

    def main_k(g):
        return km_ref[0, pl.ds(pl.multiple_of(g * tk, tk), tk), :]

    def main_vt(g):
        return vm_ref[0, :, pl.ds(pl.multiple_of(g * tk, tk), tk)]

    sfx_main = _suffix_matrix(tk)
    sfx_pref = sfx_main if tkp == tk else _suffix_matrix(tkp)
    key = lax.broadcasted_iota(jnp.int32, (tk, tq), 0)
    qry = lax.broadcasted_iota(jnp.int32, (tk, tq), 1)

    issue(kp_ref[0], 2)
    issue(main_k(n_full + n_diag - 1), 0)
    for d in reversed(range(n_diag)):
        s = (n_diag - 1 - d) % 2
        issue(main_k(n_full + d - 1 if d > 0 else jnp.maximum(n_full - 1, 0)), 1 - s)
        consume(s, main_vt(n_full + d), key + d * tk < qry, sfx_main)

    def full_body(t, carry):
        g = n_full - 1 - 2 * t
        issue(main_k(g - 1), 1)
        consume(0, main_vt(g), None, sfx_main)
        issue(main_k(jnp.maximum(g - 2, 0)), 0)
        consume(1, main_vt(g - 1), None, sfx_main)
        return carry
    lax.fori_loop(0, i * (n_diag // 2), full_body, 0)

    consume(2, vp_ref[0], lax.broadcasted_iota(jnp.int32, (tkp, tq), 0) < pref_valid, sfx_pref)
    o_ref[0] = acc_ref[...].T.astype(BF16)


def _sb_attn(q, kp, vtp, km, vtm, *, tq, tk, pref_valid):
    nb, tq_total, _ = q.shape
    tkp = kp.shape[1]
    t_main = km.shape[1]
    n_diag = tq // tk
    assert n_diag % 2 == 0, "main blocks are consumed in pairs"
    kernel = functools.partial(_sb_kernel, tq=tq, tk=tk, tkp=tkp, pref_valid=pref_valid, n_diag=n_diag)
    return pl.pallas_call(
        kernel,
        grid=(nb, SB_WIDTH // LANES, tq_total // tq),
        in_specs=[pl.BlockSpec((1, tq, LANES), lambda b, h, i: (b, i, h)),
                  pl.BlockSpec((1, tkp, LANES), lambda b, h, i: (0, 0, h)),
                  pl.BlockSpec((1, LANES, tkp), lambda b, h, i: (0, h, 0)),
                  pl.BlockSpec((1, t_main, LANES), lambda b, h, i: (b, 0, h)),
                  pl.BlockSpec((1, LANES, t_main), lambda b, h, i: (b, h, 0))],
        out_specs=pl.BlockSpec((1, tq, LANES), lambda b, h, i: (b, i, h)),
        out_shape=jax.ShapeDtypeStruct((nb, tq_total, SB_WIDTH), BF16),
        scratch_shapes=[pltpu.VMEM((2, 1, tq), F32), pltpu.VMEM((LANES, tq), F32),
                        pltpu.VMEM((3, 2, max(tk, tkp), tq), F32)],
        compiler_params=pltpu.CompilerParams(dimension_semantics=("arbitrary",) * 3,
                                             vmem_limit_bytes=VMEM_LIMIT),
        name="sb_attn",
    )(q, kp, vtp, km, vtm)


def _diff_decode_kernel(lq1_ref, lk1_ref, lq2_ref, lk2_ref, subg_ref, q_ref, ck_ref, cv_ref, kn_ref, vn_ref,
                        o_ref, m_ref, l_ref, acc_ref, *, chunk):
    kc = pl.program_id(1)
    nq = q_ref.shape[0]

    @pl.when(kc == 0)
    def _():
        m_ref[...] = jnp.full(m_ref.shape, NEG, F32)
        l_ref[...] = jnp.zeros(l_ref.shape, F32)
        acc_ref[...] = jnp.zeros(acc_ref.shape, F32)

    def update(h, k, v):
        n = k.shape[0]
        qs = _split_halves(q_ref[:, h * LANES:(h + 1) * LANES])
        v_ones = jnp.concatenate([v, jnp.ones((n, LANES), BF16)], axis=1)
        ps, alphas = [], []
        for a in range(2):
            s = lax.dot_general(qs[a], k, _NT, preferred_element_type=F32)
            m_prev = m_ref[2 * h + a]
            m_new = jnp.maximum(m_prev, jnp.max(s, axis=1, keepdims=True))
            alphas.append(jnp.exp2(m_prev - m_new))
            ps.append(jnp.exp2(s - m_new).astype(BF16))
            m_ref[2 * h + a] = m_new
        pv = jnp.dot(jnp.concatenate(ps, axis=0), v_ones, preferred_element_type=F32)
        for a in range(2):
            part = pv[a * nq:(a + 1) * nq]
            acc_ref[2 * h + a] = alphas[a] * acc_ref[2 * h + a] + part[:, :LANES]
            l_ref[2 * h + a] = alphas[a] * l_ref[2 * h + a] + part[:, LANES:]

    for h in range(N_HEADS):
        update(h, ck_ref[pl.ds(h, chunk, stride=N_HEADS), :].astype(BF16),
               cv_ref[pl.ds(h, chunk, stride=N_HEADS), :].astype(BF16))

    @pl.when(kc == pl.num_programs(1) - 1)
    def _():
        lam = (jnp.exp(jnp.sum(lq1_ref[...] * lk1_ref[...], axis=1, keepdims=True))
               - jnp.exp(jnp.sum(lq2_ref[...] * lk2_ref[...], axis=1, keepdims=True)) + LAM_INIT)
        for h in range(N_HEADS):
            cols = slice(h * LANES, (h + 1) * LANES)
            update(h, kn_ref[:, cols], vn_ref[:, cols])
            o = acc_ref[2 * h] / l_ref[2 * h] - lam * (acc_ref[2 * h + 1] / l_ref[2 * h + 1])
            o_ref[:, cols] = (_rms(o) * subg_ref[...] * (1.0 - LAM_INIT)).astype(BF16)


def _diff_decode(lams, subg_row, q, ck, cv, kn, vn, *, chunk):
    nb, nq, _ = q.shape
    n_chunks = ck.shape[1] // (chunk * N_HEADS)
    small = _const_spec((1, HEAD_DIM))
    tok = pl.BlockSpec((None, nq, DIFF_WIDTH), lambda b, c: (b, 0, 0))
    cache = pl.BlockSpec((None, chunk * N_HEADS, LANES), lambda b, c: (b, c, 0))
    return pl.pallas_call(
        functools.partial(_diff_decode_kernel, chunk=chunk),
        grid=(nb, n_chunks),
        in_specs=[small, small, small, small, _const_spec((1, LANES)), tok, cache, cache, tok, tok],
        out_specs=tok,
        out_shape=jax.ShapeDtypeStruct((nb, nq, DIFF_WIDTH), BF16),
        scratch_shapes=[pltpu.VMEM((2 * N_HEADS, nq, 1), F32), pltpu.VMEM((2 * N_HEADS, nq, LANES), F32),
                        pltpu.VMEM((2 * N_HEADS, nq, LANES), F32)],
        compiler_params=pltpu.CompilerParams(dimension_semantics=("arbitrary",) * 2,
                                             vmem_limit_bytes=VMEM_LIMIT),
        name="diff_decode",
    )(*lams, subg_row, q, ck, cv, kn, vn)


SB_BLOCK = 256


def _sb_decode_kernel(q_ref, ckt_ref, cvt_ref, knt_ref, vnt_ref, o_ref):
    nq = q_ref.shape[0]
    past = ckt_ref.shape[1]
    n_blk = past // SB_BLOCK
    def suffix_cols(n):
        r = lax.broadcasted_iota(jnp.int32, (n, n + LANES), 0)
        c = lax.broadcasted_iota(jnp.int32, (n, n + LANES), 1)
        return jnp.where((r > c) | (c >= n), 1.0, 0.0).astype(BF16)
    sfx_blk = suffix_cols(SB_BLOCK)
    sfx_new = suffix_cols(LANES)
    row = lax.broadcasted_iota(jnp.int32, (nq, LANES), 0)
    lane = lax.broadcasted_iota(jnp.int32, (nq, LANES), 1)
    new_mask = lane < row

    def log1m_of(u):
        nu = -u
        return jnp.minimum(nu, 0.0) - jnp.log(1.0 + jnp.exp2(jnp.minimum(u, nu))) * LOG2E

    for j in range(ckt_ref.shape[0] // LANES):
        rows = slice(j * LANES, (j + 1) * LANES)
        kt = ckt_ref[rows, :].astype(BF16)
        vt = cvt_ref[rows, :].astype(BF16)
        qs = _split_halves(q_ref[:, rows])
        ws, wns = [], []
        for a in range(2):
            un = jnp.dot(qs[a], knt_ref[rows, :], preferred_element_type=F32)
            ln = jnp.where(new_mask, log1m_of(un), 0.0)
            sn = jnp.dot(ln.astype(BF16), sfx_new, preferred_element_type=F32)
            wns.append(jnp.exp2(jnp.where(new_mask, un + ln + sn[:, :LANES], NEG)).astype(BF16))
            carry = sn[:, LANES:]
            u = jnp.dot(qs[a], kt, preferred_element_type=F32)
            l1m = log1m_of(u)
            stacked = jnp.concatenate(
                [l1m[:, b * SB_BLOCK:(b + 1) * SB_BLOCK] for b in range(n_blk)], axis=0).astype(BF16)
            sfx = jnp.dot(stacked, sfx_blk, preferred_element_type=F32)
            ts = [None] * n_blk
            for b in reversed(range(n_blk)):
                cols = slice(b * SB_BLOCK, (b + 1) * SB_BLOCK)
                part = sfx[b * nq:(b + 1) * nq]
                c2 = jnp.concatenate([carry] * (SB_BLOCK // LANES), axis=1)
                ts[b] = u[:, cols] + l1m[:, cols] + part[:, :SB_BLOCK] + c2
                carry = carry + part[:, SB_BLOCK:]
            ws.append(jnp.exp2(jnp.concatenate(ts, axis=1)).astype(BF16))
        o = (lax.dot_general(jnp.concatenate(ws, axis=0), vt, _NT, preferred_element_type=F32)
             + lax.dot_general(jnp.concatenate(wns, axis=0), vnt_ref[rows, :], _NT,
                               preferred_element_type=F32))
        o_ref[:, rows] = jnp.where(lane < HEAD_DIM, o[:nq], o[nq:]).astype(BF16)


def _sb_decode(q, ckt, cvt, knt, vnt):
    nb, nq, _ = q.shape
    assert nq <= LANES
    past = ckt.shape[2]
    width = 2 * LANES
    tok = pl.BlockSpec((None, nq, width), lambda b, g: (b, 0, g))
    cache = pl.BlockSpec((None, width, past), lambda b, g: (b, g, 0))
    new = pl.BlockSpec((None, width, LANES), lambda b, g: (b, g, 0))
    return pl.pallas_call(
        _sb_decode_kernel,
        grid=(nb, SB_WIDTH // width),
        in_specs=[tok, cache, cache, new, new],
        out_specs=tok,
        out_shape=jax.ShapeDtypeStruct((nb, nq, SB_WIDTH), BF16),
        compiler_params=pltpu.CompilerParams(dimension_semantics=("arbitrary",) * 2,
                                             vmem_limit_bytes=VMEM_LIMIT),
        name="sb_decode",
    )(q, ckt, cvt, knt, vnt)


def _out_kernel(x_ref, od_ref, os_ref, gmix_ref, wg_ref, wdo_ref, wso_ref, wo_ref, gffn_ref,
                w1_ref, w2_ref, y_ref):
    x = x_ref[...]
    h = (_rms(x) * gmix_ref[...]).astype(BF16)
    gate = jax.nn.sigmoid(jnp.dot(h, wg_ref[...], preferred_element_type=F32))
    a = jnp.dot(od_ref[...], wdo_ref[...], preferred_element_type=F32)
    b = jnp.dot(os_ref[...], wso_ref[...], preferred_element_type=F32)
    merged = (gate[:, :D_MODEL] * a + gate[:, D_MODEL:] * b).astype(BF16)
    x1 = x + jnp.dot(merged, wo_ref[...], preferred_element_type=F32)
    h2 = (_rms(x1) * gffn_ref[...]).astype(BF16)
    y = x1
    for c in range(D_FF // D_MODEL):
        sl = slice(c * D_MODEL, (c + 1) * D_MODEL)
        f = jnp.maximum(jnp.dot(h2, w1_ref[:, sl], preferred_element_type=F32), 0.0)
        y = y + jnp.dot((f * f).astype(BF16), w2_ref[sl, :], preferred_element_type=F32)
    y_ref[...] = y


def _out(x, od, osb, gmix, wg, wdo, wso, wo, gffn, w1, w2, *, tm):
    rows = x.shape[0]
    row = lambda w: pl.BlockSpec((tm, w), lambda i: (i, 0))
    return pl.pallas_call(
        _out_kernel,
        grid=(rows // tm,),
        in_specs=[row(D_MODEL), row(DIFF_WIDTH), row(SB_WIDTH), _const_spec((1, D_MODEL)),
                  _const_spec((D_MODEL, 2 * D_MODEL)), _const_spec((DIFF_WIDTH, D_MODEL)),
                  _const_spec((SB_WIDTH, D_MODEL)), _const_spec((D_MODEL, D_MODEL)),
                  _const_spec((1, D_MODEL)), _const_spec((D_MODEL, D_FF)), _const_spec((D_FF, D_MODEL))],
        out_specs=row(D_MODEL),
        out_shape=jax.ShapeDtypeStruct((rows, D_MODEL), F32),
        compiler_params=pltpu.CompilerParams(dimension_semantics=("arbitrary",),
                                             vmem_limit_bytes=VMEM_LIMIT),
        name="out",
    )(x, od, osb, gmix, wg, wdo, wso, wo, gffn, w1, w2)


def _rope_tables(pos):
    half = ROT_DIM // 2
    inv = ROPE_THETA ** (-jnp.arange(0, ROT_DIM, 2, dtype=F32) / ROT_DIM)
    ang = pos.astype(F32)[:, None] * inv[None, :]
    cos, sin = jnp.cos(ang), jnp.sin(ang)
    n = pos.shape[0]
    pad = jnp.zeros((n, HEAD_DIM - ROT_DIM), F32)
    z8 = jnp.zeros((n, half), F32)
    cos64 = jnp.concatenate([cos, cos, pad + 1.0], axis=1)
    sa64 = jnp.concatenate([-sin, z8, pad], axis=1)
    sb64 = jnp.concatenate([z8, sin, pad], axis=1)
    rep = lambda t: jnp.concatenate([t, t], axis=1)
    return rep(cos64), rep(sa64), rep(sb64)


def kernel(x_prompt, x_sample, cache_diff_k, cache_diff_v, cache_sb_k, cache_sb_v, meta_tokens,
           g_mix, w_in, q_norm_g, k_norm_g, lam_q1, lam_k1, lam_q2, lam_k2, sub_g,
           w_diff_out, w_sb_out, w_out, g_ffn, w_ff1, w_ff2):
    nb, seq, _ = x_prompt.shape
    db, dseq, _ = x_sample.shape
    past = cache_diff_k.shape[2]
    lyr = 0

    w_in_b = w_in[lyr].astype(BF16)
    w_qkv, w_gate = w_in_b[:, :QKV_COLS], w_in_b[:, QKV_COLS:]
    wdo, wso, wo = (w_diff_out[lyr].astype(BF16), w_sb_out[lyr].astype(BF16), w_out[lyr].astype(BF16))
    w1, w2 = w_ff1[lyr].astype(BF16), w_ff2[lyr].astype(BF16)
    gmix = g_mix[lyr].reshape(1, D_MODEL)
    gffn = g_ffn[lyr].reshape(1, D_MODEL)
    qg = jnp.tile(q_norm_g[lyr], COL_BLOCK // HEAD_DIM).reshape(1, COL_BLOCK)
    kg = jnp.tile(k_norm_g[lyr], COL_BLOCK // HEAD_DIM).reshape(1, COL_BLOCK)
    subg_col = sub_g[lyr].reshape(LANES, 1)
    lams = [t[lyr].reshape(1, HEAD_DIM) for t in (lam_q1, lam_k1, lam_q2, lam_k2)]
    grp = jnp.arange(COL_BLOCK, dtype=jnp.int32) // HEAD_DIM
    gmat = jnp.where(grp[:, None] == grp[None, :], 1.0 / HEAD_DIM, 0.0).astype(BF16)

    main_pos = N_META + jnp.arange(seq, dtype=jnp.int32)
    small_pos = jnp.concatenate([jnp.arange(N_META, dtype=jnp.int32),
                                 jnp.tile(past + jnp.arange(dseq, dtype=jnp.int32), db)])
    x_main = x_prompt.reshape(nb * seq, D_MODEL)
    x_small = jnp.concatenate([meta_tokens.astype(F32), x_sample.reshape(db * dseq, D_MODEL)], axis=0)
    pm = _proj(x_main, gmix, w_qkv, qg, kg, gmat, *_rope_tables(main_pos), tm=512, prompt_layout=True,
               lead=N_META)
    ps = _proj(x_small, gmix, w_qkv, qg, kg, gmat, *_rope_tables(small_pos), tm=x_small.shape[0],
               prompt_layout=False)
    qd_m, kd_hm, kdb_m, vd_hm, vdt_m, qs_m, kst_m, ksb_m, vstf_m, vst_m = pm
    qd_s, kd_s, kdb_s, vd_s, vdb_s, qs_s, ks_s, ksb_s, vs_s, vsb_s = ps

    def bt(a, n, t):
        return a.reshape(n, t, a.shape[-1])

    def meta_keys(a):
        return jnp.pad(a[:N_META], ((0, LANES - N_META), (0, 0)))[None]

    def meta_vals_t(a):
        return jnp.swapaxes(meta_keys(a), 1, 2)

    od_p = _diff_attn(lams, subg_col, bt(qd_m, nb, seq), meta_keys(kdb_s), meta_vals_t(vdb_s),
                      bt(kdb_m, nb, seq), vdt_m,
                      tq=1024, tk=256, pref_valid=N_META)
    os_p = _sb_attn(bt(qs_m, nb, seq), meta_keys(ksb_s), meta_vals_t(vsb_s),
                    bt(ksb_m, nb, seq), vst_m,
                    tq=1024, tk=256, pref_valid=N_META)

    smp = lambda a: bt(a[N_META:], db, dseq)
    cdk = cache_diff_k[lyr].reshape(db, past * N_HEADS, LANES)
    cdv = cache_diff_v[lyr].reshape(db, past * N_HEADS, LANES)
    od_s = _diff_decode(lams, sub_g[lyr].reshape(1, LANES), smp(qd_s), cdk, cdv, smp(kdb_s), smp(vdb_s),
                        chunk=1024)
    sb_t = lambda c: jnp.transpose(c[lyr], (0, 2, 3, 1)).reshape(db, SB_WIDTH, past)
    new_t = lambda a: jnp.pad(jnp.swapaxes(smp(a), 1, 2), ((0, 0), (0, 0), (0, LANES - dseq)))
    os_s = _sb_decode(smp(qs_s), sb_t(cache_sb_k), sb_t(cache_sb_v), new_t(ksb_s), new_t(vsb_s))

    y_p = _out(x_main, od_p.reshape(nb * seq, DIFF_WIDTH), os_p.reshape(nb * seq, SB_WIDTH),
               gmix, w_gate, wdo, wso, wo, gffn, w1, w2, tm=512)
    y_s = _out(x_sample.reshape(db * dseq, D_MODEL), od_s.reshape(db * dseq, DIFF_WIDTH),
               os_s.reshape(db * dseq, SB_WIDTH), gmix, w_gate, wdo, wso, wo, gffn, w1, w2, tm=db * dseq)

    def diff_prompt_cache(head_major, small):
        meta = small[:N_META].reshape(N_META * N_HEADS, 2 * HEAD_DIM)
        full = _fill_lead_tokens(head_major, meta, nb=nb)
        return full.reshape(1, nb, seq + N_META, N_HEADS, 2 * HEAD_DIM)

    def sb_prompt_cache(main_t, small):
        meta_t = jnp.broadcast_to(small[:N_META].T[None], (nb, SB_WIDTH, N_META))
        full = jnp.concatenate([meta_t, main_t], axis=2).reshape(nb, N_HEADS, HEAD_DIM, seq + N_META)
        return jnp.transpose(full, (0, 3, 1, 2))[None]

    def sample_cache(small, dim):
        return small[N_META:].reshape(1, db, dseq, N_HEADS, dim)

    return (y_p.reshape(nb, seq, D_MODEL), y_s.reshape(db, dseq, D_MODEL),
            diff_prompt_cache(kd_hm, kd_s), diff_prompt_cache(vd_hm, vd_s),
            sb_prompt_cache(kst_m, ks_s), sb_prompt_cache(vstf_m, vs_s),
            sample_cache(kd_s, 2 * HEAD_DIM), sample_cache(vd_s, 2 * HEAD_DIM),
            sample_cache(ks_s, HEAD_DIM), sample_cache(vs_s, HEAD_DIM))
```

```python
import functools
import math

import jax
import jax.numpy as jnp
from jax import lax
from jax.experimental import pallas as pl
from jax.experimental.pallas import tpu as pltpu

F32 = jnp.float32
BF16 = jnp.bfloat16

D_MODEL = 1024
N_META = 16
CHUNK = 64
N_HEADS = 8
HEAD_DIM = 64
DIFF_WIDTH = N_HEADS * 2 * HEAD_DIM
SB_WIDTH = N_HEADS * HEAD_DIM
QKV_COLS = 3 * DIFF_WIDTH + 3 * SB_WIDTH
D_FF = 4 * D_MODEL
ROT_DIM = HEAD_DIM // 4
ROPE_THETA = 500000.0
EPS = 1e-6
NEG = -1e30
LOG2E = math.log2(math.e)
Q_SCALE = HEAD_DIM ** -0.5 * LOG2E
ONES_ROWS = 16
LAM_INIT = 0.8 - 0.6 * math.exp(-0.3 * 0)

LANES = 128
MXU_TILE = 256
COL_BLOCK = 512
VMEM_LIMIT = 56 * 1024 * 1024

_NT = (((1,), (1,)), ((), ()))


def _rms(x):
    return x * lax.rsqrt(jnp.mean(x * x, axis=-1, keepdims=True) + EPS)


def _const_spec(shape):
    return pl.BlockSpec(shape, lambda *_: (0,) * len(shape), pipeline_mode=pl.Buffered(1))


def _proj_kernel(x_ref, gmix_ref, w_ref, qg_ref, kg_ref, gmat_ref, cos_ref, sa_ref, sb_ref,
                 qd_ref, kd_ref, kdb_ref, vd_ref, vdb_ref, qs_ref, ks_ref, ksb_ref, vs_ref, vsb_ref,
                 *, prompt_layout):
    tm = x_ref.shape[0]
    heads_per_block = COL_BLOCK // LANES

    def store_diff(ref, j, y):
        if not prompt_layout:
            ref[:, j * COL_BLOCK:(j + 1) * COL_BLOCK] = y
            return
        for hh in range(heads_per_block):
            ref[pl.ds(j * heads_per_block + hh, tm, stride=N_HEADS), :] = y[:, hh * LANES:(hh + 1) * LANES]

    h = (_rms(x_ref[...]) * gmix_ref[...]).astype(BF16)
    cos = cos_ref[...]
    sa = sa_ref[...]
    sb = sb_ref[...]

    def col(j):
        return jnp.dot(h, w_ref[:, j * COL_BLOCK:(j + 1) * COL_BLOCK], preferred_element_type=F32)

    def normed_rot(y, g):
        sq = (y * y).astype(BF16)
        msq = jnp.concatenate(
            [jnp.dot(sq[:, c * MXU_TILE:(c + 1) * MXU_TILE], gmat_ref[...], preferred_element_type=F32)
             for c in range(COL_BLOCK // MXU_TILE)], axis=1)
        yn = y * lax.rsqrt(msq + EPS) * g
        parts = []
        for c in range(COL_BLOCK // LANES):
            t = yn[:, c * LANES:(c + 1) * LANES]
            parts.append(t * cos + pltpu.roll(t, LANES - ROT_DIM // 2, 1) * sa
                         + pltpu.roll(t, ROT_DIM // 2, 1) * sb)
        return jnp.concatenate(parts, axis=1)

    for j in range(2):
        sl = slice(j * COL_BLOCK, (j + 1) * COL_BLOCK)
        q = normed_rot(col(j), qg_ref[...])
        qd_ref[:, sl] = (q * Q_SCALE).astype(BF16)
        k = normed_rot(col(2 + j), kg_ref[...])
        store_diff(kd_ref, j, k)
        kdb_ref[:, sl] = k.astype(BF16)
        v = col(4 + j)
        store_diff(vd_ref, j, v)
        if prompt_layout:
            vdb_ref[sl, :] = v.T.astype(BF16)
        else:
            vdb_ref[:, sl] = v.astype(BF16)
    qs_ref[...] = (col(6) * Q_SCALE).astype(BF16)
    k = col(7)
    ksb_ref[...] = k.astype(BF16)
    v = col(8)
    if prompt_layout:
        ks_ref[...] = k.T
        vt = v.T
        vs_ref[...] = vt
        vsb_ref[...] = vt.astype(BF16)
    else:
        ks_ref[...] = k
        vs_ref[...] = v
        vsb_ref[...] = v.astype(BF16)


def _proj(x, gmix, w_qkv, qg, kg, gmat, cos, sa, sb, *, tm, prompt_layout, lead=0):
    rows = x.shape[0]
    seq = cos.shape[0]
    n_pos_tiles = seq // tm
    row = lambda w: pl.BlockSpec((tm, w), lambda i: (i, 0))
    tab = pl.BlockSpec((tm, LANES), lambda i: (i % n_pos_tiles, 0))
    wide = lambda dt: jax.ShapeDtypeStruct((rows, DIFF_WIDTH), dt)
    narrow = lambda dt: jax.ShapeDtypeStruct((rows, SB_WIDTH), dt)
    if prompt_layout:
        nb = rows // seq
        t_spec = lambda w: pl.BlockSpec((None, w, tm), lambda i: (i // n_pos_tiles, 0, i % n_pos_tiles))
        t_shape = lambda w, dt: jax.ShapeDtypeStruct((nb, w, seq), dt)
        hm_spec = pl.BlockSpec(
            (pl.Element(tm * N_HEADS), pl.Element(LANES)),
            lambda i: (((i // n_pos_tiles) * (seq + lead) + lead + (i % n_pos_tiles) * tm) * N_HEADS, 0))
        hm_shape = jax.ShapeDtypeStruct((nb * (seq + lead) * N_HEADS, LANES), F32)
        out_specs = [row(DIFF_WIDTH), hm_spec, row(DIFF_WIDTH), hm_spec, t_spec(DIFF_WIDTH),
                     row(SB_WIDTH), t_spec(SB_WIDTH), row(SB_WIDTH), t_spec(SB_WIDTH), t_spec(SB_WIDTH)]
        out_shape = [wide(BF16), hm_shape, wide(BF16), hm_shape, t_shape(DIFF_WIDTH, BF16),
                     narrow(BF16), t_shape(SB_WIDTH, F32), narrow(BF16), t_shape(SB_WIDTH, F32),
                     t_shape(SB_WIDTH, BF16)]
    else:
        out_specs = [row(DIFF_WIDTH)] * 5 + [row(SB_WIDTH)] * 5
        out_shape = [wide(BF16), wide(F32), wide(BF16), wide(F32), wide(BF16),
                     narrow(BF16), narrow(F32), narrow(BF16), narrow(F32), narrow(BF16)]
    return pl.pallas_call(
        functools.partial(_proj_kernel, prompt_layout=prompt_layout),
        grid=(rows // tm,),
        in_specs=[row(D_MODEL), _const_spec((1, D_MODEL)), _const_spec((D_MODEL, QKV_COLS)),
                  _const_spec((1, COL_BLOCK)), _const_spec((1, COL_BLOCK)),
                  _const_spec((MXU_TILE, MXU_TILE)), tab, tab, tab],
        out_specs=out_specs,
        out_shape=out_shape,
        compiler_params=pltpu.CompilerParams(dimension_semantics=("arbitrary",),
                                             vmem_limit_bytes=VMEM_LIMIT),
        name="proj",
    )(x, gmix, w_qkv, qg, kg, gmat, cos, sa, sb)


def _fill_lead_kernel(lead_ref, big_ref, out_ref):
    del big_ref
    out_ref[...] = lead_ref[...]


def _fill_lead_tokens(big, lead_rows, *, nb):
    n = lead_rows.shape[0]
    per_batch = big.shape[0] // nb
    return pl.pallas_call(
        _fill_lead_kernel,
        grid=(nb,),
        in_specs=[_const_spec((n, LANES)), pl.BlockSpec(memory_space=pl.ANY)],
        out_specs=pl.BlockSpec((pl.Element(n), pl.Element(LANES)), lambda b: (b * per_batch, 0)),
        out_shape=jax.ShapeDtypeStruct(big.shape, big.dtype),
        input_output_aliases={1: 0},
        compiler_params=pltpu.CompilerParams(dimension_semantics=("arbitrary",)),
        name="fill_lead_tokens",
    )(lead_rows, big)


def _split_halves(q):
    lane = lax.broadcasted_iota(jnp.int32, q.shape, 1)
    zero = jnp.zeros_like(q)
    return jnp.where(lane < HEAD_DIM, q, zero), jnp.where(lane >= HEAD_DIM, q, zero)


def _diff_kernel(lq1_ref, lk1_ref, lq2_ref, lk2_ref, subg_ref, q_ref, kp_ref, vp_ref, km_ref, vm_ref,
                 o_ref, m_ref, acc_ref, s_ref, *, tq, tk, tkp, pref_valid, n_diag):
    i = pl.program_id(2)
    n_full = i * n_diag
    qs = _split_halves(q_ref[0])
    m_ref[...] = jnp.full(m_ref.shape, NEG, F32)
    acc_ref[...] = jnp.zeros(acc_ref.shape, F32)

    def issue(kblk, slot, qlo=0):
        n = kblk.shape[0]
        for a in range(2):
            s_ref[slot, a, :n, qlo:] = lax.dot_general(kblk, qs[a][qlo:], _NT, preferred_element_type=F32)

    def consume(slot, vtblk, mask, qlo=0):
        n = vtblk.shape[1]
        vt_ones = jnp.concatenate([vtblk, jnp.ones((ONES_ROWS, n), BF16)], axis=0)
        for a in range(2):
            s = s_ref[slot, a, :n, qlo:]
            if mask is not None:
                s = jnp.where(mask[:, qlo:], s, NEG)
            m_prev = m_ref[a, :, qlo:]
            m_new = jnp.maximum(m_prev, jnp.max(s, axis=0, keepdims=True))
            alpha = jnp.exp2(m_prev - m_new)
            p = jnp.exp2(s - m_new)
            acc_ref[a, :, qlo:] = (alpha * acc_ref[a, :, qlo:]
                                   + jnp.dot(vt_ones, p.astype(BF16), preferred_element_type=F32))
            m_ref[a, :, qlo:] = m_new

    def main_k(g):
        return km_ref[0, pl.ds(pl.multiple_of(g * tk, tk), tk), :]

    def main_vt(g):
        return vm_ref[0, :, pl.ds(pl.multiple_of(g * tk, tk), tk)]

    issue(kp_ref[0], 1)
    issue(main_k(0), 0)
    consume(1, vp_ref[0], lax.broadcasted_iota(jnp.int32, (tkp, tq), 0) < pref_valid)

    def full_body(t, carry):
        issue(main_k(2 * t + 1), 1)
        consume(0, main_vt(2 * t), None)
        issue(main_k(2 * t + 2), 0)
        consume(1, main_vt(2 * t + 1), None)
        return carry
    lax.fori_loop(0, i * (n_diag // 2), full_body, 0)

    key_idx = lax.broadcasted_iota(jnp.int32, (tk, tq), 0)
    q_chunk = lax.broadcasted_iota(jnp.int32, (tk, tq), 1) // CHUNK
    for d in range(n_diag):
        g = n_full + d
        if d + 1 < n_diag:
            issue(main_k(g + 1), (d + 1) % 2, qlo=(d + 1) * tk)
        consume(d % 2, main_vt(g), (key_idx + d * tk) // CHUNK <= q_chunk, qlo=d * tk)

    lam = (jnp.exp(jnp.sum(lq1_ref[...] * lk1_ref[...], axis=1, keepdims=True))
           - jnp.exp(jnp.sum(lq2_ref[...] * lk2_ref[...], axis=1, keepdims=True)) + LAM_INIT)
    o = (acc_ref[0, :LANES] / acc_ref[0, LANES:LANES + 1]
         - lam * (acc_ref[1, :LANES] / acc_ref[1, LANES:LANES + 1]))
    o = o * lax.rsqrt(jnp.mean(o * o, axis=0, keepdims=True) + EPS) * subg_ref[...] * (1.0 - LAM_INIT)
    o_ref[0] = o.T.astype(BF16)


def _diff_attn(lams, subg_col, q, kp, vtp, km, vtm, *, tq, tk, pref_valid):
    nb, tq_total, _ = q.shape
    tkp = kp.shape[1]
    t_main = km.shape[1]
    n_diag = tq // tk
    assert n_diag % 2 == 0, "main blocks are consumed in pairs"
    kernel = functools.partial(_diff_kernel, tq=tq, tk=tk, tkp=tkp, pref_valid=pref_valid, n_diag=n_diag)
    small = _const_spec((1, HEAD_DIM))
    return pl.pallas_call(
        kernel,
        grid=(nb, N_HEADS, tq_total // tq),
        in_specs=[small, small, small, small, _const_spec((LANES, 1)),
                  pl.BlockSpec((1, tq, LANES), lambda b, h, i: (b, i, h)),
                  pl.BlockSpec((1, tkp, LANES), lambda b, h, i: (0, 0, h)),
                  pl.BlockSpec((1, LANES, tkp), lambda b, h, i: (0, h, 0)),
                  pl.BlockSpec((1, t_main, LANES), lambda b, h, i: (b, 0, h)),
                  pl.BlockSpec((1, LANES, t_main), lambda b, h, i: (b, h, 0))],
        out_specs=pl.BlockSpec((1, tq, LANES), lambda b, h, i: (b, i, h)),
        out_shape=jax.ShapeDtypeStruct((nb, tq_total, DIFF_WIDTH), BF16),
        scratch_shapes=[pltpu.VMEM((2, 1, tq), F32), pltpu.VMEM((2, LANES + ONES_ROWS, tq), F32),
                        pltpu.VMEM((2, 2, max(tk, tkp), tq), F32)],
        compiler_params=pltpu.CompilerParams(dimension_semantics=("arbitrary",) * 3,
                                             vmem_limit_bytes=VMEM_LIMIT),
        name="diff_attn",
    )(*lams, subg_col, q, kp, vtp, km, vtm)


def _suffix_matrix(n):
    r = lax.broadcasted_iota(jnp.int32, (n + ONES_ROWS, n), 0)
    c = lax.broadcasted_iota(jnp.int32, (n + ONES_ROWS, n), 1)
    return jnp.where((c >= r) | (r == n), 1.0, 0.0).astype(BF16)


def _sb_kernel(q_ref, kp_ref, vp_ref, km_ref, vm_ref, o_ref, c_ref, acc_ref, z_ref,
               *, tq, tk, tkp, pref_valid, n_diag):
    i = pl.program_id(2)
    n_full = i * n_diag
    qs = _split_halves(q_ref[0])
    c_ref[...] = jnp.zeros(c_ref.shape, F32)
    acc_ref[...] = jnp.zeros(acc_ref.shape, F32)

    def issue(kblk, slot, qlo=0):
        n = kblk.shape[0]
        for a in range(2):
            z_ref[slot, a, :n, qlo:] = lax.dot_general(kblk, qs[a][qlo:], _NT, preferred_element_type=F32)

    def consume(slot, vtblk, mask, sfx_mat, qlo=0):
        n = vtblk.shape[1]
        if mask is not None:
            mask = mask[:, qlo:]
        for a in range(2):
            rows = slice(a * HEAD_DIM, (a + 1) * HEAD_DIM)
            u = z_ref[slot, a, :n, qlo:]
            neg_part = jnp.minimum(u, 0.0)
            d = neg_part - u
            log1m = d - jnp.log(1.0 + jnp.exp2(neg_part + d)) * LOG2E
            if mask is not None:
                log1m = jnp.where(mask, log1m, 0.0)
            sfx = jnp.dot(sfx_mat, log1m.astype(BF16), preferred_element_type=F32)
            t = u + sfx[:n]
            if mask is not None:
                t = jnp.where(mask, t, NEG)
            pv = jnp.dot(vtblk[rows, :], jnp.exp2(t).astype(BF16), preferred_element_type=F32)
            acc_ref[rows, qlo:] += pv * jnp.exp2(c_ref[a, :, qlo:])
            c_ref[a, :, qlo:] += sfx[n:n + 1]

    def main_k(g):
        return km_ref[0, pl.ds(pl.multiple_of(g * tk, tk), tk), :]

    def main_vt(g):
        return vm_ref[0, :, pl.ds(pl.multiple_of(g * tk, tk), tk)]

    sfx_main = _suffix_matrix(tk)
    sfx_pref = sfx_main if tkp == tk else _suffix_matrix(tkp)
    key = lax.broadcasted_iota(jnp.int32, (tk, tq), 0)
    qry = lax.broadcasted_iota(jnp.int32, (tk, tq), 1)

    issue(kp_ref[0], 2)
    issue(main_k(n_full + n_diag - 1), 0, qlo=(n_diag - 1) * tk)
    for d in reversed(range(n_diag)):
        s = (n_diag - 1 - d) % 2
        if d > 0:
            issue(main_k(n_full + d - 1), 1 - s, qlo=(d - 1) * tk)
        else:
            issue(main_k(jnp.maximum(n_full - 1, 0)), 1 - s)
        consume(s, main_vt(n_full + d), key + d * tk < qry, sfx_main, qlo=d * tk)

    def full_body(t, carry):
        g = n_full - 1 - 2 * t
        issue(main_k(g - 1), 1)
        consume(0, main_vt(g), None, sfx_main)
        issue(main_k(jnp.maximum(g - 2, 0)), 0)
        consume(1, main_vt(g - 1), None, sfx_main)
        return carry
    lax.fori_loop(0, i * (n_diag // 2), full_body, 0)

    consume(2, vp_ref[0], lax.broadcasted_iota(jnp.int32, (tkp, tq), 0) < pref_valid, sfx_pref)
    o_ref[0] = acc_ref[...].T.astype(BF16)


def _sb_attn(q, kp, vtp, km, vtm, *, tq, tk, pref_valid):
    nb, tq_total, _ = q.shape
    tkp = kp.shape[1]
    t_main = km.shape[1]
    n_diag = tq // tk
    assert n_diag % 2 == 0, "main blocks are consumed in pairs"
    kernel = functools.partial(_sb_kernel, tq=tq, tk=tk, tkp=tkp, pref_valid=pref_valid, n_diag=n_diag)
    return pl.pallas_call(
        kernel,
        grid=(nb, SB_WIDTH // LANES, tq_total // tq),
        in_specs=[pl.BlockSpec((1, tq, LANES), lambda b, h, i: (b, i, h)),
                  pl.BlockSpec((1, tkp, LANES), lambda b, h, i: (0, 0, h)),
                  pl.BlockSpec((1, LANES, tkp), lambda b, h, i: (0, h, 0)),
                  pl.BlockSpec((1, t_main, LANES), lambda b, h, i: (b, 0, h)),
                  pl.BlockSpec((1, LANES, t_main), lambda b, h, i: (b, h, 0))],
        out_specs=pl.BlockSpec((1, tq, LANES), lambda b, h, i: (b, i, h)),
        out_shape=jax.ShapeDtypeStruct((nb, tq_total, SB_WIDTH), BF16),
        scratch_shapes=[pltpu.VMEM((2, 1, tq), F32), pltpu.VMEM((LANES, tq), F32),
                        pltpu.VMEM((3, 2, max(tk, tkp), tq), F32)],
        compiler_params=pltpu.CompilerParams(dimension_semantics=("arbitrary",) * 3,
                                             vmem_limit_bytes=VMEM_LIMIT),
        name="sb_attn",
    )(q, kp, vtp, km, vtm)


def _diff_decode_kernel(lq1_ref, lk1_ref, lq2_ref, lk2_ref, subg_ref, q_ref, ck_ref, cv_ref, kn_ref, vn_ref,
                        o_ref, m_ref, l_ref, acc_ref, *, chunk):
    kc = pl.program_id(1)
    nq = q_ref.shape[0]

    @pl.when(kc == 0)
    def _():
        m_ref[...] = jnp.full(m_ref.shape, NEG, F32)
        l_ref[...] = jnp.zeros(l_ref.shape, F32)
        acc_ref[...] = jnp.zeros(acc_ref.shape, F32)

    def update(h, k, v):
        n = k.shape[0]
        qs = _split_halves(q_ref[:, h * LANES:(h + 1) * LANES])
        v_ones = jnp.concatenate([v, jnp.ones((n, LANES), BF16)], axis=1)
        ps, alphas = [], []
        for a in range(2):
            s = lax.dot_general(qs[a], k, _NT, preferred_element_type=F32)
            m_prev = m_ref[2 * h + a]
            m_new = jnp.maximum(m_prev, jnp.max(s, axis=1, keepdims=True))
            alphas.append(jnp.exp2(m_prev - m_new))
            ps.append(jnp.exp2(s - m_new).astype(BF16))
            m_ref[2 * h + a] = m_new
        pv = jnp.dot(jnp.concatenate(ps, axis=0), v_ones, preferred_element_type=F32)
        for a in range(2):
            part = pv[a * nq:(a + 1) * nq]
            acc_ref[2 * h + a] = alphas[a] * acc_ref[2 * h + a] + part[:, :LANES]
            l_ref[2 * h + a] = alphas[a] * l_ref[2 * h + a] + part[:, LANES:]

    for h in range(N_HEADS):
        update(h, ck_ref[pl.ds(h, chunk, stride=N_HEADS), :].astype(BF16),
               cv_ref[pl.ds(h, chunk, stride=N_HEADS), :].astype(BF16))

    @pl.when(kc == pl.num_programs(1) - 1)
    def _():
        lam = (jnp.exp(jnp.sum(lq1_ref[...] * lk1_ref[...], axis=1, keepdims=True))
               - jnp.exp(jnp.sum(lq2_ref[...] * lk2_ref[...], axis=1, keepdims=True)) + LAM_INIT)
        for h in range(N_HEADS):
            cols = slice(h * LANES, (h + 1) * LANES)
            update(h, kn_ref[:, cols], vn_ref[:, cols])
            o = acc_ref[2 * h] / l_ref[2 * h] - lam * (acc_ref[2 * h + 1] / l_ref[2 * h + 1])
            o_ref[:, cols] = (_rms(o) * subg_ref[...] * (1.0 - LAM_INIT)).astype(BF16)


def _diff_decode(lams, subg_row, q, ck, cv, kn, vn, *, chunk):
    nb, nq, _ = q.shape
    n_chunks = ck.shape[1] // (chunk * N_HEADS)
    small = _const_spec((1, HEAD_DIM))
    tok = pl.BlockSpec((None, nq, DIFF_WIDTH), lambda b, c: (b, 0, 0))
    cache = pl.BlockSpec((None, chunk * N_HEADS, LANES), lambda b, c: (b, c, 0))
    return pl.pallas_call(
        functools.partial(_diff_decode_kernel, chunk=chunk),
        grid=(nb, n_chunks),
        in_specs=[small, small, small, small, _const_spec((1, LANES)), tok, cache, cache, tok, tok],
        out_specs=tok,
        out_shape=jax.ShapeDtypeStruct((nb, nq, DIFF_WIDTH), BF16),
        scratch_shapes=[pltpu.VMEM((2 * N_HEADS, nq, 1), F32), pltpu.VMEM((2 * N_HEADS, nq, LANES), F32),
                        pltpu.VMEM((2 * N_HEADS, nq, LANES), F32)],
        compiler_params=pltpu.CompilerParams(dimension_semantics=("arbitrary",) * 2,
                                             vmem_limit_bytes=VMEM_LIMIT),
        name="diff_decode",
    )(*lams, subg_row, q, ck, cv, kn, vn)


SB_BLOCK = 256


def _sb_decode_kernel(q_ref, ckt_ref, cvt_ref, knt_ref, vnt_ref, o_ref):
    nq = q_ref.shape[0]
    past = ckt_ref.shape[1]
    n_blk = past // SB_BLOCK
    def suffix_cols(n):
        r = lax.broadcasted_iota(jnp.int32, (n, n + LANES), 0)
        c = lax.broadcasted_iota(jnp.int32, (n, n + LANES), 1)
        return jnp.where((r > c) | (c >= n), 1.0, 0.0).astype(BF16)
    sfx_blk = suffix_cols(SB_BLOCK)
    sfx_new = suffix_cols(LANES)
    row = lax.broadcasted_iota(jnp.int32, (nq, LANES), 0)
    lane = lax.broadcasted_iota(jnp.int32, (nq, LANES), 1)
    new_mask = lane < row

    def log1m_of(u):
        nu = -u
        return jnp.minimum(nu, 0.0) - jnp.log(1.0 + jnp.exp2(jnp.minimum(u, nu))) * LOG2E

    for j in range(ckt_ref.shape[0] // LANES):
        rows = slice(j * LANES, (j + 1) * LANES)
        kt = ckt_ref[rows, :].astype(BF16)
        vt = cvt_ref[rows, :].astype(BF16)
        qs = _split_halves(q_ref[:, rows])
        ws, wns = [], []
        for a in range(2):
            un = jnp.dot(qs[a], knt_ref[rows, :], preferred_element_type=F32)
            ln = jnp.where(new_mask, log1m_of(un), 0.0)
            sn = jnp.dot(ln.astype(BF16), sfx_new, preferred_element_type=F32)
            wns.append(jnp.exp2(jnp.where(new_mask, un + ln + sn[:, :LANES], NEG)).astype(BF16))
            carry = sn[:, LANES:]
            u = jnp.dot(qs[a], kt, preferred_element_type=F32)
            l1m = log1m_of(u)
            stacked = jnp.concatenate(
                [l1m[:, b * SB_BLOCK:(b + 1) * SB_BLOCK] for b in range(n_blk)], axis=0).astype(BF16)
            sfx = jnp.dot(stacked, sfx_blk, preferred_element_type=F32)
            ts = [None] * n_blk
            for b in reversed(range(n_blk)):
                cols = slice(b * SB_BLOCK, (b + 1) * SB_BLOCK)
                part = sfx[b * nq:(b + 1) * nq]
                c2 = jnp.concatenate([carry] * (SB_BLOCK // LANES), axis=1)
                ts[b] = u[:, cols] + l1m[:, cols] + part[:, :SB_BLOCK] + c2
                carry = carry + part[:, SB_BLOCK:]
            ws.append(jnp.exp2(jnp.concatenate(ts, axis=1)).astype(BF16))
        o = (lax.dot_general(jnp.concatenate(ws, axis=0), vt, _NT, preferred_element_type=F32)
             + lax.dot_general(jnp.concatenate(wns, axis=0), vnt_ref[rows, :], _NT,
                               preferred_element_type=F32))
        o_ref[:, rows] = jnp.where(lane < HEAD_DIM, o[:nq], o[nq:]).astype(BF16)


def _sb_decode(q, ckt, cvt, knt, vnt):
    nb, nq, _ = q.shape
    assert nq <= LANES
    past = ckt.shape[2]
    width = 2 * LANES
    tok = pl.BlockSpec((None, nq, width), lambda b, g: (b, 0, g))
    cache = pl.BlockSpec((None, width, past), lambda b, g: (b, g, 0))
    new = pl.BlockSpec((None, width, LANES), lambda b, g: (b, g, 0))
    return pl.pallas_call(
        _sb_decode_kernel,
        grid=(nb, SB_WIDTH // width),
        in_specs=[tok, cache, cache, new, new],
        out_specs=tok,
        out_shape=jax.ShapeDtypeStruct((nb, nq, SB_WIDTH), BF16),
        compiler_params=pltpu.CompilerParams(dimension_semantics=("arbitrary",) * 2,
                                             vmem_limit_bytes=VMEM_LIMIT),
        name="sb_decode",
    )(q, ckt, cvt, knt, vnt)


def _out_kernel(x_ref, od_ref, os_ref, gmix_ref, wg_ref, wdo_ref, wso_ref, wo_ref, gffn_ref,
                w1_ref, w2_ref, y_ref):
    x = x_ref[...]
    h = (_rms(x) * gmix_ref[...]).astype(BF16)
    gate = jax.nn.sigmoid(jnp.dot(h, wg_ref[...], preferred_element_type=F32))
    a = jnp.dot(od_ref[...], wdo_ref[...], preferred_element_type=F32)
    b = jnp.dot(os_ref[...], wso_ref[...], preferred_element_type=F32)
    merged = (gate[:, :D_MODEL] * a + gate[:, D_MODEL:] * b).astype(BF16)
    x1 = x + jnp.dot(merged, wo_ref[...], preferred_element_type=F32)
    h2 = (_rms(x1) * gffn_ref[...]).astype(BF16)
    y = x1
    for c in range(D_FF // D_MODEL):
        sl = slice(c * D_MODEL, (c + 1) * D_MODEL)
        f = jnp.maximum(jnp.dot(h2, w1_ref[:, sl], preferred_element_type=F32), 0.0)
        y = y + jnp.dot((f * f).astype(BF16), w2_ref[sl, :], preferred_element_type=F32)
    y_ref[...] = y


def _out(x, od, osb, gmix, wg, wdo, wso, wo, gffn, w1, w2, *, tm):
    rows = x.shape[0]
    row = lambda w: pl.BlockSpec((tm, w), lambda i: (i, 0))
    return pl.pallas_call(
        _out_kernel,
        grid=(rows // tm,),
        in_specs=[row(D_MODEL), row(DIFF_WIDTH), row(SB_WIDTH), _const_spec((1, D_MODEL)),
                  _const_spec((D_MODEL, 2 * D_MODEL)), _const_spec((DIFF_WIDTH, D_MODEL)),
                  _const_spec((SB_WIDTH, D_MODEL)), _const_spec((D_MODEL, D_MODEL)),
                  _const_spec((1, D_MODEL)), _const_spec((D_MODEL, D_FF)), _const_spec((D_FF, D_MODEL))],
        out_specs=row(D_MODEL),
        out_shape=jax.ShapeDtypeStruct((rows, D_MODEL), F32),
        compiler_params=pltpu.CompilerParams(dimension_semantics=("arbitrary",),
                                             vmem_limit_bytes=VMEM_LIMIT),
        name="out",
    )(x, od, osb, gmix, wg, wdo, wso, wo, gffn, w1, w2)


def _rope_tables(pos):
    half = ROT_DIM // 2
    inv = ROPE_THETA ** (-jnp.arange(0, ROT_DIM, 2, dtype=F32) / ROT_DIM)
    ang = pos.astype(F32)[:, None] * inv[None, :]
    cos, sin = jnp.cos(ang), jnp.sin(ang)
    n = pos.shape[0]
    pad = jnp.zeros((n, HEAD_DIM - ROT_DIM), F32)
    z8 = jnp.zeros((n, half), F32)
    cos64 = jnp.concatenate([cos, cos, pad + 1.0], axis=1)
    sa64 = jnp.concatenate([-sin, z8, pad], axis=1)
    sb64 = jnp.concatenate([z8, sin, pad], axis=1)
    rep = lambda t: jnp.concatenate([t, t], axis=1)
    return rep(cos64), rep(sa64), rep(sb64)


def kernel(x_prompt, x_sample, cache_diff_k, cache_diff_v, cache_sb_k, cache_sb_v, meta_tokens,
           g_mix, w_in, q_norm_g, k_norm_g, lam_q1, lam_k1, lam_q2, lam_k2, sub_g,
           w_diff_out, w_sb_out, w_out, g_ffn, w_ff1, w_ff2):
    nb, seq, _ = x_prompt.shape
    db, dseq, _ = x_sample.shape
    past = cache_diff_k.shape[2]
    lyr = 0

    w_in_b = w_in[lyr].astype(BF16)
    w_qkv, w_gate = w_in_b[:, :QKV_COLS], w_in_b[:, QKV_COLS:]
    wdo, wso, wo = (w_diff_out[lyr].astype(BF16), w_sb_out[lyr].astype(BF16), w_out[lyr].astype(BF16))
    w1, w2 = w_ff1[lyr].astype(BF16), w_ff2[lyr].astype(BF16)
    gmix = g_mix[lyr].reshape(1, D_MODEL)
    gffn = g_ffn[lyr].reshape(1, D_MODEL)
    qg = jnp.tile(q_norm_g[lyr], COL_BLOCK // HEAD_DIM).reshape(1, COL_BLOCK)
    kg = jnp.tile(k_norm_g[lyr], COL_BLOCK // HEAD_DIM).reshape(1, COL_BLOCK)
    subg_col = sub_g[lyr].reshape(LANES, 1)
    lams = [t[lyr].reshape(1, HEAD_DIM) for t in (lam_q1, lam_k1, lam_q2, lam_k2)]
    grp = jnp.arange(MXU_TILE, dtype=jnp.int32) // HEAD_DIM
    gmat = jnp.where(grp[:, None] == grp[None, :], 1.0 / HEAD_DIM, 0.0).astype(BF16)

    main_pos = N_META + jnp.arange(seq, dtype=jnp.int32)
    small_pos = jnp.concatenate([jnp.arange(N_META, dtype=jnp.int32),
                                 jnp.tile(past + jnp.arange(dseq, dtype=jnp.int32), db)])
    x_main = x_prompt.reshape(nb * seq, D_MODEL)
    x_small = jnp.concatenate([meta_tokens.astype(F32), x_sample.reshape(db * dseq, D_MODEL)], axis=0)
    pm = _proj(x_main, gmix, w_qkv, qg, kg, gmat, *_rope_tables(main_pos), tm=512, prompt_layout=True,
               lead=N_META)
    ps = _proj(x_small, gmix, w_qkv, qg, kg, gmat, *_rope_tables(small_pos), tm=x_small.shape[0],
               prompt_layout=False)
    qd_m, kd_hm, kdb_m, vd_hm, vdt_m, qs_m, kst_m, ksb_m, vstf_m, vst_m = pm
    qd_s, kd_s, kdb_s, vd_s, vdb_s, qs_s, ks_s, ksb_s, vs_s, vsb_s = ps

    def bt(a, n, t):
        return a.reshape(n, t, a.shape[-1])

    def meta_keys(a):
        return jnp.pad(a[:N_META], ((0, LANES - N_META), (0, 0)))[None]

    def meta_vals_t(a):
        return jnp.swapaxes(meta_keys(a), 1, 2)

    od_p = _diff_attn(lams, subg_col, bt(qd_m, nb, seq), meta_keys(kdb_s), meta_vals_t(vdb_s),
                      bt(kdb_m, nb, seq), vdt_m,
                      tq=1024, tk=512, pref_valid=N_META)
    os_p = _sb_attn(bt(qs_m, nb, seq), meta_keys(ksb_s), meta_vals_t(vsb_s),
                    bt(ksb_m, nb, seq), vst_m,
                    tq=1024, tk=256, pref_valid=N_META)

    smp = lambda a: bt(a[N_META:], db, dseq)
    cdk = cache_diff_k[lyr].reshape(db, past * N_HEADS, LANES)
    cdv = cache_diff_v[lyr].reshape(db, past * N_HEADS, LANES)
    od_s = _diff_decode(lams, sub_g[lyr].reshape(1, LANES), smp(qd_s), cdk, cdv, smp(kdb_s), smp(vdb_s),
                        chunk=1024)
    sb_t = lambda c: jnp.transpose(c[lyr], (0, 2, 3, 1)).reshape(db, SB_WIDTH, past)
    new_t = lambda a: jnp.pad(jnp.swapaxes(smp(a), 1, 2), ((0, 0), (0, 0), (0, LANES - dseq)))
    os_s = _sb_decode(smp(qs_s), sb_t(cache_sb_k), sb_t(cache_sb_v), new_t(ksb_s), new_t(vsb_s))

    y_p = _out(x_main, od_p.reshape(nb * seq, DIFF_WIDTH), os_p.reshape(nb * seq, SB_WIDTH),
               gmix, w_gate, wdo, wso, wo, gffn, w1, w2, tm=512)
    y_s = _out(x_sample.reshape(db * dseq, D_MODEL), od_s.reshape(db * dseq, DIFF_WIDTH),
               os_s.reshape(db * dseq, SB_WIDTH), gmix, w_gate, wdo, wso, wo, gffn, w1, w2, tm=db * dseq)

    def diff_prompt_cache(head_major, small):
        meta = small[:N_META].reshape(N_META * N_HEADS, 2 * HEAD_DIM)
        full = _fill_lead_tokens(head_major, meta, nb=nb)
        return full.reshape(1, nb, seq + N_META, N_HEADS, 2 * HEAD_DIM)

    def sb_prompt_cache(main_t, small):
        meta_t = jnp.broadcast_to(small[:N_META].T[None], (nb, SB_WIDTH, N_META))
        full = jnp.concatenate([meta_t, main_t], axis=2).reshape(nb, N_HEADS, HEAD_DIM, seq + N_META)
        return jnp.transpose(full, (0, 3, 1, 2))[None]

    def sample_cache(small, dim):
        return small[N_META:].reshape(1, db, dseq, N_HEADS, dim)

    return (y_p.reshape(nb, seq, D_MODEL), y_s.reshape(db, dseq, D_MODEL),
            diff_prompt_cache(kd_hm, kd_s), diff_prompt_cache(vd_hm, vd_s),
            sb_prompt_cache(kst_m, ks_s), sb_prompt_cache(vstf_m, vs_s),
            sample_cache(kd_s, 2 * HEAD_DIM), sample_cache(vd_s, 2 * HEAD_DIM),
            sample_cache(ks_s, HEAD_DIM), sample_cache(vs_s, HEAD_DIM))
```

```python
import functools
import math

import jax
import jax.numpy as jnp
from jax import lax
from jax.experimental import pallas as pl
from jax.experimental.pallas import tpu as pltpu

F32 = jnp.float32
BF16 = jnp.bfloat16

D_MODEL = 1024
N_META = 16
CHUNK = 64
N_HEADS = 8
HEAD_DIM = 64
DIFF_WIDTH = N_HEADS * 2 * HEAD_DIM
SB_WIDTH = N_HEADS * HEAD_DIM
QKV_COLS = 3 * DIFF_WIDTH + 3 * SB_WIDTH
D_FF = 4 * D_MODEL
ROT_DIM = HEAD_DIM // 4
ROPE_THETA = 500000.0
EPS = 1e-6
NEG = -1e30
LOG2E = math.log2(math.e)
Q_SCALE = HEAD_DIM ** -0.5 * LOG2E
ONES_ROWS = 16
LAM_INIT = 0.8 - 0.6 * math.exp(-0.3 * 0)

LANES = 128
MXU_TILE = 256
COL_BLOCK = 512
VMEM_LIMIT = 56 * 1024 * 1024

_NT = (((1,), (1,)), ((), ()))


def _rms(x):
    return x * lax.rsqrt(jnp.mean(x * x, axis=-1, keepdims=True) + EPS)


def _const_spec(shape):
    return pl.BlockSpec(shape, lambda *_: (0,) * len(shape), pipeline_mode=pl.Buffered(1))


def _proj_kernel(x_ref, gmix_ref, w_ref, qg_ref, kg_ref, gmat_ref, cos_ref, sa_ref, sb_ref,
                 qd_ref, kd_ref, kdb_ref, vd_ref, vdb_ref, qs_ref, ks_ref, ksb_ref, vs_ref, vsb_ref,
                 *, prompt_layout):
    tm = x_ref.shape[0]
    heads_per_block = COL_BLOCK // LANES

    def store_diff(ref, j, y):
        if not prompt_layout:
            ref[:, j * COL_BLOCK:(j + 1) * COL_BLOCK] = y
            return
        for hh in range(heads_per_block):
            ref[pl.ds(j * heads_per_block + hh, tm, stride=N_HEADS), :] = y[:, hh * LANES:(hh + 1) * LANES]

    h = (_rms(x_ref[...]) * gmix_ref[...]).astype(BF16)
    cos = cos_ref[...]
    sa = sa_ref[...]
    sb = sb_ref[...]

    def col(j):
        return jnp.dot(h, w_ref[:, j * COL_BLOCK:(j + 1) * COL_BLOCK], preferred_element_type=F32)

    def normed_rot(y, g):
        sq = (y * y).astype(BF16)
        msq = jnp.concatenate(
            [jnp.dot(sq[:, c * MXU_TILE:(c + 1) * MXU_TILE], gmat_ref[...], preferred_element_type=F32)
             for c in range(COL_BLOCK // MXU_TILE)], axis=1)
        yn = y * lax.rsqrt(msq + EPS) * g
        parts = []
        for c in range(COL_BLOCK // LANES):
            t = yn[:, c * LANES:(c + 1) * LANES]
            parts.append(t * cos + pltpu.roll(t, LANES - ROT_DIM // 2, 1) * sa
                         + pltpu.roll(t, ROT_DIM // 2, 1) * sb)
        return jnp.concatenate(parts, axis=1)

    for j in range(2):
        sl = slice(j * COL_BLOCK, (j + 1) * COL_BLOCK)
        q = normed_rot(col(j), qg_ref[...])
        qd_ref[:, sl] = (q * Q_SCALE).astype(BF16)
        k = normed_rot(col(2 + j), kg_ref[...])
        store_diff(kd_ref, j, k)
        kdb_ref[:, sl] = k.astype(BF16)
        v = col(4 + j)
        store_diff(vd_ref, j, v)
        if prompt_layout:
            vdb_ref[sl, :] = v.T.astype(BF16)
        else:
            vdb_ref[:, sl] = v.astype(BF16)
    qs_ref[...] = (col(6) * Q_SCALE).astype(BF16)
    k = col(7)
    ksb_ref[...] = k.astype(BF16)
    v = col(8)
    if prompt_layout:
        ks_ref[...] = k.T
        vt = v.T
        vs_ref[...] = vt
        vsb_ref[...] = vt.astype(BF16)
    else:
        ks_ref[...] = k
        vs_ref[...] = v
        vsb_ref[...] = v.astype(BF16)


def _proj(x, gmix, w_qkv, qg, kg, gmat, cos, sa, sb, *, tm, prompt_layout, lead=0):
    rows = x.shape[0]
    seq = cos.shape[0]
    n_pos_tiles = seq // tm
    row = lambda w: pl.BlockSpec((tm, w), lambda i: (i, 0))
    tab = pl.BlockSpec((tm, LANES), lambda i: (i % n_pos_tiles, 0))
    wide = lambda dt: jax.ShapeDtypeStruct((rows, DIFF_WIDTH), dt)
    narrow = lambda dt: jax.ShapeDtypeStruct((rows, SB_WIDTH), dt)
    if prompt_layout:
        nb = rows // seq
        t_spec = lambda w: pl.BlockSpec((None, w, tm), lambda i: (i // n_pos_tiles, 0, i % n_pos_tiles))
        t_shape = lambda w, dt: jax.ShapeDtypeStruct((nb, w, seq), dt)
        hm_spec = pl.BlockSpec(
            (pl.Element(tm * N_HEADS), pl.Element(LANES)),
            lambda i: (((i // n_pos_tiles) * (seq + lead) + lead + (i % n_pos_tiles) * tm) * N_HEADS, 0))
        hm_shape = jax.ShapeDtypeStruct((nb * (seq + lead) * N_HEADS, LANES), F32)
        out_specs = [row(DIFF_WIDTH), hm_spec, row(DIFF_WIDTH), hm_spec, t_spec(DIFF_WIDTH),
                     row(SB_WIDTH), t_spec(SB_WIDTH), row(SB_WIDTH), t_spec(SB_WIDTH), t_spec(SB_WIDTH)]
        out_shape = [wide(BF16), hm_shape, wide(BF16), hm_shape, t_shape(DIFF_WIDTH, BF16),
                     narrow(BF16), t_shape(SB_WIDTH, F32), narrow(BF16), t_shape(SB_WIDTH, F32),
                     t_shape(SB_WIDTH, BF16)]
    else:
        out_specs = [row(DIFF_WIDTH)] * 5 + [row(SB_WIDTH)] * 5
        out_shape = [wide(BF16), wide(F32), wide(BF16), wide(F32), wide(BF16),
                     narrow(BF16), narrow(F32), narrow(BF16), narrow(F32), narrow(BF16)]
    return pl.pallas_call(
        functools.partial(_proj_kernel, prompt_layout=prompt_layout),
        grid=(rows // tm,),
        in_specs=[row(D_MODEL), _const_spec((1, D_MODEL)), _const_spec((D_MODEL, QKV_COLS)),
                  _const_spec((1, COL_BLOCK)), _const_spec((1, COL_BLOCK)),
                  _const_spec((MXU_TILE, MXU_TILE)), tab, tab, tab],
        out_specs=out_specs,
        out_shape=out_shape,
        compiler_params=pltpu.CompilerParams(dimension_semantics=("arbitrary",),
                                             vmem_limit_bytes=VMEM_LIMIT),
        name="proj",
    )(x, gmix, w_qkv, qg, kg, gmat, cos, sa, sb)


def _fill_lead_kernel(lead_ref, big_ref, out_ref):
    del big_ref
    out_ref[...] = lead_ref[...]


def _fill_lead_tokens(big, lead_rows, *, nb):
    n = lead_rows.shape[0]
    per_batch = big.shape[0] // nb
    return pl.pallas_call(
        _fill_lead_kernel,
        grid=(nb,),
        in_specs=[_const_spec((n, LANES)), pl.BlockSpec(memory_space=pl.ANY)],
        out_specs=pl.BlockSpec((pl.Element(n), pl.Element(LANES)), lambda b: (b * per_batch, 0)),
        out_shape=jax.ShapeDtypeStruct(big.shape, big.dtype),
        input_output_aliases={1: 0},
        compiler_params=pltpu.CompilerParams(dimension_semantics=("arbitrary",)),
        name="fill_lead_tokens",
    )(lead_rows, big)


def _split_halves(q):
    lane = lax.broadcasted_iota(jnp.int32, q.shape, 1)
    zero = jnp.zeros_like(q)
    return jnp.where(lane < HEAD_DIM, q, zero), jnp.where(lane >= HEAD_DIM, q, zero)


def _diff_kernel(lq1_ref, lk1_ref, lq2_ref, lk2_ref, subg_ref, q_ref, kp_ref, vp_ref, km_ref, vm_ref,
                 o_ref, m_ref, acc_ref, s_ref, *, tq, tk, tkp, pref_valid, n_diag):
    i = pl.program_id(2)
    n_full = i * n_diag
    qs = _split_halves(q_ref[0])
    m_ref[...] = jnp.full(m_ref.shape, NEG, F32)
    acc_ref[...] = jnp.zeros(acc_ref.shape, F32)

    def step(nxt, cur):
        if cur is not None:
            slot, vtblk, mask, qlo = cur
            n = vtblk.shape[1]
            vt_ones = jnp.concatenate([vtblk, jnp.ones((ONES_ROWS, n), BF16)], axis=0)
        for a in range(2):
            if cur is not None:
                s = s_ref[slot, a, :n, qlo:]
                if mask is not None:
                    s = jnp.where(mask[:, qlo:], s, NEG)
                m_prev = m_ref[a, :, qlo:]
                m_new = jnp.maximum(m_prev, jnp.max(s, axis=0, keepdims=True))
                alpha = jnp.exp2(m_prev - m_new)
                p = jnp.exp2(s - m_new)
            if nxt is not None:
                kblk, nslot, nqlo = nxt
                s_ref[nslot, a, :kblk.shape[0], nqlo:] = lax.dot_general(
                    kblk, qs[a][nqlo:], _NT, preferred_element_type=F32)
            if cur is not None:
                acc_ref[a, :, qlo:] = (alpha * acc_ref[a, :, qlo:]
                                       + jnp.dot(vt_ones, p.astype(BF16), preferred_element_type=F32))
                m_ref[a, :, qlo:] = m_new

    def main_k(g):
        return km_ref[0, pl.ds(pl.multiple_of(g * tk, tk), tk), :]

    def main_vt(g):
        return vm_ref[0, :, pl.ds(pl.multiple_of(g * tk, tk), tk)]

    step((kp_ref[0], 1, 0), None)
    step((main_k(0), 0, 0), (1, vp_ref[0], lax.broadcasted_iota(jnp.int32, (tkp, tq), 0) < pref_valid, 0))

    def full_body(t, carry):
        step((main_k(2 * t + 1), 1, 0), (0, main_vt(2 * t), None, 0))
        step((main_k(2 * t + 2), 0, 0), (1, main_vt(2 * t + 1), None, 0))
        return carry
    lax.fori_loop(0, i * (n_diag // 2), full_body, 0)

    key_idx = lax.broadcasted_iota(jnp.int32, (tk, tq), 0)
    q_chunk = lax.broadcasted_iota(jnp.int32, (tk, tq), 1) // CHUNK
    for d in range(n_diag):
        g = n_full + d
        nxt = (main_k(g + 1), (d + 1) % 2, (d + 1) * tk) if d + 1 < n_diag else None
        step(nxt, (d % 2, main_vt(g), (key_idx + d * tk) // CHUNK <= q_chunk, d * tk))

    lam = (jnp.exp(jnp.sum(lq1_ref[...] * lk1_ref[...], axis=1, keepdims=True))
           - jnp.exp(jnp.sum(lq2_ref[...] * lk2_ref[...], axis=1, keepdims=True)) + LAM_INIT)
    o = (acc_ref[0, :LANES] / acc_ref[0, LANES:LANES + 1]
         - lam * (acc_ref[1, :LANES] / acc_ref[1, LANES:LANES + 1]))
    o = o * lax.rsqrt(jnp.mean(o * o, axis=0, keepdims=True) + EPS) * subg_ref[...] * (1.0 - LAM_INIT)
    o_ref[0] = o.T.astype(BF16)


def _diff_attn(lams, subg_col, q, kp, vtp, km, vtm, *, tq, tk, pref_valid):
    nb, tq_total, _ = q.shape
    tkp = kp.shape[1]
    t_main = km.shape[1]
    n_diag = tq // tk
    assert n_diag % 2 == 0, "main blocks are consumed in pairs"
    kernel = functools.partial(_diff_kernel, tq=tq, tk=tk, tkp=tkp, pref_valid=pref_valid, n_diag=n_diag)
    small = _const_spec((1, HEAD_DIM))
    return pl.pallas_call(
        kernel,
        grid=(nb, N_HEADS, tq_total // tq),
        in_specs=[small, small, small, small, _const_spec((LANES, 1)),
                  pl.BlockSpec((1, tq, LANES), lambda b, h, i: (b, i, h)),
                  pl.BlockSpec((1, tkp, LANES), lambda b, h, i: (0, 0, h)),
                  pl.BlockSpec((1, LANES, tkp), lambda b, h, i: (0, h, 0)),
                  pl.BlockSpec((1, t_main, LANES), lambda b, h, i: (b, 0, h)),
                  pl.BlockSpec((1, LANES, t_main), lambda b, h, i: (b, h, 0))],
        out_specs=pl.BlockSpec((1, tq, LANES), lambda b, h, i: (b, i, h)),
        out_shape=jax.ShapeDtypeStruct((nb, tq_total, DIFF_WIDTH), BF16),
        scratch_shapes=[pltpu.VMEM((2, 1, tq), F32), pltpu.VMEM((2, LANES + ONES_ROWS, tq), F32),
                        pltpu.VMEM((2, 2, max(tk, tkp), tq), F32)],
        compiler_params=pltpu.CompilerParams(dimension_semantics=("arbitrary",) * 3,
                                             vmem_limit_bytes=VMEM_LIMIT),
        name="diff_attn",
    )(*lams, subg_col, q, kp, vtp, km, vtm)


def _suffix_matrix(n):
    r = lax.broadcasted_iota(jnp.int32, (n + ONES_ROWS, n), 0)
    c = lax.broadcasted_iota(jnp.int32, (n + ONES_ROWS, n), 1)
    return jnp.where((c >= r) | (r == n), 1.0, 0.0).astype(BF16)


def _sb_kernel(q_ref, kp_ref, vp_ref, km_ref, vm_ref, o_ref, c_ref, acc_ref, z_ref,
               *, tq, tk, tkp, pref_valid, n_diag):
    i = pl.program_id(2)
    n_full = i * n_diag
    qs = _split_halves(q_ref[0])
    c_ref[...] = jnp.zeros(c_ref.shape, F32)
    acc_ref[...] = jnp.zeros(acc_ref.shape, F32)

    def step(nxt, cur):
        if cur is not None:
            slot, vtblk, mask, sfx_mat, qlo = cur
            n = vtblk.shape[1]
            if mask is not None:
                mask = mask[:, qlo:]
        for a in range(2):
            if nxt is not None:
                kblk, nslot, nqlo = nxt
                z_ref[nslot, a, :kblk.shape[0], nqlo:] = lax.dot_general(
                    kblk, qs[a][nqlo:], _NT, preferred_element_type=F32)
            if cur is None:
                continue
            rows = slice(a * HEAD_DIM, (a + 1) * HEAD_DIM)
            u = z_ref[slot, a, :n, qlo:]
            neg_part = jnp.minimum(u, 0.0)
            d = neg_part - u
            log1m = d - jnp.log(1.0 + jnp.exp2(neg_part + d)) * LOG2E
            if mask is not None:
                log1m = jnp.where(mask, log1m, 0.0)
            sfx = jnp.dot(sfx_mat, log1m.astype(BF16), preferred_element_type=F32)
            t = u + sfx[:n]
            if mask is not None:
                t = jnp.where(mask, t, NEG)
            pv = jnp.dot(vtblk[rows, :], jnp.exp2(t).astype(BF16), preferred_element_type=F32)
            acc_ref[rows, qlo:] += pv * jnp.exp2(c_ref[a, :, qlo:])
            c_ref[a, :, qlo:] += sfx[n:n + 1]

    def main_k(g):
        return km_ref[0, pl.ds(pl.multiple_of(g * tk, tk), tk), :]

    def main_vt(g):
        return vm_ref[0, :, pl.ds(pl.multiple_of(g * tk, tk), tk)]

    sfx_main = _suffix_matrix(tk)
    sfx_pref = sfx_main if tkp == tk else _suffix_matrix(tkp)
    key = lax.broadcasted_iota(jnp.int32, (tk, tq), 0)
    qry = lax.broadcasted_iota(jnp.int32, (tk, tq), 1)

    step((kp_ref[0], 2, 0), None)
    step((main_k(n_full + n_diag - 1), 0, (n_diag - 1) * tk), None)
    for d in reversed(range(n_diag)):
        s = (n_diag - 1 - d) % 2
        if d > 0:
            nxt = (main_k(n_full + d - 1), 1 - s, (d - 1) * tk)
        else:
            nxt = (main_k(jnp.maximum(n_full - 1, 0)), 1 - s, 0)
        step(nxt, (s, main_vt(n_full + d), key + d * tk < qry, sfx_main, d * tk))

    def full_body(t, carry):
        g = n_full - 1 - 2 * t
        step((main_k(g - 1), 1, 0), (0, main_vt(g), None, sfx_main, 0))
        step((main_k(jnp.maximum(g - 2, 0)), 0, 0), (1, main_vt(g - 1), None, sfx_main, 0))
        return carry
    lax.fori_loop(0, i * (n_diag // 2), full_body, 0)

    step(None, (2, vp_ref[0], lax.broadcasted_iota(jnp.int32, (tkp, tq), 0) < pref_valid, sfx_pref, 0))
    o_ref[0] = acc_ref[...].T.astype(BF16)


def _sb_attn(q, kp, vtp, km, vtm, *, tq, tk, pref_valid):
    nb, tq_total, _ = q.shape
    tkp = kp.shape[1]
    t_main = km.shape[1]
    n_diag = tq // tk
    assert n_diag % 2 == 0, "main blocks are consumed in pairs"
    kernel = functools.partial(_sb_kernel, tq=tq, tk=tk, tkp=tkp, pref_valid=pref_valid, n_diag=n_diag)
    return pl.pallas_call(
        kernel,
        grid=(nb, SB_WIDTH // LANES, tq_total // tq),
        in_specs=[pl.BlockSpec((1, tq, LANES), lambda b, h, i: (b, i, h)),
                  pl.BlockSpec((1, tkp, LANES), lambda b, h, i: (0, 0, h)),
                  pl.BlockSpec((1, LANES, tkp), lambda b, h, i: (0, h, 0)),
                  pl.BlockSpec((1, t_main, LANES), lambda b, h, i: (b, 0, h)),
                  pl.BlockSpec((1, LANES, t_main), lambda b, h, i: (b, h, 0))],
        out_specs=pl.BlockSpec((1, tq, LANES), lambda b, h, i: (b, i, h)),
        out_shape=jax.ShapeDtypeStruct((nb, tq_total, SB_WIDTH), BF16),
        scratch_shapes=[pltpu.VMEM((2, 1, tq), F32), pltpu.VMEM((LANES, tq), F32),
                        pltpu.VMEM((3, 2, max(tk, tkp), tq), F32)],
        compiler_params=pltpu.CompilerParams(dimension_semantics=("arbitrary",) * 3,
                                             vmem_limit_bytes=VMEM_LIMIT),
        name="sb_attn",
    )(q, kp, vtp, km, vtm)


def _diff_decode_kernel(lq1_ref, lk1_ref, lq2_ref, lk2_ref, subg_ref, q_ref, ck_ref, cv_ref, kn_ref, vn_ref,
                        o_ref, m_ref, l_ref, acc_ref, *, chunk):
    kc = pl.program_id(1)
    nq = q_ref.shape[0]

    @pl.when(kc == 0)
    def _():
        m_ref[...] = jnp.full(m_ref.shape, NEG, F32)
        l_ref[...] = jnp.zeros(l_ref.shape, F32)
        acc_ref[...] = jnp.zeros(acc_ref.shape, F32)

    def update(h, k, v):
        n = k.shape[0]
        qs = _split_halves(q_ref[:, h * LANES:(h + 1) * LANES])
        v_ones = jnp.concatenate([v, jnp.ones((n, LANES), BF16)], axis=1)
        ps, alphas = [], []
        for a in range(2):
            s = lax.dot_general(qs[a], k, _NT, preferred_element_type=F32)
            m_prev = m_ref[2 * h + a]
            m_new = jnp.maximum(m_prev, jnp.max(s, axis=1, keepdims=True))
            alphas.append(jnp.exp2(m_prev - m_new))
            ps.append(jnp.exp2(s - m_new).astype(BF16))
            m_ref[2 * h + a] = m_new
        pv = jnp.dot(jnp.concatenate(ps, axis=0), v_ones, preferred_element_type=F32)
        for a in range(2):
            part = pv[a * nq:(a + 1) * nq]
            acc_ref[2 * h + a] = alphas[a] * acc_ref[2 * h + a] + part[:, :LANES]
            l_ref[2 * h + a] = alphas[a] * l_ref[2 * h + a] + part[:, LANES:]

    for h in range(N_HEADS):
        update(h, ck_ref[pl.ds(h, chunk, stride=N_HEADS), :].astype(BF16),
               cv_ref[pl.ds(h, chunk, stride=N_HEADS), :].astype(BF16))

    @pl.when(kc == pl.num_programs(1) - 1)
    def _():
        lam = (jnp.exp(jnp.sum(lq1_ref[...] * lk1_ref[...], axis=1, keepdims=True))
               - jnp.exp(jnp.sum(lq2_ref[...] * lk2_ref[...], axis=1, keepdims=True)) + LAM_INIT)
        for h in range(N_HEADS):
            cols = slice(h * LANES, (h + 1) * LANES)
            update(h, kn_ref[:, cols], vn_ref[:, cols])
            o = acc_ref[2 * h] / l_ref[2 * h] - lam * (acc_ref[2 * h + 1] / l_ref[2 * h + 1])
            o_ref[:, cols] = (_rms(o) * subg_ref[...] * (1.0 - LAM_INIT)).astype(BF16)


def _diff_decode(lams, subg_row, q, ck, cv, kn, vn, *, chunk):
    nb, nq, _ = q.shape
    n_chunks = ck.shape[1] // (chunk * N_HEADS)
    small = _const_spec((1, HEAD_DIM))
    tok = pl.BlockSpec((None, nq, DIFF_WIDTH), lambda b, c: (b, 0, 0))
    cache = pl.BlockSpec((None, chunk * N_HEADS, LANES), lambda b, c: (b, c, 0))
    return pl.pallas_call(
        functools.partial(_diff_decode_kernel, chunk=chunk),
        grid=(nb, n_chunks),
        in_specs=[small, small, small, small, _const_spec((1, LANES)), tok, cache, cache, tok, tok],
        out_specs=tok,
        out_shape=jax.ShapeDtypeStruct((nb, nq, DIFF_WIDTH), BF16),
        scratch_shapes=[pltpu.VMEM((2 * N_HEADS, nq, 1), F32), pltpu.VMEM((2 * N_HEADS, nq, LANES), F32),
                        pltpu.VMEM((2 * N_HEADS, nq, LANES), F32)],
        compiler_params=pltpu.CompilerParams(dimension_semantics=("arbitrary",) * 2,
                                             vmem_limit_bytes=VMEM_LIMIT),
        name="diff_decode",
    )(*lams, subg_row, q, ck, cv, kn, vn)


SB_BLOCK = 256


def _sb_decode_kernel(q_ref, ckt_ref, cvt_ref, knt_ref, vnt_ref, o_ref):
    nq = q_ref.shape[0]
    past = ckt_ref.shape[1]
    n_blk = past // SB_BLOCK
    def suffix_cols(n):
        r = lax.broadcasted_iota(jnp.int32, (n, n + LANES), 0)
        c = lax.broadcasted_iota(jnp.int32, (n, n + LANES), 1)
        return jnp.where((r > c) | (c >= n), 1.0, 0.0).astype(BF16)
    sfx_blk = suffix_cols(SB_BLOCK)
    sfx_new = suffix_cols(LANES)
    row = lax.broadcasted_iota(jnp.int32, (nq, LANES), 0)
    lane = lax.broadcasted_iota(jnp.int32, (nq, LANES), 1)
    new_mask = lane < row

    def log1m_of(u):
        nu = -u
        return jnp.minimum(nu, 0.0) - jnp.log(1.0 + jnp.exp2(jnp.minimum(u, nu))) * LOG2E

    for j in range(ckt_ref.shape[0] // LANES):
        rows = slice(j * LANES, (j + 1) * LANES)
        kt = ckt_ref[rows, :].astype(BF16)
        vt = cvt_ref[rows, :].astype(BF16)
        qs = _split_halves(q_ref[:, rows])
        ws, wns = [], []
        for a in range(2):
            un = jnp.dot(qs[a], knt_ref[rows, :], preferred_element_type=F32)
            ln = jnp.where(new_mask, log1m_of(un), 0.0)
            sn = jnp.dot(ln.astype(BF16), sfx_new, preferred_element_type=F32)
            wns.append(jnp.exp2(jnp.where(new_mask, un + ln + sn[:, :LANES], NEG)).astype(BF16))
            carry = sn[:, LANES:]
            u = jnp.dot(qs[a], kt, preferred_element_type=F32)
            l1m = log1m_of(u)
            stacked = jnp.concatenate(
                [l1m[:, b * SB_BLOCK:(b + 1) * SB_BLOCK] for b in range(n_blk)], axis=0).astype(BF16)
            sfx = jnp.dot(stacked, sfx_blk, preferred_element_type=F32)
            ts = [None] * n_blk
            for b in reversed(range(n_blk)):
                cols = slice(b * SB_BLOCK, (b + 1) * SB_BLOCK)
                part = sfx[b * nq:(b + 1) * nq]
                c2 = jnp.concatenate([carry] * (SB_BLOCK // LANES), axis=1)
                ts[b] = u[:, cols] + l1m[:, cols] + part[:, :SB_BLOCK] + c2
                carry = carry + part[:, SB_BLOCK:]
            ws.append(jnp.exp2(jnp.concatenate(ts, axis=1)).astype(BF16))
        o = (lax.dot_general(jnp.concatenate(ws, axis=0), vt, _NT, preferred_element_type=F32)
             + lax.dot_general(jnp.concatenate(wns, axis=0), vnt_ref[rows, :], _NT,
                               preferred_element_type=F32))
        o_ref[:, rows] = jnp.where(lane < HEAD_DIM, o[:nq], o[nq:]).astype(BF16)


def _sb_decode(q, ckt, cvt, knt, vnt):
    nb, nq, _ = q.shape
    assert nq <= LANES
    past = ckt.shape[2]
    width = 2 * LANES
    tok = pl.BlockSpec((None, nq, width), lambda b, g: (b, 0, g))
    cache = pl.BlockSpec((None, width, past), lambda b, g: (b, g, 0))
    new = pl.BlockSpec((None, width, LANES), lambda b, g: (b, g, 0))
    return pl.pallas_call(
        _sb_decode_kernel,
        grid=(nb, SB_WIDTH // width),
        in_specs=[tok, cache, cache, new, new],
        out_specs=tok,
        out_shape=jax.ShapeDtypeStruct((nb, nq, SB_WIDTH), BF16),
        compiler_params=pltpu.CompilerParams(dimension_semantics=("arbitrary",) * 2,
                                             vmem_limit_bytes=VMEM_LIMIT),
        name="sb_decode",
    )(q, ckt, cvt, knt, vnt)


def _out_kernel(x_ref, od_ref, os_ref, gmix_ref, wg_ref, wdo_ref, wso_ref, wo_ref, gffn_ref,
                w1_ref, w2_ref, y_ref):
    x = x_ref[...]
    h = (_rms(x) * gmix_ref[...]).astype(BF16)
    gate = jax.nn.sigmoid(jnp.dot(h, wg_ref[...], preferred_element_type=F32))
    a = jnp.dot(od_ref[...], wdo_ref[...], preferred_element_type=F32)
    b = jnp.dot(os_ref[...], wso_ref[...], preferred_element_type=F32)
    merged = (gate[:, :D_MODEL] * a + gate[:, D_MODEL:] * b).astype(BF16)
    x1 = x + jnp.dot(merged, wo_ref[...], preferred_element_type=F32)
    h2 = (_rms(x1) * gffn_ref[...]).astype(BF16)
    y = x1
    for c in range(D_FF // D_MODEL):
        sl = slice(c * D_MODEL, (c + 1) * D_MODEL)
        f = jnp.maximum(jnp.dot(h2, w1_ref[:, sl], preferred_element_type=F32), 0.0)
        y = y + jnp.dot((f * f).astype(BF16), w2_ref[sl, :], preferred_element_type=F32)
    y_ref[...] = y


def _out(x, od, osb, gmix, wg, wdo, wso, wo, gffn, w1, w2, *, tm):
    rows = x.shape[0]
    row = lambda w: pl.BlockSpec((tm, w), lambda i: (i, 0))
    return pl.pallas_call(
        _out_kernel,
        grid=(rows // tm,),
        in_specs=[row(D_MODEL), row(DIFF_WIDTH), row(SB_WIDTH), _const_spec((1, D_MODEL)),
                  _const_spec((D_MODEL, 2 * D_MODEL)), _const_spec((DIFF_WIDTH, D_MODEL)),
                  _const_spec((SB_WIDTH, D_MODEL)), _const_spec((D_MODEL, D_MODEL)),
                  _const_spec((1, D_MODEL)), _const_spec((D_MODEL, D_FF)), _const_spec((D_FF, D_MODEL))],
        out_specs=row(D_MODEL),
        out_shape=jax.ShapeDtypeStruct((rows, D_MODEL), F32),
        compiler_params=pltpu.CompilerParams(dimension_semantics=("arbitrary",),
                                             vmem_limit_bytes=VMEM_LIMIT),
        name="out",
    )(x, od, osb, gmix, wg, wdo, wso, wo, gffn, w1, w2)


def _rope_tables(pos):
    half = ROT_DIM // 2
    inv = ROPE_THETA ** (-jnp.arange(0, ROT_DIM, 2, dtype=F32) / ROT_DIM)
    ang = pos.astype(F32)[:, None] * inv[None, :]
    cos, sin = jnp.cos(ang), jnp.sin(ang)
    n = pos.shape[0]
    pad = jnp.zeros((n, HEAD_DIM - ROT_DIM), F32)
    z8 = jnp.zeros((n, half), F32)
    cos64 = jnp.concatenate([cos, cos, pad + 1.0], axis=1)
    sa64 = jnp.concatenate([-sin, z8, pad], axis=1)
    sb64 = jnp.concatenate([z8, sin, pad], axis=1)
    rep = lambda t: jnp.concatenate([t, t], axis=1)
    return rep(cos64), rep(sa64), rep(sb64)


def kernel(x_prompt, x_sample, cache_diff_k, cache_diff_v, cache_sb_k, cache_sb_v, meta_tokens,
           g_mix, w_in, q_norm_g, k_norm_g, lam_q1, lam_k1, lam_q2, lam_k2, sub_g,
           w_diff_out, w_sb_out, w_out, g_ffn, w_ff1, w_ff2):
    nb, seq, _ = x_prompt.shape
    db, dseq, _ = x_sample.shape
    past = cache_diff_k.shape[2]
    lyr = 0

    w_in_b = w_in[lyr].astype(BF16)
    w_qkv, w_gate = w_in_b[:, :QKV_COLS], w_in_b[:, QKV_COLS:]
    wdo, wso, wo = (w_diff_out[lyr].astype(BF16), w_sb_out[lyr].astype(BF16), w_out[lyr].astype(BF16))
    w1, w2 = w_ff1[lyr].astype(BF16), w_ff2[lyr].astype(BF16)
    gmix = g_mix[lyr].reshape(1, D_MODEL)
    gffn = g_ffn[lyr].reshape(1, D_MODEL)
    qg = jnp.tile(q_norm_g[lyr], COL_BLOCK // HEAD_DIM).reshape(1, COL_BLOCK)
    kg = jnp.tile(k_norm_g[lyr], COL_BLOCK // HEAD_DIM).reshape(1, COL_BLOCK)
    subg_col = sub_g[lyr].reshape(LANES, 1)
    lams = [t[lyr].reshape(1, HEAD_DIM) for t in (lam_q1, lam_k1, lam_q2, lam_k2)]
    grp = jnp.arange(MXU_TILE, dtype=jnp.int32) // HEAD_DIM
    gmat = jnp.where(grp[:, None] == grp[None, :], 1.0 / HEAD_DIM, 0.0).astype(BF16)

    main_pos = N_META + jnp.arange(seq, dtype=jnp.int32)
    small_pos = jnp.concatenate([jnp.arange(N_META, dtype=jnp.int32),
                                 jnp.tile(past + jnp.arange(dseq, dtype=jnp.int32), db)])
    x_main = x_prompt.reshape(nb * seq, D_MODEL)
    x_small = jnp.concatenate([meta_tokens.astype(F32), x_sample.reshape(db * dseq, D_MODEL)], axis=0)
    pm = _proj(x_main, gmix, w_qkv, qg, kg, gmat, *_rope_tables(main_pos), tm=512, prompt_layout=True,
               lead=N_META)
    ps = _proj(x_small, gmix, w_qkv, qg, kg, gmat, *_rope_tables(small_pos), tm=x_small.shape[0],
               prompt_layout=False)
    qd_m, kd_hm, kdb_m, vd_hm, vdt_m, qs_m, kst_m, ksb_m, vstf_m, vst_m = pm
    qd_s, kd_s, kdb_s, vd_s, vdb_s, qs_s, ks_s, ksb_s, vs_s, vsb_s = ps

    def bt(a, n, t):
        return a.reshape(n, t, a.shape[-1])

    def meta_keys(a):
        return jnp.pad(a[:N_META], ((0, LANES - N_META), (0, 0)))[None]

    def meta_vals_t(a):
        return jnp.swapaxes(meta_keys(a), 1, 2)

    od_p = _diff_attn(lams, subg_col, bt(qd_m, nb, seq), meta_keys(kdb_s), meta_vals_t(vdb_s),
                      bt(kdb_m, nb, seq), vdt_m,
                      tq=1024, tk=512, pref_valid=N_META)
    os_p = _sb_attn(bt(qs_m, nb, seq), meta_keys(ksb_s), meta_vals_t(vsb_s),
                    bt(ksb_m, nb, seq), vst_m,
                    tq=1024, tk=256, pref_valid=N_META)

    smp = lambda a: bt(a[N_META:], db, dseq)
    cdk = cache_diff_k[lyr].reshape(db, past * N_HEADS, LANES)
    cdv = cache_diff_v[lyr].reshape(db, past * N_HEADS, LANES)
    od_s = _diff_decode(lams, sub_g[lyr].reshape(1, LANES), smp(qd_s), cdk, cdv, smp(kdb_s), smp(vdb_s),
                        chunk=2048)
    sb_t = lambda c: jnp.transpose(c[lyr], (0, 2, 3, 1)).reshape(db, SB_WIDTH, past)
    new_t = lambda a: jnp.pad(jnp.swapaxes(smp(a), 1, 2), ((0, 0), (0, 0), (0, LANES - dseq)))
    os_s = _sb_decode(smp(qs_s), sb_t(cache_sb_k), sb_t(cache_sb_v), new_t(ksb_s), new_t(vsb_s))

    y_p = _out(x_main, od_p.reshape(nb * seq, DIFF_WIDTH), os_p.reshape(nb * seq, SB_WIDTH),
               gmix, w_gate, wdo, wso, wo, gffn, w1, w2, tm=512)
    y_s = _out(x_sample.reshape(db * dseq, D_MODEL), od_s.reshape(db * dseq, DIFF_WIDTH),
               os_s.reshape(db * dseq, SB_WIDTH), gmix, w_gate, wdo, wso, wo, gffn, w1, w2, tm=db * dseq)

    def diff_prompt_cache(head_major, small):
        meta = small[:N_META].reshape(N_META * N_HEADS, 2 * HEAD_DIM)
        full = _fill_lead_tokens(head_major, meta, nb=nb)
        return full.reshape(1, nb, seq + N_META, N_HEADS, 2 * HEAD_DIM)

    def sb_prompt_cache(main_t, small):
        meta_t = jnp.broadcast_to(small[:N_META].T[None], (nb, SB_WIDTH, N_META))
        full = jnp.concatenate([meta_t, main_t], axis=2).reshape(nb, N_HEADS, HEAD_DIM, seq + N_META)
        return jnp.transpose(full, (0, 3, 1, 2))[None]

    def sample_cache(small, dim):
        return small[N_META:].reshape(1, db, dseq, N_HEADS, dim)

    return (y_p.reshape(nb, seq, D_MODEL), y_s.reshape(db, dseq, D_MODEL),
            diff_prompt_cache(kd_hm, kd_s), diff_prompt_cache(vd_hm, vd_s),
            sb_prompt_cache(kst_m, ks_s), sb_prompt_cache(vstf_m, vs_s),
            sample_cache(kd_s, 2 * HEAD_DIM), sample_cache(vd_s, 2 * HEAD_DIM),
            sample_cache(ks_s, HEAD_DIM), sample_cache(vs_s, HEAD_DIM))
```

```python
import functools
import math

import jax
import jax.numpy as jnp
from jax import lax
from jax.experimental import pallas as pl
from jax.experimental.pallas import tpu as pltpu

F32 = jnp.float32
BF16 = jnp.bfloat16

D_MODEL = 1024
N_META = 16
CHUNK = 64
N_HEADS = 8
HEAD_DIM = 64
DIFF_WIDTH = N_HEADS * 2 * HEAD_DIM
SB_WIDTH = N_HEADS * HEAD_DIM
QKV_COLS = 3 * DIFF_WIDTH + 3 * SB_WIDTH
D_FF = 4 * D_MODEL
ROT_DIM = HEAD_DIM // 4
ROPE_THETA = 500000.0
EPS = 1e-6
NEG = -1e30
LOG2E = math.log2(math.e)
Q_SCALE = HEAD_DIM ** -0.5 * LOG2E
ONES_ROWS = 16
LAM_INIT = 0.8 - 0.6 * math.exp(-0.3 * 0)

LANES = 128
MXU_TILE = 256
COL_BLOCK = 512
VMEM_LIMIT = 56 * 1024 * 1024

_NT = (((1,), (1,)), ((), ()))


def _rms(x):
    return x * lax.rsqrt(jnp.mean(x * x, axis=-1, keepdims=True) + EPS)


def _const_spec(shape):
    return pl.BlockSpec(shape, lambda *_: (0,) * len(shape), pipeline_mode=pl.Buffered(1))


def _proj_kernel(x_ref, gmix_ref, w_ref, qg_ref, kg_ref, gmat_ref, cos_ref, sa_ref, sb_ref,
                 qd_ref, kd_ref, kdb_ref, vd_ref, vdb_ref, qs_ref, ks_ref, ksb_ref, vs_ref, vsb_ref,
                 *, prompt_layout):
    tm = x_ref.shape[0]
    heads_per_block = COL_BLOCK // LANES

    def store_diff(ref, j, y):
        if not prompt_layout:
            ref[:, j * COL_BLOCK:(j + 1) * COL_BLOCK] = y
            return
        for hh in range(heads_per_block):
            ref[pl.ds(j * heads_per_block + hh, tm, stride=N_HEADS), :] = y[:, hh * LANES:(hh + 1) * LANES]

    h = (_rms(x_ref[...]) * gmix_ref[...]).astype(BF16)
    cos = cos_ref[...]
    sa = sa_ref[...]
    sb = sb_ref[...]

    def col(j):
        return jnp.dot(h, w_ref[:, j * COL_BLOCK:(j + 1) * COL_BLOCK], preferred_element_type=F32)

    def normed_rot(y, g):
        sq = (y * y).astype(BF16)
        msq = jnp.concatenate(
            [jnp.dot(sq[:, c * MXU_TILE:(c + 1) * MXU_TILE], gmat_ref[...], preferred_element_type=F32)
             for c in range(COL_BLOCK // MXU_TILE)], axis=1)
        yn = y * lax.rsqrt(msq + EPS) * g
        parts = []
        for c in range(COL_BLOCK // LANES):
            t = yn[:, c * LANES:(c + 1) * LANES]
            parts.append(t * cos + pltpu.roll(t, LANES - ROT_DIM // 2, 1) * sa
                         + pltpu.roll(t, ROT_DIM // 2, 1) * sb)
        return jnp.concatenate(parts, axis=1)

    for j in range(2):
        sl = slice(j * COL_BLOCK, (j + 1) * COL_BLOCK)
        q = normed_rot(col(j), qg_ref[...])
        qd_ref[:, sl] = (q * Q_SCALE).astype(BF16)
        k = normed_rot(col(2 + j), kg_ref[...])
        store_diff(kd_ref, j, k)
        kdb_ref[:, sl] = k.astype(BF16)
        v = col(4 + j)
        store_diff(vd_ref, j, v)
        if prompt_layout:
            vdb_ref[sl, :] = v.T.astype(BF16)
        else:
            vdb_ref[:, sl] = v.astype(BF16)
    qs_ref[...] = (col(6) * Q_SCALE).astype(BF16)
    k = col(7)
    ksb_ref[...] = k.astype(BF16)
    v = col(8)
    if prompt_layout:
        ks_ref[...] = k.T
        vt = v.T
        vs_ref[...] = vt
        vsb_ref[...] = vt.astype(BF16)
    else:
        ks_ref[...] = k
        vs_ref[...] = v
        vsb_ref[...] = v.astype(BF16)


def _proj(x, gmix, w_qkv, qg, kg, gmat, cos, sa, sb, *, tm, prompt_layout, lead=0):
    rows = x.shape[0]
    seq = cos.shape[0]
    n_pos_tiles = seq // tm
    row = lambda w: pl.BlockSpec((tm, w), lambda i: (i, 0))
    tab = pl.BlockSpec((tm, LANES), lambda i: (i % n_pos_tiles, 0))
    wide = lambda dt: jax.ShapeDtypeStruct((rows, DIFF_WIDTH), dt)
    narrow = lambda dt: jax.ShapeDtypeStruct((rows, SB_WIDTH), dt)
    if prompt_layout:
        nb = rows // seq
        t_spec = lambda w: pl.BlockSpec((None, w, tm), lambda i: (i // n_pos_tiles, 0, i % n_pos_tiles))
        t_shape = lambda w, dt: jax.ShapeDtypeStruct((nb, w, seq), dt)
        hm_spec = pl.BlockSpec(
            (pl.Element(tm * N_HEADS), pl.Element(LANES)),
            lambda i: (((i // n_pos_tiles) * (seq + lead) + lead + (i % n_pos_tiles) * tm) * N_HEADS, 0))
        hm_shape = jax.ShapeDtypeStruct((nb * (seq + lead) * N_HEADS, LANES), F32)
        out_specs = [row(DIFF_WIDTH), hm_spec, row(DIFF_WIDTH), hm_spec, t_spec(DIFF_WIDTH),
                     row(SB_WIDTH), t_spec(SB_WIDTH), row(SB_WIDTH), t_spec(SB_WIDTH), t_spec(SB_WIDTH)]
        out_shape = [wide(BF16), hm_shape, wide(BF16), hm_shape, t_shape(DIFF_WIDTH, BF16),
                     narrow(BF16), t_shape(SB_WIDTH, F32), narrow(BF16), t_shape(SB_WIDTH, F32),
                     t_shape(SB_WIDTH, BF16)]
    else:
        out_specs = [row(DIFF_WIDTH)] * 5 + [row(SB_WIDTH)] * 5
        out_shape = [wide(BF16), wide(F32), wide(BF16), wide(F32), wide(BF16),
                     narrow(BF16), narrow(F32), narrow(BF16), narrow(F32), narrow(BF16)]
    return pl.pallas_call(
        functools.partial(_proj_kernel, prompt_layout=prompt_layout),
        grid=(rows // tm,),
        in_specs=[row(D_MODEL), _const_spec((1, D_MODEL)), _const_spec((D_MODEL, QKV_COLS)),
                  _const_spec((1, COL_BLOCK)), _const_spec((1, COL_BLOCK)),
                  _const_spec((MXU_TILE, MXU_TILE)), tab, tab, tab],
        out_specs=out_specs,
        out_shape=out_shape,
        compiler_params=pltpu.CompilerParams(dimension_semantics=("arbitrary",),
                                             vmem_limit_bytes=VMEM_LIMIT),
        name="proj",
    )(x, gmix, w_qkv, qg, kg, gmat, cos, sa, sb)


def _fill_lead_kernel(lead_ref, big_ref, out_ref):
    del big_ref
    out_ref[...] = lead_ref[...]


def _fill_lead_tokens(big, lead_rows, *, nb):
    n = lead_rows.shape[0]
    per_batch = big.shape[0] // nb
    return pl.pallas_call(
        _fill_lead_kernel,
        grid=(nb,),
        in_specs=[_const_spec((n, LANES)), pl.BlockSpec(memory_space=pl.ANY)],
        out_specs=pl.BlockSpec((pl.Element(n), pl.Element(LANES)), lambda b: (b * per_batch, 0)),
        out_shape=jax.ShapeDtypeStruct(big.shape, big.dtype),
        input_output_aliases={1: 0},
        compiler_params=pltpu.CompilerParams(dimension_semantics=("arbitrary",)),
        name="fill_lead_tokens",
    )(lead_rows, big)


def _split_halves(q):
    lane = lax.broadcasted_iota(jnp.int32, q.shape, 1)
    zero = jnp.zeros_like(q)
    return jnp.where(lane < HEAD_DIM, q, zero), jnp.where(lane >= HEAD_DIM, q, zero)


def _diff_kernel(lq1_ref, lk1_ref, lq2_ref, lk2_ref, subg_ref, q_ref, kp_ref, vp_ref, km_ref, vm_ref,
                 o_ref, m_ref, acc_ref, s_ref, *, tq, tk, tkp, pref_valid, n_diag):
    i = pl.program_id(2)
    n_full = i * n_diag
    qs = _split_halves(q_ref[0])
    m_ref[...] = jnp.full(m_ref.shape, NEG, F32)
    acc_ref[...] = jnp.zeros(acc_ref.shape, F32)

    def step(nxt, cur):
        if cur is not None:
            slot, vtblk, mask, qlo = cur
            n = vtblk.shape[1]
            vt_ones = jnp.concatenate([vtblk, jnp.ones((ONES_ROWS, n), BF16)], axis=0)
        for a in range(2):
            if cur is not None:
                s = s_ref[slot, a, :n, qlo:]
                if mask is not None:
                    s = jnp.where(mask[:, qlo:], s, NEG)
                m_prev = m_ref[a, :, qlo:]
                m_new = jnp.maximum(m_prev, jnp.max(s, axis=0, keepdims=True))
                alpha = jnp.exp2(m_prev - m_new)
                p = jnp.exp2(s - m_new)
            if nxt is not None:
                kblk, nslot, nqlo = nxt
                s_ref[nslot, a, :kblk.shape[0], nqlo:] = lax.dot_general(
                    kblk, qs[a][nqlo:], _NT, preferred_element_type=F32)
            if cur is not None:
                acc_ref[a, :, qlo:] = (alpha * acc_ref[a, :, qlo:]
                                       + jnp.dot(vt_ones, p.astype(BF16), preferred_element_type=F32))
                m_ref[a, :, qlo:] = m_new

    def main_k(g):
        return km_ref[0, pl.ds(pl.multiple_of(g * tk, tk), tk), :]

    def main_vt(g):
        return vm_ref[0, :, pl.ds(pl.multiple_of(g * tk, tk), tk)]

    step((kp_ref[0], 1, 0), None)
    step((main_k(0), 0, 0), (1, vp_ref[0], lax.broadcasted_iota(jnp.int32, (tkp, tq), 0) < pref_valid, 0))

    def full_body(t, carry):
        step((main_k(2 * t + 1), 1, 0), (0, main_vt(2 * t), None, 0))
        step((main_k(2 * t + 2), 0, 0), (1, main_vt(2 * t + 1), None, 0))
        return carry
    lax.fori_loop(0, i * (n_diag // 2), full_body, 0)

    key_idx = lax.broadcasted_iota(jnp.int32, (tk, tq), 0)
    q_chunk = lax.broadcasted_iota(jnp.int32, (tk, tq), 1) // CHUNK
    for d in range(n_diag):
        g = n_full + d
        nxt = (main_k(g + 1), (d + 1) % 2, (d + 1) * tk) if d + 1 < n_diag else None
        step(nxt, (d % 2, main_vt(g), (key_idx + d * tk) // CHUNK <= q_chunk, d * tk))

    lam = (jnp.exp(jnp.sum(lq1_ref[...] * lk1_ref[...], axis=1, keepdims=True))
           - jnp.exp(jnp.sum(lq2_ref[...] * lk2_ref[...], axis=1, keepdims=True)) + LAM_INIT)
    o = (acc_ref[0, :LANES] / acc_ref[0, LANES:LANES + 1]
         - lam * (acc_ref[1, :LANES] / acc_ref[1, LANES:LANES + 1]))
    o = o * lax.rsqrt(jnp.mean(o * o, axis=0, keepdims=True) + EPS) * subg_ref[...] * (1.0 - LAM_INIT)
    o_ref[0] = o.T.astype(BF16)


def _diff_attn(lams, subg_col, q, kp, vtp, km, vtm, *, tq, tk, pref_valid):
    nb, tq_total, _ = q.shape
    tkp = kp.shape[1]
    t_main = km.shape[1]
    n_diag = tq // tk
    assert n_diag % 2 == 0, "main blocks are consumed in pairs"
    kernel = functools.partial(_diff_kernel, tq=tq, tk=tk, tkp=tkp, pref_valid=pref_valid, n_diag=n_diag)
    small = _const_spec((1, HEAD_DIM))
    return pl.pallas_call(
        kernel,
        grid=(nb, N_HEADS, tq_total // tq),
        in_specs=[small, small, small, small, _const_spec((LANES, 1)),
                  pl.BlockSpec((1, tq, LANES), lambda b, h, i: (b, i, h)),
                  pl.BlockSpec((1, tkp, LANES), lambda b, h, i: (0, 0, h)),
                  pl.BlockSpec((1, LANES, tkp), lambda b, h, i: (0, h, 0)),
                  pl.BlockSpec((1, t_main, LANES), lambda b, h, i: (b, 0, h)),
                  pl.BlockSpec((1, LANES, t_main), lambda b, h, i: (b, h, 0))],
        out_specs=pl.BlockSpec((1, tq, LANES), lambda b, h, i: (b, i, h)),
        out_shape=jax.ShapeDtypeStruct((nb, tq_total, DIFF_WIDTH), BF16),
        scratch_shapes=[pltpu.VMEM((2, 1, tq), F32), pltpu.VMEM((2, LANES + ONES_ROWS, tq), F32),
                        pltpu.VMEM((2, 2, max(tk, tkp), tq), F32)],
        compiler_params=pltpu.CompilerParams(dimension_semantics=("arbitrary",) * 3,
                                             vmem_limit_bytes=VMEM_LIMIT),
        name="diff_attn",
    )(*lams, subg_col, q, kp, vtp, km, vtm)


def _suffix_matrix(n):
    r = lax.broadcasted_iota(jnp.int32, (n + ONES_ROWS, n), 0)
    c = lax.broadcasted_iota(jnp.int32, (n + ONES_ROWS, n), 1)
    return jnp.where((c >= r) | (r == n), 1.0, 0.0).astype(BF16)


def _sb_kernel(q_ref, kp_ref, vp_ref, km_ref, vm_ref, o_ref, c_ref, acc_ref, z_ref,
               *, tq, tk, tkp, pref_valid, n_diag):
    i = pl.program_id(2)
    n_full = i * n_diag
    qs = _split_halves(q_ref[0])
    c_ref[...] = jnp.zeros(c_ref.shape, F32)
    acc_ref[...] = jnp.zeros(acc_ref.shape, F32)

    def step(nxt, cur):
        if nxt is not None:
            kblk, nslot, nqlo = nxt
            for a in range(2):
                z_ref[nslot, a, :kblk.shape[0], nqlo:] = lax.dot_general(
                    kblk, qs[a][nqlo:], _NT, preferred_element_type=F32)
        if cur is None:
            return
        slot, vtblk, mask, sfx_mat, qlo = cur
        n = vtblk.shape[1]
        if mask is not None:
            mask = mask[:, qlo:]
        for a in range(2):
            rows = slice(a * HEAD_DIM, (a + 1) * HEAD_DIM)
            u = z_ref[slot, a, :n, qlo:]
            neg_part = jnp.minimum(u, 0.0)
            d = neg_part - u
            log1m = d - jnp.log(1.0 + jnp.exp2(neg_part + d)) * LOG2E
            if mask is not None:
                log1m = jnp.where(mask, log1m, 0.0)
            sfx = jnp.dot(sfx_mat, log1m.astype(BF16), preferred_element_type=F32)
            t = u + sfx[:n]
            if mask is not None:
                t = jnp.where(mask, t, NEG)
            pv = jnp.dot(vtblk[rows, :], jnp.exp2(t).astype(BF16), preferred_element_type=F32)
            acc_ref[rows, qlo:] += pv * jnp.exp2(c_ref[a, :, qlo:])
            c_ref[a, :, qlo:] += sfx[n:n + 1]

    def main_k(g):
        return km_ref[0, pl.ds(pl.multiple_of(g * tk, tk), tk), :]

    def main_vt(g):
        return vm_ref[0, :, pl.ds(pl.multiple_of(g * tk, tk), tk)]

    sfx_main = _suffix_matrix(tk)
    sfx_pref = sfx_main if tkp == tk else _suffix_matrix(tkp)
    key = lax.broadcasted_iota(jnp.int32, (tk, tq), 0)
    qry = lax.broadcasted_iota(jnp.int32, (tk, tq), 1)

    step((kp_ref[0], 2, 0), None)
    step((main_k(n_full + n_diag - 1), 0, (n_diag - 1) * tk), None)
    for d in reversed(range(n_diag)):
        s = (n_diag - 1 - d) % 2
        if d > 0:
            nxt = (main_k(n_full + d - 1), 1 - s, (d - 1) * tk)
        else:
            nxt = (main_k(jnp.maximum(n_full - 1, 0)), 1 - s, 0)
        step(nxt, (s, main_vt(n_full + d), key + d * tk < qry, sfx_main, d * tk))

    def full_body(t, carry):
        g = n_full - 1 - 2 * t
        step((main_k(g - 1), 1, 0), (0, main_vt(g), None, sfx_main, 0))
        step((main_k(jnp.maximum(g - 2, 0)), 0, 0), (1, main_vt(g - 1), None, sfx_main, 0))
        return carry
    lax.fori_loop(0, i * (n_diag // 2), full_body, 0)

    step(None, (2, vp_ref[0], lax.broadcasted_iota(jnp.int32, (tkp, tq), 0) < pref_valid, sfx_pref, 0))
    o_ref[0] = acc_ref[...].T.astype(BF16)


def _sb_attn(q, kp, vtp, km, vtm, *, tq, tk, pref_valid):
    nb, tq_total, _ = q.shape
    tkp = kp.shape[1]
    t_main = km.shape[1]
    n_diag = tq // tk
    assert n_diag % 2 == 0, "main blocks are consumed in pairs"
    kernel = functools.partial(_sb_kernel, tq=tq, tk=tk, tkp=tkp, pref_valid=pref_valid, n_diag=n_diag)
    return pl.pallas_call(
        kernel,
        grid=(nb, SB_WIDTH // LANES, tq_total // tq),
        in_specs=[pl.BlockSpec((1, tq, LANES), lambda b, h, i: (b, i, h)),
                  pl.BlockSpec((1, tkp, LANES), lambda b, h, i: (0, 0, h)),
                  pl.BlockSpec((1, LANES, tkp), lambda b, h, i: (0, h, 0)),
                  pl.BlockSpec((1, t_main, LANES), lambda b, h, i: (b, 0, h)),
                  pl.BlockSpec((1, LANES, t_main), lambda b, h, i: (b, h, 0))],
        out_specs=pl.BlockSpec((1, tq, LANES), lambda b, h, i: (b, i, h)),
        out_shape=jax.ShapeDtypeStruct((nb, tq_total, SB_WIDTH), BF16),
        scratch_shapes=[pltpu.VMEM((2, 1, tq), F32), pltpu.VMEM((LANES, tq), F32),
                        pltpu.VMEM((3, 2, max(tk, tkp), tq), F32)],
        compiler_params=pltpu.CompilerParams(dimension_semantics=("arbitrary",) * 3,
                                             vmem_limit_bytes=VMEM_LIMIT),
        name="sb_attn",
    )(q, kp, vtp, km, vtm)


def _diff_decode_kernel(lq1_ref, lk1_ref, lq2_ref, lk2_ref, subg_ref, q_ref, ck_ref, cv_ref, kn_ref, vn_ref,
                        o_ref, m_ref, l_ref, acc_ref, *, chunk):
    kc = pl.program_id(1)
    nq = q_ref.shape[0]

    @pl.when(kc == 0)
    def _():
        m_ref[...] = jnp.full(m_ref.shape, NEG, F32)
        l_ref[...] = jnp.zeros(l_ref.shape, F32)
        acc_ref[...] = jnp.zeros(acc_ref.shape, F32)

    def update(h, k, v):
        n = k.shape[0]
        qs = _split_halves(q_ref[:, h * LANES:(h + 1) * LANES])
        v_ones = jnp.concatenate([v, jnp.ones((n, LANES), BF16)], axis=1)
        ps, alphas = [], []
        for a in range(2):
            s = lax.dot_general(qs[a], k, _NT, preferred_element_type=F32)
            m_prev = m_ref[2 * h + a]
            m_new = jnp.maximum(m_prev, jnp.max(s, axis=1, keepdims=True))
            alphas.append(jnp.exp2(m_prev - m_new))
            ps.append(jnp.exp2(s - m_new).astype(BF16))
            m_ref[2 * h + a] = m_new
        pv = jnp.dot(jnp.concatenate(ps, axis=0), v_ones, preferred_element_type=F32)
        for a in range(2):
            part = pv[a * nq:(a + 1) * nq]
            acc_ref[2 * h + a] = alphas[a] * acc_ref[2 * h + a] + part[:, :LANES]
            l_ref[2 * h + a] = alphas[a] * l_ref[2 * h + a] + part[:, LANES:]

    for h in range(N_HEADS):
        update(h, ck_ref[pl.ds(h, chunk, stride=N_HEADS), :].astype(BF16),
               cv_ref[pl.ds(h, chunk, stride=N_HEADS), :].astype(BF16))

    @pl.when(kc == pl.num_programs(1) - 1)
    def _():
        lam = (jnp.exp(jnp.sum(lq1_ref[...] * lk1_ref[...], axis=1, keepdims=True))
               - jnp.exp(jnp.sum(lq2_ref[...] * lk2_ref[...], axis=1, keepdims=True)) + LAM_INIT)
        for h in range(N_HEADS):
            cols = slice(h * LANES, (h + 1) * LANES)
            update(h, kn_ref[:, cols], vn_ref[:, cols])
            o = acc_ref[2 * h] / l_ref[2 * h] - lam * (acc_ref[2 * h + 1] / l_ref[2 * h + 1])
            o_ref[:, cols] = (_rms(o) * subg_ref[...] * (1.0 - LAM_INIT)).astype(BF16)


def _diff_decode(lams, subg_row, q, ck, cv, kn, vn, *, chunk):
    nb, nq, _ = q.shape
    n_chunks = ck.shape[1] // (chunk * N_HEADS)
    small = _const_spec((1, HEAD_DIM))
    tok = pl.BlockSpec((None, nq, DIFF_WIDTH), lambda b, c: (b, 0, 0))
    cache = pl.BlockSpec((None, chunk * N_HEADS, LANES), lambda b, c: (b, c, 0))
    return pl.pallas_call(
        functools.partial(_diff_decode_kernel, chunk=chunk),
        grid=(nb, n_chunks),
        in_specs=[small, small, small, small, _const_spec((1, LANES)), tok, cache, cache, tok, tok],
        out_specs=tok,
        out_shape=jax.ShapeDtypeStruct((nb, nq, DIFF_WIDTH), BF16),
        scratch_shapes=[pltpu.VMEM((2 * N_HEADS, nq, 1), F32), pltpu.VMEM((2 * N_HEADS, nq, LANES), F32),
                        pltpu.VMEM((2 * N_HEADS, nq, LANES), F32)],
        compiler_params=pltpu.CompilerParams(dimension_semantics=("arbitrary",) * 2,
                                             vmem_limit_bytes=VMEM_LIMIT),
        name="diff_decode",
    )(*lams, subg_row, q, ck, cv, kn, vn)


SB_BLOCK = 256


def _sb_decode_kernel(q_ref, ckt_ref, cvt_ref, knt_ref, vnt_ref, o_ref):
    nq = q_ref.shape[0]
    past = ckt_ref.shape[1]
    n_blk = past // SB_BLOCK
    def suffix_cols(n):
        r = lax.broadcasted_iota(jnp.int32, (n, n + LANES), 0)
        c = lax.broadcasted_iota(jnp.int32, (n, n + LANES), 1)
        return jnp.where((r > c) | (c >= n), 1.0, 0.0).astype(BF16)
    sfx_blk = suffix_cols(SB_BLOCK)
    sfx_new = suffix_cols(LANES)
    row = lax.broadcasted_iota(jnp.int32, (nq, LANES), 0)
    lane = lax.broadcasted_iota(jnp.int32, (nq, LANES), 1)
    new_mask = lane < row

    def log1m_of(u):
        nu = -u
        return jnp.minimum(nu, 0.0) - jnp.log(1.0 + jnp.exp2(jnp.minimum(u, nu))) * LOG2E

    for j in range(ckt_ref.shape[0] // LANES):
        rows = slice(j * LANES, (j + 1) * LANES)
        kt = ckt_ref[rows, :].astype(BF16)
        vt = cvt_ref[rows, :].astype(BF16)
        qs = _split_halves(q_ref[:, rows])
        ws, wns = [], []
        for a in range(2):
            un = jnp.dot(qs[a], knt_ref[rows, :], preferred_element_type=F32)
            ln = jnp.where(new_mask, log1m_of(un), 0.0)
            sn = jnp.dot(ln.astype(BF16), sfx_new, preferred_element_type=F32)
            wns.append(jnp.exp2(jnp.where(new_mask, un + ln + sn[:, :LANES], NEG)).astype(BF16))
            carry = sn[:, LANES:]
            u = jnp.dot(qs[a], kt, preferred_element_type=F32)
            l1m = log1m_of(u)
            stacked = jnp.concatenate(
                [l1m[:, b * SB_BLOCK:(b + 1) * SB_BLOCK] for b in range(n_blk)], axis=0).astype(BF16)
            sfx = jnp.dot(stacked, sfx_blk, preferred_element_type=F32)
            ts = [None] * n_blk
            for b in reversed(range(n_blk)):
                cols = slice(b * SB_BLOCK, (b + 1) * SB_BLOCK)
                part = sfx[b * nq:(b + 1) * nq]
                c2 = jnp.concatenate([carry] * (SB_BLOCK // LANES), axis=1)
                ts[b] = u[:, cols] + l1m[:, cols] + part[:, :SB_BLOCK] + c2
                carry = carry + part[:, SB_BLOCK:]
            ws.append(jnp.exp2(jnp.concatenate(ts, axis=1)).astype(BF16))
        o = (lax.dot_general(jnp.concatenate(ws, axis=0), vt, _NT, preferred_element_type=F32)
             + lax.dot_general(jnp.concatenate(wns, axis=0), vnt_ref[rows, :], _NT,
                               preferred_element_type=F32))
        o_ref[:, rows] = jnp.where(lane < HEAD_DIM, o[:nq], o[nq:]).astype(BF16)


def _sb_decode(q, ckt, cvt, knt, vnt):
    nb, nq, _ = q.shape
    assert nq <= LANES
    past = ckt.shape[2]
    width = 2 * LANES
    tok = pl.BlockSpec((None, nq, width), lambda b, g: (b, 0, g))
    cache = pl.BlockSpec((None, width, past), lambda b, g: (b, g, 0))
    new = pl.BlockSpec((None, width, LANES), lambda b, g: (b, g, 0))
    return pl.pallas_call(
        _sb_decode_kernel,
        grid=(nb, SB_WIDTH // width),
        in_specs=[tok, cache, cache, new, new],
        out_specs=tok,
        out_shape=jax.ShapeDtypeStruct((nb, nq, SB_WIDTH), BF16),
        compiler_params=pltpu.CompilerParams(dimension_semantics=("arbitrary",) * 2,
                                             vmem_limit_bytes=VMEM_LIMIT),
        name="sb_decode",
    )(q, ckt, cvt, knt, vnt)


def _out_kernel(x_ref, od_ref, os_ref, gmix_ref, wg_ref, wdo_ref, wso_ref, wo_ref, gffn_ref,
                w1_ref, w2_ref, y_ref):
    x = x_ref[...]
    h = (_rms(x) * gmix_ref[...]).astype(BF16)
    gate = jax.nn.sigmoid(jnp.dot(h, wg_ref[...], preferred_element_type=F32))
    a = jnp.dot(od_ref[...], wdo_ref[...], preferred_element_type=F32)
    b = jnp.dot(os_ref[...], wso_ref[...], preferred_element_type=F32)
    merged = (gate[:, :D_MODEL] * a + gate[:, D_MODEL:] * b).astype(BF16)
    x1 = x + jnp.dot(merged, wo_ref[...], preferred_element_type=F32)
    h2 = (_rms(x1) * gffn_ref[...]).astype(BF16)
    y = x1
    for c in range(D_FF // D_MODEL):
        sl = slice(c * D_MODEL, (c + 1) * D_MODEL)
        f = jnp.maximum(jnp.dot(h2, w1_ref[:, sl], preferred_element_type=F32), 0.0)
        y = y + jnp.dot((f * f).astype(BF16), w2_ref[sl, :], preferred_element_type=F32)
    y_ref[...] = y


def _out(x, od, osb, gmix, wg, wdo, wso, wo, gffn, w1, w2, *, tm):
    rows = x.shape[0]
    row = lambda w: pl.BlockSpec((tm, w), lambda i: (i, 0))
    return pl.pallas_call(
        _out_kernel,
        grid=(rows // tm,),
        in_specs=[row(D_MODEL), row(DIFF_WIDTH), row(SB_WIDTH), _const_spec((1, D_MODEL)),
                  _const_spec((D_MODEL, 2 * D_MODEL)), _const_spec((DIFF_WIDTH, D_MODEL)),
                  _const_spec((SB_WIDTH, D_MODEL)), _const_spec((D_MODEL, D_MODEL)),
                  _const_spec((1, D_MODEL)), _const_spec((D_MODEL, D_FF)), _const_spec((D_FF, D_MODEL))],
        out_specs=row(D_MODEL),
        out_shape=jax.ShapeDtypeStruct((rows, D_MODEL), F32),
        compiler_params=pltpu.CompilerParams(dimension_semantics=("arbitrary",),
                                             vmem_limit_bytes=VMEM_LIMIT),
        name="out",
    )(x, od, osb, gmix, wg, wdo, wso, wo, gffn, w1, w2)


def _rope_tables(pos):
    half = ROT_DIM // 2
    d = jnp.arange(LANES, dtype=jnp.int32) % HEAD_DIM
    inv = ROPE_THETA ** (-(2 * (d % half)).astype(F32) / ROT_DIM)
    ang = pos.astype(F32)[:, None] * inv[None, :]
    cos, sin = jnp.cos(ang), jnp.sin(ang)
    cos_t = jnp.where(d < ROT_DIM, cos, 1.0)
    sa_t = jnp.where(d < half, -sin, 0.0)
    sb_t = jnp.where((d >= half) & (d < ROT_DIM), sin, 0.0)
    return cos_t, sa_t, sb_t


def kernel(x_prompt, x_sample, cache_diff_k, cache_diff_v, cache_sb_k, cache_sb_v, meta_tokens,
           g_mix, w_in, q_norm_g, k_norm_g, lam_q1, lam_k1, lam_q2, lam_k2, sub_g,
           w_diff_out, w_sb_out, w_out, g_ffn, w_ff1, w_ff2):
    nb, seq, _ = x_prompt.shape
    db, dseq, _ = x_sample.shape
    past = cache_diff_k.shape[2]
    lyr = 0

    w_qkv, w_gate = w_in[lyr, :, :QKV_COLS].astype(BF16), w_in[lyr, :, QKV_COLS:].astype(BF16)
    wdo, wso, wo = (w_diff_out[lyr].astype(BF16), w_sb_out[lyr].astype(BF16), w_out[lyr].astype(BF16))
    w1, w2 = w_ff1[lyr].astype(BF16), w_ff2[lyr].astype(BF16)
    gmix = g_mix[lyr].reshape(1, D_MODEL)
    gffn = g_ffn[lyr].reshape(1, D_MODEL)
    qg = jnp.tile(q_norm_g[lyr], COL_BLOCK // HEAD_DIM).reshape(1, COL_BLOCK)
    kg = jnp.tile(k_norm_g[lyr], COL_BLOCK // HEAD_DIM).reshape(1, COL_BLOCK)
    subg_col = sub_g[lyr].reshape(LANES, 1)
    lams = [t[lyr].reshape(1, HEAD_DIM) for t in (lam_q1, lam_k1, lam_q2, lam_k2)]
    grp = jnp.arange(MXU_TILE, dtype=jnp.int32) // HEAD_DIM
    gmat = jnp.where(grp[:, None] == grp[None, :], 1.0 / HEAD_DIM, 0.0).astype(BF16)

    main_pos = N_META + jnp.arange(seq, dtype=jnp.int32)
    small_pos = jnp.concatenate([jnp.arange(N_META, dtype=jnp.int32),
                                 jnp.tile(past + jnp.arange(dseq, dtype=jnp.int32), db)])
    x_main = x_prompt.reshape(nb * seq, D_MODEL)
    x_small = jnp.concatenate([meta_tokens.astype(F32), x_sample.reshape(db * dseq, D_MODEL)], axis=0)
    pm = _proj(x_main, gmix, w_qkv, qg, kg, gmat, *_rope_tables(main_pos), tm=512, prompt_layout=True,
               lead=N_META)
    ps = _proj(x_small, gmix, w_qkv, qg, kg, gmat, *_rope_tables(small_pos), tm=x_small.shape[0],
               prompt_layout=False)
    qd_m, kd_hm, kdb_m, vd_hm, vdt_m, qs_m, kst_m, ksb_m, vstf_m, vst_m = pm
    qd_s, kd_s, kdb_s, vd_s, vdb_s, qs_s, ks_s, ksb_s, vs_s, vsb_s = ps

    def bt(a, n, t):
        return a.reshape(n, t, a.shape[-1])

    def meta_keys(a):
        return jnp.pad(a[:N_META], ((0, LANES - N_META), (0, 0)))[None]

    def meta_vals_t(a):
        return jnp.swapaxes(meta_keys(a), 1, 2)

    od_p = _diff_attn(lams, subg_col, bt(qd_m, nb, seq), meta_keys(kdb_s), meta_vals_t(vdb_s),
                      bt(kdb_m, nb, seq), vdt_m,
                      tq=2048, tk=512, pref_valid=N_META)
    os_p = _sb_attn(bt(qs_m, nb, seq), meta_keys(ksb_s), meta_vals_t(vsb_s),
                    bt(ksb_m, nb, seq), vst_m,
                    tq=2048, tk=256, pref_valid=N_META)

    smp = lambda a: bt(a[N_META:], db, dseq)
    cdk = cache_diff_k[lyr].reshape(db, past * N_HEADS, LANES)
    cdv = cache_diff_v[lyr].reshape(db, past * N_HEADS, LANES)
    od_s = _diff_decode(lams, sub_g[lyr].reshape(1, LANES), smp(qd_s), cdk, cdv, smp(kdb_s), smp(vdb_s),
                        chunk=2048)
    sb_t = lambda c: jnp.transpose(c[lyr], (0, 2, 3, 1)).reshape(db, SB_WIDTH, past)
    new_t = lambda a: jnp.pad(jnp.swapaxes(smp(a), 1, 2), ((0, 0), (0, 0), (0, LANES - dseq)))
    os_s = _sb_decode(smp(qs_s), sb_t(cache_sb_k), sb_t(cache_sb_v), new_t(ksb_s), new_t(vsb_s))

    y_p = _out(x_main, od_p.reshape(nb * seq, DIFF_WIDTH), os_p.reshape(nb * seq, SB_WIDTH),
               gmix, w_gate, wdo, wso, wo, gffn, w1, w2, tm=512)
    y_s = _out(x_sample.reshape(db * dseq, D_MODEL), od_s.reshape(db * dseq, DIFF_WIDTH),
               os_s.reshape(db * dseq, SB_WIDTH), gmix, w_gate, wdo, wso, wo, gffn, w1, w2, tm=db * dseq)

    def diff_prompt_cache(head_major, small):
        meta = small[:N_META].reshape(N_META * N_HEADS, 2 * HEAD_DIM)
        full = _fill_lead_tokens(head_major, meta, nb=nb)
        return full.reshape(1, nb, seq + N_META, N_HEADS, 2 * HEAD_DIM)

    def sb_prompt_cache(main_t, small):
        meta_t = jnp.broadcast_to(small[:N_META].T[None], (nb, SB_WIDTH, N_META))
        full = jnp.concatenate([meta_t, main_t], axis=2).reshape(nb, N_HEADS, HEAD_DIM, seq + N_META)
        return jnp.transpose(full, (0, 3, 1, 2))[None]

    def sample_cache(small, dim):
        return small[N_META:].reshape(1, db, dseq, N_HEADS, dim)

    return (y_p.reshape(nb, seq, D_MODEL), y_s.reshape(db, dseq, D_MODEL),
            diff_prompt_cache(kd_hm, kd_s), diff_prompt_cache(vd_hm, vd_s),
            sb_prompt_cache(kst_m, ks_s), sb_prompt_cache(vstf_m, vs_s),
            sample_cache(kd_s, 2 * HEAD_DIM), sample_cache(vd_s, 2 * HEAD_DIM),
            sample_cache(ks_s, HEAD_DIM), sample_cache(vs_s, HEAD_DIM))
```

```python
import functools
import math

import jax
import jax.numpy as jnp
from jax import lax
from jax.experimental import pallas as pl
from jax.experimental.pallas import tpu as pltpu

F32 = jnp.float32
BF16 = jnp.bfloat16

D_MODEL = 1024
N_META = 16
CHUNK = 64
N_HEADS = 8
HEAD_DIM = 64
DIFF_WIDTH = N_HEADS * 2 * HEAD_DIM
SB_WIDTH = N_HEADS * HEAD_DIM
QKV_COLS = 3 * DIFF_WIDTH + 3 * SB_WIDTH
D_FF = 4 * D_MODEL
ROT_DIM = HEAD_DIM // 4
ROPE_THETA = 500000.0
EPS = 1e-6
NEG = -1e30
SKIP_BELOW = -160.0
LOG2E = math.log2(math.e)
Q_SCALE = HEAD_DIM ** -0.5 * LOG2E
ONES_ROWS = 16
LAM_INIT = 0.8 - 0.6 * math.exp(-0.3 * 0)

LANES = 128
MXU_TILE = 256
COL_BLOCK = 512
VMEM_LIMIT = 56 * 1024 * 1024

_NT = (((1,), (1,)), ((), ()))


def _rms(x):
    return x * lax.rsqrt(jnp.mean(x * x, axis=-1, keepdims=True) + EPS)


def _const_spec(shape):
    return pl.BlockSpec(shape, lambda *_: (0,) * len(shape), pipeline_mode=pl.Buffered(1))


def _proj_kernel(x_ref, gmix_ref, w_ref, qg_ref, kg_ref, gmat_ref, cos_ref, sa_ref, sb_ref,
                 qd_ref, kd_ref, kdb_ref, vd_ref, vdb_ref, qs_ref, ks_ref, ksb_ref, vs_ref, vsb_ref,
                 *, prompt_layout):
    tm = x_ref.shape[0]
    heads_per_block = COL_BLOCK // LANES

    def store_diff(ref, j, y):
        if not prompt_layout:
            ref[:, j * COL_BLOCK:(j + 1) * COL_BLOCK] = y
            return
        for hh in range(heads_per_block):
            ref[pl.ds(j * heads_per_block + hh, tm, stride=N_HEADS), :] = y[:, hh * LANES:(hh + 1) * LANES]

    h = (_rms(x_ref[...]) * gmix_ref[...]).astype(BF16)
    cos = cos_ref[...]
    sa = sa_ref[...]
    sb = sb_ref[...]

    def col(j):
        return jnp.dot(h, w_ref[:, j * COL_BLOCK:(j + 1) * COL_BLOCK], preferred_element_type=F32)

    def normed_rot(y, g):
        sq = (y * y).astype(BF16)
        msq = jnp.concatenate(
            [jnp.dot(sq[:, c * MXU_TILE:(c + 1) * MXU_TILE], gmat_ref[...], preferred_element_type=F32)
             for c in range(COL_BLOCK // MXU_TILE)], axis=1)
        yn = y * lax.rsqrt(msq + EPS) * g
        parts = []
        for c in range(COL_BLOCK // LANES):
            t = yn[:, c * LANES:(c + 1) * LANES]
            parts.append(t * cos + pltpu.roll(t, LANES - ROT_DIM // 2, 1) * sa
                         + pltpu.roll(t, ROT_DIM // 2, 1) * sb)
        return jnp.concatenate(parts, axis=1)

    for j in range(2):
        sl = slice(j * COL_BLOCK, (j + 1) * COL_BLOCK)
        q = normed_rot(col(j), qg_ref[...])
        qd_ref[:, sl] = (q * Q_SCALE).astype(BF16)
        k = normed_rot(col(2 + j), kg_ref[...])
        store_diff(kd_ref, j, k)
        kdb_ref[:, sl] = k.astype(BF16)
        v = col(4 + j)
        store_diff(vd_ref, j, v)
        if prompt_layout:
            vdb_ref[sl, :] = v.T.astype(BF16)
        else:
            vdb_ref[:, sl] = v.astype(BF16)
    qs_ref[...] = (col(6) * Q_SCALE).astype(BF16)
    k = col(7)
    ksb_ref[...] = k.astype(BF16)
    v = col(8)
    if prompt_layout:
        ks_ref[...] = k.T
        vt = v.T
        vs_ref[...] = vt
        vsb_ref[...] = vt.astype(BF16)
    else:
        ks_ref[...] = k
        vs_ref[...] = v
        vsb_ref[...] = v.astype(BF16)


def _proj(x, gmix, w_qkv, qg, kg, gmat, cos, sa, sb, *, tm, prompt_layout, lead=0):
    rows = x.shape[0]
    seq = cos.shape[0]
    n_pos_tiles = seq // tm
    row = lambda w: pl.BlockSpec((tm, w), lambda i: (i, 0))
    tab = pl.BlockSpec((tm, LANES), lambda i: (i % n_pos_tiles, 0))
    wide = lambda dt: jax.ShapeDtypeStruct((rows, DIFF_WIDTH), dt)
    narrow = lambda dt: jax.ShapeDtypeStruct((rows, SB_WIDTH), dt)
    if prompt_layout:
        nb = rows // seq
        t_spec = lambda w: pl.BlockSpec((None, w, tm), lambda i: (i // n_pos_tiles, 0, i % n_pos_tiles))
        t_shape = lambda w, dt: jax.ShapeDtypeStruct((nb, w, seq), dt)
        hm_spec = pl.BlockSpec(
            (pl.Element(tm * N_HEADS), pl.Element(LANES)),
            lambda i: (((i // n_pos_tiles) * (seq + lead) + lead + (i % n_pos_tiles) * tm) * N_HEADS, 0))
        hm_shape = jax.ShapeDtypeStruct((nb * (seq + lead) * N_HEADS, LANES), F32)
        out_specs = [row(DIFF_WIDTH), hm_spec, row(DIFF_WIDTH), hm_spec, t_spec(DIFF_WIDTH),
                     row(SB_WIDTH), t_spec(SB_WIDTH), row(SB_WIDTH), t_spec(SB_WIDTH), t_spec(SB_WIDTH)]
        out_shape = [wide(BF16), hm_shape, wide(BF16), hm_shape, t_shape(DIFF_WIDTH, BF16),
                     narrow(BF16), t_shape(SB_WIDTH, F32), narrow(BF16), t_shape(SB_WIDTH, F32),
                     t_shape(SB_WIDTH, BF16)]
    else:
        out_specs = [row(DIFF_WIDTH)] * 5 + [row(SB_WIDTH)] * 5
        out_shape = [wide(BF16), wide(F32), wide(BF16), wide(F32), wide(BF16),
                     narrow(BF16), narrow(F32), narrow(BF16), narrow(F32), narrow(BF16)]
    return pl.pallas_call(
        functools.partial(_proj_kernel, prompt_layout=prompt_layout),
        grid=(rows // tm,),
        in_specs=[row(D_MODEL), _const_spec((1, D_MODEL)), _const_spec((D_MODEL, QKV_COLS)),
                  _const_spec((1, COL_BLOCK)), _const_spec((1, COL_BLOCK)),
                  _const_spec((MXU_TILE, MXU_TILE)), tab, tab, tab],
        out_specs=out_specs,
        out_shape=out_shape,
        compiler_params=pltpu.CompilerParams(dimension_semantics=("arbitrary",),
                                             vmem_limit_bytes=VMEM_LIMIT),
        name="proj",
    )(x, gmix, w_qkv, qg, kg, gmat, cos, sa, sb)


def _fill_lead_kernel(lead_ref, big_ref, out_ref):
    del big_ref
    out_ref[...] = lead_ref[...]


def _fill_lead_tokens(big, lead_rows, *, nb):
    n = lead_rows.shape[0]
    per_batch = big.shape[0] // nb
    return pl.pallas_call(
        _fill_lead_kernel,
        grid=(nb,),
        in_specs=[_const_spec((n, LANES)), pl.BlockSpec(memory_space=pl.ANY)],
        out_specs=pl.BlockSpec((pl.Element(n), pl.Element(LANES)), lambda b: (b * per_batch, 0)),
        out_shape=jax.ShapeDtypeStruct(big.shape, big.dtype),
        input_output_aliases={1: 0},
        compiler_params=pltpu.CompilerParams(dimension_semantics=("arbitrary",)),
        name="fill_lead_tokens",
    )(lead_rows, big)


def _split_halves(q):
    lane = lax.broadcasted_iota(jnp.int32, q.shape, 1)
    zero = jnp.zeros_like(q)
    return jnp.where(lane < HEAD_DIM, q, zero), jnp.where(lane >= HEAD_DIM, q, zero)


def _diff_kernel(lq1_ref, lk1_ref, lq2_ref, lk2_ref, subg_ref, q_ref, kp_ref, vp_ref, km_ref, vm_ref,
                 o_ref, m_ref, acc_ref, s_ref, *, tq, tk, tkp, pref_valid, n_diag):
    i = pl.program_id(2)
    n_full = i * n_diag
    qs = _split_halves(q_ref[0])
    m_ref[...] = jnp.full(m_ref.shape, NEG, F32)
    acc_ref[...] = jnp.zeros(acc_ref.shape, F32)

    def step(nxt, cur):
        if cur is not None:
            slot, vtblk, mask, qlo = cur
            n = vtblk.shape[1]
            vt_ones = jnp.concatenate([vtblk, jnp.ones((ONES_ROWS, n), BF16)], axis=0)
        for a in range(2):
            if cur is not None:
                s = s_ref[slot, a, :n, qlo:]
                if mask is not None:
                    s = jnp.where(mask[:, qlo:], s, NEG)
                m_prev = m_ref[a, :, qlo:]
                m_new = jnp.maximum(m_prev, jnp.max(s, axis=0, keepdims=True))
                alpha = jnp.exp2(m_prev - m_new)
                p = jnp.exp2(s - m_new)
            if nxt is not None:
                kblk, nslot, nqlo = nxt
                s_ref[nslot, a, :kblk.shape[0], nqlo:] = lax.dot_general(
                    kblk, qs[a][nqlo:], _NT, preferred_element_type=F32)
            if cur is not None:
                acc_ref[a, :, qlo:] = (alpha * acc_ref[a, :, qlo:]
                                       + jnp.dot(vt_ones, p.astype(BF16), preferred_element_type=F32))
                m_ref[a, :, qlo:] = m_new

    def main_k(g):
        return km_ref[0, pl.ds(pl.multiple_of(g * tk, tk), tk), :]

    def main_vt(g):
        return vm_ref[0, :, pl.ds(pl.multiple_of(g * tk, tk), tk)]

    step((kp_ref[0], 1, 0), None)
    step((main_k(0), 0, 0), (1, vp_ref[0], lax.broadcasted_iota(jnp.int32, (tkp, tq), 0) < pref_valid, 0))

    def full_body(t, carry):
        step((main_k(2 * t + 1), 1, 0), (0, main_vt(2 * t), None, 0))
        step((main_k(2 * t + 2), 0, 0), (1, main_vt(2 * t + 1), None, 0))
        return carry
    lax.fori_loop(0, i * (n_diag // 2), full_body, 0)

    key_idx = lax.broadcasted_iota(jnp.int32, (tk, tq), 0)
    q_chunk = lax.broadcasted_iota(jnp.int32, (tk, tq), 1) // CHUNK
    for d in range(n_diag):
        g = n_full + d
        nxt = (main_k(g + 1), (d + 1) % 2, (d + 1) * tk) if d + 1 < n_diag else None
        step(nxt, (d % 2, main_vt(g), (key_idx + d * tk) // CHUNK <= q_chunk, d * tk))

    lam = (jnp.exp(jnp.sum(lq1_ref[...] * lk1_ref[...], axis=1, keepdims=True))
           - jnp.exp(jnp.sum(lq2_ref[...] * lk2_ref[...], axis=1, keepdims=True)) + LAM_INIT)
    o = (acc_ref[0, :LANES] / acc_ref[0, LANES:LANES + 1]
         - lam * (acc_ref[1, :LANES] / acc_ref[1, LANES:LANES + 1]))
    o = o * lax.rsqrt(jnp.mean(o * o, axis=0, keepdims=True) + EPS) * subg_ref[...] * (1.0 - LAM_INIT)
    o_ref[0] = o.T.astype(BF16)


def _diff_attn(lams, subg_col, q, kp, vtp, km, vtm, *, tq, tk, pref_valid):
    nb, tq_total, _ = q.shape
    tkp = kp.shape[1]
    t_main = km.shape[1]
    n_diag = tq // tk
    assert n_diag % 2 == 0, "main blocks are consumed in pairs"
    kernel = functools.partial(_diff_kernel, tq=tq, tk=tk, tkp=tkp, pref_valid=pref_valid, n_diag=n_diag)
    small = _const_spec((1, HEAD_DIM))
    return pl.pallas_call(
        kernel,
        grid=(nb, N_HEADS, tq_total // tq),
        in_specs=[small, small, small, small, _const_spec((LANES, 1)),
                  pl.BlockSpec((1, tq, LANES), lambda b, h, i: (b, i, h)),
                  pl.BlockSpec((1, tkp, LANES), lambda b, h, i: (0, 0, h)),
                  pl.BlockSpec((1, LANES, tkp), lambda b, h, i: (0, h, 0)),
                  pl.BlockSpec((1, t_main, LANES), lambda b, h, i: (b, 0, h)),
                  pl.BlockSpec((1, LANES, t_main), lambda b, h, i: (b, h, 0))],
        out_specs=pl.BlockSpec((1, tq, LANES), lambda b, h, i: (b, i, h)),
        out_shape=jax.ShapeDtypeStruct((nb, tq_total, DIFF_WIDTH), BF16),
        scratch_shapes=[pltpu.VMEM((2, 1, tq), F32), pltpu.VMEM((2, LANES + ONES_ROWS, tq), F32),
                        pltpu.VMEM((2, 2, max(tk, tkp), tq), F32)],
        compiler_params=pltpu.CompilerParams(dimension_semantics=("arbitrary",) * 3,
                                             vmem_limit_bytes=VMEM_LIMIT),
        name="diff_attn",
    )(*lams, subg_col, q, kp, vtp, km, vtm)


def _suffix_matrix(n):
    r = lax.broadcasted_iota(jnp.int32, (n + ONES_ROWS, n), 0)
    c = lax.broadcasted_iota(jnp.int32, (n + ONES_ROWS, n), 1)
    return jnp.where((c >= r) | (r == n), 1.0, 0.0).astype(BF16)


def _sb_kernel(q_ref, kp_ref, vp_ref, km_ref, vm_ref, o_ref, c_ref, acc_ref, z_ref,
               *, tq, tk, tkp, pref_valid, n_diag):
    i = pl.program_id(2)
    n_full = i * n_diag
    qs = _split_halves(q_ref[0])
    c_ref[...] = jnp.zeros(c_ref.shape, F32)
    acc_ref[...] = jnp.zeros(acc_ref.shape, F32)

    def step(nxt, cur):
        if nxt is not None:
            kblk, nslot, nqlo = nxt
            for a in range(2):
                z_ref[nslot, a, :kblk.shape[0], nqlo:] = lax.dot_general(
                    kblk, qs[a][nqlo:], _NT, preferred_element_type=F32)
        if cur is None:
            return
        slot, vtblk, mask, sfx_mat, qlo = cur
        n = vtblk.shape[1]
        if mask is not None:
            mask = mask[:, qlo:]
        for a in range(2):
            rows = slice(a * HEAD_DIM, (a + 1) * HEAD_DIM)
            u = z_ref[slot, a, :n, qlo:]
            neg_part = jnp.minimum(u, 0.0)
            d = neg_part - u
            log1m = d - jnp.log(1.0 + jnp.exp2(neg_part + d)) * LOG2E
            if mask is not None:
                log1m = jnp.where(mask, log1m, 0.0)
            sfx = jnp.dot(sfx_mat, log1m.astype(BF16), preferred_element_type=F32)
            t = u + sfx[:n]
            if mask is not None:
                t = jnp.where(mask, t, NEG)
            pv = jnp.dot(vtblk[rows, :], jnp.exp2(t).astype(BF16), preferred_element_type=F32)
            acc_ref[rows, qlo:] += pv * jnp.exp2(c_ref[a, :, qlo:])
            c_ref[a, :, qlo:] += sfx[n:n + 1]

    def main_k(g):
        return km_ref[0, pl.ds(pl.multiple_of(g * tk, tk), tk), :]

    def main_vt(g):
        return vm_ref[0, :, pl.ds(pl.multiple_of(g * tk, tk), tk)]

    sfx_main = _suffix_matrix(tk)
    sfx_pref = sfx_main if tkp == tk else _suffix_matrix(tkp)
    key = lax.broadcasted_iota(jnp.int32, (tk, tq), 0)
    qry = lax.broadcasted_iota(jnp.int32, (tk, tq), 1)

    step((kp_ref[0], 2, 0), None)
    step((main_k(n_full + n_diag - 1), 0, (n_diag - 1) * tk), None)
    for d in reversed(range(n_diag)):
        s = (n_diag - 1 - d) % 2
        if d > 0:
            nxt = (main_k(n_full + d - 1), 1 - s, (d - 1) * tk)
        else:
            nxt = (main_k(jnp.maximum(n_full - 1, 0)), 1 - s, 0)
        step(nxt, (s, main_vt(n_full + d), key + d * tk < qry, sfx_main, d * tk))

    def live():
        return jnp.max(c_ref[...]) > SKIP_BELOW

    def full_body(t):
        g = n_full - 1 - 2 * t
        step((main_k(g - 1), 1, 0), (0, main_vt(g), None, sfx_main, 0))
        step((main_k(jnp.maximum(g - 2, 0)), 0, 0), (1, main_vt(g - 1), None, sfx_main, 0))
        return t + 1
    n_pairs = i * (n_diag // 2)
    lax.while_loop(lambda t: jnp.logical_and(t < n_pairs, live()), full_body, jnp.int32(0))

    @pl.when(live())
    def _():
        step(None, (2, vp_ref[0], lax.broadcasted_iota(jnp.int32, (tkp, tq), 0) < pref_valid, sfx_pref, 0))
    o_ref[0] = acc_ref[...].T.astype(BF16)


def _sb_attn(q, kp, vtp, km, vtm, *, tq, tk, pref_valid):
    nb, tq_total, _ = q.shape
    tkp = kp.shape[1]
    t_main = km.shape[1]
    n_diag = tq // tk
    assert n_diag % 2 == 0, "main blocks are consumed in pairs"
    kernel = functools.partial(_sb_kernel, tq=tq, tk=tk, tkp=tkp, pref_valid=pref_valid, n_diag=n_diag)
    return pl.pallas_call(
        kernel,
        grid=(nb, SB_WIDTH // LANES, tq_total // tq),
        in_specs=[pl.BlockSpec((1, tq, LANES), lambda b, h, i: (b, i, h)),
                  pl.BlockSpec((1, tkp, LANES), lambda b, h, i: (0, 0, h)),
                  pl.BlockSpec((1, LANES, tkp), lambda b, h, i: (0, h, 0)),
                  pl.BlockSpec((1, t_main, LANES), lambda b, h, i: (b, 0, h)),
                  pl.BlockSpec((1, LANES, t_main), lambda b, h, i: (b, h, 0))],
        out_specs=pl.BlockSpec((1, tq, LANES), lambda b, h, i: (b, i, h)),
        out_shape=jax.ShapeDtypeStruct((nb, tq_total, SB_WIDTH), BF16),
        scratch_shapes=[pltpu.VMEM((2, 1, tq), F32), pltpu.VMEM((LANES, tq), F32),
                        pltpu.VMEM((3, 2, max(tk, tkp), tq), F32)],
        compiler_params=pltpu.CompilerParams(dimension_semantics=("arbitrary",) * 3,
                                             vmem_limit_bytes=VMEM_LIMIT),
        name="sb_attn",
    )(q, kp, vtp, km, vtm)


def _diff_decode_kernel(lq1_ref, lk1_ref, lq2_ref, lk2_ref, subg_ref, q_ref, ck_ref, cv_ref, kn_ref, vn_ref,
                        o_ref, m_ref, l_ref, acc_ref, *, chunk):
    kc = pl.program_id(1)
    nq = q_ref.shape[0]

    @pl.when(kc == 0)
    def _():
        m_ref[...] = jnp.full(m_ref.shape, NEG, F32)
        l_ref[...] = jnp.zeros(l_ref.shape, F32)
        acc_ref[...] = jnp.zeros(acc_ref.shape, F32)

    def update(h, k, v):
        n = k.shape[0]
        qs = _split_halves(q_ref[:, h * LANES:(h + 1) * LANES])
        v_ones = jnp.concatenate([v, jnp.ones((n, LANES), BF16)], axis=1)
        ps, alphas = [], []
        for a in range(2):
            s = lax.dot_general(qs[a], k, _NT, preferred_element_type=F32)
            m_prev = m_ref[2 * h + a]
            m_new = jnp.maximum(m_prev, jnp.max(s, axis=1, keepdims=True))
            alphas.append(jnp.exp2(m_prev - m_new))
            ps.append(jnp.exp2(s - m_new).astype(BF16))
            m_ref[2 * h + a] = m_new
        pv = jnp.dot(jnp.concatenate(ps, axis=0), v_ones, preferred_element_type=F32)
        for a in range(2):
            part = pv[a * nq:(a + 1) * nq]
            acc_ref[2 * h + a] = alphas[a] * acc_ref[2 * h + a] + part[:, :LANES]
            l_ref[2 * h + a] = alphas[a] * l_ref[2 * h + a] + part[:, LANES:]

    for h in range(N_HEADS):
        update(h, ck_ref[pl.ds(h, chunk, stride=N_HEADS), :].astype(BF16),
               cv_ref[pl.ds(h, chunk, stride=N_HEADS), :].astype(BF16))

    @pl.when(kc == pl.num_programs(1) - 1)
    def _():
        lam = (jnp.exp(jnp.sum(lq1_ref[...] * lk1_ref[...], axis=1, keepdims=True))
               - jnp.exp(jnp.sum(lq2_ref[...] * lk2_ref[...], axis=1, keepdims=True)) + LAM_INIT)
        for h in range(N_HEADS):
            cols = slice(h * LANES, (h + 1) * LANES)
            update(h, kn_ref[:, cols], vn_ref[:, cols])
            o = acc_ref[2 * h] / l_ref[2 * h] - lam * (acc_ref[2 * h + 1] / l_ref[2 * h + 1])
            o_ref[:, cols] = (_rms(o) * subg_ref[...] * (1.0 - LAM_INIT)).astype(BF16)


def _diff_decode(lams, subg_row, q, ck, cv, kn, vn, *, chunk):
    nb, nq, _ = q.shape
    n_chunks = ck.shape[1] // (chunk * N_HEADS)
    small = _const_spec((1, HEAD_DIM))
    tok = pl.BlockSpec((None, nq, DIFF_WIDTH), lambda b, c: (b, 0, 0))
    cache = pl.BlockSpec((None, chunk * N_HEADS, LANES), lambda b, c: (b, c, 0))
    return pl.pallas_call(
        functools.partial(_diff_decode_kernel, chunk=chunk),
        grid=(nb, n_chunks),
        in_specs=[small, small, small, small, _const_spec((1, LANES)), tok, cache, cache, tok, tok],
        out_specs=tok,
        out_shape=jax.ShapeDtypeStruct((nb, nq, DIFF_WIDTH), BF16),
        scratch_shapes=[pltpu.VMEM((2 * N_HEADS, nq, 1), F32), pltpu.VMEM((2 * N_HEADS, nq, LANES), F32),
                        pltpu.VMEM((2 * N_HEADS, nq, LANES), F32)],
        compiler_params=pltpu.CompilerParams(dimension_semantics=("arbitrary",) * 2,
                                             vmem_limit_bytes=VMEM_LIMIT),
        name="diff_decode",
    )(*lams, subg_row, q, ck, cv, kn, vn)


SB_BLOCK = 256


def _sb_decode_kernel(q_ref, ckt_ref, cvt_ref, knt_ref, vnt_ref, o_ref):
    nq = q_ref.shape[0]
    past = ckt_ref.shape[1]
    n_blk = past // SB_BLOCK
    def suffix_cols(n):
        r = lax.broadcasted_iota(jnp.int32, (n, n + LANES), 0)
        c = lax.broadcasted_iota(jnp.int32, (n, n + LANES), 1)
        return jnp.where((r > c) | (c >= n), 1.0, 0.0).astype(BF16)
    sfx_blk = suffix_cols(SB_BLOCK)
    sfx_new = suffix_cols(LANES)
    row = lax.broadcasted_iota(jnp.int32, (nq, LANES), 0)
    lane = lax.broadcasted_iota(jnp.int32, (nq, LANES), 1)
    new_mask = lane < row

    def log1m_of(u):
        nu = -u
        return jnp.minimum(nu, 0.0) - jnp.log(1.0 + jnp.exp2(jnp.minimum(u, nu))) * LOG2E

    for j in range(ckt_ref.shape[0] // LANES):
        rows = slice(j * LANES, (j + 1) * LANES)
        kt = ckt_ref[rows, :].astype(BF16)
        vt = cvt_ref[rows, :].astype(BF16)
        qs = _split_halves(q_ref[:, rows])
        ws, wns = [], []
        for a in range(2):
            un = jnp.dot(qs[a], knt_ref[rows, :], preferred_element_type=F32)
            ln = jnp.where(new_mask, log1m_of(un), 0.0)
            sn = jnp.dot(ln.astype(BF16), sfx_new, preferred_element_type=F32)
            wns.append(jnp.exp2(jnp.where(new_mask, un + ln + sn[:, :LANES], NEG)).astype(BF16))
            carry = sn[:, LANES:]
            u = jnp.dot(qs[a], kt, preferred_element_type=F32)
            l1m = log1m_of(u)
            stacked = jnp.concatenate(
                [l1m[:, b * SB_BLOCK:(b + 1) * SB_BLOCK] for b in range(n_blk)], axis=0).astype(BF16)
            sfx = jnp.dot(stacked, sfx_blk, preferred_element_type=F32)
            ts = [None] * n_blk
            for b in reversed(range(n_blk)):
                cols = slice(b * SB_BLOCK, (b + 1) * SB_BLOCK)
                part = sfx[b * nq:(b + 1) * nq]
                c2 = jnp.concatenate([carry] * (SB_BLOCK // LANES), axis=1)
                ts[b] = u[:, cols] + l1m[:, cols] + part[:, :SB_BLOCK] + c2
                carry = carry + part[:, SB_BLOCK:]
            ws.append(jnp.exp2(jnp.concatenate(ts, axis=1)).astype(BF16))
        o = (lax.dot_general(jnp.concatenate(ws, axis=0), vt, _NT, preferred_element_type=F32)
             + lax.dot_general(jnp.concatenate(wns, axis=0), vnt_ref[rows, :], _NT,
                               preferred_element_type=F32))
        o_ref[:, rows] = jnp.where(lane < HEAD_DIM, o[:nq], o[nq:]).astype(BF16)


def _sb_decode(q, ckt, cvt, knt, vnt):
    nb, nq, _ = q.shape
    assert nq <= LANES
    past = ckt.shape[2]
    width = 2 * LANES
    tok = pl.BlockSpec((None, nq, width), lambda b, g: (b, 0, g))
    cache = pl.BlockSpec((None, width, past), lambda b, g: (b, g, 0))
    new = pl.BlockSpec((None, width, LANES), lambda b, g: (b, g, 0))
    return pl.pallas_call(
        _sb_decode_kernel,
        grid=(nb, SB_WIDTH // width),
        in_specs=[tok, cache, cache, new, new],
        out_specs=tok,
        out_shape=jax.ShapeDtypeStruct((nb, nq, SB_WIDTH), BF16),
        compiler_params=pltpu.CompilerParams(dimension_semantics=("arbitrary",) * 2,
                                             vmem_limit_bytes=VMEM_LIMIT),
        name="sb_decode",
    )(q, ckt, cvt, knt, vnt)


def _out_kernel(x_ref, od_ref, os_ref, gmix_ref, wg_ref, wdo_ref, wso_ref, wo_ref, gffn_ref,
                w1_ref, w2_ref, y_ref):
    x = x_ref[...]
    h = (_rms(x) * gmix_ref[...]).astype(BF16)
    gate = jax.nn.sigmoid(jnp.dot(h, wg_ref[...], preferred_element_type=F32))
    a = jnp.dot(od_ref[...], wdo_ref[...], preferred_element_type=F32)
    b = jnp.dot(os_ref[...], wso_ref[...], preferred_element_type=F32)
    merged = (gate[:, :D_MODEL] * a + gate[:, D_MODEL:] * b).astype(BF16)
    x1 = x + jnp.dot(merged, wo_ref[...], preferred_element_type=F32)
    h2 = (_rms(x1) * gffn_ref[...]).astype(BF16)
    y = x1
    for c in range(D_FF // D_MODEL):
        sl = slice(c * D_MODEL, (c + 1) * D_MODEL)
        f = jnp.maximum(jnp.dot(h2, w1_ref[:, sl], preferred_element_type=F32), 0.0)
        y = y + jnp.dot((f * f).astype(BF16), w2_ref[sl, :], preferred_element_type=F32)
    y_ref[...] = y


def _out(x, od, osb, gmix, wg, wdo, wso, wo, gffn, w1, w2, *, tm):
    rows = x.shape[0]
    row = lambda w: pl.BlockSpec((tm, w), lambda i: (i, 0))
    return pl.pallas_call(
        _out_kernel,
        grid=(rows // tm,),
        in_specs=[row(D_MODEL), row(DIFF_WIDTH), row(SB_WIDTH), _const_spec((1, D_MODEL)),
                  _const_spec((D_MODEL, 2 * D_MODEL)), _const_spec((DIFF_WIDTH, D_MODEL)),
                  _const_spec((SB_WIDTH, D_MODEL)), _const_spec((D_MODEL, D_MODEL)),
                  _const_spec((1, D_MODEL)), _const_spec((D_MODEL, D_FF)), _const_spec((D_FF, D_MODEL))],
        out_specs=row(D_MODEL),
        out_shape=jax.ShapeDtypeStruct((rows, D_MODEL), F32),
        compiler_params=pltpu.CompilerParams(dimension_semantics=("arbitrary",),
                                             vmem_limit_bytes=VMEM_LIMIT),
        name="out",
    )(x, od, osb, gmix, wg, wdo, wso, wo, gffn, w1, w2)


def _rope_tables(pos):
    half = ROT_DIM // 2
    d = jnp.arange(LANES, dtype=jnp.int32) % HEAD_DIM
    inv = ROPE_THETA ** (-(2 * (d % half)).astype(F32) / ROT_DIM)
    ang = pos.astype(F32)[:, None] * inv[None, :]
    cos, sin = jnp.cos(ang), jnp.sin(ang)
    cos_t = jnp.where(d < ROT_DIM, cos, 1.0)
    sa_t = jnp.where(d < half, -sin, 0.0)
    sb_t = jnp.where((d >= half) & (d < ROT_DIM), sin, 0.0)
    return cos_t, sa_t, sb_t


def kernel(x_prompt, x_sample, cache_diff_k, cache_diff_v, cache_sb_k, cache_sb_v, meta_tokens,
           g_mix, w_in, q_norm_g, k_norm_g, lam_q1, lam_k1, lam_q2, lam_k2, sub_g,
           w_diff_out, w_sb_out, w_out, g_ffn, w_ff1, w_ff2):
    nb, seq, _ = x_prompt.shape
    db, dseq, _ = x_sample.shape
    past = cache_diff_k.shape[2]
    lyr = 0

    w_qkv, w_gate = w_in[lyr, :, :QKV_COLS].astype(BF16), w_in[lyr, :, QKV_COLS:].astype(BF16)
    wdo, wso, wo = (w_diff_out[lyr].astype(BF16), w_sb_out[lyr].astype(BF16), w_out[lyr].astype(BF16))
    w1, w2 = w_ff1[lyr].astype(BF16), w_ff2[lyr].astype(BF16)
    gmix = g_mix[lyr].reshape(1, D_MODEL)
    gffn = g_ffn[lyr].reshape(1, D_MODEL)
    qg = jnp.tile(q_norm_g[lyr], COL_BLOCK // HEAD_DIM).reshape(1, COL_BLOCK)
    kg = jnp.tile(k_norm_g[lyr], COL_BLOCK // HEAD_DIM).reshape(1, COL_BLOCK)
    subg_col = sub_g[lyr].reshape(LANES, 1)
    lams = [t[lyr].reshape(1, HEAD_DIM) for t in (lam_q1, lam_k1, lam_q2, lam_k2)]
    grp = jnp.arange(MXU_TILE, dtype=jnp.int32) // HEAD_DIM
    gmat = jnp.where(grp[:, None] == grp[None, :], 1.0 / HEAD_DIM, 0.0).astype(BF16)

    main_pos = N_META + jnp.arange(seq, dtype=jnp.int32)
    small_pos = jnp.concatenate([jnp.arange(N_META, dtype=jnp.int32),
                                 jnp.tile(past + jnp.arange(dseq, dtype=jnp.int32), db)])
    x_main = x_prompt.reshape(nb * seq, D_MODEL)
    x_small = jnp.concatenate([meta_tokens.astype(F32), x_sample.reshape(db * dseq, D_MODEL)], axis=0)
    pm = _proj(x_main, gmix, w_qkv, qg, kg, gmat, *_rope_tables(main_pos), tm=512, prompt_layout=True,
               lead=N_META)
    ps = _proj(x_small, gmix, w_qkv, qg, kg, gmat, *_rope_tables(small_pos), tm=x_small.shape[0],
               prompt_layout=False)
    qd_m, kd_hm, kdb_m, vd_hm, vdt_m, qs_m, kst_m, ksb_m, vstf_m, vst_m = pm
    qd_s, kd_s, kdb_s, vd_s, vdb_s, qs_s, ks_s, ksb_s, vs_s, vsb_s = ps

    def bt(a, n, t):
        return a.reshape(n, t, a.shape[-1])

    def meta_keys(a):
        return jnp.pad(a[:N_META], ((0, LANES - N_META), (0, 0)))[None]

    def meta_vals_t(a):
        return jnp.swapaxes(meta_keys(a), 1, 2)

    od_p = _diff_attn(lams, subg_col, bt(qd_m, nb, seq), meta_keys(kdb_s), meta_vals_t(vdb_s),
                      bt(kdb_m, nb, seq), vdt_m,
                      tq=2048, tk=512, pref_valid=N_META)
    os_p = _sb_attn(bt(qs_m, nb, seq), meta_keys(ksb_s), meta_vals_t(vsb_s),
                    bt(ksb_m, nb, seq), vst_m,
                    tq=2048, tk=256, pref_valid=N_META)

    smp = lambda a: bt(a[N_META:], db, dseq)
    cdk = cache_diff_k[lyr].reshape(db, past * N_HEADS, LANES)
    cdv = cache_diff_v[lyr].reshape(db, past * N_HEADS, LANES)
    od_s = _diff_decode(lams, sub_g[lyr].reshape(1, LANES), smp(qd_s), cdk, cdv, smp(kdb_s), smp(vdb_s),
                        chunk=2048)
    sb_t = lambda c: jnp.transpose(c[lyr], (0, 2, 3, 1)).reshape(db, SB_WIDTH, past)
    new_t = lambda a: jnp.pad(jnp.swapaxes(smp(a), 1, 2), ((0, 0), (0, 0), (0, LANES - dseq)))
    os_s = _sb_decode(smp(qs_s), sb_t(cache_sb_k), sb_t(cache_sb_v), new_t(ksb_s), new_t(vsb_s))

    y_p = _out(x_main, od_p.reshape(nb * seq, DIFF_WIDTH), os_p.reshape(nb * seq, SB_WIDTH),
               gmix, w_gate, wdo, wso, wo, gffn, w1, w2, tm=512)
    y_s = _out(x_sample.reshape(db * dseq, D_MODEL), od_s.reshape(db * dseq, DIFF_WIDTH),
               os_s.reshape(db * dseq, SB_WIDTH), gmix, w_gate, wdo, wso, wo, gffn, w1, w2, tm=db * dseq)

    def diff_prompt_cache(head_major, small):
        meta = small[:N_META].reshape(N_META * N_HEADS, 2 * HEAD_DIM)
        full = _fill_lead_tokens(head_major, meta, nb=nb)
        return full.reshape(1, nb, seq + N_META, N_HEADS, 2 * HEAD_DIM)

    def sb_prompt_cache(main_t, small):
        meta_t = jnp.broadcast_to(small[:N_META].T[None], (nb, SB_WIDTH, N_META))
        full = jnp.concatenate([meta_t, main_t], axis=2).reshape(nb, N_HEADS, HEAD_DIM, seq + N_META)
        return jnp.transpose(full, (0, 3, 1, 2))[None]

    def sample_cache(small, dim):
        return small[N_META:].reshape(1, db, dseq, N_HEADS, dim)

    return (y_p.reshape(nb, seq, D_MODEL), y_s.reshape(db, dseq, D_MODEL),
            diff_prompt_cache(kd_hm, kd_s), diff_prompt_cache(vd_hm, vd_s),
            sb_prompt_cache(kst_m, ks_s), sb_prompt_cache(vstf_m, vs_s),
            sample_cache(kd_s, 2 * HEAD_DIM), sample_cache(vd_s, 2 * HEAD_DIM),
            sample_cache(ks_s, HEAD_DIM), sample_cache(vs_s, HEAD_DIM))
```

```python
import functools
import math

import jax
import jax.numpy as jnp
from jax import lax
from jax.experimental import pallas as pl
from jax.experimental.pallas import tpu as pltpu

F32 = jnp.float32
BF16 = jnp.bfloat16

D_MODEL = 1024
N_META = 16
CHUNK = 64
N_HEADS = 8
HEAD_DIM = 64
DIFF_WIDTH = N_HEADS * 2 * HEAD_DIM
SB_WIDTH = N_HEADS * HEAD_DIM
QKV_COLS = 3 * DIFF_WIDTH + 3 * SB_WIDTH
D_FF = 4 * D_MODEL
ROT_DIM = HEAD_DIM // 4
ROPE_THETA = 500000.0
EPS = 1e-6
NEG = -1e30
SKIP_BELOW = -160.0
LOG2E = math.log2(math.e)
Q_SCALE = HEAD_DIM ** -0.5 * LOG2E
ONES_ROWS = 16
LAM_INIT = 0.8 - 0.6 * math.exp(-0.3 * 0)

LANES = 128
MXU_TILE = 256
COL_BLOCK = 512
VMEM_LIMIT = 56 * 1024 * 1024

_NT = (((1,), (1,)), ((), ()))


def _rms(x):
    return x * lax.rsqrt(jnp.mean(x * x, axis=-1, keepdims=True) + EPS)


def _const_spec(shape):
    return pl.BlockSpec(shape, lambda *_: (0,) * len(shape), pipeline_mode=pl.Buffered(1))


def _proj_kernel(x_ref, gmix_ref, w_ref, qg_ref, kg_ref, gmat_ref, cos_ref, sa_ref, sb_ref,
                 qd_ref, kd_ref, kdb_ref, vd_ref, vdb_ref, qs_ref, ks_ref, ksb_ref, vs_ref, vsb_ref,
                 *, prompt_layout):
    tm = x_ref.shape[0]
    heads_per_block = COL_BLOCK // LANES

    def store_diff(ref, j, y):
        if not prompt_layout:
            ref[:, j * COL_BLOCK:(j + 1) * COL_BLOCK] = y
            return
        for hh in range(heads_per_block):
            ref[pl.ds(j * heads_per_block + hh, tm, stride=N_HEADS), :] = y[:, hh * LANES:(hh + 1) * LANES]

    h = (_rms(x_ref[...]) * gmix_ref[...]).astype(BF16)
    cos = cos_ref[...]
    sa = sa_ref[...]
    sb = sb_ref[...]

    def col(j):
        return jnp.dot(h, w_ref[:, j * COL_BLOCK:(j + 1) * COL_BLOCK], preferred_element_type=F32)

    def normed_rot(y, g):
        sq = (y * y).astype(BF16)
        msq = jnp.concatenate(
            [jnp.dot(sq[:, c * MXU_TILE:(c + 1) * MXU_TILE], gmat_ref[...], preferred_element_type=F32)
             for c in range(COL_BLOCK // MXU_TILE)], axis=1)
        yn = y * lax.rsqrt(msq + EPS) * g
        parts = []
        for c in range(COL_BLOCK // LANES):
            t = yn[:, c * LANES:(c + 1) * LANES]
            parts.append(t * cos + pltpu.roll(t, LANES - ROT_DIM // 2, 1) * sa
                         + pltpu.roll(t, ROT_DIM // 2, 1) * sb)
        return jnp.concatenate(parts, axis=1)

    for j in range(2):
        sl = slice(j * COL_BLOCK, (j + 1) * COL_BLOCK)
        q = normed_rot(col(j), qg_ref[...])
        qd_ref[:, sl] = (q * Q_SCALE).astype(BF16)
        k = normed_rot(col(2 + j), kg_ref[...])
        store_diff(kd_ref, j, k)
        kdb_ref[:, sl] = k.astype(BF16)
        v = col(4 + j)
        store_diff(vd_ref, j, v)
        if prompt_layout:
            vdb_ref[sl, :] = v.T.astype(BF16)
        else:
            vdb_ref[:, sl] = v.astype(BF16)
    qs_ref[...] = (col(6) * Q_SCALE).astype(BF16)
    k = col(7)
    ksb_ref[...] = k.astype(BF16)
    v = col(8)
    if prompt_layout:
        ks_ref[...] = k.T
        vt = v.T
        vs_ref[...] = vt
        vsb_ref[...] = vt.astype(BF16)
    else:
        ks_ref[...] = k
        vs_ref[...] = v
        vsb_ref[...] = v.astype(BF16)


def _proj(x, gmix, w_qkv, qg, kg, gmat, cos, sa, sb, *, tm, prompt_layout, lead=0):
    rows = x.shape[0]
    seq = cos.shape[0]
    n_pos_tiles = seq // tm
    row = lambda w: pl.BlockSpec((tm, w), lambda i: (i, 0))
    tab = pl.BlockSpec((tm, LANES), lambda i: (i % n_pos_tiles, 0))
    wide = lambda dt: jax.ShapeDtypeStruct((rows, DIFF_WIDTH), dt)
    narrow = lambda dt: jax.ShapeDtypeStruct((rows, SB_WIDTH), dt)
    if prompt_layout:
        nb = rows // seq
        t_spec = lambda w: pl.BlockSpec((None, w, tm), lambda i: (i // n_pos_tiles, 0, i % n_pos_tiles))
        t_shape = lambda w, dt: jax.ShapeDtypeStruct((nb, w, seq), dt)
        hm_spec = pl.BlockSpec(
            (pl.Element(tm * N_HEADS), pl.Element(LANES)),
            lambda i: (((i // n_pos_tiles) * (seq + lead) + lead + (i % n_pos_tiles) * tm) * N_HEADS, 0))
        hm_shape = jax.ShapeDtypeStruct((nb * (seq + lead) * N_HEADS, LANES), F32)
        out_specs = [row(DIFF_WIDTH), hm_spec, row(DIFF_WIDTH), hm_spec, t_spec(DIFF_WIDTH),
                     row(SB_WIDTH), t_spec(SB_WIDTH), row(SB_WIDTH), t_spec(SB_WIDTH), t_spec(SB_WIDTH)]
        out_shape = [wide(BF16), hm_shape, wide(BF16), hm_shape, t_shape(DIFF_WIDTH, BF16),
                     narrow(BF16), t_shape(SB_WIDTH, F32), narrow(BF16), t_shape(SB_WIDTH, F32),
                     t_shape(SB_WIDTH, BF16)]
    else:
        out_specs = [row(DIFF_WIDTH)] * 5 + [row(SB_WIDTH)] * 5
        out_shape = [wide(BF16), wide(F32), wide(BF16), wide(F32), wide(BF16),
                     narrow(BF16), narrow(F32), narrow(BF16), narrow(F32), narrow(BF16)]
    return pl.pallas_call(
        functools.partial(_proj_kernel, prompt_layout=prompt_layout),
        grid=(rows // tm,),
        in_specs=[row(D_MODEL), _const_spec((1, D_MODEL)), _const_spec((D_MODEL, QKV_COLS)),
                  _const_spec((1, COL_BLOCK)), _const_spec((1, COL_BLOCK)),
                  _const_spec((MXU_TILE, MXU_TILE)), tab, tab, tab],
        out_specs=out_specs,
        out_shape=out_shape,
        compiler_params=pltpu.CompilerParams(dimension_semantics=("arbitrary",),
                                             vmem_limit_bytes=VMEM_LIMIT),
        name="proj",
    )(x, gmix, w_qkv, qg, kg, gmat, cos, sa, sb)


def _fill_lead_kernel(lead_ref, big_ref, out_ref):
    del big_ref
    out_ref[...] = lead_ref[...]


def _fill_lead_tokens(big, lead_rows, *, nb):
    n = lead_rows.shape[0]
    per_batch = big.shape[0] // nb
    return pl.pallas_call(
        _fill_lead_kernel,
        grid=(nb,),
        in_specs=[_const_spec((n, LANES)), pl.BlockSpec(memory_space=pl.ANY)],
        out_specs=pl.BlockSpec((pl.Element(n), pl.Element(LANES)), lambda b: (b * per_batch, 0)),
        out_shape=jax.ShapeDtypeStruct(big.shape, big.dtype),
        input_output_aliases={1: 0},
        compiler_params=pltpu.CompilerParams(dimension_semantics=("arbitrary",)),
        name="fill_lead_tokens",
    )(lead_rows, big)


def _split_halves(q):
    lane = lax.broadcasted_iota(jnp.int32, q.shape, 1)
    zero = jnp.zeros_like(q)
    return jnp.where(lane < HEAD_DIM, q, zero), jnp.where(lane >= HEAD_DIM, q, zero)


def _diff_kernel(lq1_ref, lk1_ref, lq2_ref, lk2_ref, subg_ref, q_ref, kp_ref, vp_ref, km_ref, vm_ref,
                 o_ref, m_ref, acc_ref, s_ref, *, tq, tk, tkp, pref_valid, n_diag):
    i = pl.program_id(2)
    n_full = i * n_diag
    qs = _split_halves(q_ref[0])
    m_ref[...] = jnp.full(m_ref.shape, NEG, F32)
    acc_ref[...] = jnp.zeros(acc_ref.shape, F32)

    def step(nxt, cur):
        if cur is not None:
            slot, vtblk, mask, qlo = cur
            n = vtblk.shape[1]
            vt_ones = jnp.concatenate([vtblk, jnp.ones((ONES_ROWS, n), BF16)], axis=0)
        for a in range(2):
            if cur is not None:
                s = s_ref[slot, a, :n, qlo:]
                if mask is not None:
                    s = jnp.where(mask[:, qlo:], s, NEG)
                m_prev = m_ref[a, :, qlo:]
                m_new = jnp.maximum(m_prev, jnp.max(s, axis=0, keepdims=True))
                alpha = jnp.exp2(m_prev - m_new)
                p = jnp.exp2(s - m_new)
            if nxt is not None:
                kblk, nslot, nqlo = nxt
                s_ref[nslot, a, :kblk.shape[0], nqlo:] = lax.dot_general(
                    kblk, qs[a][nqlo:], _NT, preferred_element_type=F32)
            if cur is not None:
                acc_ref[a, :, qlo:] = (alpha * acc_ref[a, :, qlo:]
                                       + jnp.dot(vt_ones, p.astype(BF16), preferred_element_type=F32))
                m_ref[a, :, qlo:] = m_new

    def main_k(g):
        return km_ref[0, pl.ds(pl.multiple_of(g * tk, tk), tk), :]

    def main_vt(g):
        return vm_ref[0, :, pl.ds(pl.multiple_of(g * tk, tk), tk)]

    step((kp_ref[0], 1, 0), None)
    step((main_k(0), 0, 0), (1, vp_ref[0], lax.broadcasted_iota(jnp.int32, (tkp, tq), 0) < pref_valid, 0))

    def full_body(t, carry):
        step((main_k(2 * t + 1), 1, 0), (0, main_vt(2 * t), None, 0))
        step((main_k(2 * t + 2), 0, 0), (1, main_vt(2 * t + 1), None, 0))
        return carry
    lax.fori_loop(0, i * (n_diag // 2), full_body, 0)

    key_idx = lax.broadcasted_iota(jnp.int32, (tk, tq), 0)
    q_chunk = lax.broadcasted_iota(jnp.int32, (tk, tq), 1) // CHUNK
    for d in range(n_diag):
        g = n_full + d
        nxt = (main_k(g + 1), (d + 1) % 2, (d + 1) * tk) if d + 1 < n_diag else None
        step(nxt, (d % 2, main_vt(g), (key_idx + d * tk) // CHUNK <= q_chunk, d * tk))

    lam = (jnp.exp(jnp.sum(lq1_ref[...] * lk1_ref[...], axis=1, keepdims=True))
           - jnp.exp(jnp.sum(lq2_ref[...] * lk2_ref[...], axis=1, keepdims=True)) + LAM_INIT)
    o = (acc_ref[0, :LANES] / acc_ref[0, LANES:LANES + 1]
         - lam * (acc_ref[1, :LANES] / acc_ref[1, LANES:LANES + 1]))
    o = o * lax.rsqrt(jnp.mean(o * o, axis=0, keepdims=True) + EPS) * subg_ref[...] * (1.0 - LAM_INIT)
    o_ref[0] = o.T.astype(BF16)


def _diff_attn(lams, subg_col, q, kp, vtp, km, vtm, *, tq, tk, pref_valid):
    nb, tq_total, _ = q.shape
    tkp = kp.shape[1]
    t_main = km.shape[1]
    n_diag = tq // tk
    assert n_diag % 2 == 0, "main blocks are consumed in pairs"
    kernel = functools.partial(_diff_kernel, tq=tq, tk=tk, tkp=tkp, pref_valid=pref_valid, n_diag=n_diag)
    small = _const_spec((1, HEAD_DIM))
    return pl.pallas_call(
        kernel,
        grid=(nb, N_HEADS, tq_total // tq),
        in_specs=[small, small, small, small, _const_spec((LANES, 1)),
                  pl.BlockSpec((1, tq, LANES), lambda b, h, i: (b, i, h)),
                  pl.BlockSpec((1, tkp, LANES), lambda b, h, i: (0, 0, h)),
                  pl.BlockSpec((1, LANES, tkp), lambda b, h, i: (0, h, 0)),
                  pl.BlockSpec((1, t_main, LANES), lambda b, h, i: (b, 0, h)),
                  pl.BlockSpec((1, LANES, t_main), lambda b, h, i: (b, h, 0))],
        out_specs=pl.BlockSpec((1, tq, LANES), lambda b, h, i: (b, i, h)),
        out_shape=jax.ShapeDtypeStruct((nb, tq_total, DIFF_WIDTH), BF16),
        scratch_shapes=[pltpu.VMEM((2, 1, tq), F32), pltpu.VMEM((2, LANES + ONES_ROWS, tq), F32),
                        pltpu.VMEM((2, 2, max(tk, tkp), tq), F32)],
        compiler_params=pltpu.CompilerParams(dimension_semantics=("arbitrary",) * 3,
                                             vmem_limit_bytes=VMEM_LIMIT),
        name="diff_attn",
    )(*lams, subg_col, q, kp, vtp, km, vtm)


def _suffix_matrix(n):
    r = lax.broadcasted_iota(jnp.int32, (n + ONES_ROWS, n), 0)
    c = lax.broadcasted_iota(jnp.int32, (n + ONES_ROWS, n), 1)
    return jnp.where((c >= r) | (r == n), 1.0, 0.0).astype(BF16)


def _sb_kernel(q_ref, kp_ref, vp_ref, km_ref, vm_ref, o_ref, c_ref, acc_ref, z_ref,
               *, tq, tk, tkp, pref_valid, n_diag):
    i = pl.program_id(2)
    n_full = i * n_diag
    qs = _split_halves(q_ref[0])
    c_ref[...] = jnp.zeros(c_ref.shape, F32)
    acc_ref[...] = jnp.zeros(acc_ref.shape, F32)

    def step(nxt, cur):
        if nxt is not None:
            kblk, nslot, nqlo = nxt
            for a in range(2):
                z_ref[nslot, a, :kblk.shape[0], nqlo:] = lax.dot_general(
                    kblk, qs[a][nqlo:], _NT, preferred_element_type=F32)
        if cur is None:
            return
        slot, vtblk, mask, sfx_mat, qlo = cur
        n = vtblk.shape[1]
        if mask is not None:
            mask = mask[:, qlo:]
        for a in range(2):
            rows = slice(a * HEAD_DIM, (a + 1) * HEAD_DIM)
            u = z_ref[slot, a, :n, qlo:]
            neg_part = jnp.minimum(u, 0.0)
            d = neg_part - u
            log1m = d - jnp.log(1.0 + jnp.exp2(neg_part + d)) * LOG2E
            if mask is not None:
                log1m = jnp.where(mask, log1m, 0.0)
            sfx = jnp.dot(sfx_mat, log1m.astype(BF16), preferred_element_type=F32)
            t = u + sfx[:n]
            if mask is not None:
                t = jnp.where(mask, t, NEG)
            pv = jnp.dot(vtblk[rows, :], jnp.exp2(t).astype(BF16), preferred_element_type=F32)
            acc_ref[rows, qlo:] += pv * jnp.exp2(c_ref[a, :, qlo:])
            c_ref[a, :, qlo:] += sfx[n:n + 1]

    def main_k(g):
        return km_ref[0, pl.ds(pl.multiple_of(g * tk, tk), tk), :]

    def main_vt(g):
        return vm_ref[0, :, pl.ds(pl.multiple_of(g * tk, tk), tk)]

    sfx_main = _suffix_matrix(tk)
    sfx_pref = sfx_main if tkp == tk else _suffix_matrix(tkp)
    key = lax.broadcasted_iota(jnp.int32, (tk, tq), 0)
    qry = lax.broadcasted_iota(jnp.int32, (tk, tq), 1)

    step((kp_ref[0], 2, 0), None)
    step((main_k(n_full + n_diag - 1), 0, (n_diag - 1) * tk), None)
    for d in reversed(range(n_diag)):
        s = (n_diag - 1 - d) % 2
        if d > 0:
            nxt = (main_k(n_full + d - 1), 1 - s, (d - 1) * tk)
        else:
            nxt = (main_k(jnp.maximum(n_full - 1, 0)), 1 - s, 0)
        step(nxt, (s, main_vt(n_full + d), key + d * tk < qry, sfx_main, d * tk))

    def live():
        return jnp.max(c_ref[...]) > SKIP_BELOW

    def full_body(t):
        g = n_full - 1 - 2 * t
        step((main_k(g - 1), 1, 0), (0, main_vt(g), None, sfx_main, 0))
        step((main_k(jnp.maximum(g - 2, 0)), 0, 0), (1, main_vt(g - 1), None, sfx_main, 0))
        return t + 1
    n_pairs = i * (n_diag // 2)
    lax.while_loop(lambda t: jnp.logical_and(t < n_pairs, live()), full_body, jnp.int32(0))

    @pl.when(live())
    def _():
        step(None, (2, vp_ref[0], lax.broadcasted_iota(jnp.int32, (tkp, tq), 0) < pref_valid, sfx_pref, 0))
    o_ref[0] = acc_ref[...].T.astype(BF16)


def _sb_attn(q, kp, vtp, km, vtm, *, tq, tk, pref_valid):
    nb, tq_total, _ = q.shape
    tkp = kp.shape[1]
    t_main = km.shape[1]
    n_diag = tq // tk
    assert n_diag % 2 == 0, "main blocks are consumed in pairs"
    kernel = functools.partial(_sb_kernel, tq=tq, tk=tk, tkp=tkp, pref_valid=pref_valid, n_diag=n_diag)
    return pl.pallas_call(
        kernel,
        grid=(nb, SB_WIDTH // LANES, tq_total // tq),
        in_specs=[pl.BlockSpec((1, tq, LANES), lambda b, h, i: (b, i, h)),
                  pl.BlockSpec((1, tkp, LANES), lambda b, h, i: (0, 0, h)),
                  pl.BlockSpec((1, LANES, tkp), lambda b, h, i: (0, h, 0)),
                  pl.BlockSpec((1, t_main, LANES), lambda b, h, i: (b, 0, h)),
                  pl.BlockSpec((1, LANES, t_main), lambda b, h, i: (b, h, 0))],
        out_specs=pl.BlockSpec((1, tq, LANES), lambda b, h, i: (b, i, h)),
        out_shape=jax.ShapeDtypeStruct((nb, tq_total, SB_WIDTH), BF16),
        scratch_shapes=[pltpu.VMEM((2, 1, tq), F32), pltpu.VMEM((LANES, tq), F32),
                        pltpu.VMEM((3, 2, max(tk, tkp), tq), F32)],
        compiler_params=pltpu.CompilerParams(dimension_semantics=("arbitrary",) * 3,
                                             vmem_limit_bytes=VMEM_LIMIT),
        name="sb_attn",
    )(q, kp, vtp, km, vtm)


def _diff_decode_kernel(lq1_ref, lk1_ref, lq2_ref, lk2_ref, subg_ref, q_ref, ck_ref, cv_ref, kn_ref, vn_ref,
                        o_ref, m_ref, l_ref, acc_ref, *, chunk):
    kc = pl.program_id(1)
    nq = q_ref.shape[0]

    @pl.when(kc == 0)
    def _():
        m_ref[...] = jnp.full(m_ref.shape, NEG, F32)
        l_ref[...] = jnp.zeros(l_ref.shape, F32)
        acc_ref[...] = jnp.zeros(acc_ref.shape, F32)

    def update(h, k, v):
        n = k.shape[0]
        qs = _split_halves(q_ref[:, h * LANES:(h + 1) * LANES])
        v_ones = jnp.concatenate([v, jnp.ones((n, LANES), BF16)], axis=1)
        ps, alphas = [], []
        for a in range(2):
            s = lax.dot_general(qs[a], k, _NT, preferred_element_type=F32)
            m_prev = m_ref[2 * h + a]
            m_new = jnp.maximum(m_prev, jnp.max(s, axis=1, keepdims=True))
            alphas.append(jnp.exp2(m_prev - m_new))
            ps.append(jnp.exp2(s - m_new).astype(BF16))
            m_ref[2 * h + a] = m_new
        pv = jnp.dot(jnp.concatenate(ps, axis=0), v_ones, preferred_element_type=F32)
        for a in range(2):
            part = pv[a * nq:(a + 1) * nq]
            acc_ref[2 * h + a] = alphas[a] * acc_ref[2 * h + a] + part[:, :LANES]
            l_ref[2 * h + a] = alphas[a] * l_ref[2 * h + a] + part[:, LANES:]

    for h in range(N_HEADS):
        update(h, ck_ref[pl.ds(h, chunk, stride=N_HEADS), :].astype(BF16),
               cv_ref[pl.ds(h, chunk, stride=N_HEADS), :].astype(BF16))

    @pl.when(kc == pl.num_programs(1) - 1)
    def _():
        lam = (jnp.exp(jnp.sum(lq1_ref[...] * lk1_ref[...], axis=1, keepdims=True))
               - jnp.exp(jnp.sum(lq2_ref[...] * lk2_ref[...], axis=1, keepdims=True)) + LAM_INIT)
        for h in range(N_HEADS):
            cols = slice(h * LANES, (h + 1) * LANES)
            update(h, kn_ref[:, cols], vn_ref[:, cols])
            o = acc_ref[2 * h] / l_ref[2 * h] - lam * (acc_ref[2 * h + 1] / l_ref[2 * h + 1])
            o_ref[:, cols] = (_rms(o) * subg_ref[...] * (1.0 - LAM_INIT)).astype(BF16)


def _diff_decode(lams, subg_row, q, ck, cv, kn, vn, *, chunk):
    nb, nq, _ = q.shape
    n_chunks = ck.shape[1] // (chunk * N_HEADS)
    small = _const_spec((1, HEAD_DIM))
    tok = pl.BlockSpec((None, nq, DIFF_WIDTH), lambda b, c: (b, 0, 0))
    cache = pl.BlockSpec((None, chunk * N_HEADS, LANES), lambda b, c: (b, c, 0))
    return pl.pallas_call(
        functools.partial(_diff_decode_kernel, chunk=chunk),
        grid=(nb, n_chunks),
        in_specs=[small, small, small, small, _const_spec((1, LANES)), tok, cache, cache, tok, tok],
        out_specs=tok,
        out_shape=jax.ShapeDtypeStruct((nb, nq, DIFF_WIDTH), BF16),
        scratch_shapes=[pltpu.VMEM((2 * N_HEADS, nq, 1), F32), pltpu.VMEM((2 * N_HEADS, nq, LANES), F32),
                        pltpu.VMEM((2 * N_HEADS, nq, LANES), F32)],
        compiler_params=pltpu.CompilerParams(dimension_semantics=("arbitrary",) * 2,
                                             vmem_limit_bytes=VMEM_LIMIT),
        name="diff_decode",
    )(*lams, subg_row, q, ck, cv, kn, vn)


SB_BLOCK = 256


def _sb_decode_kernel(q_ref, ckt_ref, cvt_ref, knt_ref, vnt_ref, o_ref):
    nq = q_ref.shape[0]
    past = ckt_ref.shape[1]
    n_blk = past // SB_BLOCK
    def suffix_cols(n):
        r = lax.broadcasted_iota(jnp.int32, (n, n + LANES), 0)
        c = lax.broadcasted_iota(jnp.int32, (n, n + LANES), 1)
        return jnp.where((r > c) | (c >= n), 1.0, 0.0).astype(BF16)
    sfx_blk = suffix_cols(SB_BLOCK)
    sfx_new = suffix_cols(LANES)
    row = lax.broadcasted_iota(jnp.int32, (nq, LANES), 0)
    lane = lax.broadcasted_iota(jnp.int32, (nq, LANES), 1)
    new_mask = lane < row

    def log1m_of(u):
        nu = -u
        return jnp.minimum(nu, 0.0) - jnp.log(1.0 + jnp.exp2(jnp.minimum(u, nu))) * LOG2E

    for j in range(ckt_ref.shape[0] // LANES):
        rows = slice(j * LANES, (j + 1) * LANES)
        kt = ckt_ref[rows, :].astype(BF16)
        vt = cvt_ref[rows, :].astype(BF16)
        qs = _split_halves(q_ref[:, rows])
        ws, wns = [], []
        for a in range(2):
            un = jnp.dot(qs[a], knt_ref[rows, :], preferred_element_type=F32)
            ln = jnp.where(new_mask, log1m_of(un), 0.0)
            sn = jnp.dot(ln.astype(BF16), sfx_new, preferred_element_type=F32)
            wns.append(jnp.exp2(jnp.where(new_mask, un + ln + sn[:, :LANES], NEG)).astype(BF16))
            carry = sn[:, LANES:]
            u = jnp.dot(qs[a], kt, preferred_element_type=F32)
            l1m = log1m_of(u)
            stacked = jnp.concatenate(
                [l1m[:, b * SB_BLOCK:(b + 1) * SB_BLOCK] for b in range(n_blk)], axis=0).astype(BF16)
            sfx = jnp.dot(stacked, sfx_blk, preferred_element_type=F32)
            ts = [None] * n_blk
            for b in reversed(range(n_blk)):
                cols = slice(b * SB_BLOCK, (b + 1) * SB_BLOCK)
                part = sfx[b * nq:(b + 1) * nq]
                c2 = jnp.concatenate([carry] * (SB_BLOCK // LANES), axis=1)
                ts[b] = u[:, cols] + l1m[:, cols] + part[:, :SB_BLOCK] + c2
                carry = carry + part[:, SB_BLOCK:]
            ws.append(jnp.exp2(jnp.concatenate(ts, axis=1)).astype(BF16))
        o = (lax.dot_general(jnp.concatenate(ws, axis=0), vt, _NT, preferred_element_type=F32)
             + lax.dot_general(jnp.concatenate(wns, axis=0), vnt_ref[rows, :], _NT,
                               preferred_element_type=F32))
        o_ref[:, rows] = jnp.where(lane < HEAD_DIM, o[:nq], o[nq:]).astype(BF16)


def _sb_decode(q, ckt, cvt, knt, vnt):
    nb, nq, _ = q.shape
    assert nq <= LANES
    past = ckt.shape[2]
    width = 2 * LANES
    tok = pl.BlockSpec((None, nq, width), lambda b, g: (b, 0, g))
    cache = pl.BlockSpec((None, width, past), lambda b, g: (b, g, 0))
    new = pl.BlockSpec((None, width, LANES), lambda b, g: (b, g, 0))
    return pl.pallas_call(
        _sb_decode_kernel,
        grid=(nb, SB_WIDTH // width),
        in_specs=[tok, cache, cache, new, new],
        out_specs=tok,
        out_shape=jax.ShapeDtypeStruct((nb, nq, SB_WIDTH), BF16),
        compiler_params=pltpu.CompilerParams(dimension_semantics=("arbitrary",) * 2,
                                             vmem_limit_bytes=VMEM_LIMIT),
        name="sb_decode",
    )(q, ckt, cvt, knt, vnt)


def _out_kernel(x_ref, od_ref, os_ref, gmix_ref, wg_ref, wdo_ref, wso_ref, wo_ref, gffn_ref,
                w1_ref, w2_ref, y_ref):
    x = x_ref[...]
    h = (_rms(x) * gmix_ref[...]).astype(BF16)
    gate = jax.nn.sigmoid(jnp.dot(h, wg_ref[...], preferred_element_type=F32))
    a = jnp.dot(od_ref[...], wdo_ref[...], preferred_element_type=F32)
    b = jnp.dot(os_ref[...], wso_ref[...], preferred_element_type=F32)
    merged = (gate[:, :D_MODEL] * a + gate[:, D_MODEL:] * b).astype(BF16)
    x1 = x + jnp.dot(merged, wo_ref[...], preferred_element_type=F32)
    h2 = (_rms(x1) * gffn_ref[...]).astype(BF16)
    y = x1
    for c in range(D_FF // D_MODEL):
        sl = slice(c * D_MODEL, (c + 1) * D_MODEL)
        f = jnp.maximum(jnp.dot(h2, w1_ref[:, sl], preferred_element_type=F32), 0.0)
        y = y + jnp.dot((f * f).astype(BF16), w2_ref[sl, :], preferred_element_type=F32)
    y_ref[...] = y


def _out(x, od, osb, gmix, wg, wdo, wso, wo, gffn, w1, w2, *, tm):
    rows = x.shape[0]
    row = lambda w: pl.BlockSpec((tm, w), lambda i: (i, 0))
    return pl.pallas_call(
        _out_kernel,
        grid=(rows // tm,),
        in_specs=[row(D_MODEL), row(DIFF_WIDTH), row(SB_WIDTH), _const_spec((1, D_MODEL)),
                  _const_spec((D_MODEL, 2 * D_MODEL)), _const_spec((DIFF_WIDTH, D_MODEL)),
                  _const_spec((SB_WIDTH, D_MODEL)), _const_spec((D_MODEL, D_MODEL)),
                  _const_spec((1, D_MODEL)), _const_spec((D_MODEL, D_FF)), _const_spec((D_FF, D_MODEL))],
        out_specs=row(D_MODEL),
        out_shape=jax.ShapeDtypeStruct((rows, D_MODEL), F32),
        compiler_params=pltpu.CompilerParams(dimension_semantics=("arbitrary",),
                                             vmem_limit_bytes=VMEM_LIMIT),
        name="out",
    )(x, od, osb, gmix, wg, wdo, wso, wo, gffn, w1, w2)


def _rope_tables(pos):
    half = ROT_DIM // 2
    d = jnp.arange(LANES, dtype=jnp.int32) % HEAD_DIM
    inv = ROPE_THETA ** (-(2 * (d % half)).astype(F32) / ROT_DIM)
    ang = pos.astype(F32)[:, None] * inv[None, :]
    cos, sin = jnp.cos(ang), jnp.sin(ang)
    cos_t = jnp.where(d < ROT_DIM, cos, 1.0)
    sa_t = jnp.where(d < half, -sin, 0.0)
    sb_t = jnp.where((d >= half) & (d < ROT_DIM), sin, 0.0)
    return cos_t, sa_t, sb_t


def kernel(x_prompt, x_sample, cache_diff_k, cache_diff_v, cache_sb_k, cache_sb_v, meta_tokens,
           g_mix, w_in, q_norm_g, k_norm_g, lam_q1, lam_k1, lam_q2, lam_k2, sub_g,
           w_diff_out, w_sb_out, w_out, g_ffn, w_ff1, w_ff2):
    nb, seq, _ = x_prompt.shape
    db, dseq, _ = x_sample.shape
    past = cache_diff_k.shape[2]
    lyr = 0

    w_qkv, w_gate = w_in[lyr, :, :QKV_COLS].astype(BF16), w_in[lyr, :, QKV_COLS:].astype(BF16)
    wdo, wso, wo = (w_diff_out[lyr].astype(BF16), w_sb_out[lyr].astype(BF16), w_out[lyr].astype(BF16))
    w1, w2 = w_ff1[lyr].astype(BF16), w_ff2[lyr].astype(BF16)
    gmix = g_mix[lyr].reshape(1, D_MODEL)
    gffn = g_ffn[lyr].reshape(1, D_MODEL)
    qg = jnp.tile(q_norm_g[lyr], COL_BLOCK // HEAD_DIM).reshape(1, COL_BLOCK)
    kg = jnp.tile(k_norm_g[lyr], COL_BLOCK // HEAD_DIM).reshape(1, COL_BLOCK)
    subg_col = sub_g[lyr].reshape(LANES, 1)
    lams = [t[lyr].reshape(1, HEAD_DIM) for t in (lam_q1, lam_k1, lam_q2, lam_k2)]
    grp = jnp.arange(MXU_TILE, dtype=jnp.int32) // HEAD_DIM
    gmat = jnp.where(grp[:, None] == grp[None, :], 1.0 / HEAD_DIM, 0.0).astype(BF16)

    main_pos = N_META + jnp.arange(seq, dtype=jnp.int32)
    small_pos = jnp.concatenate([jnp.arange(N_META, dtype=jnp.int32),
                                 jnp.tile(past + jnp.arange(dseq, dtype=jnp.int32), db)])
    x_main = x_prompt.reshape(nb * seq, D_MODEL)
    x_small = jnp.concatenate([meta_tokens.astype(F32), x_sample.reshape(db * dseq, D_MODEL)], axis=0)
    pm = _proj(x_main, gmix, w_qkv, qg, kg, gmat, *_rope_tables(main_pos), tm=512, prompt_layout=True,
               lead=N_META)
    ps = _proj(x_small, gmix, w_qkv, qg, kg, gmat, *_rope_tables(small_pos), tm=x_small.shape[0],
               prompt_layout=False)
    qd_m, kd_hm, kdb_m, vd_hm, vdt_m, qs_m, kst_m, ksb_m, vstf_m, vst_m = pm
    qd_s, kd_s, kdb_s, vd_s, vdb_s, qs_s, ks_s, ksb_s, vs_s, vsb_s = ps

    def bt(a, n, t):
        return a.reshape(n, t, a.shape[-1])

    def meta_keys(a):
        return jnp.pad(a[:N_META], ((0, LANES - N_META), (0, 0)))[None]

    def meta_vals_t(a):
        return jnp.swapaxes(meta_keys(a), 1, 2)

    od_p = _diff_attn(lams, subg_col, bt(qd_m, nb, seq), meta_keys(kdb_s), meta_vals_t(vdb_s),
                      bt(kdb_m, nb, seq), vdt_m,
                      tq=2048, tk=512, pref_valid=N_META)
    os_p = _sb_attn(bt(qs_m, nb, seq), meta_keys(ksb_s), meta_vals_t(vsb_s),
                    bt(ksb_m, nb, seq), vst_m,
                    tq=512, tk=256, pref_valid=N_META)

    smp = lambda a: bt(a[N_META:], db, dseq)
    cdk = cache_diff_k[lyr].reshape(db, past * N_HEADS, LANES)
    cdv = cache_diff_v[lyr].reshape(db, past * N_HEADS, LANES)
    od_s = _diff_decode(lams, sub_g[lyr].reshape(1, LANES), smp(qd_s), cdk, cdv, smp(kdb_s), smp(vdb_s),
                        chunk=2048)
    sb_t = lambda c: jnp.transpose(c[lyr], (0, 2, 3, 1)).reshape(db, SB_WIDTH, past)
    new_t = lambda a: jnp.pad(jnp.swapaxes(smp(a), 1, 2), ((0, 0), (0, 0), (0, LANES - dseq)))
    os_s = _sb_decode(smp(qs_s), sb_t(cache_sb_k), sb_t(cache_sb_v), new_t(ksb_s), new_t(vsb_s))

    y_p = _out(x_main, od_p.reshape(nb * seq, DIFF_WIDTH), os_p.reshape(nb * seq, SB_WIDTH),
               gmix, w_gate, wdo, wso, wo, gffn, w1, w2, tm=512)
    y_s = _out(x_sample.reshape(db * dseq, D_MODEL), od_s.reshape(db * dseq, DIFF_WIDTH),
               os_s.reshape(db * dseq, SB_WIDTH), gmix, w_gate, wdo, wso, wo, gffn, w1, w2, tm=db * dseq)

    def diff_prompt_cache(head_major, small):
        meta = small[:N_META].reshape(N_META * N_HEADS, 2 * HEAD_DIM)
        full = _fill_lead_tokens(head_major, meta, nb=nb)
        return full.reshape(1, nb, seq + N_META, N_HEADS, 2 * HEAD_DIM)

    def sb_prompt_cache(main_t, small):
        meta_t = jnp.broadcast_to(small[:N_META].T[None], (nb, SB_WIDTH, N_META))
        full = jnp.concatenate([meta_t, main_t], axis=2).reshape(nb, N_HEADS, HEAD_DIM, seq + N_META)
        return jnp.transpose(full, (0, 3, 1, 2))[None]

    def sample_cache(small, dim):
        return small[N_META:].reshape(1, db, dseq, N_HEADS, dim)

    return (y_p.reshape(nb, seq, D_MODEL), y_s.reshape(db, dseq, D_MODEL),
            diff_prompt_cache(kd_hm, kd_s), diff_prompt_cache(vd_hm, vd_s),
            sb_prompt_cache(kst_m, ks_s), sb_prompt_cache(vstf_m, vs_s),
            sample_cache(kd_s, 2 * HEAD_DIM), sample_cache(vd_s, 2 * HEAD_DIM),
            sample_cache(ks_s, HEAD_DIM), sample_cache(vs_s, HEAD_DIM))
```

```python
import functools
import math

import jax
import jax.numpy as jnp
from jax import lax
from jax.experimental import pallas as pl
from jax.experimental.pallas import tpu as pltpu

F32 = jnp.float32
BF16 = jnp.bfloat16

D_MODEL = 1024
N_META = 16
CHUNK = 64
N_HEADS = 8
HEAD_DIM = 64
DIFF_WIDTH = N_HEADS * 2 * HEAD_DIM
SB_WIDTH = N_HEADS * HEAD_DIM
QKV_COLS = 3 * DIFF_WIDTH + 3 * SB_WIDTH
D_FF = 4 * D_MODEL
ROT_DIM = HEAD_DIM // 4
ROPE_THETA = 500000.0
EPS = 1e-6
NEG = -1e30
SKIP_BELOW = -160.0
LOG2E = math.log2(math.e)
Q_SCALE = HEAD_DIM ** -0.5 * LOG2E
ONES_ROWS = 16
LAM_INIT = 0.8 - 0.6 * math.exp(-0.3 * 0)

LANES = 128
MXU_TILE = 256
COL_BLOCK = 512
VMEM_LIMIT = 56 * 1024 * 1024

_NT = (((1,), (1,)), ((), ()))


def _rms(x):
    return x * lax.rsqrt(jnp.mean(x * x, axis=-1, keepdims=True) + EPS)


def _const_spec(shape):
    return pl.BlockSpec(shape, lambda *_: (0,) * len(shape), pipeline_mode=pl.Buffered(1))


def _proj_kernel(x_ref, gmix_ref, w_ref, qg_ref, kg_ref, gmat_ref, cos_ref, sa_ref, sb_ref,
                 qd_ref, kd_ref, kdb_ref, vd_ref, vdb_ref, qs_ref, ks_ref, ksb_ref, vs_ref, vsb_ref,
                 *, prompt_layout):
    tm = x_ref.shape[0]
    heads_per_block = COL_BLOCK // LANES

    def store_diff(ref, j, y):
        if not prompt_layout:
            ref[:, j * COL_BLOCK:(j + 1) * COL_BLOCK] = y
            return
        for hh in range(heads_per_block):
            ref[pl.ds(j * heads_per_block + hh, tm, stride=N_HEADS), :] = y[:, hh * LANES:(hh + 1) * LANES]

    h = (_rms(x_ref[...]) * gmix_ref[...]).astype(BF16)
    cos = cos_ref[...]
    sa = sa_ref[...]
    sb = sb_ref[...]

    def col(j):
        return jnp.dot(h, w_ref[:, j * COL_BLOCK:(j + 1) * COL_BLOCK], preferred_element_type=F32)

    def normed_rot(y, g):
        sq = (y * y).astype(BF16)
        msq = jnp.concatenate(
            [jnp.dot(sq[:, c * MXU_TILE:(c + 1) * MXU_TILE], gmat_ref[...], preferred_element_type=F32)
             for c in range(COL_BLOCK // MXU_TILE)], axis=1)
        yn = y * lax.rsqrt(msq + EPS) * g
        parts = []
        for c in range(COL_BLOCK // LANES):
            t = yn[:, c * LANES:(c + 1) * LANES]
            parts.append(t * cos + pltpu.roll(t, LANES - ROT_DIM // 2, 1) * sa
                         + pltpu.roll(t, ROT_DIM // 2, 1) * sb)
        return jnp.concatenate(parts, axis=1)

    for j in range(2):
        sl = slice(j * COL_BLOCK, (j + 1) * COL_BLOCK)
        q = normed_rot(col(j), qg_ref[...])
        qd_ref[:, sl] = (q * Q_SCALE).astype(BF16)
        k = normed_rot(col(2 + j), kg_ref[...])
        store_diff(kd_ref, j, k)
        kdb_ref[:, sl] = k.astype(BF16)
        v = col(4 + j)
        store_diff(vd_ref, j, v)
        if prompt_layout:
            vdb_ref[sl, :] = v.T.astype(BF16)
        else:
            vdb_ref[:, sl] = v.astype(BF16)
    qs_ref[...] = (col(6) * Q_SCALE).astype(BF16)
    k = col(7)
    ksb_ref[...] = k.astype(BF16)
    v = col(8)
    if prompt_layout:
        ks_ref[...] = k.T
        vt = v.T
        vs_ref[...] = vt
        vsb_ref[...] = vt.astype(BF16)
    else:
        ks_ref[...] = k
        vs_ref[...] = v
        vsb_ref[...] = v.astype(BF16)


def _proj(x, gmix, w_qkv, qg, kg, gmat, cos, sa, sb, *, tm, prompt_layout, lead=0):
    rows = x.shape[0]
    seq = cos.shape[0]
    n_pos_tiles = seq // tm
    row = lambda w: pl.BlockSpec((tm, w), lambda i: (i, 0))
    tab = pl.BlockSpec((tm, LANES), lambda i: (i % n_pos_tiles, 0))
    wide = lambda dt: jax.ShapeDtypeStruct((rows, DIFF_WIDTH), dt)
    narrow = lambda dt: jax.ShapeDtypeStruct((rows, SB_WIDTH), dt)
    if prompt_layout:
        nb = rows // seq
        t_spec = lambda w: pl.BlockSpec((None, w, tm), lambda i: (i // n_pos_tiles, 0, i % n_pos_tiles))
        t_shape = lambda w, dt: jax.ShapeDtypeStruct((nb, w, seq), dt)
        hm_spec = pl.BlockSpec(
            (pl.Element(tm * N_HEADS), pl.Element(LANES)),
            lambda i: (((i // n_pos_tiles) * (seq + lead) + lead + (i % n_pos_tiles) * tm) * N_HEADS, 0))
        hm_shape = jax.ShapeDtypeStruct((nb * (seq + lead) * N_HEADS, LANES), F32)
        out_specs = [row(DIFF_WIDTH), hm_spec, row(DIFF_WIDTH), hm_spec, t_spec(DIFF_WIDTH),
                     row(SB_WIDTH), t_spec(SB_WIDTH), row(SB_WIDTH), t_spec(SB_WIDTH), t_spec(SB_WIDTH)]
        out_shape = [wide(BF16), hm_shape, wide(BF16), hm_shape, t_shape(DIFF_WIDTH, BF16),
                     narrow(BF16), t_shape(SB_WIDTH, F32), narrow(BF16), t_shape(SB_WIDTH, F32),
                     t_shape(SB_WIDTH, BF16)]
    else:
        out_specs = [row(DIFF_WIDTH)] * 5 + [row(SB_WIDTH)] * 5
        out_shape = [wide(BF16), wide(F32), wide(BF16), wide(F32), wide(BF16),
                     narrow(BF16), narrow(F32), narrow(BF16), narrow(F32), narrow(BF16)]
    return pl.pallas_call(
        functools.partial(_proj_kernel, prompt_layout=prompt_layout),
        grid=(rows // tm,),
        in_specs=[row(D_MODEL), _const_spec((1, D_MODEL)), _const_spec((D_MODEL, QKV_COLS)),
                  _const_spec((1, COL_BLOCK)), _const_spec((1, COL_BLOCK)),
                  _const_spec((MXU_TILE, MXU_TILE)), tab, tab, tab],
        out_specs=out_specs,
        out_shape=out_shape,
        compiler_params=pltpu.CompilerParams(dimension_semantics=("arbitrary",),
                                             vmem_limit_bytes=VMEM_LIMIT),
        name="proj",
    )(x, gmix, w_qkv, qg, kg, gmat, cos, sa, sb)


def _fill_lead_kernel(lead_ref, big_ref, out_ref):
    del big_ref
    out_ref[...] = lead_ref[...]


def _fill_lead_tokens(big, lead_rows, *, nb):
    n = lead_rows.shape[0]
    per_batch = big.shape[0] // nb
    return pl.pallas_call(
        _fill_lead_kernel,
        grid=(nb,),
        in_specs=[_const_spec((n, LANES)), pl.BlockSpec(memory_space=pl.ANY)],
        out_specs=pl.BlockSpec((pl.Element(n), pl.Element(LANES)), lambda b: (b * per_batch, 0)),
        out_shape=jax.ShapeDtypeStruct(big.shape, big.dtype),
        input_output_aliases={1: 0},
        compiler_params=pltpu.CompilerParams(dimension_semantics=("arbitrary",)),
        name="fill_lead_tokens",
    )(lead_rows, big)


def _split_halves(q):
    lane = lax.broadcasted_iota(jnp.int32, q.shape, 1)
    zero = jnp.zeros_like(q)
    return jnp.where(lane < HEAD_DIM, q, zero), jnp.where(lane >= HEAD_DIM, q, zero)


def _diff_kernel(lq1_ref, lk1_ref, lq2_ref, lk2_ref, subg_ref, q_ref, kp_ref, vp_ref, km_ref, vm_ref,
                 o_ref, m_ref, acc_ref, s_ref, *, tq, tk, tkp, pref_valid, n_diag):
    i = pl.program_id(2)
    n_full = i * n_diag
    qs = _split_halves(q_ref[0])
    m_ref[...] = jnp.full(m_ref.shape, NEG, F32)
    acc_ref[...] = jnp.zeros(acc_ref.shape, F32)

    def step(nxt, cur):
        if cur is not None:
            slot, vtblk, mask, qlo = cur
            n = vtblk.shape[1]
            vt_ones = jnp.concatenate([vtblk, jnp.ones((ONES_ROWS, n), BF16)], axis=0)
        for a in range(2):
            if cur is not None:
                s = s_ref[slot, a, :n, qlo:]
                if mask is not None:
                    s = jnp.where(mask[:, qlo:], s, NEG)
                m_prev = m_ref[a, :, qlo:]
                m_new = jnp.maximum(m_prev, jnp.max(s, axis=0, keepdims=True))
                alpha = jnp.exp2(m_prev - m_new)
                p = jnp.exp2(s - m_new)
            if nxt is not None:
                kblk, nslot, nqlo = nxt
                s_ref[nslot, a, :kblk.shape[0], nqlo:] = lax.dot_general(
                    kblk, qs[a][nqlo:], _NT, preferred_element_type=F32)
            if cur is not None:
                acc_ref[a, :, qlo:] = (alpha * acc_ref[a, :, qlo:]
                                       + jnp.dot(vt_ones, p.astype(BF16), preferred_element_type=F32))
                m_ref[a, :, qlo:] = m_new

    def main_k(g):
        return km_ref[0, pl.ds(pl.multiple_of(g * tk, tk), tk), :]

    def main_vt(g):
        return vm_ref[0, :, pl.ds(pl.multiple_of(g * tk, tk), tk)]

    step((kp_ref[0], 1, 0), None)
    step((main_k(0), 0, 0), (1, vp_ref[0], lax.broadcasted_iota(jnp.int32, (tkp, tq), 0) < pref_valid, 0))

    def full_body(t, carry):
        step((main_k(2 * t + 1), 1, 0), (0, main_vt(2 * t), None, 0))
        step((main_k(2 * t + 2), 0, 0), (1, main_vt(2 * t + 1), None, 0))
        return carry
    lax.fori_loop(0, i * (n_diag // 2), full_body, 0)

    key_idx = lax.broadcasted_iota(jnp.int32, (tk, tq), 0)
    q_chunk = lax.broadcasted_iota(jnp.int32, (tk, tq), 1) // CHUNK
    for d in range(n_diag):
        g = n_full + d
        nxt = (main_k(g + 1), (d + 1) % 2, (d + 1) * tk) if d + 1 < n_diag else None
        step(nxt, (d % 2, main_vt(g), (key_idx + d * tk) // CHUNK <= q_chunk, d * tk))

    lam = (jnp.exp(jnp.sum(lq1_ref[...] * lk1_ref[...], axis=1, keepdims=True))
           - jnp.exp(jnp.sum(lq2_ref[...] * lk2_ref[...], axis=1, keepdims=True)) + LAM_INIT)
    o = (acc_ref[0, :LANES] / acc_ref[0, LANES:LANES + 1]
         - lam * (acc_ref[1, :LANES] / acc_ref[1, LANES:LANES + 1]))
    o = o * lax.rsqrt(jnp.mean(o * o, axis=0, keepdims=True) + EPS) * subg_ref[...] * (1.0 - LAM_INIT)
    o_ref[0] = o.T.astype(BF16)


def _diff_attn(lams, subg_col, q, kp, vtp, km, vtm, *, tq, tk, pref_valid):
    nb, tq_total, _ = q.shape
    tkp = kp.shape[1]
    t_main = km.shape[1]
    n_diag = tq // tk
    assert n_diag % 2 == 0, "main blocks are consumed in pairs"
    kernel = functools.partial(_diff_kernel, tq=tq, tk=tk, tkp=tkp, pref_valid=pref_valid, n_diag=n_diag)
    small = _const_spec((1, HEAD_DIM))
    return pl.pallas_call(
        kernel,
        grid=(nb, N_HEADS, tq_total // tq),
        in_specs=[small, small, small, small, _const_spec((LANES, 1)),
                  pl.BlockSpec((1, tq, LANES), lambda b, h, i: (b, i, h)),
                  pl.BlockSpec((1, tkp, LANES), lambda b, h, i: (0, 0, h)),
                  pl.BlockSpec((1, LANES, tkp), lambda b, h, i: (0, h, 0)),
                  pl.BlockSpec((1, t_main, LANES), lambda b, h, i: (b, 0, h)),
                  pl.BlockSpec((1, LANES, t_main), lambda b, h, i: (b, h, 0))],
        out_specs=pl.BlockSpec((1, tq, LANES), lambda b, h, i: (b, i, h)),
        out_shape=jax.ShapeDtypeStruct((nb, tq_total, DIFF_WIDTH), BF16),
        scratch_shapes=[pltpu.VMEM((2, 1, tq), F32), pltpu.VMEM((2, LANES + ONES_ROWS, tq), F32),
                        pltpu.VMEM((2, 2, max(tk, tkp), tq), F32)],
        compiler_params=pltpu.CompilerParams(dimension_semantics=("arbitrary",) * 3,
                                             vmem_limit_bytes=VMEM_LIMIT),
        name="diff_attn",
    )(*lams, subg_col, q, kp, vtp, km, vtm)


def _suffix_matrix(n):
    r = lax.broadcasted_iota(jnp.int32, (n + ONES_ROWS, n), 0)
    c = lax.broadcasted_iota(jnp.int32, (n + ONES_ROWS, n), 1)
    return jnp.where((c >= r) | (r == n), 1.0, 0.0).astype(BF16)


def _sb_kernel(q_ref, kp_ref, vp_ref, km_ref, vm_ref, o_ref, c_ref, acc_ref, z_ref,
               *, tq, tk, tkp, pref_valid, n_diag):
    i = pl.program_id(2)
    n_full = i * n_diag
    qs = _split_halves(q_ref[0])
    c_ref[...] = jnp.zeros(c_ref.shape, F32)
    acc_ref[...] = jnp.zeros(acc_ref.shape, F32)

    def step(nxt, cur):
        if nxt is not None:
            kblk, nslot, nqlo = nxt
            for a in range(2):
                z_ref[nslot, a, :kblk.shape[0], nqlo:] = lax.dot_general(
                    kblk, qs[a][nqlo:], _NT, preferred_element_type=F32)
        if cur is None:
            return
        slot, vtblk, mask, sfx_mat, qlo = cur
        n = vtblk.shape[1]
        if mask is not None:
            mask = mask[:, qlo:]
        for a in range(2):
            rows = slice(a * HEAD_DIM, (a + 1) * HEAD_DIM)
            u = z_ref[slot, a, :n, qlo:]
            neg_part = jnp.minimum(u, 0.0)
            d = neg_part - u
            log1m = d - jnp.log(1.0 + jnp.exp2(neg_part + d)) * LOG2E
            if mask is not None:
                log1m = jnp.where(mask, log1m, 0.0)
            sfx = jnp.dot(sfx_mat, log1m.astype(BF16), preferred_element_type=F32)
            t = u + sfx[:n]
            if mask is not None:
                t = jnp.where(mask, t, NEG)
            pv = jnp.dot(vtblk[rows, :], jnp.exp2(t).astype(BF16), preferred_element_type=F32)
            acc_ref[rows, qlo:] += pv * jnp.exp2(c_ref[a, :, qlo:])
            c_ref[a, :, qlo:] += sfx[n:n + 1]

    def main_k(g):
        return km_ref[0, pl.ds(pl.multiple_of(g * tk, tk), tk), :]

    def main_vt(g):
        return vm_ref[0, :, pl.ds(pl.multiple_of(g * tk, tk), tk)]

    sfx_main = _suffix_matrix(tk)
    sfx_pref = sfx_main if tkp == tk else _suffix_matrix(tkp)
    key = lax.broadcasted_iota(jnp.int32, (tk, tq), 0)
    qry = lax.broadcasted_iota(jnp.int32, (tk, tq), 1)

    step((kp_ref[0], 2, 0), None)
    step((main_k(n_full + n_diag - 1), 0, (n_diag - 1) * tk), None)
    for d in reversed(range(n_diag)):
        s = (n_diag - 1 - d) % 2
        if d > 0:
            nxt = (main_k(n_full + d - 1), 1 - s, (d - 1) * tk)
        else:
            nxt = (main_k(jnp.maximum(n_full - 1, 0)), 1 - s, 0)
        step(nxt, (s, main_vt(n_full + d), key + d * tk < qry, sfx_main, d * tk))

    def live():
        return jnp.max(c_ref[...]) > SKIP_BELOW

    def full_body(t):
        g = n_full - 1 - 2 * t
        step((main_k(g - 1), 1, 0), (0, main_vt(g), None, sfx_main, 0))
        step((main_k(jnp.maximum(g - 2, 0)), 0, 0), (1, main_vt(g - 1), None, sfx_main, 0))
        return t + 1
    n_pairs = i * (n_diag // 2)
    lax.while_loop(lambda t: jnp.logical_and(t < n_pairs, live()), full_body, jnp.int32(0))

    @pl.when(live())
    def _():
        step(None, (2, vp_ref[0], lax.broadcasted_iota(jnp.int32, (tkp, tq), 0) < pref_valid, sfx_pref, 0))
    o_ref[0] = acc_ref[...].T.astype(BF16)


def _sb_attn(q, kp, vtp, km, vtm, *, tq, tk, pref_valid):
    nb, tq_total, _ = q.shape
    tkp = kp.shape[1]
    t_main = km.shape[1]
    n_diag = tq // tk
    assert n_diag % 2 == 0, "main blocks are consumed in pairs"
    kernel = functools.partial(_sb_kernel, tq=tq, tk=tk, tkp=tkp, pref_valid=pref_valid, n_diag=n_diag)
    return pl.pallas_call(
        kernel,
        grid=(nb, SB_WIDTH // LANES, tq_total // tq),
        in_specs=[pl.BlockSpec((1, tq, LANES), lambda b, h, i: (b, i, h)),
                  pl.BlockSpec((1, tkp, LANES), lambda b, h, i: (0, 0, h)),
                  pl.BlockSpec((1, LANES, tkp), lambda b, h, i: (0, h, 0)),
                  pl.BlockSpec((1, t_main, LANES), lambda b, h, i: (b, 0, h)),
                  pl.BlockSpec((1, LANES, t_main), lambda b, h, i: (b, h, 0))],
        out_specs=pl.BlockSpec((1, tq, LANES), lambda b, h, i: (b, i, h)),
        out_shape=jax.ShapeDtypeStruct((nb, tq_total, SB_WIDTH), BF16),
        scratch_shapes=[pltpu.VMEM((2, 1, tq), F32), pltpu.VMEM((LANES, tq), F32),
                        pltpu.VMEM((3, 2, max(tk, tkp), tq), F32)],
        compiler_params=pltpu.CompilerParams(dimension_semantics=("arbitrary",) * 3,
                                             vmem_limit_bytes=VMEM_LIMIT),
        name="sb_attn",
    )(q, kp, vtp, km, vtm)


def _diff_decode_kernel(lq1_ref, lk1_ref, lq2_ref, lk2_ref, subg_ref, q_ref, ck_ref, cv_ref, kn_ref, vn_ref,
                        o_ref, m_ref, l_ref, acc_ref, *, chunk):
    kc = pl.program_id(1)
    nq = q_ref.shape[0]

    @pl.when(kc == 0)
    def _():
        m_ref[...] = jnp.full(m_ref.shape, NEG, F32)
        l_ref[...] = jnp.zeros(l_ref.shape, F32)
        acc_ref[...] = jnp.zeros(acc_ref.shape, F32)

    def update(h, k, v):
        n = k.shape[0]
        qs = _split_halves(q_ref[:, h * LANES:(h + 1) * LANES])
        v_ones = jnp.concatenate([v, jnp.ones((n, LANES), BF16)], axis=1)
        ps, alphas = [], []
        for a in range(2):
            s = lax.dot_general(qs[a], k, _NT, preferred_element_type=F32)
            m_prev = m_ref[2 * h + a]
            m_new = jnp.maximum(m_prev, jnp.max(s, axis=1, keepdims=True))
            alphas.append(jnp.exp2(m_prev - m_new))
            ps.append(jnp.exp2(s - m_new).astype(BF16))
            m_ref[2 * h + a] = m_new
        pv = jnp.dot(jnp.concatenate(ps, axis=0), v_ones, preferred_element_type=F32)
        for a in range(2):
            part = pv[a * nq:(a + 1) * nq]
            acc_ref[2 * h + a] = alphas[a] * acc_ref[2 * h + a] + part[:, :LANES]
            l_ref[2 * h + a] = alphas[a] * l_ref[2 * h + a] + part[:, LANES:]

    for h in range(N_HEADS):
        update(h, ck_ref[pl.ds(h, chunk, stride=N_HEADS), :].astype(BF16),
               cv_ref[pl.ds(h, chunk, stride=N_HEADS), :].astype(BF16))

    @pl.when(kc == pl.num_programs(1) - 1)
    def _():
        lam = (jnp.exp(jnp.sum(lq1_ref[...] * lk1_ref[...], axis=1, keepdims=True))
               - jnp.exp(jnp.sum(lq2_ref[...] * lk2_ref[...], axis=1, keepdims=True)) + LAM_INIT)
        for h in range(N_HEADS):
            cols = slice(h * LANES, (h + 1) * LANES)
            update(h, kn_ref[:, cols], vn_ref[:, cols])
            o = acc_ref[2 * h] / l_ref[2 * h] - lam * (acc_ref[2 * h + 1] / l_ref[2 * h + 1])
            o_ref[:, cols] = (_rms(o) * subg_ref[...] * (1.0 - LAM_INIT)).astype(BF16)


def _diff_decode(lams, subg_row, q, ck, cv, kn, vn, *, chunk):
    nb, nq, _ = q.shape
    n_chunks = ck.shape[1] // (chunk * N_HEADS)
    small = _const_spec((1, HEAD_DIM))
    tok = pl.BlockSpec((None, nq, DIFF_WIDTH), lambda b, c: (b, 0, 0))
    cache = pl.BlockSpec((None, chunk * N_HEADS, LANES), lambda b, c: (b, c, 0))
    return pl.pallas_call(
        functools.partial(_diff_decode_kernel, chunk=chunk),
        grid=(nb, n_chunks),
        in_specs=[small, small, small, small, _const_spec((1, LANES)), tok, cache, cache, tok, tok],
        out_specs=tok,
        out_shape=jax.ShapeDtypeStruct((nb, nq, DIFF_WIDTH), BF16),
        scratch_shapes=[pltpu.VMEM((2 * N_HEADS, nq, 1), F32), pltpu.VMEM((2 * N_HEADS, nq, LANES), F32),
                        pltpu.VMEM((2 * N_HEADS, nq, LANES), F32)],
        compiler_params=pltpu.CompilerParams(dimension_semantics=("arbitrary",) * 2,
                                             vmem_limit_bytes=VMEM_LIMIT),
        name="diff_decode",
    )(*lams, subg_row, q, ck, cv, kn, vn)


SB_BLOCK = 256


def _sb_decode_kernel(q_ref, ckt_ref, cvt_ref, knt_ref, vnt_ref, o_ref):
    nq = q_ref.shape[0]
    past = ckt_ref.shape[1]
    n_blk = past // SB_BLOCK
    def suffix_cols(n):
        r = lax.broadcasted_iota(jnp.int32, (n, n + LANES), 0)
        c = lax.broadcasted_iota(jnp.int32, (n, n + LANES), 1)
        return jnp.where((r > c) | (c >= n), 1.0, 0.0).astype(BF16)
    sfx_blk = suffix_cols(SB_BLOCK)
    sfx_new = suffix_cols(LANES)
    row = lax.broadcasted_iota(jnp.int32, (nq, LANES), 0)
    lane = lax.broadcasted_iota(jnp.int32, (nq, LANES), 1)
    new_mask = lane < row

    def log1m_of(u):
        nu = -u
        return jnp.minimum(nu, 0.0) - jnp.log(1.0 + jnp.exp2(jnp.minimum(u, nu))) * LOG2E

    for j in range(ckt_ref.shape[0] // LANES):
        rows = slice(j * LANES, (j + 1) * LANES)
        kt = ckt_ref[rows, :].astype(BF16)
        vt = cvt_ref[rows, :].astype(BF16)
        qs = _split_halves(q_ref[:, rows])
        ws, wns = [], []
        for a in range(2):
            un = jnp.dot(qs[a], knt_ref[rows, :], preferred_element_type=F32)
            ln = jnp.where(new_mask, log1m_of(un), 0.0)
            sn = jnp.dot(ln.astype(BF16), sfx_new, preferred_element_type=F32)
            wns.append(jnp.exp2(jnp.where(new_mask, un + ln + sn[:, :LANES], NEG)).astype(BF16))
            carry = sn[:, LANES:]
            u = jnp.dot(qs[a], kt, preferred_element_type=F32)
            l1m = log1m_of(u)
            stacked = jnp.concatenate(
                [l1m[:, b * SB_BLOCK:(b + 1) * SB_BLOCK] for b in range(n_blk)], axis=0).astype(BF16)
            sfx = jnp.dot(stacked, sfx_blk, preferred_element_type=F32)
            ts = [None] * n_blk
            for b in reversed(range(n_blk)):
                cols = slice(b * SB_BLOCK, (b + 1) * SB_BLOCK)
                part = sfx[b * nq:(b + 1) * nq]
                c2 = jnp.concatenate([carry] * (SB_BLOCK // LANES), axis=1)
                ts[b] = u[:, cols] + l1m[:, cols] + part[:, :SB_BLOCK] + c2
                carry = carry + part[:, SB_BLOCK:]
            ws.append(jnp.exp2(jnp.concatenate(ts, axis=1)).astype(BF16))
        o = (lax.dot_general(jnp.concatenate(ws, axis=0), vt, _NT, preferred_element_type=F32)
             + lax.dot_general(jnp.concatenate(wns, axis=0), vnt_ref[rows, :], _NT,
                               preferred_element_type=F32))
        o_ref[:, rows] = jnp.where(lane < HEAD_DIM, o[:nq], o[nq:]).astype(BF16)


def _sb_decode(q, ckt, cvt, knt, vnt):
    nb, nq, _ = q.shape
    assert nq <= LANES
    past = ckt.shape[2]
    width = 2 * LANES
    tok = pl.BlockSpec((None, nq, width), lambda b, g: (b, 0, g))
    cache = pl.BlockSpec((None, width, past), lambda b, g: (b, g, 0))
    new = pl.BlockSpec((None, width, LANES), lambda b, g: (b, g, 0))
    return pl.pallas_call(
        _sb_decode_kernel,
        grid=(nb, SB_WIDTH // width),
        in_specs=[tok, cache, cache, new, new],
        out_specs=tok,
        out_shape=jax.ShapeDtypeStruct((nb, nq, SB_WIDTH), BF16),
        compiler_params=pltpu.CompilerParams(dimension_semantics=("arbitrary",) * 2,
                                             vmem_limit_bytes=VMEM_LIMIT),
        name="sb_decode",
    )(q, ckt, cvt, knt, vnt)


def _out_kernel(x_ref, od_ref, os_ref, gmix_ref, wg_ref, wdo_ref, wso_ref, wo_ref, gffn_ref,
                w1_ref, w2_ref, y_ref):
    x = x_ref[...]
    h = (_rms(x) * gmix_ref[...]).astype(BF16)
    gate = jax.nn.sigmoid(jnp.dot(h, wg_ref[...], preferred_element_type=F32))
    a = jnp.dot(od_ref[...], wdo_ref[...], preferred_element_type=F32)
    b = jnp.dot(os_ref[...], wso_ref[...], preferred_element_type=F32)
    merged = (gate[:, :D_MODEL] * a + gate[:, D_MODEL:] * b).astype(BF16)
    x1 = x + jnp.dot(merged, wo_ref[...], preferred_element_type=F32)
    h2 = (_rms(x1) * gffn_ref[...]).astype(BF16)
    y = x1
    for c in range(D_FF // D_MODEL):
        sl = slice(c * D_MODEL, (c + 1) * D_MODEL)
        f = jnp.maximum(jnp.dot(h2, w1_ref[:, sl], preferred_element_type=F32), 0.0)
        y = y + jnp.dot((f * f).astype(BF16), w2_ref[sl, :], preferred_element_type=F32)
    y_ref[...] = y


def _out(x, od, osb, gmix, wg, wdo, wso, wo, gffn, w1, w2, *, tm):
    rows = x.shape[0]
    row = lambda w: pl.BlockSpec((tm, w), lambda i: (i, 0))
    return pl.pallas_call(
        _out_kernel,
        grid=(rows // tm,),
        in_specs=[row(D_MODEL), row(DIFF_WIDTH), row(SB_WIDTH), _const_spec((1, D_MODEL)),
                  _const_spec((D_MODEL, 2 * D_MODEL)), _const_spec((DIFF_WIDTH, D_MODEL)),
                  _const_spec((SB_WIDTH, D_MODEL)), _const_spec((D_MODEL, D_MODEL)),
                  _const_spec((1, D_MODEL)), _const_spec((D_MODEL, D_FF)), _const_spec((D_FF, D_MODEL))],
        out_specs=row(D_MODEL),
        out_shape=jax.ShapeDtypeStruct((rows, D_MODEL), F32),
        compiler_params=pltpu.CompilerParams(dimension_semantics=("arbitrary",),
                                             vmem_limit_bytes=VMEM_LIMIT),
        name="out",
    )(x, od, osb, gmix, wg, wdo, wso, wo, gffn, w1, w2)


def _rope_tables(pos):
    half = ROT_DIM // 2
    d = jnp.arange(LANES, dtype=jnp.int32) % HEAD_DIM
    inv = ROPE_THETA ** (-(2 * (d % half)).astype(F32) / ROT_DIM)
    ang = pos.astype(F32)[:, None] * inv[None, :]
    cos, sin = jnp.cos(ang), jnp.sin(ang)
    cos_t = jnp.where(d < ROT_DIM, cos, 1.0)
    sa_t = jnp.where(d < half, -sin, 0.0)
    sb_t = jnp.where((d >= half) & (d < ROT_DIM), sin, 0.0)
    return cos_t, sa_t, sb_t


def kernel(x_prompt, x_sample, cache_diff_k, cache_diff_v, cache_sb_k, cache_sb_v, meta_tokens,
           g_mix, w_in, q_norm_g, k_norm_g, lam_q1, lam_k1, lam_q2, lam_k2, sub_g,
           w_diff_out, w_sb_out, w_out, g_ffn, w_ff1, w_ff2):
    nb, seq, _ = x_prompt.shape
    db, dseq, _ = x_sample.shape
    past = cache_diff_k.shape[2]
    lyr = 0

    w_qkv, w_gate = w_in[lyr, :, :QKV_COLS].astype(BF16), w_in[lyr, :, QKV_COLS:].astype(BF16)
    wdo, wso, wo = (w_diff_out[lyr].astype(BF16), w_sb_out[lyr].astype(BF16), w_out[lyr].astype(BF16))
    w1, w2 = w_ff1[lyr].astype(BF16), w_ff2[lyr].astype(BF16)
    gmix = g_mix[lyr].reshape(1, D_MODEL)
    gffn = g_ffn[lyr].reshape(1, D_MODEL)
    qg = jnp.tile(q_norm_g[lyr], COL_BLOCK // HEAD_DIM).reshape(1, COL_BLOCK)
    kg = jnp.tile(k_norm_g[lyr], COL_BLOCK // HEAD_DIM).reshape(1, COL_BLOCK)
    subg_col = sub_g[lyr].reshape(LANES, 1)
    lams = [t[lyr].reshape(1, HEAD_DIM) for t in (lam_q1, lam_k1, lam_q2, lam_k2)]
    grp = jnp.arange(MXU_TILE, dtype=jnp.int32) // HEAD_DIM
    gmat = jnp.where(grp[:, None] == grp[None, :], 1.0 / HEAD_DIM, 0.0).astype(BF16)

    main_pos = N_META + jnp.arange(seq, dtype=jnp.int32)
    small_pos = jnp.concatenate([jnp.arange(N_META, dtype=jnp.int32),
                                 jnp.tile(past + jnp.arange(dseq, dtype=jnp.int32), db)])
    x_main = x_prompt.reshape(nb * seq, D_MODEL)
    x_small = jnp.concatenate([meta_tokens.astype(F32), x_sample.reshape(db * dseq, D_MODEL)], axis=0)
    pm = _proj(x_main, gmix, w_qkv, qg, kg, gmat, *_rope_tables(main_pos), tm=512, prompt_layout=True,
               lead=N_META)
    ps = _proj(x_small, gmix, w_qkv, qg, kg, gmat, *_rope_tables(small_pos), tm=x_small.shape[0],
               prompt_layout=False)
    qd_m, kd_hm, kdb_m, vd_hm, vdt_m, qs_m, kst_m, ksb_m, vstf_m, vst_m = pm
    qd_s, kd_s, kdb_s, vd_s, vdb_s, qs_s, ks_s, ksb_s, vs_s, vsb_s = ps

    def bt(a, n, t):
        return a.reshape(n, t, a.shape[-1])

    def meta_keys(a):
        return jnp.pad(a[:N_META], ((0, LANES - N_META), (0, 0)))[None]

    def meta_vals_t(a):
        return jnp.swapaxes(meta_keys(a), 1, 2)

    od_p = _diff_attn(lams, subg_col, bt(qd_m, nb, seq), meta_keys(kdb_s), meta_vals_t(vdb_s),
                      bt(kdb_m, nb, seq), vdt_m,
                      tq=2048, tk=512, pref_valid=N_META)
    os_p = _sb_attn(bt(qs_m, nb, seq), meta_keys(ksb_s), meta_vals_t(vsb_s),
                    bt(ksb_m, nb, seq), vst_m,
                    tq=1024, tk=256, pref_valid=N_META)

    smp = lambda a: bt(a[N_META:], db, dseq)
    cdk = cache_diff_k[lyr].reshape(db, past * N_HEADS, LANES)
    cdv = cache_diff_v[lyr].reshape(db, past * N_HEADS, LANES)
    od_s = _diff_decode(lams, sub_g[lyr].reshape(1, LANES), smp(qd_s), cdk, cdv, smp(kdb_s), smp(vdb_s),
                        chunk=2048)
    sb_t = lambda c: jnp.transpose(c[lyr], (0, 2, 3, 1)).reshape(db, SB_WIDTH, past)
    new_t = lambda a: jnp.pad(jnp.swapaxes(smp(a), 1, 2), ((0, 0), (0, 0), (0, LANES - dseq)))
    os_s = _sb_decode(smp(qs_s), sb_t(cache_sb_k), sb_t(cache_sb_v), new_t(ksb_s), new_t(vsb_s))

    y_p = _out(x_main, od_p.reshape(nb * seq, DIFF_WIDTH), os_p.reshape(nb * seq, SB_WIDTH),
               gmix, w_gate, wdo, wso, wo, gffn, w1, w2, tm=512)
    y_s = _out(x_sample.reshape(db * dseq, D_MODEL), od_s.reshape(db * dseq, DIFF_WIDTH),
               os_s.reshape(db * dseq, SB_WIDTH), gmix, w_gate, wdo, wso, wo, gffn, w1, w2, tm=db * dseq)

    def diff_prompt_cache(head_major, small):
        meta = small[:N_META].reshape(N_META * N_HEADS, 2 * HEAD_DIM)
        full = _fill_lead_tokens(head_major, meta, nb=nb)
        return full.reshape(1, nb, seq + N_META, N_HEADS, 2 * HEAD_DIM)

    def sb_prompt_cache(main_t, small):
        meta_t = jnp.broadcast_to(small[:N_META].T[None], (nb, SB_WIDTH, N_META))
        full = jnp.concatenate([meta_t, main_t], axis=2).reshape(nb, N_HEADS, HEAD_DIM, seq + N_META)
        return jnp.transpose(full, (0, 3, 1, 2))[None]

    def sample_cache(small, dim):
        return small[N_META:].reshape(1, db, dseq, N_HEADS, dim)

    return (y_p.reshape(nb, seq, D_MODEL), y_s.reshape(db, dseq, D_MODEL),
            diff_prompt_cache(kd_hm, kd_s), diff_prompt_cache(vd_hm, vd_s),
            sb_prompt_cache(kst_m, ks_s), sb_prompt_cache(vstf_m, vs_s),
            sample_cache(kd_s, 2 * HEAD_DIM), sample_cache(vd_s, 2 * HEAD_DIM),
            sample_cache(ks_s, HEAD_DIM), sample_cache(vs_s, HEAD_DIM))
```

```python
import functools
import math

import jax
import jax.numpy as jnp
from jax import lax
from jax.experimental import pallas as pl
from jax.experimental.pallas import tpu as pltpu

F32 = jnp.float32
BF16 = jnp.bfloat16

D_MODEL = 1024
N_META = 16
CHUNK = 64
N_HEADS = 8
HEAD_DIM = 64
DIFF_WIDTH = N_HEADS * 2 * HEAD_DIM
SB_WIDTH = N_HEADS * HEAD_DIM
QKV_COLS = 3 * DIFF_WIDTH + 3 * SB_WIDTH
D_FF = 4 * D_MODEL
ROT_DIM = HEAD_DIM // 4
ROPE_THETA = 500000.0
EPS = 1e-6
NEG = -1e30
SKIP_BELOW = -160.0
LOG2E = math.log2(math.e)
Q_SCALE = HEAD_DIM ** -0.5 * LOG2E
ONES_ROWS = 16
LAM_INIT = 0.8 - 0.6 * math.exp(-0.3 * 0)

LANES = 128
MXU_TILE = 256
COL_BLOCK = 512
VMEM_LIMIT = 56 * 1024 * 1024

_NT = (((1,), (1,)), ((), ()))


def _rms(x):
    return x * lax.rsqrt(jnp.mean(x * x, axis=-1, keepdims=True) + EPS)


def _const_spec(shape):
    return pl.BlockSpec(shape, lambda *_: (0,) * len(shape), pipeline_mode=pl.Buffered(1))


def _proj_kernel(x_ref, gmix_ref, w_ref, qg_ref, kg_ref, gmat_ref, cos_ref, sa_ref, sb_ref,
                 qd_ref, kd_ref, kdb_ref, vd_ref, vdb_ref, qs_ref, ks_ref, ksb_ref, vs_ref, vsb_ref,
                 *, prompt_layout):
    tm = x_ref.shape[0]
    heads_per_block = COL_BLOCK // LANES

    def store_diff(ref, j, y):
        if not prompt_layout:
            ref[:, j * COL_BLOCK:(j + 1) * COL_BLOCK] = y
            return
        for hh in range(heads_per_block):
            ref[pl.ds(j * heads_per_block + hh, tm, stride=N_HEADS), :] = y[:, hh * LANES:(hh + 1) * LANES]

    h = (_rms(x_ref[...]) * gmix_ref[...]).astype(BF16)
    cos = cos_ref[...]
    sa = sa_ref[...]
    sb = sb_ref[...]

    def col(j):
        return jnp.dot(h, w_ref[:, j * COL_BLOCK:(j + 1) * COL_BLOCK], preferred_element_type=F32)

    def normed_rot(y, g):
        sq = (y * y).astype(BF16)
        msq = jnp.concatenate(
            [jnp.dot(sq[:, c * MXU_TILE:(c + 1) * MXU_TILE], gmat_ref[...], preferred_element_type=F32)
             for c in range(COL_BLOCK // MXU_TILE)], axis=1)
        yn = y * lax.rsqrt(msq + EPS) * g
        parts = []
        for c in range(COL_BLOCK // LANES):
            t = yn[:, c * LANES:(c + 1) * LANES]
            parts.append(t * cos + pltpu.roll(t, LANES - ROT_DIM // 2, 1) * sa
                         + pltpu.roll(t, ROT_DIM // 2, 1) * sb)
        return jnp.concatenate(parts, axis=1)

    for j in range(2):
        sl = slice(j * COL_BLOCK, (j + 1) * COL_BLOCK)
        q = normed_rot(col(j), qg_ref[...])
        qd_ref[:, sl] = (q * Q_SCALE).astype(BF16)
        k = normed_rot(col(2 + j), kg_ref[...])
        store_diff(kd_ref, j, k)
        kdb_ref[:, sl] = k.astype(BF16)
        v = col(4 + j)
        store_diff(vd_ref, j, v)
        if prompt_layout:
            vdb_ref[sl, :] = v.T.astype(BF16)
        else:
            vdb_ref[:, sl] = v.astype(BF16)
    qs_ref[...] = (col(6) * Q_SCALE).astype(BF16)
    k = col(7)
    ksb_ref[...] = k.astype(BF16)
    v = col(8)
    if prompt_layout:
        ks_ref[...] = k.T
        vt = v.T
        vs_ref[...] = vt
        vsb_ref[...] = vt.astype(BF16)
    else:
        ks_ref[...] = k
        vs_ref[...] = v
        vsb_ref[...] = v.astype(BF16)


def _proj(x, gmix, w_qkv, qg, kg, gmat, cos, sa, sb, *, tm, prompt_layout, lead=0):
    rows = x.shape[0]
    seq = cos.shape[0]
    n_pos_tiles = seq // tm
    row = lambda w: pl.BlockSpec((tm, w), lambda i: (i, 0))
    tab = pl.BlockSpec((tm, LANES), lambda i: (i % n_pos_tiles, 0))
    wide = lambda dt: jax.ShapeDtypeStruct((rows, DIFF_WIDTH), dt)
    narrow = lambda dt: jax.ShapeDtypeStruct((rows, SB_WIDTH), dt)
    if prompt_layout:
        nb = rows // seq
        t_spec = lambda w: pl.BlockSpec((None, w, tm), lambda i: (i // n_pos_tiles, 0, i % n_pos_tiles))
        t_shape = lambda w, dt: jax.ShapeDtypeStruct((nb, w, seq), dt)
        hm_spec = pl.BlockSpec(
            (pl.Element(tm * N_HEADS), pl.Element(LANES)),
            lambda i: (((i // n_pos_tiles) * (seq + lead) + lead + (i % n_pos_tiles) * tm) * N_HEADS, 0))
        hm_shape = jax.ShapeDtypeStruct((nb * (seq + lead) * N_HEADS, LANES), F32)
        out_specs = [row(DIFF_WIDTH), hm_spec, row(DIFF_WIDTH), hm_spec, t_spec(DIFF_WIDTH),
                     row(SB_WIDTH), t_spec(SB_WIDTH), row(SB_WIDTH), t_spec(SB_WIDTH), t_spec(SB_WIDTH)]
        out_shape = [wide(BF16), hm_shape, wide(BF16), hm_shape, t_shape(DIFF_WIDTH, BF16),
                     narrow(BF16), t_shape(SB_WIDTH, F32), narrow(BF16), t_shape(SB_WIDTH, F32),
                     t_shape(SB_WIDTH, BF16)]
    else:
        out_specs = [row(DIFF_WIDTH)] * 5 + [row(SB_WIDTH)] * 5
        out_shape = [wide(BF16), wide(F32), wide(BF16), wide(F32), wide(BF16),
                     narrow(BF16), narrow(F32), narrow(BF16), narrow(F32), narrow(BF16)]
    return pl.pallas_call(
        functools.partial(_proj_kernel, prompt_layout=prompt_layout),
        grid=(rows // tm,),
        in_specs=[row(D_MODEL), _const_spec((1, D_MODEL)), _const_spec((D_MODEL, QKV_COLS)),
                  _const_spec((1, COL_BLOCK)), _const_spec((1, COL_BLOCK)),
                  _const_spec((MXU_TILE, MXU_TILE)), tab, tab, tab],
        out_specs=out_specs,
        out_shape=out_shape,
        compiler_params=pltpu.CompilerParams(dimension_semantics=("arbitrary",),
                                             vmem_limit_bytes=VMEM_LIMIT),
        name="proj",
    )(x, gmix, w_qkv, qg, kg, gmat, cos, sa, sb)


def _fill_lead_kernel(lead_ref, big_ref, out_ref):
    del big_ref
    out_ref[...] = lead_ref[...]


def _fill_lead_tokens(big, lead_rows, *, nb):
    n = lead_rows.shape[0]
    per_batch = big.shape[0] // nb
    return pl.pallas_call(
        _fill_lead_kernel,
        grid=(nb,),
        in_specs=[_const_spec((n, LANES)), pl.BlockSpec(memory_space=pl.ANY)],
        out_specs=pl.BlockSpec((pl.Element(n), pl.Element(LANES)), lambda b: (b * per_batch, 0)),
        out_shape=jax.ShapeDtypeStruct(big.shape, big.dtype),
        input_output_aliases={1: 0},
        compiler_params=pltpu.CompilerParams(dimension_semantics=("arbitrary",)),
        name="fill_lead_tokens",
    )(lead_rows, big)


def _split_halves(q):
    lane = lax.broadcasted_iota(jnp.int32, q.shape, 1)
    zero = jnp.zeros_like(q)
    return jnp.where(lane < HEAD_DIM, q, zero), jnp.where(lane >= HEAD_DIM, q, zero)


def _diff_kernel(lq1_ref, lk1_ref, lq2_ref, lk2_ref, subg_ref, q_ref, kp_ref, vp_ref, km_ref, vm_ref,
                 o_ref, m_ref, acc_ref, s_ref, *, tq, tk, tkp, pref_valid, n_diag):
    i = pl.program_id(2)
    n_full = i * n_diag
    qs = _split_halves(q_ref[0])
    m_ref[...] = jnp.full(m_ref.shape, NEG, F32)
    acc_ref[...] = jnp.zeros(acc_ref.shape, F32)

    def step(nxt, cur):
        if cur is not None:
            slot, vtblk, mask, qlo = cur
            n = vtblk.shape[1]
            vt_ones = jnp.concatenate([vtblk, jnp.ones((ONES_ROWS, n), BF16)], axis=0)
        for a in range(2):
            if cur is not None:
                s = s_ref[slot, a, :n, qlo:]
                if mask is not None:
                    s = jnp.where(mask[:, qlo:], s, NEG)
                m_prev = m_ref[a, :, qlo:]
                m_new = jnp.maximum(m_prev, jnp.max(s, axis=0, keepdims=True))
                alpha = jnp.exp2(m_prev - m_new)
                p = jnp.exp2(s - m_new)
            if nxt is not None:
                kblk, nslot, nqlo = nxt
                s_ref[nslot, a, :kblk.shape[0], nqlo:] = lax.dot_general(
                    kblk, qs[a][nqlo:], _NT, preferred_element_type=F32)
            if cur is not None:
                acc_ref[a, :, qlo:] = (alpha * acc_ref[a, :, qlo:]
                                       + jnp.dot(vt_ones, p.astype(BF16), preferred_element_type=F32))
                m_ref[a, :, qlo:] = m_new

    def main_k(g):
        return km_ref[0, pl.ds(pl.multiple_of(g * tk, tk), tk), :]

    def main_vt(g):
        return vm_ref[0, :, pl.ds(pl.multiple_of(g * tk, tk), tk)]

    step((kp_ref[0], 1, 0), None)
    step((main_k(0), 0, 0), (1, vp_ref[0], lax.broadcasted_iota(jnp.int32, (tkp, tq), 0) < pref_valid, 0))

    def full_body(t, carry):
        step((main_k(2 * t + 1), 1, 0), (0, main_vt(2 * t), None, 0))
        step((main_k(2 * t + 2), 0, 0), (1, main_vt(2 * t + 1), None, 0))
        return carry
    lax.fori_loop(0, i * (n_diag // 2), full_body, 0)

    key_idx = lax.broadcasted_iota(jnp.int32, (tk, tq), 0)
    q_chunk = lax.broadcasted_iota(jnp.int32, (tk, tq), 1) // CHUNK
    for d in range(n_diag):
        g = n_full + d
        nxt = (main_k(g + 1), (d + 1) % 2, (d + 1) * tk) if d + 1 < n_diag else None
        step(nxt, (d % 2, main_vt(g), (key_idx + d * tk) // CHUNK <= q_chunk, d * tk))

    lam = (jnp.exp(jnp.sum(lq1_ref[...] * lk1_ref[...], axis=1, keepdims=True))
           - jnp.exp(jnp.sum(lq2_ref[...] * lk2_ref[...], axis=1, keepdims=True)) + LAM_INIT)
    o = (acc_ref[0, :LANES] / acc_ref[0, LANES:LANES + 1]
         - lam * (acc_ref[1, :LANES] / acc_ref[1, LANES:LANES + 1]))
    o = o * lax.rsqrt(jnp.mean(o * o, axis=0, keepdims=True) + EPS) * subg_ref[...] * (1.0 - LAM_INIT)
    o_ref[0] = o.T.astype(BF16)


def _diff_attn(lams, subg_col, q, kp, vtp, km, vtm, *, tq, tk, pref_valid):
    nb, tq_total, _ = q.shape
    tkp = kp.shape[1]
    t_main = km.shape[1]
    n_diag = tq // tk
    assert n_diag % 2 == 0, "main blocks are consumed in pairs"
    kernel = functools.partial(_diff_kernel, tq=tq, tk=tk, tkp=tkp, pref_valid=pref_valid, n_diag=n_diag)
    small = _const_spec((1, HEAD_DIM))
    return pl.pallas_call(
        kernel,
        grid=(nb, N_HEADS, tq_total // tq),
        in_specs=[small, small, small, small, _const_spec((LANES, 1)),
                  pl.BlockSpec((1, tq, LANES), lambda b, h, i: (b, i, h)),
                  pl.BlockSpec((1, tkp, LANES), lambda b, h, i: (0, 0, h)),
                  pl.BlockSpec((1, LANES, tkp), lambda b, h, i: (0, h, 0)),
                  pl.BlockSpec((1, t_main, LANES), lambda b, h, i: (b, 0, h)),
                  pl.BlockSpec((1, LANES, t_main), lambda b, h, i: (b, h, 0))],
        out_specs=pl.BlockSpec((1, tq, LANES), lambda b, h, i: (b, i, h)),
        out_shape=jax.ShapeDtypeStruct((nb, tq_total, DIFF_WIDTH), BF16),
        scratch_shapes=[pltpu.VMEM((2, 1, tq), F32), pltpu.VMEM((2, LANES + ONES_ROWS, tq), F32),
                        pltpu.VMEM((2, 2, max(tk, tkp), tq), F32)],
        compiler_params=pltpu.CompilerParams(dimension_semantics=("arbitrary",) * 3,
                                             vmem_limit_bytes=VMEM_LIMIT),
        name="diff_attn",
    )(*lams, subg_col, q, kp, vtp, km, vtm)


def _suffix_matrix(n):
    r = lax.broadcasted_iota(jnp.int32, (n + ONES_ROWS, n), 0)
    c = lax.broadcasted_iota(jnp.int32, (n + ONES_ROWS, n), 1)
    return jnp.where((c >= r) | (r == n), 1.0, 0.0).astype(BF16)


def _sb_kernel(q_ref, kp_ref, vp_ref, km_ref, vm_ref, o_ref, c_ref, acc_ref, z_ref,
               *, tq, tk, tkp, pref_valid, n_diag):
    i = pl.program_id(2)
    n_full = i * n_diag
    qs = _split_halves(q_ref[0])
    c_ref[...] = jnp.zeros(c_ref.shape, F32)
    acc_ref[...] = jnp.zeros(acc_ref.shape, F32)

    def step(nxt, cur):
        if nxt is not None:
            kblk, nslot, nqlo = nxt
            for a in range(2):
                z_ref[nslot, a, :kblk.shape[0], nqlo:] = lax.dot_general(
                    kblk, qs[a][nqlo:], _NT, preferred_element_type=F32)
        if cur is None:
            return
        slot, vtblk, mask, sfx_mat, qlo, qhi = cur
        n = vtblk.shape[1]
        if mask is not None:
            mask = mask[:, qlo:qhi]
        for a in range(2):
            rows = slice(a * HEAD_DIM, (a + 1) * HEAD_DIM)
            u = z_ref[slot, a, :n, qlo:qhi]
            neg_part = jnp.minimum(u, 0.0)
            d = neg_part - u
            log1m = d - jnp.log(1.0 + jnp.exp2(neg_part + d)) * LOG2E
            if mask is not None:
                log1m = jnp.where(mask, log1m, 0.0)
            sfx = jnp.dot(sfx_mat, log1m.astype(BF16), preferred_element_type=F32)
            t = u + sfx[:n]
            if mask is not None:
                t = jnp.where(mask, t, NEG)
            pv = jnp.dot(vtblk[rows, :], jnp.exp2(t).astype(BF16), preferred_element_type=F32)
            acc_ref[rows, qlo:qhi] += pv * jnp.exp2(c_ref[a, :, qlo:qhi])
            c_ref[a, :, qlo:qhi] += sfx[n:n + 1]

    def main_k(g):
        return km_ref[0, pl.ds(pl.multiple_of(g * tk, tk), tk), :]

    def main_vt(g):
        return vm_ref[0, :, pl.ds(pl.multiple_of(g * tk, tk), tk)]

    sfx_main = _suffix_matrix(tk)
    sfx_pref = sfx_main if tkp == tk else _suffix_matrix(tkp)
    key = lax.broadcasted_iota(jnp.int32, (tk, tq), 0)
    qry = lax.broadcasted_iota(jnp.int32, (tk, tq), 1)

    def live(lo=0):
        return jnp.max(c_ref[:, :, lo:]) > SKIP_BELOW

    window = 2 * tk

    def fold(nxt, slot, vtblk, mask, sfx_mat, qlo):
        near_hi = min(qlo + window, tq)
        step(nxt, (slot, vtblk, mask, sfx_mat, qlo, near_hi))
        if near_hi < tq:
            @pl.when(live(near_hi))
            def _():
                step(None, (slot, vtblk, None, sfx_mat, near_hi, tq))

    step((kp_ref[0], 2, 0), None)
    step((main_k(n_full + n_diag - 1), 0, (n_diag - 1) * tk), None)
    for d in reversed(range(n_diag)):
        s = (n_diag - 1 - d) % 2
        if d > 0:
            nxt = (main_k(n_full + d - 1), 1 - s, (d - 1) * tk)
        else:
            nxt = (main_k(jnp.maximum(n_full - 1, 0)), 1 - s, 0)
        fold(nxt, s, main_vt(n_full + d), key + d * tk < qry, sfx_main, d * tk)

    def full_body(t):
        g = n_full - 1 - 2 * t
        fold((main_k(g - 1), 1, 0), 0, main_vt(g), None, sfx_main, 0)
        fold((main_k(jnp.maximum(g - 2, 0)), 0, 0), 1, main_vt(g - 1), None, sfx_main, 0)
        return t + 1
    n_pairs = i * (n_diag // 2)
    lax.while_loop(lambda t: jnp.logical_and(t < n_pairs, live()), full_body, jnp.int32(0))

    @pl.when(live())
    def _():
        pmask = lax.broadcasted_iota(jnp.int32, (tkp, tq), 0) < pref_valid
        step(None, (2, vp_ref[0], pmask, sfx_pref, 0, tq))
    o_ref[0] = acc_ref[...].T.astype(BF16)


def _sb_attn(q, kp, vtp, km, vtm, *, tq, tk, pref_valid):
    nb, tq_total, _ = q.shape
    tkp = kp.shape[1]
    t_main = km.shape[1]
    n_diag = tq // tk
    assert n_diag % 2 == 0, "main blocks are consumed in pairs"
    kernel = functools.partial(_sb_kernel, tq=tq, tk=tk, tkp=tkp, pref_valid=pref_valid, n_diag=n_diag)
    return pl.pallas_call(
        kernel,
        grid=(nb, SB_WIDTH // LANES, tq_total // tq),
        in_specs=[pl.BlockSpec((1, tq, LANES), lambda b, h, i: (b, i, h)),
                  pl.BlockSpec((1, tkp, LANES), lambda b, h, i: (0, 0, h)),
                  pl.BlockSpec((1, LANES, tkp), lambda b, h, i: (0, h, 0)),
                  pl.BlockSpec((1, t_main, LANES), lambda b, h, i: (b, 0, h)),
                  pl.BlockSpec((1, LANES, t_main), lambda b, h, i: (b, h, 0))],
        out_specs=pl.BlockSpec((1, tq, LANES), lambda b, h, i: (b, i, h)),
        out_shape=jax.ShapeDtypeStruct((nb, tq_total, SB_WIDTH), BF16),
        scratch_shapes=[pltpu.VMEM((2, 1, tq), F32), pltpu.VMEM((LANES, tq), F32),
                        pltpu.VMEM((3, 2, max(tk, tkp), tq), F32)],
        compiler_params=pltpu.CompilerParams(dimension_semantics=("arbitrary",) * 3,
                                             vmem_limit_bytes=VMEM_LIMIT),
        name="sb_attn",
    )(q, kp, vtp, km, vtm)


def _diff_decode_kernel(lq1_ref, lk1_ref, lq2_ref, lk2_ref, subg_ref, q_ref, ck_ref, cv_ref, kn_ref, vn_ref,
                        o_ref, m_ref, l_ref, acc_ref, *, chunk):
    kc = pl.program_id(1)
    nq = q_ref.shape[0]

    @pl.when(kc == 0)
    def _():
        m_ref[...] = jnp.full(m_ref.shape, NEG, F32)
        l_ref[...] = jnp.zeros(l_ref.shape, F32)
        acc_ref[...] = jnp.zeros(acc_ref.shape, F32)

    def update(h, k, v):
        n = k.shape[0]
        qs = _split_halves(q_ref[:, h * LANES:(h + 1) * LANES])
        v_ones = jnp.concatenate([v, jnp.ones((n, LANES), BF16)], axis=1)
        ps, alphas = [], []
        for a in range(2):
            s = lax.dot_general(qs[a], k, _NT, preferred_element_type=F32)
            m_prev = m_ref[2 * h + a]
            m_new = jnp.maximum(m_prev, jnp.max(s, axis=1, keepdims=True))
            alphas.append(jnp.exp2(m_prev - m_new))
            ps.append(jnp.exp2(s - m_new).astype(BF16))
            m_ref[2 * h + a] = m_new
        pv = jnp.dot(jnp.concatenate(ps, axis=0), v_ones, preferred_element_type=F32)
        for a in range(2):
            part = pv[a * nq:(a + 1) * nq]
            acc_ref[2 * h + a] = alphas[a] * acc_ref[2 * h + a] + part[:, :LANES]
            l_ref[2 * h + a] = alphas[a] * l_ref[2 * h + a] + part[:, LANES:]

    for h in range(N_HEADS):
        update(h, ck_ref[pl.ds(h, chunk, stride=N_HEADS), :].astype(BF16),
               cv_ref[pl.ds(h, chunk, stride=N_HEADS), :].astype(BF16))

    @pl.when(kc == pl.num_programs(1) - 1)
    def _():
        lam = (jnp.exp(jnp.sum(lq1_ref[...] * lk1_ref[...], axis=1, keepdims=True))
               - jnp.exp(jnp.sum(lq2_ref[...] * lk2_ref[...], axis=1, keepdims=True)) + LAM_INIT)
        for h in range(N_HEADS):
            cols = slice(h * LANES, (h + 1) * LANES)
            update(h, kn_ref[:, cols], vn_ref[:, cols])
            o = acc_ref[2 * h] / l_ref[2 * h] - lam * (acc_ref[2 * h + 1] / l_ref[2 * h + 1])
            o_ref[:, cols] = (_rms(o) * subg_ref[...] * (1.0 - LAM_INIT)).astype(BF16)


def _diff_decode(lams, subg_row, q, ck, cv, kn, vn, *, chunk):
    nb, nq, _ = q.shape
    n_chunks = ck.shape[1] // (chunk * N_HEADS)
    small = _const_spec((1, HEAD_DIM))
    tok = pl.BlockSpec((None, nq, DIFF_WIDTH), lambda b, c: (b, 0, 0))
    cache = pl.BlockSpec((None, chunk * N_HEADS, LANES), lambda b, c: (b, c, 0))
    return pl.pallas_call(
        functools.partial(_diff_decode_kernel, chunk=chunk),
        grid=(nb, n_chunks),
        in_specs=[small, small, small, small, _const_spec((1, LANES)), tok, cache, cache, tok, tok],
        out_specs=tok,
        out_shape=jax.ShapeDtypeStruct((nb, nq, DIFF_WIDTH), BF16),
        scratch_shapes=[pltpu.VMEM((2 * N_HEADS, nq, 1), F32), pltpu.VMEM((2 * N_HEADS, nq, LANES), F32),
                        pltpu.VMEM((2 * N_HEADS, nq, LANES), F32)],
        compiler_params=pltpu.CompilerParams(dimension_semantics=("arbitrary",) * 2,
                                             vmem_limit_bytes=VMEM_LIMIT),
        name="diff_decode",
    )(*lams, subg_row, q, ck, cv, kn, vn)


SB_BLOCK = 256


def _sb_decode_kernel(q_ref, ckt_ref, cvt_ref, knt_ref, vnt_ref, o_ref):
    nq = q_ref.shape[0]
    past = ckt_ref.shape[1]
    n_blk = past // SB_BLOCK
    def suffix_cols(n):
        r = lax.broadcasted_iota(jnp.int32, (n, n + LANES), 0)
        c = lax.broadcasted_iota(jnp.int32, (n, n + LANES), 1)
        return jnp.where((r > c) | (c >= n), 1.0, 0.0).astype(BF16)
    sfx_blk = suffix_cols(SB_BLOCK)
    sfx_new = suffix_cols(LANES)
    row = lax.broadcasted_iota(jnp.int32, (nq, LANES), 0)
    lane = lax.broadcasted_iota(jnp.int32, (nq, LANES), 1)
    new_mask = lane < row

    def log1m_of(u):
        nu = -u
        return jnp.minimum(nu, 0.0) - jnp.log(1.0 + jnp.exp2(jnp.minimum(u, nu))) * LOG2E

    for j in range(ckt_ref.shape[0] // LANES):
        rows = slice(j * LANES, (j + 1) * LANES)
        kt = ckt_ref[rows, :].astype(BF16)
        vt = cvt_ref[rows, :].astype(BF16)
        qs = _split_halves(q_ref[:, rows])
        ws, wns = [], []
        for a in range(2):
            un = jnp.dot(qs[a], knt_ref[rows, :], preferred_element_type=F32)
            ln = jnp.where(new_mask, log1m_of(un), 0.0)
            sn = jnp.dot(ln.astype(BF16), sfx_new, preferred_element_type=F32)
            wns.append(jnp.exp2(jnp.where(new_mask, un + ln + sn[:, :LANES], NEG)).astype(BF16))
            carry = sn[:, LANES:]
            u = jnp.dot(qs[a], kt, preferred_element_type=F32)
            l1m = log1m_of(u)
            stacked = jnp.concatenate(
                [l1m[:, b * SB_BLOCK:(b + 1) * SB_BLOCK] for b in range(n_blk)], axis=0).astype(BF16)
            sfx = jnp.dot(stacked, sfx_blk, preferred_element_type=F32)
            ts = [None] * n_blk
            for b in reversed(range(n_blk)):
                cols = slice(b * SB_BLOCK, (b + 1) * SB_BLOCK)
                part = sfx[b * nq:(b + 1) * nq]
                c2 = jnp.concatenate([carry] * (SB_BLOCK // LANES), axis=1)
                ts[b] = u[:, cols] + l1m[:, cols] + part[:, :SB_BLOCK] + c2
                carry = carry + part[:, SB_BLOCK:]
            ws.append(jnp.exp2(jnp.concatenate(ts, axis=1)).astype(BF16))
        o = (lax.dot_general(jnp.concatenate(ws, axis=0), vt, _NT, preferred_element_type=F32)
             + lax.dot_general(jnp.concatenate(wns, axis=0), vnt_ref[rows, :], _NT,
                               preferred_element_type=F32))
        o_ref[:, rows] = jnp.where(lane < HEAD_DIM, o[:nq], o[nq:]).astype(BF16)


def _sb_decode(q, ckt, cvt, knt, vnt):
    nb, nq, _ = q.shape
    assert nq <= LANES
    past = ckt.shape[2]
    width = 2 * LANES
    tok = pl.BlockSpec((None, nq, width), lambda b, g: (b, 0, g))
    cache = pl.BlockSpec((None, width, past), lambda b, g: (b, g, 0))
    new = pl.BlockSpec((None, width, LANES), lambda b, g: (b, g, 0))
    return pl.pallas_call(
        _sb_decode_kernel,
        grid=(nb, SB_WIDTH // width),
        in_specs=[tok, cache, cache, new, new],
        out_specs=tok,
        out_shape=jax.ShapeDtypeStruct((nb, nq, SB_WIDTH), BF16),
        compiler_params=pltpu.CompilerParams(dimension_semantics=("arbitrary",) * 2,
                                             vmem_limit_bytes=VMEM_LIMIT),
        name="sb_decode",
    )(q, ckt, cvt, knt, vnt)


def _out_kernel(x_ref, od_ref, os_ref, gmix_ref, wg_ref, wdo_ref, wso_ref, wo_ref, gffn_ref,
                w1_ref, w2_ref, y_ref):
    x = x_ref[...]
    h = (_rms(x) * gmix_ref[...]).astype(BF16)
    gate = jax.nn.sigmoid(jnp.dot(h, wg_ref[...], preferred_element_type=F32))
    a = jnp.dot(od_ref[...], wdo_ref[...], preferred_element_type=F32)
    b = jnp.dot(os_ref[...], wso_ref[...], preferred_element_type=F32)
    merged = (gate[:, :D_MODEL] * a + gate[:, D_MODEL:] * b).astype(BF16)
    x1 = x + jnp.dot(merged, wo_ref[...], preferred_element_type=F32)
    h2 = (_rms(x1) * gffn_ref[...]).astype(BF16)
    y = x1
    for c in range(D_FF // D_MODEL):
        sl = slice(c * D_MODEL, (c + 1) * D_MODEL)
        f = jnp.maximum(jnp.dot(h2, w1_ref[:, sl], preferred_element_type=F32), 0.0)
        y = y + jnp.dot((f * f).astype(BF16), w2_ref[sl, :], preferred_element_type=F32)
    y_ref[...] = y


def _out(x, od, osb, gmix, wg, wdo, wso, wo, gffn, w1, w2, *, tm):
    rows = x.shape[0]
    row = lambda w: pl.BlockSpec((tm, w), lambda i: (i, 0))
    return pl.pallas_call(
        _out_kernel,
        grid=(rows // tm,),
        in_specs=[row(D_MODEL), row(DIFF_WIDTH), row(SB_WIDTH), _const_spec((1, D_MODEL)),
                  _const_spec((D_MODEL, 2 * D_MODEL)), _const_spec((DIFF_WIDTH, D_MODEL)),
                  _const_spec((SB_WIDTH, D_MODEL)), _const_spec((D_MODEL, D_MODEL)),
                  _const_spec((1, D_MODEL)), _const_spec((D_MODEL, D_FF)), _const_spec((D_FF, D_MODEL))],
        out_specs=row(D_MODEL),
        out_shape=jax.ShapeDtypeStruct((rows, D_MODEL), F32),
        compiler_params=pltpu.CompilerParams(dimension_semantics=("arbitrary",),
                                             vmem_limit_bytes=VMEM_LIMIT),
        name="out",
    )(x, od, osb, gmix, wg, wdo, wso, wo, gffn, w1, w2)


def _rope_tables(pos):
    half = ROT_DIM // 2
    d = jnp.arange(LANES, dtype=jnp.int32) % HEAD_DIM
    inv = ROPE_THETA ** (-(2 * (d % half)).astype(F32) / ROT_DIM)
    ang = pos.astype(F32)[:, None] * inv[None, :]
    cos, sin = jnp.cos(ang), jnp.sin(ang)
    cos_t = jnp.where(d < ROT_DIM, cos, 1.0)
    sa_t = jnp.where(d < half, -sin, 0.0)
    sb_t = jnp.where((d >= half) & (d < ROT_DIM), sin, 0.0)
    return cos_t, sa_t, sb_t


def kernel(x_prompt, x_sample, cache_diff_k, cache_diff_v, cache_sb_k, cache_sb_v, meta_tokens,
           g_mix, w_in, q_norm_g, k_norm_g, lam_q1, lam_k1, lam_q2, lam_k2, sub_g,
           w_diff_out, w_sb_out, w_out, g_ffn, w_ff1, w_ff2):
    nb, seq, _ = x_prompt.shape
    db, dseq, _ = x_sample.shape
    past = cache_diff_k.shape[2]
    lyr = 0

    w_qkv, w_gate = w_in[lyr, :, :QKV_COLS].astype(BF16), w_in[lyr, :, QKV_COLS:].astype(BF16)
    wdo, wso, wo = (w_diff_out[lyr].astype(BF16), w_sb_out[lyr].astype(BF16), w_out[lyr].astype(BF16))
    w1, w2 = w_ff1[lyr].astype(BF16), w_ff2[lyr].astype(BF16)
    gmix = g_mix[lyr].reshape(1, D_MODEL)
    gffn = g_ffn[lyr].reshape(1, D_MODEL)
    qg = jnp.tile(q_norm_g[lyr], COL_BLOCK // HEAD_DIM).reshape(1, COL_BLOCK)
    kg = jnp.tile(k_norm_g[lyr], COL_BLOCK // HEAD_DIM).reshape(1, COL_BLOCK)
    subg_col = sub_g[lyr].reshape(LANES, 1)
    lams = [t[lyr].reshape(1, HEAD_DIM) for t in (lam_q1, lam_k1, lam_q2, lam_k2)]
    grp = jnp.arange(MXU_TILE, dtype=jnp.int32) // HEAD_DIM
    gmat = jnp.where(grp[:, None] == grp[None, :], 1.0 / HEAD_DIM, 0.0).astype(BF16)

    main_pos = N_META + jnp.arange(seq, dtype=jnp.int32)
    small_pos = jnp.concatenate([jnp.arange(N_META, dtype=jnp.int32),
                                 jnp.tile(past + jnp.arange(dseq, dtype=jnp.int32), db)])
    x_main = x_prompt.reshape(nb * seq, D_MODEL)
    x_small = jnp.concatenate([meta_tokens.astype(F32), x_sample.reshape(db * dseq, D_MODEL)], axis=0)
    pm = _proj(x_main, gmix, w_qkv, qg, kg, gmat, *_rope_tables(main_pos), tm=512, prompt_layout=True,
               lead=N_META)
    ps = _proj(x_small, gmix, w_qkv, qg, kg, gmat, *_rope_tables(small_pos), tm=x_small.shape[0],
               prompt_layout=False)
    qd_m, kd_hm, kdb_m, vd_hm, vdt_m, qs_m, kst_m, ksb_m, vstf_m, vst_m = pm
    qd_s, kd_s, kdb_s, vd_s, vdb_s, qs_s, ks_s, ksb_s, vs_s, vsb_s = ps

    def bt(a, n, t):
        return a.reshape(n, t, a.shape[-1])

    def meta_keys(a):
        return jnp.pad(a[:N_META], ((0, LANES - N_META), (0, 0)))[None]

    def meta_vals_t(a):
        return jnp.swapaxes(meta_keys(a), 1, 2)

    od_p = _diff_attn(lams, subg_col, bt(qd_m, nb, seq), meta_keys(kdb_s), meta_vals_t(vdb_s),
                      bt(kdb_m, nb, seq), vdt_m,
                      tq=2048, tk=512, pref_valid=N_META)
    os_p = _sb_attn(bt(qs_m, nb, seq), meta_keys(ksb_s), meta_vals_t(vsb_s),
                    bt(ksb_m, nb, seq), vst_m,
                    tq=1024, tk=256, pref_valid=N_META)

    smp = lambda a: bt(a[N_META:], db, dseq)
    cdk = cache_diff_k[lyr].reshape(db, past * N_HEADS, LANES)
    cdv = cache_diff_v[lyr].reshape(db, past * N_HEADS, LANES)
    od_s = _diff_decode(lams, sub_g[lyr].reshape(1, LANES), smp(qd_s), cdk, cdv, smp(kdb_s), smp(vdb_s),
                        chunk=2048)
    sb_t = lambda c: jnp.transpose(c[lyr], (0, 2, 3, 1)).reshape(db, SB_WIDTH, past)
    new_t = lambda a: jnp.pad(jnp.swapaxes(smp(a), 1, 2), ((0, 0), (0, 0), (0, LANES - dseq)))
    os_s = _sb_decode(smp(qs_s), sb_t(cache_sb_k), sb_t(cache_sb_v), new_t(ksb_s), new_t(vsb_s))

    y_p = _out(x_main, od_p.reshape(nb * seq, DIFF_WIDTH), os_p.reshape(nb * seq, SB_WIDTH),
               gmix, w_gate, wdo, wso, wo, gffn, w1, w2, tm=512)
    y_s = _out(x_sample.reshape(db * dseq, D_MODEL), od_s.reshape(db * dseq, DIFF_WIDTH),
               os_s.reshape(db * dseq, SB_WIDTH), gmix, w_gate, wdo, wso, wo, gffn, w1, w2, tm=db * dseq)

    def diff_prompt_cache(head_major, small):
        meta = small[:N_META].reshape(N_META * N_HEADS, 2 * HEAD_DIM)
        full = _fill_lead_tokens(head_major, meta, nb=nb)
        return full.reshape(1, nb, seq + N_META, N_HEADS, 2 * HEAD_DIM)

    def sb_prompt_cache(main_t, small):
        meta_t = jnp.broadcast_to(small[:N_META].T[None], (nb, SB_WIDTH, N_META))
        full = jnp.concatenate([meta_t, main_t], axis=2).reshape(nb, N_HEADS, HEAD_DIM, seq + N_META)
        return jnp.transpose(full, (0, 3, 1, 2))[None]

    def sample_cache(small, dim):
        return small[N_META:].reshape(1, db, dseq, N_HEADS, dim)

    return (y_p.reshape(nb, seq, D_MODEL), y_s.reshape(db, dseq, D_MODEL),
            diff_prompt_cache(kd_hm, kd_s), diff_prompt_cache(vd_hm, vd_s),
            sb_prompt_cache(kst_m, ks_s), sb_prompt_cache(vstf_m, vs_s),
            sample_cache(kd_s, 2 * HEAD_DIM), sample_cache(vd_s, 2 * HEAD_DIM),
            sample_cache(ks_s, HEAD_DIM), sample_cache(vs_s, HEAD_DIM))
```

```python
import functools
import math

import jax
import jax.numpy as jnp
from jax import lax
from jax.experimental import pallas as pl
from jax.experimental.pallas import tpu as pltpu

F32 = jnp.float32
BF16 = jnp.bfloat16

D_MODEL = 1024
N_META = 16
CHUNK = 64
N_HEADS = 8
HEAD_DIM = 64
DIFF_WIDTH = N_HEADS * 2 * HEAD_DIM
SB_WIDTH = N_HEADS * HEAD_DIM
QKV_COLS = 3 * DIFF_WIDTH + 3 * SB_WIDTH
D_FF = 4 * D_MODEL
ROT_DIM = HEAD_DIM // 4
ROPE_THETA = 500000.0
EPS = 1e-6
NEG = -1e30
SKIP_BELOW = -160.0
LOG2E = math.log2(math.e)
Q_SCALE = HEAD_DIM ** -0.5 * LOG2E
ONES_ROWS = 16
LAM_INIT = 0.8 - 0.6 * math.exp(-0.3 * 0)

LANES = 128
MXU_TILE = 256
COL_BLOCK = 512
VMEM_LIMIT = 56 * 1024 * 1024

ROW_TILE = 512
DIFF_Q_TILE, DIFF_KEY_BLOCK = 2048, 512
SB_Q_TILE, SB_KEY_BLOCK = 1024, MXU_TILE
DECODE_CHUNK = 2048

_NT = (((1,), (1,)), ((), ()))


def _rms(x):
    return x * lax.rsqrt(jnp.mean(x * x, axis=-1, keepdims=True) + EPS)


def _const_spec(shape):
    return pl.BlockSpec(shape, lambda *_: (0,) * len(shape), pipeline_mode=pl.Buffered(1))


def _proj_kernel(x_ref, gmix_ref, w_ref, qg_ref, kg_ref, gmat_ref, cos_ref, sa_ref, sb_ref,
                 qd_ref, kd_ref, kdb_ref, vd_ref, vdb_ref, qs_ref, ks_ref, ksb_ref, vs_ref, vsb_ref,
                 *, prompt_layout):
    tm = x_ref.shape[0]
    heads_per_block = COL_BLOCK // LANES

    def store_diff(ref, j, y):
        if not prompt_layout:
            ref[:, j * COL_BLOCK:(j + 1) * COL_BLOCK] = y
            return
        for hh in range(heads_per_block):
            ref[pl.ds(j * heads_per_block + hh, tm, stride=N_HEADS), :] = y[:, hh * LANES:(hh + 1) * LANES]

    h = (_rms(x_ref[...]) * gmix_ref[...]).astype(BF16)
    cos = cos_ref[...]
    sa = sa_ref[...]
    sb = sb_ref[...]

    def col(j):
        return jnp.dot(h, w_ref[:, j * COL_BLOCK:(j + 1) * COL_BLOCK], preferred_element_type=F32)

    def normed_rot(y, g):
        sq = (y * y).astype(BF16)
        msq = jnp.concatenate(
            [jnp.dot(sq[:, c * MXU_TILE:(c + 1) * MXU_TILE], gmat_ref[...], preferred_element_type=F32)
             for c in range(COL_BLOCK // MXU_TILE)], axis=1)
        yn = y * lax.rsqrt(msq + EPS) * g
        parts = []
        for c in range(COL_BLOCK // LANES):
            t = yn[:, c * LANES:(c + 1) * LANES]
            parts.append(t * cos + pltpu.roll(t, LANES - ROT_DIM // 2, 1) * sa
                         + pltpu.roll(t, ROT_DIM // 2, 1) * sb)
        return jnp.concatenate(parts, axis=1)

    for j in range(2):
        sl = slice(j * COL_BLOCK, (j + 1) * COL_BLOCK)
        q = normed_rot(col(j), qg_ref[...])
        qd_ref[:, sl] = (q * Q_SCALE).astype(BF16)
        k = normed_rot(col(2 + j), kg_ref[...])
        store_diff(kd_ref, j, k)
        kdb_ref[:, sl] = k.astype(BF16)
        v = col(4 + j)
        store_diff(vd_ref, j, v)
        if prompt_layout:
            vdb_ref[sl, :] = v.T.astype(BF16)
        else:
            vdb_ref[:, sl] = v.astype(BF16)
    qs_ref[...] = (col(6) * Q_SCALE).astype(BF16)
    k = col(7)
    ksb_ref[...] = k.astype(BF16)
    v = col(8)
    if prompt_layout:
        ks_ref[...] = k.T
        vt = v.T
        vs_ref[...] = vt
        vsb_ref[...] = vt.astype(BF16)
    else:
        ks_ref[...] = k
        vs_ref[...] = v
        vsb_ref[...] = v.astype(BF16)


def _proj(x, gmix, w_qkv, qg, kg, gmat, cos, sa, sb, *, tm, prompt_layout, lead=0):
    rows = x.shape[0]
    seq = cos.shape[0]
    n_pos_tiles = seq // tm
    row = lambda w: pl.BlockSpec((tm, w), lambda i: (i, 0))
    tab = pl.BlockSpec((tm, LANES), lambda i: (i % n_pos_tiles, 0))
    wide = lambda dt: jax.ShapeDtypeStruct((rows, DIFF_WIDTH), dt)
    narrow = lambda dt: jax.ShapeDtypeStruct((rows, SB_WIDTH), dt)
    if prompt_layout:
        nb = rows // seq
        t_spec = lambda w: pl.BlockSpec((None, w, tm), lambda i: (i // n_pos_tiles, 0, i % n_pos_tiles))
        t_shape = lambda w, dt: jax.ShapeDtypeStruct((nb, w, seq), dt)
        hm_spec = pl.BlockSpec(
            (pl.Element(tm * N_HEADS), pl.Element(LANES)),
            lambda i: (((i // n_pos_tiles) * (seq + lead) + lead + (i % n_pos_tiles) * tm) * N_HEADS, 0))
        hm_shape = jax.ShapeDtypeStruct((nb * (seq + lead) * N_HEADS, LANES), F32)
        out_specs = [row(DIFF_WIDTH), hm_spec, row(DIFF_WIDTH), hm_spec, t_spec(DIFF_WIDTH),
                     row(SB_WIDTH), t_spec(SB_WIDTH), row(SB_WIDTH), t_spec(SB_WIDTH), t_spec(SB_WIDTH)]
        out_shape = [wide(BF16), hm_shape, wide(BF16), hm_shape, t_shape(DIFF_WIDTH, BF16),
                     narrow(BF16), t_shape(SB_WIDTH, F32), narrow(BF16), t_shape(SB_WIDTH, F32),
                     t_shape(SB_WIDTH, BF16)]
    else:
        out_specs = [row(DIFF_WIDTH)] * 5 + [row(SB_WIDTH)] * 5
        out_shape = [wide(BF16), wide(F32), wide(BF16), wide(F32), wide(BF16),
                     narrow(BF16), narrow(F32), narrow(BF16), narrow(F32), narrow(BF16)]
    return pl.pallas_call(
        functools.partial(_proj_kernel, prompt_layout=prompt_layout),
        grid=(rows // tm,),
        in_specs=[row(D_MODEL), _const_spec((1, D_MODEL)), _const_spec((D_MODEL, QKV_COLS)),
                  _const_spec((1, COL_BLOCK)), _const_spec((1, COL_BLOCK)),
                  _const_spec((MXU_TILE, MXU_TILE)), tab, tab, tab],
        out_specs=out_specs,
        out_shape=out_shape,
        compiler_params=pltpu.CompilerParams(dimension_semantics=("arbitrary",),
                                             vmem_limit_bytes=VMEM_LIMIT),
        name="proj",
    )(x, gmix, w_qkv, qg, kg, gmat, cos, sa, sb)


def _fill_lead_kernel(lead_ref, big_ref, out_ref):
    del big_ref
    out_ref[...] = lead_ref[...]


def _fill_lead_tokens(big, lead_rows, *, nb):
    n = lead_rows.shape[0]
    per_batch = big.shape[0] // nb
    return pl.pallas_call(
        _fill_lead_kernel,
        grid=(nb,),
        in_specs=[_const_spec((n, LANES)), pl.BlockSpec(memory_space=pl.ANY)],
        out_specs=pl.BlockSpec((pl.Element(n), pl.Element(LANES)), lambda b: (b * per_batch, 0)),
        out_shape=jax.ShapeDtypeStruct(big.shape, big.dtype),
        input_output_aliases={1: 0},
        compiler_params=pltpu.CompilerParams(dimension_semantics=("arbitrary",)),
        name="fill_lead_tokens",
    )(lead_rows, big)


def _split_halves(q):
    lane = lax.broadcasted_iota(jnp.int32, q.shape, 1)
    zero = jnp.zeros_like(q)
    return jnp.where(lane < HEAD_DIM, q, zero), jnp.where(lane >= HEAD_DIM, q, zero)


def _diff_kernel(lq1_ref, lk1_ref, lq2_ref, lk2_ref, subg_ref, q_ref, kp_ref, vp_ref, km_ref, vm_ref,
                 o_ref, m_ref, acc_ref, s_ref, *, tq, tk, tkp, pref_valid, n_diag):
    i = pl.program_id(2)
    n_full = i * n_diag
    qs = _split_halves(q_ref[0])
    m_ref[...] = jnp.full(m_ref.shape, NEG, F32)
    acc_ref[...] = jnp.zeros(acc_ref.shape, F32)

    def step(nxt, cur):
        if cur is not None:
            slot, vtblk, mask, qlo = cur
            n = vtblk.shape[1]
            vt_ones = jnp.concatenate([vtblk, jnp.ones((ONES_ROWS, n), BF16)], axis=0)
        for a in range(2):
            if cur is not None:
                s = s_ref[slot, a, :n, qlo:]
                if mask is not None:
                    s = jnp.where(mask[:, qlo:], s, NEG)
                m_prev = m_ref[a, :, qlo:]
                m_new = jnp.maximum(m_prev, jnp.max(s, axis=0, keepdims=True))
                alpha = jnp.exp2(m_prev - m_new)
                p = jnp.exp2(s - m_new)
            if nxt is not None:
                kblk, nslot, nqlo = nxt
                s_ref[nslot, a, :kblk.shape[0], nqlo:] = lax.dot_general(
                    kblk, qs[a][nqlo:], _NT, preferred_element_type=F32)
            if cur is not None:
                acc_ref[a, :, qlo:] = (alpha * acc_ref[a, :, qlo:]
                                       + jnp.dot(vt_ones, p.astype(BF16), preferred_element_type=F32))
                m_ref[a, :, qlo:] = m_new

    def main_k(g):
        return km_ref[0, pl.ds(pl.multiple_of(g * tk, tk), tk), :]

    def main_vt(g):
        return vm_ref[0, :, pl.ds(pl.multiple_of(g * tk, tk), tk)]

    step((kp_ref[0], 1, 0), None)
    step((main_k(0), 0, 0), (1, vp_ref[0], lax.broadcasted_iota(jnp.int32, (tkp, tq), 0) < pref_valid, 0))

    def full_body(t, carry):
        step((main_k(2 * t + 1), 1, 0), (0, main_vt(2 * t), None, 0))
        step((main_k(2 * t + 2), 0, 0), (1, main_vt(2 * t + 1), None, 0))
        return carry
    lax.fori_loop(0, i * (n_diag // 2), full_body, 0)

    key_idx = lax.broadcasted_iota(jnp.int32, (tk, tq), 0)
    q_chunk = lax.broadcasted_iota(jnp.int32, (tk, tq), 1) // CHUNK
    for d in range(n_diag):
        g = n_full + d
        nxt = (main_k(g + 1), (d + 1) % 2, (d + 1) * tk) if d + 1 < n_diag else None
        step(nxt, (d % 2, main_vt(g), (key_idx + d * tk) // CHUNK <= q_chunk, d * tk))

    lam = (jnp.exp(jnp.sum(lq1_ref[...] * lk1_ref[...], axis=1, keepdims=True))
           - jnp.exp(jnp.sum(lq2_ref[...] * lk2_ref[...], axis=1, keepdims=True)) + LAM_INIT)
    o = (acc_ref[0, :LANES] / acc_ref[0, LANES:LANES + 1]
         - lam * (acc_ref[1, :LANES] / acc_ref[1, LANES:LANES + 1]))
    o = o * lax.rsqrt(jnp.mean(o * o, axis=0, keepdims=True) + EPS) * subg_ref[...] * (1.0 - LAM_INIT)
    o_ref[0] = o.T.astype(BF16)


def _diff_attn(lams, subg_col, q, kp, vtp, km, vtm, *, tq, tk, pref_valid):
    nb, tq_total, _ = q.shape
    tkp = kp.shape[1]
    t_main = km.shape[1]
    n_diag = tq // tk
    assert n_diag % 2 == 0, "main blocks are consumed in pairs"
    kernel = functools.partial(_diff_kernel, tq=tq, tk=tk, tkp=tkp, pref_valid=pref_valid, n_diag=n_diag)
    small = _const_spec((1, HEAD_DIM))
    return pl.pallas_call(
        kernel,
        grid=(nb, N_HEADS, tq_total // tq),
        in_specs=[small, small, small, small, _const_spec((LANES, 1)),
                  pl.BlockSpec((1, tq, LANES), lambda b, h, i: (b, i, h)),
                  pl.BlockSpec((1, tkp, LANES), lambda b, h, i: (0, 0, h)),
                  pl.BlockSpec((1, LANES, tkp), lambda b, h, i: (0, h, 0)),
                  pl.BlockSpec((1, t_main, LANES), lambda b, h, i: (b, 0, h)),
                  pl.BlockSpec((1, LANES, t_main), lambda b, h, i: (b, h, 0))],
        out_specs=pl.BlockSpec((1, tq, LANES), lambda b, h, i: (b, i, h)),
        out_shape=jax.ShapeDtypeStruct((nb, tq_total, DIFF_WIDTH), BF16),
        scratch_shapes=[pltpu.VMEM((2, 1, tq), F32), pltpu.VMEM((2, LANES + ONES_ROWS, tq), F32),
                        pltpu.VMEM((2, 2, max(tk, tkp), tq), F32)],
        compiler_params=pltpu.CompilerParams(dimension_semantics=("arbitrary",) * 3,
                                             vmem_limit_bytes=VMEM_LIMIT),
        name="diff_attn",
    )(*lams, subg_col, q, kp, vtp, km, vtm)


def _suffix_matrix(n):
    r = lax.broadcasted_iota(jnp.int32, (n + ONES_ROWS, n), 0)
    c = lax.broadcasted_iota(jnp.int32, (n + ONES_ROWS, n), 1)
    return jnp.where((c >= r) | (r == n), 1.0, 0.0).astype(BF16)


def _sb_kernel(q_ref, kp_ref, vp_ref, km_ref, vm_ref, o_ref, c_ref, acc_ref, z_ref,
               *, tq, tk, tkp, pref_valid, n_diag):
    i = pl.program_id(2)
    n_full = i * n_diag
    qs = _split_halves(q_ref[0])
    c_ref[...] = jnp.zeros(c_ref.shape, F32)
    acc_ref[...] = jnp.zeros(acc_ref.shape, F32)

    def step(nxt, cur):
        if nxt is not None:
            kblk, nslot, nqlo = nxt
            for a in range(2):
                z_ref[nslot, a, :kblk.shape[0], nqlo:] = lax.dot_general(
                    kblk, qs[a][nqlo:], _NT, preferred_element_type=F32)
        if cur is None:
            return
        slot, vtblk, mask, sfx_mat, qlo, qhi = cur
        n = vtblk.shape[1]
        if mask is not None:
            mask = mask[:, qlo:qhi]
        for a in range(2):
            rows = slice(a * HEAD_DIM, (a + 1) * HEAD_DIM)
            u = z_ref[slot, a, :n, qlo:qhi]
            neg_part = jnp.minimum(u, 0.0)
            d = neg_part - u
            log1m = d - jnp.log(1.0 + jnp.exp2(neg_part + d)) * LOG2E
            if mask is not None:
                log1m = jnp.where(mask, log1m, 0.0)
            sfx = jnp.dot(sfx_mat, log1m.astype(BF16), preferred_element_type=F32)
            t = u + sfx[:n]
            if mask is not None:
                t = jnp.where(mask, t, NEG)
            pv = jnp.dot(vtblk[rows, :], jnp.exp2(t).astype(BF16), preferred_element_type=F32)
            acc_ref[rows, qlo:qhi] += pv * jnp.exp2(c_ref[a, :, qlo:qhi])
            c_ref[a, :, qlo:qhi] += sfx[n:n + 1]

    def main_k(g):
        return km_ref[0, pl.ds(pl.multiple_of(g * tk, tk), tk), :]

    def main_vt(g):
        return vm_ref[0, :, pl.ds(pl.multiple_of(g * tk, tk), tk)]

    sfx_main = _suffix_matrix(tk)
    sfx_pref = sfx_main if tkp == tk else _suffix_matrix(tkp)
    key = lax.broadcasted_iota(jnp.int32, (tk, tq), 0)
    qry = lax.broadcasted_iota(jnp.int32, (tk, tq), 1)

    def live(lo=0):
        return jnp.max(c_ref[:, :, lo:]) > SKIP_BELOW

    window = 2 * tk

    def fold(nxt, slot, vtblk, mask, sfx_mat, qlo):
        near_hi = min(qlo + window, tq)
        step(nxt, (slot, vtblk, mask, sfx_mat, qlo, near_hi))
        if near_hi < tq:
            @pl.when(live(near_hi))
            def _():
                step(None, (slot, vtblk, None, sfx_mat, near_hi, tq))

    step((kp_ref[0], 2, 0), None)
    step((main_k(n_full + n_diag - 1), 0, (n_diag - 1) * tk), None)
    for d in reversed(range(n_diag)):
        s = (n_diag - 1 - d) % 2
        if d > 0:
            nxt = (main_k(n_full + d - 1), 1 - s, (d - 1) * tk)
        else:
            nxt = (main_k(jnp.maximum(n_full - 1, 0)), 1 - s, 0)
        fold(nxt, s, main_vt(n_full + d), key + d * tk < qry, sfx_main, d * tk)

    def full_body(t):
        g = n_full - 1 - 2 * t
        fold((main_k(g - 1), 1, 0), 0, main_vt(g), None, sfx_main, 0)
        fold((main_k(jnp.maximum(g - 2, 0)), 0, 0), 1, main_vt(g - 1), None, sfx_main, 0)
        return t + 1
    n_pairs = i * (n_diag // 2)
    lax.while_loop(lambda t: jnp.logical_and(t < n_pairs, live()), full_body, jnp.int32(0))

    @pl.when(live())
    def _():
        pmask = lax.broadcasted_iota(jnp.int32, (tkp, tq), 0) < pref_valid
        step(None, (2, vp_ref[0], pmask, sfx_pref, 0, tq))
    o_ref[0] = acc_ref[...].T.astype(BF16)


def _sb_attn(q, kp, vtp, km, vtm, *, tq, tk, pref_valid):
    nb, tq_total, _ = q.shape
    tkp = kp.shape[1]
    t_main = km.shape[1]
    n_diag = tq // tk
    assert n_diag % 2 == 0, "main blocks are consumed in pairs"
    kernel = functools.partial(_sb_kernel, tq=tq, tk=tk, tkp=tkp, pref_valid=pref_valid, n_diag=n_diag)
    return pl.pallas_call(
        kernel,
        grid=(nb, SB_WIDTH // LANES, tq_total // tq),
        in_specs=[pl.BlockSpec((1, tq, LANES), lambda b, h, i: (b, i, h)),
                  pl.BlockSpec((1, tkp, LANES), lambda b, h, i: (0, 0, h)),
                  pl.BlockSpec((1, LANES, tkp), lambda b, h, i: (0, h, 0)),
                  pl.BlockSpec((1, t_main, LANES), lambda b, h, i: (b, 0, h)),
                  pl.BlockSpec((1, LANES, t_main), lambda b, h, i: (b, h, 0))],
        out_specs=pl.BlockSpec((1, tq, LANES), lambda b, h, i: (b, i, h)),
        out_shape=jax.ShapeDtypeStruct((nb, tq_total, SB_WIDTH), BF16),
        scratch_shapes=[pltpu.VMEM((2, 1, tq), F32), pltpu.VMEM((LANES, tq), F32),
                        pltpu.VMEM((3, 2, max(tk, tkp), tq), F32)],
        compiler_params=pltpu.CompilerParams(dimension_semantics=("arbitrary",) * 3,
                                             vmem_limit_bytes=VMEM_LIMIT),
        name="sb_attn",
    )(q, kp, vtp, km, vtm)


def _diff_decode_kernel(lq1_ref, lk1_ref, lq2_ref, lk2_ref, subg_ref, q_ref, ck_ref, cv_ref, kn_ref, vn_ref,
                        o_ref, m_ref, l_ref, acc_ref, *, chunk):
    kc = pl.program_id(1)
    nq = q_ref.shape[0]

    @pl.when(kc == 0)
    def _():
        m_ref[...] = jnp.full(m_ref.shape, NEG, F32)
        l_ref[...] = jnp.zeros(l_ref.shape, F32)
        acc_ref[...] = jnp.zeros(acc_ref.shape, F32)

    def update(h, k, v):
        n = k.shape[0]
        qs = _split_halves(q_ref[:, h * LANES:(h + 1) * LANES])
        v_ones = jnp.concatenate([v, jnp.ones((n, LANES), BF16)], axis=1)
        ps, alphas = [], []
        for a in range(2):
            s = lax.dot_general(qs[a], k, _NT, preferred_element_type=F32)
            m_prev = m_ref[2 * h + a]
            m_new = jnp.maximum(m_prev, jnp.max(s, axis=1, keepdims=True))
            alphas.append(jnp.exp2(m_prev - m_new))
            ps.append(jnp.exp2(s - m_new).astype(BF16))
            m_ref[2 * h + a] = m_new
        pv = jnp.dot(jnp.concatenate(ps, axis=0), v_ones, preferred_element_type=F32)
        for a in range(2):
            part = pv[a * nq:(a + 1) * nq]
            acc_ref[2 * h + a] = alphas[a] * acc_ref[2 * h + a] + part[:, :LANES]
            l_ref[2 * h + a] = alphas[a] * l_ref[2 * h + a] + part[:, LANES:]

    for h in range(N_HEADS):
        update(h, ck_ref[pl.ds(h, chunk, stride=N_HEADS), :].astype(BF16),
               cv_ref[pl.ds(h, chunk, stride=N_HEADS), :].astype(BF16))

    @pl.when(kc == pl.num_programs(1) - 1)
    def _():
        lam = (jnp.exp(jnp.sum(lq1_ref[...] * lk1_ref[...], axis=1, keepdims=True))
               - jnp.exp(jnp.sum(lq2_ref[...] * lk2_ref[...], axis=1, keepdims=True)) + LAM_INIT)
        for h in range(N_HEADS):
            cols = slice(h * LANES, (h + 1) * LANES)
            update(h, kn_ref[:, cols], vn_ref[:, cols])
            o = acc_ref[2 * h] / l_ref[2 * h] - lam * (acc_ref[2 * h + 1] / l_ref[2 * h + 1])
            o_ref[:, cols] = (_rms(o) * subg_ref[...] * (1.0 - LAM_INIT)).astype(BF16)


def _diff_decode(lams, subg_row, q, ck, cv, kn, vn, *, chunk):
    nb, nq, _ = q.shape
    n_chunks = ck.shape[1] // (chunk * N_HEADS)
    small = _const_spec((1, HEAD_DIM))
    tok = pl.BlockSpec((None, nq, DIFF_WIDTH), lambda b, c: (b, 0, 0))
    cache = pl.BlockSpec((None, chunk * N_HEADS, LANES), lambda b, c: (b, c, 0))
    return pl.pallas_call(
        functools.partial(_diff_decode_kernel, chunk=chunk),
        grid=(nb, n_chunks),
        in_specs=[small, small, small, small, _const_spec((1, LANES)), tok, cache, cache, tok, tok],
        out_specs=tok,
        out_shape=jax.ShapeDtypeStruct((nb, nq, DIFF_WIDTH), BF16),
        scratch_shapes=[pltpu.VMEM((2 * N_HEADS, nq, 1), F32), pltpu.VMEM((2 * N_HEADS, nq, LANES), F32),
                        pltpu.VMEM((2 * N_HEADS, nq, LANES), F32)],
        compiler_params=pltpu.CompilerParams(dimension_semantics=("arbitrary",) * 2,
                                             vmem_limit_bytes=VMEM_LIMIT),
        name="diff_decode",
    )(*lams, subg_row, q, ck, cv, kn, vn)


SB_BLOCK = SB_KEY_BLOCK


def _sb_decode_kernel(q_ref, ckt_ref, cvt_ref, knt_ref, vnt_ref, o_ref):
    nq = q_ref.shape[0]
    past = ckt_ref.shape[1]
    n_blk = past // SB_BLOCK
    def suffix_cols(n):
        r = lax.broadcasted_iota(jnp.int32, (n, n + LANES), 0)
        c = lax.broadcasted_iota(jnp.int32, (n, n + LANES), 1)
        return jnp.where((r > c) | (c >= n), 1.0, 0.0).astype(BF16)
    sfx_blk = suffix_cols(SB_BLOCK)
    sfx_new = suffix_cols(LANES)
    row = lax.broadcasted_iota(jnp.int32, (nq, LANES), 0)
    lane = lax.broadcasted_iota(jnp.int32, (nq, LANES), 1)
    new_mask = lane < row

    def log1m_of(u):
        nu = -u
        return jnp.minimum(nu, 0.0) - jnp.log(1.0 + jnp.exp2(jnp.minimum(u, nu))) * LOG2E

    for j in range(ckt_ref.shape[0] // LANES):
        rows = slice(j * LANES, (j + 1) * LANES)
        kt = ckt_ref[rows, :].astype(BF16)
        vt = cvt_ref[rows, :].astype(BF16)
        qs = _split_halves(q_ref[:, rows])
        ws, wns = [], []
        for a in range(2):
            un = jnp.dot(qs[a], knt_ref[rows, :], preferred_element_type=F32)
            ln = jnp.where(new_mask, log1m_of(un), 0.0)
            sn = jnp.dot(ln.astype(BF16), sfx_new, preferred_element_type=F32)
            wns.append(jnp.exp2(jnp.where(new_mask, un + ln + sn[:, :LANES], NEG)).astype(BF16))
            carry = sn[:, LANES:]
            u = jnp.dot(qs[a], kt, preferred_element_type=F32)
            l1m = log1m_of(u)
            stacked = jnp.concatenate(
                [l1m[:, b * SB_BLOCK:(b + 1) * SB_BLOCK] for b in range(n_blk)], axis=0).astype(BF16)
            sfx = jnp.dot(stacked, sfx_blk, preferred_element_type=F32)
            ts = [None] * n_blk
            for b in reversed(range(n_blk)):
                cols = slice(b * SB_BLOCK, (b + 1) * SB_BLOCK)
                part = sfx[b * nq:(b + 1) * nq]
                c2 = jnp.concatenate([carry] * (SB_BLOCK // LANES), axis=1)
                ts[b] = u[:, cols] + l1m[:, cols] + part[:, :SB_BLOCK] + c2
                carry = carry + part[:, SB_BLOCK:]
            ws.append(jnp.exp2(jnp.concatenate(ts, axis=1)).astype(BF16))
        o = (lax.dot_general(jnp.concatenate(ws, axis=0), vt, _NT, preferred_element_type=F32)
             + lax.dot_general(jnp.concatenate(wns, axis=0), vnt_ref[rows, :], _NT,
                               preferred_element_type=F32))
        o_ref[:, rows] = jnp.where(lane < HEAD_DIM, o[:nq], o[nq:]).astype(BF16)


def _sb_decode(q, ckt, cvt, knt, vnt):
    nb, nq, _ = q.shape
    assert nq <= LANES
    past = ckt.shape[2]
    width = 2 * LANES
    tok = pl.BlockSpec((None, nq, width), lambda b, g: (b, 0, g))
    cache = pl.BlockSpec((None, width, past), lambda b, g: (b, g, 0))
    new = pl.BlockSpec((None, width, LANES), lambda b, g: (b, g, 0))
    return pl.pallas_call(
        _sb_decode_kernel,
        grid=(nb, SB_WIDTH // width),
        in_specs=[tok, cache, cache, new, new],
        out_specs=tok,
        out_shape=jax.ShapeDtypeStruct((nb, nq, SB_WIDTH), BF16),
        compiler_params=pltpu.CompilerParams(dimension_semantics=("arbitrary",) * 2,
                                             vmem_limit_bytes=VMEM_LIMIT),
        name="sb_decode",
    )(q, ckt, cvt, knt, vnt)


def _out_kernel(x_ref, od_ref, os_ref, gmix_ref, wg_ref, wdo_ref, wso_ref, wo_ref, gffn_ref,
                w1_ref, w2_ref, y_ref):
    x = x_ref[...]
    h = (_rms(x) * gmix_ref[...]).astype(BF16)
    gate = jax.nn.sigmoid(jnp.dot(h, wg_ref[...], preferred_element_type=F32))
    a = jnp.dot(od_ref[...], wdo_ref[...], preferred_element_type=F32)
    b = jnp.dot(os_ref[...], wso_ref[...], preferred_element_type=F32)
    merged = (gate[:, :D_MODEL] * a + gate[:, D_MODEL:] * b).astype(BF16)
    x1 = x + jnp.dot(merged, wo_ref[...], preferred_element_type=F32)
    h2 = (_rms(x1) * gffn_ref[...]).astype(BF16)
    y = x1
    for c in range(D_FF // D_MODEL):
        sl = slice(c * D_MODEL, (c + 1) * D_MODEL)
        f = jnp.maximum(jnp.dot(h2, w1_ref[:, sl], preferred_element_type=F32), 0.0)
        y = y + jnp.dot((f * f).astype(BF16), w2_ref[sl, :], preferred_element_type=F32)
    y_ref[...] = y


def _out(x, od, osb, gmix, wg, wdo, wso, wo, gffn, w1, w2, *, tm):
    rows = x.shape[0]
    row = lambda w: pl.BlockSpec((tm, w), lambda i: (i, 0))
    return pl.pallas_call(
        _out_kernel,
        grid=(rows // tm,),
        in_specs=[row(D_MODEL), row(DIFF_WIDTH), row(SB_WIDTH), _const_spec((1, D_MODEL)),
                  _const_spec((D_MODEL, 2 * D_MODEL)), _const_spec((DIFF_WIDTH, D_MODEL)),
                  _const_spec((SB_WIDTH, D_MODEL)), _const_spec((D_MODEL, D_MODEL)),
                  _const_spec((1, D_MODEL)), _const_spec((D_MODEL, D_FF)), _const_spec((D_FF, D_MODEL))],
        out_specs=row(D_MODEL),
        out_shape=jax.ShapeDtypeStruct((rows, D_MODEL), F32),
        compiler_params=pltpu.CompilerParams(dimension_semantics=("arbitrary",),
                                             vmem_limit_bytes=VMEM_LIMIT),
        name="out",
    )(x, od, osb, gmix, wg, wdo, wso, wo, gffn, w1, w2)


def _rope_tables(pos):
    half = ROT_DIM // 2
    d = jnp.arange(LANES, dtype=jnp.int32) % HEAD_DIM
    inv = ROPE_THETA ** (-(2 * (d % half)).astype(F32) / ROT_DIM)
    ang = pos.astype(F32)[:, None] * inv[None, :]
    cos, sin = jnp.cos(ang), jnp.sin(ang)
    cos_t = jnp.where(d < ROT_DIM, cos, 1.0)
    sa_t = jnp.where(d < half, -sin, 0.0)
    sb_t = jnp.where((d >= half) & (d < ROT_DIM), sin, 0.0)
    return cos_t, sa_t, sb_t


def kernel(x_prompt, x_sample, cache_diff_k, cache_diff_v, cache_sb_k, cache_sb_v, meta_tokens,
           g_mix, w_in, q_norm_g, k_norm_g, lam_q1, lam_k1, lam_q2, lam_k2, sub_g,
           w_diff_out, w_sb_out, w_out, g_ffn, w_ff1, w_ff2):
    nb, seq, _ = x_prompt.shape
    db, dseq, _ = x_sample.shape
    past = cache_diff_k.shape[2]
    lyr = 0

    w_qkv, w_gate = w_in[lyr, :, :QKV_COLS].astype(BF16), w_in[lyr, :, QKV_COLS:].astype(BF16)
    wdo, wso, wo = (w_diff_out[lyr].astype(BF16), w_sb_out[lyr].astype(BF16), w_out[lyr].astype(BF16))
    w1, w2 = w_ff1[lyr].astype(BF16), w_ff2[lyr].astype(BF16)
    gmix = g_mix[lyr].reshape(1, D_MODEL)
    gffn = g_ffn[lyr].reshape(1, D_MODEL)
    qg = jnp.tile(q_norm_g[lyr], COL_BLOCK // HEAD_DIM).reshape(1, COL_BLOCK)
    kg = jnp.tile(k_norm_g[lyr], COL_BLOCK // HEAD_DIM).reshape(1, COL_BLOCK)
    subg_col = sub_g[lyr].reshape(LANES, 1)
    lams = [t[lyr].reshape(1, HEAD_DIM) for t in (lam_q1, lam_k1, lam_q2, lam_k2)]
    grp = jnp.arange(MXU_TILE, dtype=jnp.int32) // HEAD_DIM
    gmat = jnp.where(grp[:, None] == grp[None, :], 1.0 / HEAD_DIM, 0.0).astype(BF16)

    main_pos = N_META + jnp.arange(seq, dtype=jnp.int32)
    small_pos = jnp.concatenate([jnp.arange(N_META, dtype=jnp.int32),
                                 jnp.tile(past + jnp.arange(dseq, dtype=jnp.int32), db)])
    x_main = x_prompt.reshape(nb * seq, D_MODEL)
    x_small = jnp.concatenate([meta_tokens.astype(F32), x_sample.reshape(db * dseq, D_MODEL)], axis=0)
    pm = _proj(x_main, gmix, w_qkv, qg, kg, gmat, *_rope_tables(main_pos), tm=ROW_TILE, prompt_layout=True,
               lead=N_META)
    ps = _proj(x_small, gmix, w_qkv, qg, kg, gmat, *_rope_tables(small_pos), tm=x_small.shape[0],
               prompt_layout=False)
    qd_m, kd_hm, kdb_m, vd_hm, vdt_m, qs_m, kst_m, ksb_m, vstf_m, vst_m = pm
    qd_s, kd_s, kdb_s, vd_s, vdb_s, qs_s, ks_s, ksb_s, vs_s, vsb_s = ps

    def bt(a, n, t):
        return a.reshape(n, t, a.shape[-1])

    def meta_keys(a):
        return jnp.pad(a[:N_META], ((0, LANES - N_META), (0, 0)))[None]

    def meta_vals_t(a):
        return jnp.swapaxes(meta_keys(a), 1, 2)

    od_p = _diff_attn(lams, subg_col, bt(qd_m, nb, seq), meta_keys(kdb_s), meta_vals_t(vdb_s),
                      bt(kdb_m, nb, seq), vdt_m,
                      tq=DIFF_Q_TILE, tk=DIFF_KEY_BLOCK, pref_valid=N_META)
    os_p = _sb_attn(bt(qs_m, nb, seq), meta_keys(ksb_s), meta_vals_t(vsb_s),
                    bt(ksb_m, nb, seq), vst_m,
                    tq=SB_Q_TILE, tk=SB_KEY_BLOCK, pref_valid=N_META)

    smp = lambda a: bt(a[N_META:], db, dseq)
    cdk = cache_diff_k[lyr].reshape(db, past * N_HEADS, LANES)
    cdv = cache_diff_v[lyr].reshape(db, past * N_HEADS, LANES)
    od_s = _diff_decode(lams, sub_g[lyr].reshape(1, LANES), smp(qd_s), cdk, cdv, smp(kdb_s), smp(vdb_s),
                        chunk=DECODE_CHUNK)
    sb_t = lambda c: jnp.transpose(c[lyr], (0, 2, 3, 1)).reshape(db, SB_WIDTH, past)
    new_t = lambda a: jnp.pad(jnp.swapaxes(smp(a), 1, 2), ((0, 0), (0, 0), (0, LANES - dseq)))
    os_s = _sb_decode(smp(qs_s), sb_t(cache_sb_k), sb_t(cache_sb_v), new_t(ksb_s), new_t(vsb_s))

    y_p = _out(x_main, od_p.reshape(nb * seq, DIFF_WIDTH), os_p.reshape(nb * seq, SB_WIDTH),
               gmix, w_gate, wdo, wso, wo, gffn, w1, w2, tm=ROW_TILE)
    y_s = _out(x_sample.reshape(db * dseq, D_MODEL), od_s.reshape(db * dseq, DIFF_WIDTH),
               os_s.reshape(db * dseq, SB_WIDTH), gmix, w_gate, wdo, wso, wo, gffn, w1, w2, tm=db * dseq)

    def diff_prompt_cache(head_major, small):
        meta = small[:N_META].reshape(N_META * N_HEADS, 2 * HEAD_DIM)
        full = _fill_lead_tokens(head_major, meta, nb=nb)
        return full.reshape(1, nb, seq + N_META, N_HEADS, 2 * HEAD_DIM)

    def sb_prompt_cache(main_t, small):
        meta_t = jnp.broadcast_to(small[:N_META].T[None], (nb, SB_WIDTH, N_META))
        full = jnp.concatenate([meta_t, main_t], axis=2).reshape(nb, N_HEADS, HEAD_DIM, seq + N_META)
        return jnp.transpose(full, (0, 3, 1, 2))[None]

    def sample_cache(small, dim):
        return small[N_META:].reshape(1, db, dseq, N_HEADS, dim)

    return (y_p.reshape(nb, seq, D_MODEL), y_s.reshape(db, dseq, D_MODEL),
            diff_prompt_cache(kd_hm, kd_s), diff_prompt_cache(vd_hm, vd_s),
            sb_prompt_cache(kst_m, ks_s), sb_prompt_cache(vstf_m, vs_s),
            sample_cache(kd_s, 2 * HEAD_DIM), sample_cache(vd_s, 2 * HEAD_DIM),
            sample_cache(ks_s, HEAD_DIM), sample_cache(vs_s, HEAD_DIM))
```

```python
import functools
import math

import jax
import jax.numpy as jnp
from jax import lax
from jax.experimental import pallas as pl
from jax.experimental.pallas import tpu as pltpu

F32 = jnp.float32
BF16 = jnp.bfloat16

D_MODEL = 1024
N_META = 16
CHUNK = 64
N_HEADS = 8
HEAD_DIM = 64
DIFF_WIDTH = N_HEADS * 2 * HEAD_DIM
SB_WIDTH = N_HEADS * HEAD_DIM
QKV_COLS = 3 * DIFF_WIDTH + 3 * SB_WIDTH
D_FF = 4 * D_MODEL
ROT_DIM = HEAD_DIM // 4
ROPE_THETA = 500000.0
EPS = 1e-6
NEG = -1e30
SKIP_BELOW = -160.0
LOG2E = math.log2(math.e)
Q_SCALE = HEAD_DIM ** -0.5 * LOG2E
ONES_ROWS = 16
LAM_INIT = 0.8 - 0.6 * math.exp(-0.3 * 0)

LANES = 128
MXU_TILE = 256
COL_BLOCK = 512
VMEM_LIMIT = 56 * 1024 * 1024

ROW_TILE = 512
DIFF_Q_TILE, DIFF_KEY_BLOCK = 2048, 512
SB_Q_TILE, SB_KEY_BLOCK = 2048, MXU_TILE
DECODE_CHUNK = 2048

_NT = (((1,), (1,)), ((), ()))


def _rms(x):
    return x * lax.rsqrt(jnp.mean(x * x, axis=-1, keepdims=True) + EPS)


def _const_spec(shape):
    return pl.BlockSpec(shape, lambda *_: (0,) * len(shape), pipeline_mode=pl.Buffered(1))


def _proj_kernel(x_ref, gmix_ref, w_ref, qg_ref, kg_ref, gmat_ref, cos_ref, sa_ref, sb_ref,
                 qd_ref, kd_ref, kdb_ref, vd_ref, vdb_ref, qs_ref, ks_ref, ksb_ref, vs_ref, vsb_ref,
                 *, prompt_layout):
    tm = x_ref.shape[0]
    heads_per_block = COL_BLOCK // LANES

    def store_diff(ref, j, y):
        if not prompt_layout:
            ref[:, j * COL_BLOCK:(j + 1) * COL_BLOCK] = y
            return
        for hh in range(heads_per_block):
            ref[pl.ds(j * heads_per_block + hh, tm, stride=N_HEADS), :] = y[:, hh * LANES:(hh + 1) * LANES]

    h = (_rms(x_ref[...]) * gmix_ref[...]).astype(BF16)
    cos = cos_ref[...]
    sa = sa_ref[...]
    sb = sb_ref[...]

    def col(j):
        return jnp.dot(h, w_ref[:, j * COL_BLOCK:(j + 1) * COL_BLOCK], preferred_element_type=F32)

    def normed_rot(y, g):
        sq = (y * y).astype(BF16)
        msq = jnp.concatenate(
            [jnp.dot(sq[:, c * MXU_TILE:(c + 1) * MXU_TILE], gmat_ref[...], preferred_element_type=F32)
             for c in range(COL_BLOCK // MXU_TILE)], axis=1)
        yn = y * lax.rsqrt(msq + EPS) * g
        parts = []
        for c in range(COL_BLOCK // LANES):
            t = yn[:, c * LANES:(c + 1) * LANES]
            parts.append(t * cos + pltpu.roll(t, LANES - ROT_DIM // 2, 1) * sa
                         + pltpu.roll(t, ROT_DIM // 2, 1) * sb)
        return jnp.concatenate(parts, axis=1)

    for j in range(2):
        sl = slice(j * COL_BLOCK, (j + 1) * COL_BLOCK)
        q = normed_rot(col(j), qg_ref[...])
        qd_ref[:, sl] = (q * Q_SCALE).astype(BF16)
        k = normed_rot(col(2 + j), kg_ref[...])
        store_diff(kd_ref, j, k)
        kdb_ref[:, sl] = k.astype(BF16)
        v = col(4 + j)
        store_diff(vd_ref, j, v)
        if prompt_layout:
            vdb_ref[sl, :] = v.T.astype(BF16)
        else:
            vdb_ref[:, sl] = v.astype(BF16)
    qs_ref[...] = (col(6) * Q_SCALE).astype(BF16)
    k = col(7)
    ksb_ref[...] = k.astype(BF16)
    v = col(8)
    if prompt_layout:
        ks_ref[...] = k.T
        vt = v.T
        vs_ref[...] = vt
        vsb_ref[...] = vt.astype(BF16)
    else:
        ks_ref[...] = k
        vs_ref[...] = v
        vsb_ref[...] = v.astype(BF16)


def _proj(x, gmix, w_qkv, qg, kg, gmat, cos, sa, sb, *, tm, prompt_layout, lead=0):
    rows = x.shape[0]
    seq = cos.shape[0]
    n_pos_tiles = seq // tm
    row = lambda w: pl.BlockSpec((tm, w), lambda i: (i, 0))
    tab = pl.BlockSpec((tm, LANES), lambda i: (i % n_pos_tiles, 0))
    wide = lambda dt: jax.ShapeDtypeStruct((rows, DIFF_WIDTH), dt)
    narrow = lambda dt: jax.ShapeDtypeStruct((rows, SB_WIDTH), dt)
    if prompt_layout:
        nb = rows // seq
        t_spec = lambda w: pl.BlockSpec((None, w, tm), lambda i: (i // n_pos_tiles, 0, i % n_pos_tiles))
        t_shape = lambda w, dt: jax.ShapeDtypeStruct((nb, w, seq), dt)
        hm_spec = pl.BlockSpec(
            (pl.Element(tm * N_HEADS), pl.Element(LANES)),
            lambda i: (((i // n_pos_tiles) * (seq + lead) + lead + (i % n_pos_tiles) * tm) * N_HEADS, 0))
        hm_shape = jax.ShapeDtypeStruct((nb * (seq + lead) * N_HEADS, LANES), F32)
        out_specs = [row(DIFF_WIDTH), hm_spec, row(DIFF_WIDTH), hm_spec, t_spec(DIFF_WIDTH),
                     row(SB_WIDTH), t_spec(SB_WIDTH), row(SB_WIDTH), t_spec(SB_WIDTH), t_spec(SB_WIDTH)]
        out_shape = [wide(BF16), hm_shape, wide(BF16), hm_shape, t_shape(DIFF_WIDTH, BF16),
                     narrow(BF16), t_shape(SB_WIDTH, F32), narrow(BF16), t_shape(SB_WIDTH, F32),
                     t_shape(SB_WIDTH, BF16)]
    else:
        out_specs = [row(DIFF_WIDTH)] * 5 + [row(SB_WIDTH)] * 5
        out_shape = [wide(BF16), wide(F32), wide(BF16), wide(F32), wide(BF16),
                     narrow(BF16), narrow(F32), narrow(BF16), narrow(F32), narrow(BF16)]
    return pl.pallas_call(
        functools.partial(_proj_kernel, prompt_layout=prompt_layout),
        grid=(rows // tm,),
        in_specs=[row(D_MODEL), _const_spec((1, D_MODEL)), _const_spec((D_MODEL, QKV_COLS)),
                  _const_spec((1, COL_BLOCK)), _const_spec((1, COL_BLOCK)),
                  _const_spec((MXU_TILE, MXU_TILE)), tab, tab, tab],
        out_specs=out_specs,
        out_shape=out_shape,
        compiler_params=pltpu.CompilerParams(dimension_semantics=("arbitrary",),
                                             vmem_limit_bytes=VMEM_LIMIT),
        name="proj",
    )(x, gmix, w_qkv, qg, kg, gmat, cos, sa, sb)


def _fill_lead_kernel(lead_ref, big_ref, out_ref):
    del big_ref
    out_ref[...] = lead_ref[...]


def _fill_lead_tokens(big, lead_rows, *, nb):
    n = lead_rows.shape[0]
    per_batch = big.shape[0] // nb
    return pl.pallas_call(
        _fill_lead_kernel,
        grid=(nb,),
        in_specs=[_const_spec((n, LANES)), pl.BlockSpec(memory_space=pl.ANY)],
        out_specs=pl.BlockSpec((pl.Element(n), pl.Element(LANES)), lambda b: (b * per_batch, 0)),
        out_shape=jax.ShapeDtypeStruct(big.shape, big.dtype),
        input_output_aliases={1: 0},
        compiler_params=pltpu.CompilerParams(dimension_semantics=("arbitrary",)),
        name="fill_lead_tokens",
    )(lead_rows, big)


def _split_halves(q):
    lane = lax.broadcasted_iota(jnp.int32, q.shape, 1)
    zero = jnp.zeros_like(q)
    return jnp.where(lane < HEAD_DIM, q, zero), jnp.where(lane >= HEAD_DIM, q, zero)


def _diff_kernel(lq1_ref, lk1_ref, lq2_ref, lk2_ref, subg_ref, q_ref, kp_ref, vp_ref, km_ref, vm_ref,
                 o_ref, m_ref, acc_ref, s_ref, *, tq, tk, tkp, pref_valid, n_diag):
    i = pl.program_id(2)
    n_full = i * n_diag
    qs = _split_halves(q_ref[0])
    m_ref[...] = jnp.full(m_ref.shape, NEG, F32)
    acc_ref[...] = jnp.zeros(acc_ref.shape, F32)

    def step(nxt, cur):
        if cur is not None:
            slot, vtblk, mask, qlo = cur
            n = vtblk.shape[1]
            vt_ones = jnp.concatenate([vtblk, jnp.ones((ONES_ROWS, n), BF16)], axis=0)
        for a in range(2):
            if cur is not None:
                s = s_ref[slot, a, :n, qlo:]
                if mask is not None:
                    s = jnp.where(mask[:, qlo:], s, NEG)
                m_prev = m_ref[a, :, qlo:]
                m_new = jnp.maximum(m_prev, jnp.max(s, axis=0, keepdims=True))
                alpha = jnp.exp2(m_prev - m_new)
                p = jnp.exp2(s - m_new)
            if nxt is not None:
                kblk, nslot, nqlo = nxt
                s_ref[nslot, a, :kblk.shape[0], nqlo:] = lax.dot_general(
                    kblk, qs[a][nqlo:], _NT, preferred_element_type=F32)
            if cur is not None:
                acc_ref[a, :, qlo:] = (alpha * acc_ref[a, :, qlo:]
                                       + jnp.dot(vt_ones, p.astype(BF16), preferred_element_type=F32))
                m_ref[a, :, qlo:] = m_new

    def main_k(g):
        return km_ref[0, pl.ds(pl.multiple_of(g * tk, tk), tk), :]

    def main_vt(g):
        return vm_ref[0, :, pl.ds(pl.multiple_of(g * tk, tk), tk)]

    step((kp_ref[0], 1, 0), None)
    step((main_k(0), 0, 0), (1, vp_ref[0], lax.broadcasted_iota(jnp.int32, (tkp, tq), 0) < pref_valid, 0))

    def full_body(t, carry):
        step((main_k(2 * t + 1), 1, 0), (0, main_vt(2 * t), None, 0))
        step((main_k(2 * t + 2), 0, 0), (1, main_vt(2 * t + 1), None, 0))
        return carry
    lax.fori_loop(0, i * (n_diag // 2), full_body, 0)

    key_idx = lax.broadcasted_iota(jnp.int32, (tk, tq), 0)
    q_chunk = lax.broadcasted_iota(jnp.int32, (tk, tq), 1) // CHUNK
    for d in range(n_diag):
        g = n_full + d
        nxt = (main_k(g + 1), (d + 1) % 2, (d + 1) * tk) if d + 1 < n_diag else None
        step(nxt, (d % 2, main_vt(g), (key_idx + d * tk) // CHUNK <= q_chunk, d * tk))

    lam = (jnp.exp(jnp.sum(lq1_ref[...] * lk1_ref[...], axis=1, keepdims=True))
           - jnp.exp(jnp.sum(lq2_ref[...] * lk2_ref[...], axis=1, keepdims=True)) + LAM_INIT)
    o = (acc_ref[0, :LANES] / acc_ref[0, LANES:LANES + 1]
         - lam * (acc_ref[1, :LANES] / acc_ref[1, LANES:LANES + 1]))
    o = o * lax.rsqrt(jnp.mean(o * o, axis=0, keepdims=True) + EPS) * subg_ref[...] * (1.0 - LAM_INIT)
    o_ref[0] = o.T.astype(BF16)


def _diff_attn(lams, subg_col, q, kp, vtp, km, vtm, *, tq, tk, pref_valid):
    nb, tq_total, _ = q.shape
    tkp = kp.shape[1]
    t_main = km.shape[1]
    n_diag = tq // tk
    assert n_diag % 2 == 0, "main blocks are consumed in pairs"
    kernel = functools.partial(_diff_kernel, tq=tq, tk=tk, tkp=tkp, pref_valid=pref_valid, n_diag=n_diag)
    small = _const_spec((1, HEAD_DIM))
    return pl.pallas_call(
        kernel,
        grid=(nb, N_HEADS, tq_total // tq),
        in_specs=[small, small, small, small, _const_spec((LANES, 1)),
                  pl.BlockSpec((1, tq, LANES), lambda b, h, i: (b, i, h)),
                  pl.BlockSpec((1, tkp, LANES), lambda b, h, i: (0, 0, h)),
                  pl.BlockSpec((1, LANES, tkp), lambda b, h, i: (0, h, 0)),
                  pl.BlockSpec((1, t_main, LANES), lambda b, h, i: (b, 0, h)),
                  pl.BlockSpec((1, LANES, t_main), lambda b, h, i: (b, h, 0))],
        out_specs=pl.BlockSpec((1, tq, LANES), lambda b, h, i: (b, i, h)),
        out_shape=jax.ShapeDtypeStruct((nb, tq_total, DIFF_WIDTH), BF16),
        scratch_shapes=[pltpu.VMEM((2, 1, tq), F32), pltpu.VMEM((2, LANES + ONES_ROWS, tq), F32),
                        pltpu.VMEM((2, 2, max(tk, tkp), tq), F32)],
        compiler_params=pltpu.CompilerParams(dimension_semantics=("arbitrary",) * 3,
                                             vmem_limit_bytes=VMEM_LIMIT),
        name="diff_attn",
    )(*lams, subg_col, q, kp, vtp, km, vtm)


def _suffix_matrix(n):
    r = lax.broadcasted_iota(jnp.int32, (n + ONES_ROWS, n), 0)
    c = lax.broadcasted_iota(jnp.int32, (n + ONES_ROWS, n), 1)
    return jnp.where((c >= r) | (r == n), 1.0, 0.0).astype(BF16)


def _sb_kernel(q_ref, kp_ref, vp_ref, km_ref, vm_ref, o_ref, c_ref, acc_ref, z_ref,
               *, tq, tk, tkp, pref_valid, n_diag):
    i = pl.program_id(2)
    n_full = i * n_diag
    qs = _split_halves(q_ref[0])
    c_ref[...] = jnp.zeros(c_ref.shape, F32)
    acc_ref[...] = jnp.zeros(acc_ref.shape, F32)

    def step(nxt, cur):
        if nxt is not None:
            kblk, nslot, nqlo = nxt
            for a in range(2):
                z_ref[nslot, a, :kblk.shape[0], nqlo:] = lax.dot_general(
                    kblk, qs[a][nqlo:], _NT, preferred_element_type=F32)
        if cur is None:
            return
        slot, vtblk, mask, sfx_mat, qlo, qhi = cur
        n = vtblk.shape[1]
        if mask is not None:
            mask = mask[:, qlo:qhi]
        for a in range(2):
            rows = slice(a * HEAD_DIM, (a + 1) * HEAD_DIM)
            u = z_ref[slot, a, :n, qlo:qhi]
            neg_part = jnp.minimum(u, 0.0)
            d = neg_part - u
            log1m = d - jnp.log(1.0 + jnp.exp2(neg_part + d)) * LOG2E
            if mask is not None:
                log1m = jnp.where(mask, log1m, 0.0)
            sfx = jnp.dot(sfx_mat, log1m.astype(BF16), preferred_element_type=F32)
            t = u + sfx[:n]
            if mask is not None:
                t = jnp.where(mask, t, NEG)
            pv = jnp.dot(vtblk[rows, :], jnp.exp2(t).astype(BF16), preferred_element_type=F32)
            acc_ref[rows, qlo:qhi] += pv * jnp.exp2(c_ref[a, :, qlo:qhi])
            c_ref[a, :, qlo:qhi] += sfx[n:n + 1]

    def main_k(g):
        return km_ref[0, pl.ds(pl.multiple_of(g * tk, tk), tk), :]

    def main_vt(g):
        return vm_ref[0, :, pl.ds(pl.multiple_of(g * tk, tk), tk)]

    sfx_main = _suffix_matrix(tk)
    sfx_pref = sfx_main if tkp == tk else _suffix_matrix(tkp)
    key = lax.broadcasted_iota(jnp.int32, (tk, tq), 0)
    qry = lax.broadcasted_iota(jnp.int32, (tk, tq), 1)

    def live(lo=0):
        return jnp.max(c_ref[:, :, lo:]) > SKIP_BELOW

    window = 2 * tk

    def fold(nxt, slot, vtblk, mask, sfx_mat, qlo):
        near_hi = min(qlo + window, tq)
        step(nxt, (slot, vtblk, mask, sfx_mat, qlo, near_hi))
        if near_hi < tq:
            @pl.when(live(near_hi))
            def _():
                step(None, (slot, vtblk, None, sfx_mat, near_hi, tq))

    step((kp_ref[0], 2, 0), None)
    step((main_k(n_full + n_diag - 1), 0, (n_diag - 1) * tk), None)
    for d in reversed(range(n_diag)):
        s = (n_diag - 1 - d) % 2
        if d > 0:
            nxt = (main_k(n_full + d - 1), 1 - s, (d - 1) * tk)
        else:
            nxt = (main_k(jnp.maximum(n_full - 1, 0)), 1 - s, 0)
        fold(nxt, s, main_vt(n_full + d), key + d * tk < qry, sfx_main, d * tk)

    def full_body(t):
        g = n_full - 1 - 2 * t
        fold((main_k(g - 1), 1, 0), 0, main_vt(g), None, sfx_main, 0)
        fold((main_k(jnp.maximum(g - 2, 0)), 0, 0), 1, main_vt(g - 1), None, sfx_main, 0)
        return t + 1
    n_pairs = i * (n_diag // 2)
    lax.while_loop(lambda t: jnp.logical_and(t < n_pairs, live()), full_body, jnp.int32(0))

    @pl.when(live())
    def _():
        pmask = lax.broadcasted_iota(jnp.int32, (tkp, tq), 0) < pref_valid
        step(None, (2, vp_ref[0], pmask, sfx_pref, 0, tq))
    o_ref[0] = acc_ref[...].T.astype(BF16)


def _sb_attn(q, kp, vtp, km, vtm, *, tq, tk, pref_valid):
    nb, tq_total, _ = q.shape
    tkp = kp.shape[1]
    t_main = km.shape[1]
    n_diag = tq // tk
    assert n_diag % 2 == 0, "main blocks are consumed in pairs"
    kernel = functools.partial(_sb_kernel, tq=tq, tk=tk, tkp=tkp, pref_valid=pref_valid, n_diag=n_diag)
    return pl.pallas_call(
        kernel,
        grid=(nb, SB_WIDTH // LANES, tq_total // tq),
        in_specs=[pl.BlockSpec((1, tq, LANES), lambda b, h, i: (b, i, h)),
                  pl.BlockSpec((1, tkp, LANES), lambda b, h, i: (0, 0, h)),
                  pl.BlockSpec((1, LANES, tkp), lambda b, h, i: (0, h, 0)),
                  pl.BlockSpec((1, t_main, LANES), lambda b, h, i: (b, 0, h)),
                  pl.BlockSpec((1, LANES, t_main), lambda b, h, i: (b, h, 0))],
        out_specs=pl.BlockSpec((1, tq, LANES), lambda b, h, i: (b, i, h)),
        out_shape=jax.ShapeDtypeStruct((nb, tq_total, SB_WIDTH), BF16),
        scratch_shapes=[pltpu.VMEM((2, 1, tq), F32), pltpu.VMEM((LANES, tq), F32),
                        pltpu.VMEM((3, 2, max(tk, tkp), tq), F32)],
        compiler_params=pltpu.CompilerParams(dimension_semantics=("arbitrary",) * 3,
                                             vmem_limit_bytes=VMEM_LIMIT),
        name="sb_attn",
    )(q, kp, vtp, km, vtm)


def _diff_decode_kernel(lq1_ref, lk1_ref, lq2_ref, lk2_ref, subg_ref, q_ref, ck_ref, cv_ref, kn_ref, vn_ref,
                        o_ref, m_ref, l_ref, acc_ref, *, chunk):
    kc = pl.program_id(1)
    nq = q_ref.shape[0]

    @pl.when(kc == 0)
    def _():
        m_ref[...] = jnp.full(m_ref.shape, NEG, F32)
        l_ref[...] = jnp.zeros(l_ref.shape, F32)
        acc_ref[...] = jnp.zeros(acc_ref.shape, F32)

    def update(h, k, v):
        n = k.shape[0]
        qs = _split_halves(q_ref[:, h * LANES:(h + 1) * LANES])
        v_ones = jnp.concatenate([v, jnp.ones((n, LANES), BF16)], axis=1)
        ps, alphas = [], []
        for a in range(2):
            s = lax.dot_general(qs[a], k, _NT, preferred_element_type=F32)
            m_prev = m_ref[2 * h + a]
            m_new = jnp.maximum(m_prev, jnp.max(s, axis=1, keepdims=True))
            alphas.append(jnp.exp2(m_prev - m_new))
            ps.append(jnp.exp2(s - m_new).astype(BF16))
            m_ref[2 * h + a] = m_new
        pv = jnp.dot(jnp.concatenate(ps, axis=0), v_ones, preferred_element_type=F32)
        for a in range(2):
            part = pv[a * nq:(a + 1) * nq]
            acc_ref[2 * h + a] = alphas[a] * acc_ref[2 * h + a] + part[:, :LANES]
            l_ref[2 * h + a] = alphas[a] * l_ref[2 * h + a] + part[:, LANES:]

    for h in range(N_HEADS):
        update(h, ck_ref[pl.ds(h, chunk, stride=N_HEADS), :].astype(BF16),
               cv_ref[pl.ds(h, chunk, stride=N_HEADS), :].astype(BF16))

    @pl.when(kc == pl.num_programs(1) - 1)
    def _():
        lam = (jnp.exp(jnp.sum(lq1_ref[...] * lk1_ref[...], axis=1, keepdims=True))
               - jnp.exp(jnp.sum(lq2_ref[...] * lk2_ref[...], axis=1, keepdims=True)) + LAM_INIT)
        for h in range(N_HEADS):
            cols = slice(h * LANES, (h + 1) * LANES)
            update(h, kn_ref[:, cols], vn_ref[:, cols])
            o = acc_ref[2 * h] / l_ref[2 * h] - lam * (acc_ref[2 * h + 1] / l_ref[2 * h + 1])
            o_ref[:, cols] = (_rms(o) * subg_ref[...] * (1.0 - LAM_INIT)).astype(BF16)


def _diff_decode(lams, subg_row, q, ck, cv, kn, vn, *, chunk):
    nb, nq, _ = q.shape
    n_chunks = ck.shape[1] // (chunk * N_HEADS)
    small = _const_spec((1, HEAD_DIM))
    tok = pl.BlockSpec((None, nq, DIFF_WIDTH), lambda b, c: (b, 0, 0))
    cache = pl.BlockSpec((None, chunk * N_HEADS, LANES), lambda b, c: (b, c, 0))
    return pl.pallas_call(
        functools.partial(_diff_decode_kernel, chunk=chunk),
        grid=(nb, n_chunks),
        in_specs=[small, small, small, small, _const_spec((1, LANES)), tok, cache, cache, tok, tok],
        out_specs=tok,
        out_shape=jax.ShapeDtypeStruct((nb, nq, DIFF_WIDTH), BF16),
        scratch_shapes=[pltpu.VMEM((2 * N_HEADS, nq, 1), F32), pltpu.VMEM((2 * N_HEADS, nq, LANES), F32),
                        pltpu.VMEM((2 * N_HEADS, nq, LANES), F32)],
        compiler_params=pltpu.CompilerParams(dimension_semantics=("arbitrary",) * 2,
                                             vmem_limit_bytes=VMEM_LIMIT),
        name="diff_decode",
    )(*lams, subg_row, q, ck, cv, kn, vn)


SB_BLOCK = SB_KEY_BLOCK


def _sb_decode_kernel(q_ref, ckt_ref, cvt_ref, knt_ref, vnt_ref, o_ref):
    nq = q_ref.shape[0]
    past = ckt_ref.shape[1]
    n_blk = past // SB_BLOCK
    def suffix_cols(n):
        r = lax.broadcasted_iota(jnp.int32, (n, n + LANES), 0)
        c = lax.broadcasted_iota(jnp.int32, (n, n + LANES), 1)
        return jnp.where((r > c) | (c >= n), 1.0, 0.0).astype(BF16)
    sfx_blk = suffix_cols(SB_BLOCK)
    sfx_new = suffix_cols(LANES)
    row = lax.broadcasted_iota(jnp.int32, (nq, LANES), 0)
    lane = lax.broadcasted_iota(jnp.int32, (nq, LANES), 1)
    new_mask = lane < row

    def log1m_of(u):
        nu = -u
        return jnp.minimum(nu, 0.0) - jnp.log(1.0 + jnp.exp2(jnp.minimum(u, nu))) * LOG2E

    for j in range(ckt_ref.shape[0] // LANES):
        rows = slice(j * LANES, (j + 1) * LANES)
        kt = ckt_ref[rows, :].astype(BF16)
        vt = cvt_ref[rows, :].astype(BF16)
        qs = _split_halves(q_ref[:, rows])
        ws, wns = [], []
        for a in range(2):
            un = jnp.dot(qs[a], knt_ref[rows, :], preferred_element_type=F32)
            ln = jnp.where(new_mask, log1m_of(un), 0.0)
            sn = jnp.dot(ln.astype(BF16), sfx_new, preferred_element_type=F32)
            wns.append(jnp.exp2(jnp.where(new_mask, un + ln + sn[:, :LANES], NEG)).astype(BF16))
            carry = sn[:, LANES:]
            u = jnp.dot(qs[a], kt, preferred_element_type=F32)
            l1m = log1m_of(u)
            stacked = jnp.concatenate(
                [l1m[:, b * SB_BLOCK:(b + 1) * SB_BLOCK] for b in range(n_blk)], axis=0).astype(BF16)
            sfx = jnp.dot(stacked, sfx_blk, preferred_element_type=F32)
            ts = [None] * n_blk
            for b in reversed(range(n_blk)):
                cols = slice(b * SB_BLOCK, (b + 1) * SB_BLOCK)
                part = sfx[b * nq:(b + 1) * nq]
                c2 = jnp.concatenate([carry] * (SB_BLOCK // LANES), axis=1)
                ts[b] = u[:, cols] + l1m[:, cols] + part[:, :SB_BLOCK] + c2
                carry = carry + part[:, SB_BLOCK:]
            ws.append(jnp.exp2(jnp.concatenate(ts, axis=1)).astype(BF16))
        o = (lax.dot_general(jnp.concatenate(ws, axis=0), vt, _NT, preferred_element_type=F32)
             + lax.dot_general(jnp.concatenate(wns, axis=0), vnt_ref[rows, :], _NT,
                               preferred_element_type=F32))
        o_ref[:, rows] = jnp.where(lane < HEAD_DIM, o[:nq], o[nq:]).astype(BF16)


def _sb_decode(q, ckt, cvt, knt, vnt):
    nb, nq, _ = q.shape
    assert nq <= LANES
    past = ckt.shape[2]
    width = 2 * LANES
    tok = pl.BlockSpec((None, nq, width), lambda b, g: (b, 0, g))
    cache = pl.BlockSpec((None, width, past), lambda b, g: (b, g, 0))
    new = pl.BlockSpec((None, width, LANES), lambda b, g: (b, g, 0))
    return pl.pallas_call(
        _sb_decode_kernel,
        grid=(nb, SB_WIDTH // width),
        in_specs=[tok, cache, cache, new, new],
        out_specs=tok,
        out_shape=jax.ShapeDtypeStruct((nb, nq, SB_WIDTH), BF16),
        compiler_params=pltpu.CompilerParams(dimension_semantics=("arbitrary",) * 2,
                                             vmem_limit_bytes=VMEM_LIMIT),
        name="sb_decode",
    )(q, ckt, cvt, knt, vnt)


def _out_kernel(x_ref, od_ref, os_ref, gmix_ref, wg_ref, wdo_ref, wso_ref, wo_ref, gffn_ref,
                w1_ref, w2_ref, y_ref):
    x = x_ref[...]
    h = (_rms(x) * gmix_ref[...]).astype(BF16)
    gate = jax.nn.sigmoid(jnp.dot(h, wg_ref[...], preferred_element_type=F32))
    a = jnp.dot(od_ref[...], wdo_ref[...], preferred_element_type=F32)
    b = jnp.dot(os_ref[...], wso_ref[...], preferred_element_type=F32)
    merged = (gate[:, :D_MODEL] * a + gate[:, D_MODEL:] * b).astype(BF16)
    x1 = x + jnp.dot(merged, wo_ref[...], preferred_element_type=F32)
    h2 = (_rms(x1) * gffn_ref[...]).astype(BF16)
    y = x1
    for c in range(D_FF // D_MODEL):
        sl = slice(c * D_MODEL, (c + 1) * D_MODEL)
        f = jnp.maximum(jnp.dot(h2, w1_ref[:, sl], preferred_element_type=F32), 0.0)
        y = y + jnp.dot((f * f).astype(BF16), w2_ref[sl, :], preferred_element_type=F32)
    y_ref[...] = y


def _out(x, od, osb, gmix, wg, wdo, wso, wo, gffn, w1, w2, *, tm):
    rows = x.shape[0]
    row = lambda w: pl.BlockSpec((tm, w), lambda i: (i, 0))
    return pl.pallas_call(
        _out_kernel,
        grid=(rows // tm,),
        in_specs=[row(D_MODEL), row(DIFF_WIDTH), row(SB_WIDTH), _const_spec((1, D_MODEL)),
                  _const_spec((D_MODEL, 2 * D_MODEL)), _const_spec((DIFF_WIDTH, D_MODEL)),
                  _const_spec((SB_WIDTH, D_MODEL)), _const_spec((D_MODEL, D_MODEL)),
                  _const_spec((1, D_MODEL)), _const_spec((D_MODEL, D_FF)), _const_spec((D_FF, D_MODEL))],
        out_specs=row(D_MODEL),
        out_shape=jax.ShapeDtypeStruct((rows, D_MODEL), F32),
        compiler_params=pltpu.CompilerParams(dimension_semantics=("arbitrary",),
                                             vmem_limit_bytes=VMEM_LIMIT),
        name="out",
    )(x, od, osb, gmix, wg, wdo, wso, wo, gffn, w1, w2)


def _rope_tables(pos):
    half = ROT_DIM // 2
    d = jnp.arange(LANES, dtype=jnp.int32) % HEAD_DIM
    inv = ROPE_THETA ** (-(2 * (d % half)).astype(F32) / ROT_DIM)
    ang = pos.astype(F32)[:, None] * inv[None, :]
    cos, sin = jnp.cos(ang), jnp.sin(ang)
    cos_t = jnp.where(d < ROT_DIM, cos, 1.0)
    sa_t = jnp.where(d < half, -sin, 0.0)
    sb_t = jnp.where((d >= half) & (d < ROT_DIM), sin, 0.0)
    return cos_t, sa_t, sb_t


def kernel(x_prompt, x_sample, cache_diff_k, cache_diff_v, cache_sb_k, cache_sb_v, meta_tokens,
           g_mix, w_in, q_norm_g, k_norm_g, lam_q1, lam_k1, lam_q2, lam_k2, sub_g,
           w_diff_out, w_sb_out, w_out, g_ffn, w_ff1, w_ff2):
    nb, seq, _ = x_prompt.shape
    db, dseq, _ = x_sample.shape
    past = cache_diff_k.shape[2]
    lyr = 0

    w_qkv, w_gate = w_in[lyr, :, :QKV_COLS].astype(BF16), w_in[lyr, :, QKV_COLS:].astype(BF16)
    wdo, wso, wo = (w_diff_out[lyr].astype(BF16), w_sb_out[lyr].astype(BF16), w_out[lyr].astype(BF16))
    w1, w2 = w_ff1[lyr].astype(BF16), w_ff2[lyr].astype(BF16)
    gmix = g_mix[lyr].reshape(1, D_MODEL)
    gffn = g_ffn[lyr].reshape(1, D_MODEL)
    qg = jnp.tile(q_norm_g[lyr], COL_BLOCK // HEAD_DIM).reshape(1, COL_BLOCK)
    kg = jnp.tile(k_norm_g[lyr], COL_BLOCK // HEAD_DIM).reshape(1, COL_BLOCK)
    subg_col = sub_g[lyr].reshape(LANES, 1)
    lams = [t[lyr].reshape(1, HEAD_DIM) for t in (lam_q1, lam_k1, lam_q2, lam_k2)]
    grp = jnp.arange(MXU_TILE, dtype=jnp.int32) // HEAD_DIM
    gmat = jnp.where(grp[:, None] == grp[None, :], 1.0 / HEAD_DIM, 0.0).astype(BF16)

    main_pos = N_META + jnp.arange(seq, dtype=jnp.int32)
    small_pos = jnp.concatenate([jnp.arange(N_META, dtype=jnp.int32),
                                 jnp.tile(past + jnp.arange(dseq, dtype=jnp.int32), db)])
    x_main = x_prompt.reshape(nb * seq, D_MODEL)
    x_small = jnp.concatenate([meta_tokens.astype(F32), x_sample.reshape(db * dseq, D_MODEL)], axis=0)
    pm = _proj(x_main, gmix, w_qkv, qg, kg, gmat, *_rope_tables(main_pos), tm=ROW_TILE, prompt_layout=True,
               lead=N_META)
    ps = _proj(x_small, gmix, w_qkv, qg, kg, gmat, *_rope_tables(small_pos), tm=x_small.shape[0],
               prompt_layout=False)
    qd_m, kd_hm, kdb_m, vd_hm, vdt_m, qs_m, kst_m, ksb_m, vstf_m, vst_m = pm
    qd_s, kd_s, kdb_s, vd_s, vdb_s, qs_s, ks_s, ksb_s, vs_s, vsb_s = ps

    def bt(a, n, t):
        return a.reshape(n, t, a.shape[-1])

    def meta_keys(a):
        return jnp.pad(a[:N_META], ((0, LANES - N_META), (0, 0)))[None]

    def meta_vals_t(a):
        return jnp.swapaxes(meta_keys(a), 1, 2)

    od_p = _diff_attn(lams, subg_col, bt(qd_m, nb, seq), meta_keys(kdb_s), meta_vals_t(vdb_s),
                      bt(kdb_m, nb, seq), vdt_m,
                      tq=DIFF_Q_TILE, tk=DIFF_KEY_BLOCK, pref_valid=N_META)
    os_p = _sb_attn(bt(qs_m, nb, seq), meta_keys(ksb_s), meta_vals_t(vsb_s),
                    bt(ksb_m, nb, seq), vst_m,
                    tq=SB_Q_TILE, tk=SB_KEY_BLOCK, pref_valid=N_META)

    smp = lambda a: bt(a[N_META:], db, dseq)
    cdk = cache_diff_k[lyr].reshape(db, past * N_HEADS, LANES)
    cdv = cache_diff_v[lyr].reshape(db, past * N_HEADS, LANES)
    od_s = _diff_decode(lams, sub_g[lyr].reshape(1, LANES), smp(qd_s), cdk, cdv, smp(kdb_s), smp(vdb_s),
                        chunk=DECODE_CHUNK)
    sb_t = lambda c: jnp.transpose(c[lyr], (0, 2, 3, 1)).reshape(db, SB_WIDTH, past)
    new_t = lambda a: jnp.pad(jnp.swapaxes(smp(a), 1, 2), ((0, 0), (0, 0), (0, LANES - dseq)))
    os_s = _sb_decode(smp(qs_s), sb_t(cache_sb_k), sb_t(cache_sb_v), new_t(ksb_s), new_t(vsb_s))

    y_p = _out(x_main, od_p.reshape(nb * seq, DIFF_WIDTH), os_p.reshape(nb * seq, SB_WIDTH),
               gmix, w_gate, wdo, wso, wo, gffn, w1, w2, tm=ROW_TILE)
    y_s = _out(x_sample.reshape(db * dseq, D_MODEL), od_s.reshape(db * dseq, DIFF_WIDTH),
               os_s.reshape(db * dseq, SB_WIDTH), gmix, w_gate, wdo, wso, wo, gffn, w1, w2, tm=db * dseq)

    def diff_prompt_cache(head_major, small):
        meta = small[:N_META].reshape(N_META * N_HEADS, 2 * HEAD_DIM)
        full = _fill_lead_tokens(head_major, meta, nb=nb)
        return full.reshape(1, nb, seq + N_META, N_HEADS, 2 * HEAD_DIM)

    def sb_prompt_cache(main_t, small):
        meta_t = jnp.broadcast_to(small[:N_META].T[None], (nb, SB_WIDTH, N_META))
        full = jnp.concatenate([meta_t, main_t], axis=2).reshape(nb, N_HEADS, HEAD_DIM, seq + N_META)
        return jnp.transpose(full, (0, 3, 1, 2))[None]

    def sample_cache(small, dim):
        return small[N_META:].reshape(1, db, dseq, N_HEADS, dim)

    return (y_p.reshape(nb, seq, D_MODEL), y_s.reshape(db, dseq, D_MODEL),
            diff_prompt_cache(kd_hm, kd_s), diff_prompt_cache(vd_hm, vd_s),
            sb_prompt_cache(kst_m, ks_s), sb_prompt_cache(vstf_m, vs_s),
            sample_cache(kd_s, 2 * HEAD_DIM), sample_cache(vd_s, 2 * HEAD_DIM),
            sample_cache(ks_s, HEAD_DIM), sample_cache(vs_s, HEAD_DIM))
```

```python
import functools
import math

import jax
import jax.numpy as jnp
from jax import lax
from jax.experimental import pallas as pl
from jax.experimental.pallas import tpu as pltpu

F32 = jnp.float32
BF16 = jnp.bfloat16

D_MODEL = 1024
N_META = 16
CHUNK = 64
N_HEADS = 8
HEAD_DIM = 64
DIFF_WIDTH = N_HEADS * 2 * HEAD_DIM
SB_WIDTH = N_HEADS * HEAD_DIM
QKV_COLS = 3 * DIFF_WIDTH + 3 * SB_WIDTH
D_FF = 4 * D_MODEL
ROT_DIM = HEAD_DIM // 4
ROPE_THETA = 500000.0
EPS = 1e-6
NEG = -1e30
SKIP_BELOW = -160.0
LOG2E = math.log2(math.e)
Q_SCALE = HEAD_DIM ** -0.5 * LOG2E
ONES_ROWS = 16
LAM_INIT = 0.8 - 0.6 * math.exp(-0.3 * 0)

LANES = 128
MXU_TILE = 256
COL_BLOCK = 512
VMEM_LIMIT = 56 * 1024 * 1024

ROW_TILE = 512
DIFF_Q_TILE, DIFF_KEY_BLOCK = 2048, 512
SB_Q_TILE, SB_KEY_BLOCK = 2048, MXU_TILE
DECODE_CHUNK = 2048

_NT = (((1,), (1,)), ((), ()))


def _rms(x):
    return x * lax.rsqrt(jnp.mean(x * x, axis=-1, keepdims=True) + EPS)


def _const_spec(shape):
    return pl.BlockSpec(shape, lambda *_: (0,) * len(shape), pipeline_mode=pl.Buffered(1))


def _proj_kernel(x_ref, gmix_ref, w_ref, qg_ref, kg_ref, gmat_ref, cos_ref, sa_ref, sb_ref,
                 qd_ref, kd_ref, kdb_ref, vd_ref, vdb_ref, qs_ref, ks_ref, ksb_ref, vs_ref, vsb_ref,
                 *, prompt_layout):
    tm = x_ref.shape[0]
    heads_per_block = COL_BLOCK // LANES

    def store_diff(ref, j, y):
        if not prompt_layout:
            ref[:, j * COL_BLOCK:(j + 1) * COL_BLOCK] = y
            return
        for hh in range(heads_per_block):
            ref[pl.ds(j * heads_per_block + hh, tm, stride=N_HEADS), :] = y[:, hh * LANES:(hh + 1) * LANES]

    h = (_rms(x_ref[...]) * gmix_ref[...]).astype(BF16)
    cos = cos_ref[...]
    sa = sa_ref[...]
    sb = sb_ref[...]

    def col(j):
        return jnp.dot(h, w_ref[:, j * COL_BLOCK:(j + 1) * COL_BLOCK], preferred_element_type=F32)

    def normed_rot(y, g):
        sq = (y * y).astype(BF16)
        msq = jnp.concatenate(
            [jnp.dot(sq[:, c * MXU_TILE:(c + 1) * MXU_TILE], gmat_ref[...], preferred_element_type=F32)
             for c in range(COL_BLOCK // MXU_TILE)], axis=1)
        yn = y * lax.rsqrt(msq + EPS) * g
        parts = []
        for c in range(COL_BLOCK // LANES):
            t = yn[:, c * LANES:(c + 1) * LANES]
            parts.append(t * cos + pltpu.roll(t, LANES - ROT_DIM // 2, 1) * sa
                         + pltpu.roll(t, ROT_DIM // 2, 1) * sb)
        return jnp.concatenate(parts, axis=1)

    for j in range(2):
        sl = slice(j * COL_BLOCK, (j + 1) * COL_BLOCK)
        q = normed_rot(col(j), qg_ref[...])
        qd_ref[:, sl] = (q * Q_SCALE).astype(BF16)
        k = normed_rot(col(2 + j), kg_ref[...])
        store_diff(kd_ref, j, k)
        kdb_ref[:, sl] = k.astype(BF16)
        v = col(4 + j)
        store_diff(vd_ref, j, v)
        if prompt_layout:
            vdb_ref[sl, :] = v.T.astype(BF16)
        else:
            vdb_ref[:, sl] = v.astype(BF16)
    qs_ref[...] = (col(6) * Q_SCALE).astype(BF16)
    k = col(7)
    ksb_ref[...] = k.astype(BF16)
    v = col(8)
    if prompt_layout:
        ks_ref[...] = k.T
        vt = v.T
        vs_ref[...] = vt
        vsb_ref[...] = vt.astype(BF16)
    else:
        ks_ref[...] = k
        vs_ref[...] = v
        vsb_ref[...] = v.astype(BF16)


def _proj(x, gmix, w_qkv, qg, kg, gmat, cos, sa, sb, *, tm, prompt_layout, lead=0):
    rows = x.shape[0]
    seq = cos.shape[0]
    n_pos_tiles = seq // tm
    row = lambda w: pl.BlockSpec((tm, w), lambda i: (i, 0))
    tab = pl.BlockSpec((tm, LANES), lambda i: (i % n_pos_tiles, 0))
    wide = lambda dt: jax.ShapeDtypeStruct((rows, DIFF_WIDTH), dt)
    narrow = lambda dt: jax.ShapeDtypeStruct((rows, SB_WIDTH), dt)
    if prompt_layout:
        nb = rows // seq
        t_spec = lambda w: pl.BlockSpec((None, w, tm), lambda i: (i // n_pos_tiles, 0, i % n_pos_tiles))
        t_shape = lambda w, dt: jax.ShapeDtypeStruct((nb, w, seq), dt)
        hm_spec = pl.BlockSpec(
            (pl.Element(tm * N_HEADS), pl.Element(LANES)),
            lambda i: (((i // n_pos_tiles) * (seq + lead) + lead + (i % n_pos_tiles) * tm) * N_HEADS, 0))
        hm_shape = jax.ShapeDtypeStruct((nb * (seq + lead) * N_HEADS, LANES), F32)
        out_specs = [row(DIFF_WIDTH), hm_spec, row(DIFF_WIDTH), hm_spec, t_spec(DIFF_WIDTH),
                     row(SB_WIDTH), t_spec(SB_WIDTH), row(SB_WIDTH), t_spec(SB_WIDTH), t_spec(SB_WIDTH)]
        out_shape = [wide(BF16), hm_shape, wide(BF16), hm_shape, t_shape(DIFF_WIDTH, BF16),
                     narrow(BF16), t_shape(SB_WIDTH, F32), narrow(BF16), t_shape(SB_WIDTH, F32),
                     t_shape(SB_WIDTH, BF16)]
    else:
        out_specs = [row(DIFF_WIDTH)] * 5 + [row(SB_WIDTH)] * 5
        out_shape = [wide(BF16), wide(F32), wide(BF16), wide(F32), wide(BF16),
                     narrow(BF16), narrow(F32), narrow(BF16), narrow(F32), narrow(BF16)]
    return pl.pallas_call(
        functools.partial(_proj_kernel, prompt_layout=prompt_layout),
        grid=(rows // tm,),
        in_specs=[row(D_MODEL), _const_spec((1, D_MODEL)), _const_spec((D_MODEL, QKV_COLS)),
                  _const_spec((1, COL_BLOCK)), _const_spec((1, COL_BLOCK)),
                  _const_spec((MXU_TILE, MXU_TILE)), tab, tab, tab],
        out_specs=out_specs,
        out_shape=out_shape,
        compiler_params=pltpu.CompilerParams(dimension_semantics=("arbitrary",),
                                             vmem_limit_bytes=VMEM_LIMIT),
        name="proj",
    )(x, gmix, w_qkv, qg, kg, gmat, cos, sa, sb)


def _fill_lead_kernel(lead_ref, big_ref, out_ref):
    del big_ref
    out_ref[...] = lead_ref[...]


def _fill_lead_tokens(big, lead_rows, *, nb):
    n = lead_rows.shape[0]
    per_batch = big.shape[0] // nb
    return pl.pallas_call(
        _fill_lead_kernel,
        grid=(nb,),
        in_specs=[_const_spec((n, LANES)), pl.BlockSpec(memory_space=pl.ANY)],
        out_specs=pl.BlockSpec((pl.Element(n), pl.Element(LANES)), lambda b: (b * per_batch, 0)),
        out_shape=jax.ShapeDtypeStruct(big.shape, big.dtype),
        input_output_aliases={1: 0},
        compiler_params=pltpu.CompilerParams(dimension_semantics=("arbitrary",)),
        name="fill_lead_tokens",
    )(lead_rows, big)


def _split_halves(q):
    lane = lax.broadcasted_iota(jnp.int32, q.shape, 1)
    zero = jnp.zeros_like(q)
    return jnp.where(lane < HEAD_DIM, q, zero), jnp.where(lane >= HEAD_DIM, q, zero)


def _diff_kernel(lq1_ref, lk1_ref, lq2_ref, lk2_ref, subg_ref, q_ref, kp_ref, vp_ref, km_ref, vm_ref,
                 o_ref, m_ref, acc_ref, s_ref, *, tq, tk, tkp, pref_valid, n_diag):
    i = pl.program_id(2)
    n_full = i * n_diag
    qs = _split_halves(q_ref[0])
    m_ref[...] = jnp.full(m_ref.shape, NEG, F32)
    acc_ref[...] = jnp.zeros(acc_ref.shape, F32)

    def step(nxt, cur):
        if cur is not None:
            slot, vtblk, mask, qlo = cur
            n = vtblk.shape[1]
            vt_ones = jnp.concatenate([vtblk, jnp.ones((ONES_ROWS, n), BF16)], axis=0)
        for a in range(2):
            if cur is not None:
                s = s_ref[slot, a, :n, qlo:]
                if mask is not None:
                    s = jnp.where(mask[:, qlo:], s, NEG)
                m_prev = m_ref[a, :, qlo:]
                m_new = jnp.maximum(m_prev, jnp.max(s, axis=0, keepdims=True))
                alpha = jnp.exp2(m_prev - m_new)
                p = jnp.exp2(s - m_new)
            if nxt is not None:
                kblk, nslot, nqlo = nxt
                s_ref[nslot, a, :kblk.shape[0], nqlo:] = lax.dot_general(
                    kblk, qs[a][nqlo:], _NT, preferred_element_type=F32)
            if cur is not None:
                acc_ref[a, :, qlo:] = (alpha * acc_ref[a, :, qlo:]
                                       + jnp.dot(vt_ones, p.astype(BF16), preferred_element_type=F32))
                m_ref[a, :, qlo:] = m_new

    def main_k(g):
        return km_ref[0, pl.ds(pl.multiple_of(g * tk, tk), tk), :]

    def main_vt(g):
        return vm_ref[0, :, pl.ds(pl.multiple_of(g * tk, tk), tk)]

    step((kp_ref[0], 1, 0), None)
    step((main_k(0), 0, 0), (1, vp_ref[0], lax.broadcasted_iota(jnp.int32, (tkp, tq), 0) < pref_valid, 0))

    def full_body(t, carry):
        step((main_k(2 * t + 1), 1, 0), (0, main_vt(2 * t), None, 0))
        step((main_k(2 * t + 2), 0, 0), (1, main_vt(2 * t + 1), None, 0))
        return carry
    lax.fori_loop(0, i * (n_diag // 2), full_body, 0)

    key_idx = lax.broadcasted_iota(jnp.int32, (tk, tq), 0)
    q_chunk = lax.broadcasted_iota(jnp.int32, (tk, tq), 1) // CHUNK
    for d in range(n_diag):
        g = n_full + d
        nxt = (main_k(g + 1), (d + 1) % 2, (d + 1) * tk) if d + 1 < n_diag else None
        step(nxt, (d % 2, main_vt(g), (key_idx + d * tk) // CHUNK <= q_chunk, d * tk))

    lam = (jnp.exp(jnp.sum(lq1_ref[...] * lk1_ref[...], axis=1, keepdims=True))
           - jnp.exp(jnp.sum(lq2_ref[...] * lk2_ref[...], axis=1, keepdims=True)) + LAM_INIT)
    o = (acc_ref[0, :LANES] / acc_ref[0, LANES:LANES + 1]
         - lam * (acc_ref[1, :LANES] / acc_ref[1, LANES:LANES + 1]))
    o = o * lax.rsqrt(jnp.mean(o * o, axis=0, keepdims=True) + EPS) * subg_ref[...] * (1.0 - LAM_INIT)
    o_ref[0] = o.T.astype(BF16)


def _diff_attn(lams, subg_col, q, kp, vtp, km, vtm, *, tq, tk, pref_valid):
    nb, tq_total, _ = q.shape
    tkp = kp.shape[1]
    t_main = km.shape[1]
    n_diag = tq // tk
    assert n_diag % 2 == 0, "main blocks are consumed in pairs"
    kernel = functools.partial(_diff_kernel, tq=tq, tk=tk, tkp=tkp, pref_valid=pref_valid, n_diag=n_diag)
    small = _const_spec((1, HEAD_DIM))
    return pl.pallas_call(
        kernel,
        grid=(nb, N_HEADS, tq_total // tq),
        in_specs=[small, small, small, small, _const_spec((LANES, 1)),
                  pl.BlockSpec((1, tq, LANES), lambda b, h, i: (b, i, h)),
                  pl.BlockSpec((1, tkp, LANES), lambda b, h, i: (0, 0, h)),
                  pl.BlockSpec((1, LANES, tkp), lambda b, h, i: (0, h, 0)),
                  pl.BlockSpec((1, t_main, LANES), lambda b, h, i: (b, 0, h)),
                  pl.BlockSpec((1, LANES, t_main), lambda b, h, i: (b, h, 0))],
        out_specs=pl.BlockSpec((1, tq, LANES), lambda b, h, i: (b, i, h)),
        out_shape=jax.ShapeDtypeStruct((nb, tq_total, DIFF_WIDTH), BF16),
        scratch_shapes=[pltpu.VMEM((2, 1, tq), F32), pltpu.VMEM((2, LANES + ONES_ROWS, tq), F32),
                        pltpu.VMEM((2, 2, max(tk, tkp), tq), F32)],
        compiler_params=pltpu.CompilerParams(dimension_semantics=("arbitrary",) * 3,
                                             vmem_limit_bytes=VMEM_LIMIT),
        name="diff_attn",
    )(*lams, subg_col, q, kp, vtp, km, vtm)


def _suffix_matrix(n):
    r = lax.broadcasted_iota(jnp.int32, (n + ONES_ROWS, n), 0)
    c = lax.broadcasted_iota(jnp.int32, (n + ONES_ROWS, n), 1)
    return jnp.where((c >= r) | (r == n), 1.0, 0.0).astype(BF16)


def _sb_kernel(q_ref, kp_ref, vp_ref, km_ref, vm_ref, o_ref, c_ref, acc_ref, z_ref,
               *, tq, tk, tkp, pref_valid, n_diag):
    i = pl.program_id(2)
    n_full = i * n_diag
    qs = _split_halves(q_ref[0])
    c_ref[...] = jnp.zeros(c_ref.shape, F32)
    acc_ref[...] = jnp.zeros(acc_ref.shape, F32)

    def step(nxt, cur):
        if nxt is not None:
            kblk, nslot, nqlo = nxt
            for a in range(2):
                z_ref[nslot, a, :kblk.shape[0], nqlo:] = lax.dot_general(
                    kblk, qs[a][nqlo:], _NT, preferred_element_type=F32)
        if cur is None:
            return
        slot, vtblk, mask, sfx_mat, qlo, qhi = cur
        n = vtblk.shape[1]
        if mask is not None:
            mask = mask[:, qlo:qhi]
        for a in range(2):
            rows = slice(a * HEAD_DIM, (a + 1) * HEAD_DIM)
            u = z_ref[slot, a, :n, qlo:qhi]
            neg_part = jnp.minimum(u, 0.0)
            d = neg_part - u
            log1m = d - jnp.log(1.0 + jnp.exp2(neg_part + d)) * LOG2E
            if mask is not None:
                log1m = jnp.where(mask, log1m, 0.0)
            sfx = jnp.dot(sfx_mat, log1m.astype(BF16), preferred_element_type=F32)
            t = u + sfx[:n]
            if mask is not None:
                t = jnp.where(mask, t, NEG)
            pv = jnp.dot(vtblk[rows, :], jnp.exp2(t).astype(BF16), preferred_element_type=F32)
            acc_ref[rows, qlo:qhi] += pv * jnp.exp2(c_ref[a, :, qlo:qhi])
            c_ref[a, :, qlo:qhi] += sfx[n:n + 1]

    def main_k(g):
        return km_ref[0, pl.ds(pl.multiple_of(g * tk, tk), tk), :]

    def main_vt(g):
        return vm_ref[0, :, pl.ds(pl.multiple_of(g * tk, tk), tk)]

    sfx_main = _suffix_matrix(tk)
    sfx_pref = sfx_main if tkp == tk else _suffix_matrix(tkp)
    key = lax.broadcasted_iota(jnp.int32, (tk, tq), 0)
    qry = lax.broadcasted_iota(jnp.int32, (tk, tq), 1)

    def live(lo=0):
        return jnp.max(c_ref[:, :, lo:]) > SKIP_BELOW

    def fold(nxt, slot, vtblk, mask, sfx_mat, qlo, window):
        near_hi = min(qlo + window, tq)
        step(nxt, (slot, vtblk, mask, sfx_mat, qlo, near_hi))
        if near_hi < tq:
            @pl.when(live(near_hi))
            def _():
                step(None, (slot, vtblk, None, sfx_mat, near_hi, tq))

    step((kp_ref[0], 2, 0), None)
    step((main_k(n_full + n_diag - 1), 0, (n_diag - 1) * tk), None)
    for d in reversed(range(n_diag)):
        s = (n_diag - 1 - d) % 2
        if d > 0:
            nxt = (main_k(n_full + d - 1), 1 - s, (d - 1) * tk)
        else:
            nxt = (main_k(jnp.maximum(n_full - 1, 0)), 1 - s, 0)
        fold(nxt, s, main_vt(n_full + d), key + d * tk < qry, sfx_main, d * tk, 2 * tk)

    def full_body(t):
        g = n_full - 1 - 2 * t
        fold((main_k(g - 1), 1, 0), 0, main_vt(g), None, sfx_main, 0, tk)
        fold((main_k(jnp.maximum(g - 2, 0)), 0, 0), 1, main_vt(g - 1), None, sfx_main, 0, tk)
        return t + 1
    n_pairs = i * (n_diag // 2)
    lax.while_loop(lambda t: jnp.logical_and(t < n_pairs, live()), full_body, jnp.int32(0))

    @pl.when(live())
    def _():
        pmask = lax.broadcasted_iota(jnp.int32, (tkp, tq), 0) < pref_valid
        step(None, (2, vp_ref[0], pmask, sfx_pref, 0, tq))
    o_ref[0] = acc_ref[...].T.astype(BF16)


def _sb_attn(q, kp, vtp, km, vtm, *, tq, tk, pref_valid):
    nb, tq_total, _ = q.shape
    tkp = kp.shape[1]
    t_main = km.shape[1]
    n_diag = tq // tk
    assert n_diag % 2 == 0, "main blocks are consumed in pairs"
    kernel = functools.partial(_sb_kernel, tq=tq, tk=tk, tkp=tkp, pref_valid=pref_valid, n_diag=n_diag)
    return pl.pallas_call(
        kernel,
        grid=(nb, SB_WIDTH // LANES, tq_total // tq),
        in_specs=[pl.BlockSpec((1, tq, LANES), lambda b, h, i: (b, i, h)),
                  pl.BlockSpec((1, tkp, LANES), lambda b, h, i: (0, 0, h)),
                  pl.BlockSpec((1, LANES, tkp), lambda b, h, i: (0, h, 0)),
                  pl.BlockSpec((1, t_main, LANES), lambda b, h, i: (b, 0, h)),
                  pl.BlockSpec((1, LANES, t_main), lambda b, h, i: (b, h, 0))],
        out_specs=pl.BlockSpec((1, tq, LANES), lambda b, h, i: (b, i, h)),
        out_shape=jax.ShapeDtypeStruct((nb, tq_total, SB_WIDTH), BF16),
        scratch_shapes=[pltpu.VMEM((2, 1, tq), F32), pltpu.VMEM((LANES, tq), F32),
                        pltpu.VMEM((3, 2, max(tk, tkp), tq), F32)],
        compiler_params=pltpu.CompilerParams(dimension_semantics=("arbitrary",) * 3,
                                             vmem_limit_bytes=VMEM_LIMIT),
        name="sb_attn",
    )(q, kp, vtp, km, vtm)


def _diff_decode_kernel(lq1_ref, lk1_ref, lq2_ref, lk2_ref, subg_ref, q_ref, ck_ref, cv_ref, kn_ref, vn_ref,
                        o_ref, m_ref, l_ref, acc_ref, *, chunk):
    kc = pl.program_id(1)
    nq = q_ref.shape[0]

    @pl.when(kc == 0)
    def _():
        m_ref[...] = jnp.full(m_ref.shape, NEG, F32)
        l_ref[...] = jnp.zeros(l_ref.shape, F32)
        acc_ref[...] = jnp.zeros(acc_ref.shape, F32)

    def update(h, k, v):
        n = k.shape[0]
        qs = _split_halves(q_ref[:, h * LANES:(h + 1) * LANES])
        v_ones = jnp.concatenate([v, jnp.ones((n, LANES), BF16)], axis=1)
        ps, alphas = [], []
        for a in range(2):
            s = lax.dot_general(qs[a], k, _NT, preferred_element_type=F32)
            m_prev = m_ref[2 * h + a]
            m_new = jnp.maximum(m_prev, jnp.max(s, axis=1, keepdims=True))
            alphas.append(jnp.exp2(m_prev - m_new))
            ps.append(jnp.exp2(s - m_new).astype(BF16))
            m_ref[2 * h + a] = m_new
        pv = jnp.dot(jnp.concatenate(ps, axis=0), v_ones, preferred_element_type=F32)
        for a in range(2):
            part = pv[a * nq:(a + 1) * nq]
            acc_ref[2 * h + a] = alphas[a] * acc_ref[2 * h + a] + part[:, :LANES]
            l_ref[2 * h + a] = alphas[a] * l_ref[2 * h + a] + part[:, LANES:]

    for h in range(N_HEADS):
        update(h, ck_ref[pl.ds(h, chunk, stride=N_HEADS), :].astype(BF16),
               cv_ref[pl.ds(h, chunk, stride=N_HEADS), :].astype(BF16))

    @pl.when(kc == pl.num_programs(1) - 1)
    def _():
        lam = (jnp.exp(jnp.sum(lq1_ref[...] * lk1_ref[...], axis=1, keepdims=True))
               - jnp.exp(jnp.sum(lq2_ref[...] * lk2_ref[...], axis=1, keepdims=True)) + LAM_INIT)
        for h in range(N_HEADS):
            cols = slice(h * LANES, (h + 1) * LANES)
            update(h, kn_ref[:, cols], vn_ref[:, cols])
            o = acc_ref[2 * h] / l_ref[2 * h] - lam * (acc_ref[2 * h + 1] / l_ref[2 * h + 1])
            o_ref[:, cols] = (_rms(o) * subg_ref[...] * (1.0 - LAM_INIT)).astype(BF16)


def _diff_decode(lams, subg_row, q, ck, cv, kn, vn, *, chunk):
    nb, nq, _ = q.shape
    n_chunks = ck.shape[1] // (chunk * N_HEADS)
    small = _const_spec((1, HEAD_DIM))
    tok = pl.BlockSpec((None, nq, DIFF_WIDTH), lambda b, c: (b, 0, 0))
    cache = pl.BlockSpec((None, chunk * N_HEADS, LANES), lambda b, c: (b, c, 0))
    return pl.pallas_call(
        functools.partial(_diff_decode_kernel, chunk=chunk),
        grid=(nb, n_chunks),
        in_specs=[small, small, small, small, _const_spec((1, LANES)), tok, cache, cache, tok, tok],
        out_specs=tok,
        out_shape=jax.ShapeDtypeStruct((nb, nq, DIFF_WIDTH), BF16),
        scratch_shapes=[pltpu.VMEM((2 * N_HEADS, nq, 1), F32), pltpu.VMEM((2 * N_HEADS, nq, LANES), F32),
                        pltpu.VMEM((2 * N_HEADS, nq, LANES), F32)],
        compiler_params=pltpu.CompilerParams(dimension_semantics=("arbitrary",) * 2,
                                             vmem_limit_bytes=VMEM_LIMIT),
        name="diff_decode",
    )(*lams, subg_row, q, ck, cv, kn, vn)


SB_BLOCK = SB_KEY_BLOCK


def _sb_decode_kernel(q_ref, ckt_ref, cvt_ref, knt_ref, vnt_ref, o_ref):
    nq = q_ref.shape[0]
    past = ckt_ref.shape[1]
    n_blk = past // SB_BLOCK
    def suffix_cols(n):
        r = lax.broadcasted_iota(jnp.int32, (n, n + LANES), 0)
        c = lax.broadcasted_iota(jnp.int32, (n, n + LANES), 1)
        return jnp.where((r > c) | (c >= n), 1.0, 0.0).astype(BF16)
    sfx_blk = suffix_cols(SB_BLOCK)
    sfx_new = suffix_cols(LANES)
    row = lax.broadcasted_iota(jnp.int32, (nq, LANES), 0)
    lane = lax.broadcasted_iota(jnp.int32, (nq, LANES), 1)
    new_mask = lane < row

    def log1m_of(u):
        nu = -u
        return jnp.minimum(nu, 0.0) - jnp.log(1.0 + jnp.exp2(jnp.minimum(u, nu))) * LOG2E

    for j in range(ckt_ref.shape[0] // LANES):
        rows = slice(j * LANES, (j + 1) * LANES)
        kt = ckt_ref[rows, :].astype(BF16)
        vt = cvt_ref[rows, :].astype(BF16)
        qs = _split_halves(q_ref[:, rows])
        ws, wns = [], []
        for a in range(2):
            un = jnp.dot(qs[a], knt_ref[rows, :], preferred_element_type=F32)
            ln = jnp.where(new_mask, log1m_of(un), 0.0)
            sn = jnp.dot(ln.astype(BF16), sfx_new, preferred_element_type=F32)
            wns.append(jnp.exp2(jnp.where(new_mask, un + ln + sn[:, :LANES], NEG)).astype(BF16))
            carry = sn[:, LANES:]
            u = jnp.dot(qs[a], kt, preferred_element_type=F32)
            l1m = log1m_of(u)
            stacked = jnp.concatenate(
                [l1m[:, b * SB_BLOCK:(b + 1) * SB_BLOCK] for b in range(n_blk)], axis=0).astype(BF16)
            sfx = jnp.dot(stacked, sfx_blk, preferred_element_type=F32)
            ts = [None] * n_blk
            for b in reversed(range(n_blk)):
                cols = slice(b * SB_BLOCK, (b + 1) * SB_BLOCK)
                part = sfx[b * nq:(b + 1) * nq]
                c2 = jnp.concatenate([carry] * (SB_BLOCK // LANES), axis=1)
                ts[b] = u[:, cols] + l1m[:, cols] + part[:, :SB_BLOCK] + c2
                carry = carry + part[:, SB_BLOCK:]
            ws.append(jnp.exp2(jnp.concatenate(ts, axis=1)).astype(BF16))
        o = (lax.dot_general(jnp.concatenate(ws, axis=0), vt, _NT, preferred_element_type=F32)
             + lax.dot_general(jnp.concatenate(wns, axis=0), vnt_ref[rows, :], _NT,
                               preferred_element_type=F32))
        o_ref[:, rows] = jnp.where(lane < HEAD_DIM, o[:nq], o[nq:]).astype(BF16)


def _sb_decode(q, ckt, cvt, knt, vnt):
    nb, nq, _ = q.shape
    assert nq <= LANES
    past = ckt.shape[2]
    width = 2 * LANES
    tok = pl.BlockSpec((None, nq, width), lambda b, g: (b, 0, g))
    cache = pl.BlockSpec((None, width, past), lambda b, g: (b, g, 0))
    new = pl.BlockSpec((None, width, LANES), lambda b, g: (b, g, 0))
    return pl.pallas_call(
        _sb_decode_kernel,
        grid=(nb, SB_WIDTH // width),
        in_specs=[tok, cache, cache, new, new],
        out_specs=tok,
        out_shape=jax.ShapeDtypeStruct((nb, nq, SB_WIDTH), BF16),
        compiler_params=pltpu.CompilerParams(dimension_semantics=("arbitrary",) * 2,
                                             vmem_limit_bytes=VMEM_LIMIT),
        name="sb_decode",
    )(q, ckt, cvt, knt, vnt)


def _out_kernel(x_ref, od_ref, os_ref, gmix_ref, wg_ref, wdo_ref, wso_ref, wo_ref, gffn_ref,
                w1_ref, w2_ref, y_ref):
    x = x_ref[...]
    h = (_rms(x) * gmix_ref[...]).astype(BF16)
    gate = jax.nn.sigmoid(jnp.dot(h, wg_ref[...], preferred_element_type=F32))
    a = jnp.dot(od_ref[...], wdo_ref[...], preferred_element_type=F32)
    b = jnp.dot(os_ref[...], wso_ref[...], preferred_element_type=F32)
    merged = (gate[:, :D_MODEL] * a + gate[:, D_MODEL:] * b).astype(BF16)
    x1 = x + jnp.dot(merged, wo_ref[...], preferred_element_type=F32)
    h2 = (_rms(x1) * gffn_ref[...]).astype(BF16)
    y = x1
    for c in range(D_FF // D_MODEL):
        sl = slice(c * D_MODEL, (c + 1) * D_MODEL)
        f = jnp.maximum(jnp.dot(h2, w1_ref[:, sl], preferred_element_type=F32), 0.0)
        y = y + jnp.dot((f * f).astype(BF16), w2_ref[sl, :], preferred_element_type=F32)
    y_ref[...] = y


def _out(x, od, osb, gmix, wg, wdo, wso, wo, gffn, w1, w2, *, tm):
    rows = x.shape[0]
    row = lambda w: pl.BlockSpec((tm, w), lambda i: (i, 0))
    return pl.pallas_call(
        _out_kernel,
        grid=(rows // tm,),
        in_specs=[row(D_MODEL), row(DIFF_WIDTH), row(SB_WIDTH), _const_spec((1, D_MODEL)),
                  _const_spec((D_MODEL, 2 * D_MODEL)), _const_spec((DIFF_WIDTH, D_MODEL)),
                  _const_spec((SB_WIDTH, D_MODEL)), _const_spec((D_MODEL, D_MODEL)),
                  _const_spec((1, D_MODEL)), _const_spec((D_MODEL, D_FF)), _const_spec((D_FF, D_MODEL))],
        out_specs=row(D_MODEL),
        out_shape=jax.ShapeDtypeStruct((rows, D_MODEL), F32),
        compiler_params=pltpu.CompilerParams(dimension_semantics=("arbitrary",),
                                             vmem_limit_bytes=VMEM_LIMIT),
        name="out",
    )(x, od, osb, gmix, wg, wdo, wso, wo, gffn, w1, w2)


def _rope_tables(pos):
    half = ROT_DIM // 2
    d = jnp.arange(LANES, dtype=jnp.int32) % HEAD_DIM
    inv = ROPE_THETA ** (-(2 * (d % half)).astype(F32) / ROT_DIM)
    ang = pos.astype(F32)[:, None] * inv[None, :]
    cos, sin = jnp.cos(ang), jnp.sin(ang)
    cos_t = jnp.where(d < ROT_DIM, cos, 1.0)
    sa_t = jnp.where(d < half, -sin, 0.0)
    sb_t = jnp.where((d >= half) & (d < ROT_DIM), sin, 0.0)
    return cos_t, sa_t, sb_t


def kernel(x_prompt, x_sample, cache_diff_k, cache_diff_v, cache_sb_k, cache_sb_v, meta_tokens,
           g_mix, w_in, q_norm_g, k_norm_g, lam_q1, lam_k1, lam_q2, lam_k2, sub_g,
           w_diff_out, w_sb_out, w_out, g_ffn, w_ff1, w_ff2):
    nb, seq, _ = x_prompt.shape
    db, dseq, _ = x_sample.shape
    past = cache_diff_k.shape[2]
    lyr = 0

    w_qkv, w_gate = w_in[lyr, :, :QKV_COLS].astype(BF16), w_in[lyr, :, QKV_COLS:].astype(BF16)
    wdo, wso, wo = (w_diff_out[lyr].astype(BF16), w_sb_out[lyr].astype(BF16), w_out[lyr].astype(BF16))
    w1, w2 = w_ff1[lyr].astype(BF16), w_ff2[lyr].astype(BF16)
    gmix = g_mix[lyr].reshape(1, D_MODEL)
    gffn = g_ffn[lyr].reshape(1, D_MODEL)
    qg = jnp.tile(q_norm_g[lyr], COL_BLOCK // HEAD_DIM).reshape(1, COL_BLOCK)
    kg = jnp.tile(k_norm_g[lyr], COL_BLOCK // HEAD_DIM).reshape(1, COL_BLOCK)
    subg_col = sub_g[lyr].reshape(LANES, 1)
    lams = [t[lyr].reshape(1, HEAD_DIM) for t in (lam_q1, lam_k1, lam_q2, lam_k2)]
    grp = jnp.arange(MXU_TILE, dtype=jnp.int32) // HEAD_DIM
    gmat = jnp.where(grp[:, None] == grp[None, :], 1.0 / HEAD_DIM, 0.0).astype(BF16)

    main_pos = N_META + jnp.arange(seq, dtype=jnp.int32)
    small_pos = jnp.concatenate([jnp.arange(N_META, dtype=jnp.int32),
                                 jnp.tile(past + jnp.arange(dseq, dtype=jnp.int32), db)])
    x_main = x_prompt.reshape(nb * seq, D_MODEL)
    x_small = jnp.concatenate([meta_tokens.astype(F32), x_sample.reshape(db * dseq, D_MODEL)], axis=0)
    pm = _proj(x_main, gmix, w_qkv, qg, kg, gmat, *_rope_tables(main_pos), tm=ROW_TILE, prompt_layout=True,
               lead=N_META)
    ps = _proj(x_small, gmix, w_qkv, qg, kg, gmat, *_rope_tables(small_pos), tm=x_small.shape[0],
               prompt_layout=False)
    qd_m, kd_hm, kdb_m, vd_hm, vdt_m, qs_m, kst_m, ksb_m, vstf_m, vst_m = pm
    qd_s, kd_s, kdb_s, vd_s, vdb_s, qs_s, ks_s, ksb_s, vs_s, vsb_s = ps

    def bt(a, n, t):
        return a.reshape(n, t, a.shape[-1])

    def meta_keys(a):
        return jnp.pad(a[:N_META], ((0, LANES - N_META), (0, 0)))[None]

    def meta_vals_t(a):
        return jnp.swapaxes(meta_keys(a), 1, 2)

    od_p = _diff_attn(lams, subg_col, bt(qd_m, nb, seq), meta_keys(kdb_s), meta_vals_t(vdb_s),
                      bt(kdb_m, nb, seq), vdt_m,
                      tq=DIFF_Q_TILE, tk=DIFF_KEY_BLOCK, pref_valid=N_META)
    os_p = _sb_attn(bt(qs_m, nb, seq), meta_keys(ksb_s), meta_vals_t(vsb_s),
                    bt(ksb_m, nb, seq), vst_m,
                    tq=SB_Q_TILE, tk=SB_KEY_BLOCK, pref_valid=N_META)

    smp = lambda a: bt(a[N_META:], db, dseq)
    cdk = cache_diff_k[lyr].reshape(db, past * N_HEADS, LANES)
    cdv = cache_diff_v[lyr].reshape(db, past * N_HEADS, LANES)
    od_s = _diff_decode(lams, sub_g[lyr].reshape(1, LANES), smp(qd_s), cdk, cdv, smp(kdb_s), smp(vdb_s),
                        chunk=DECODE_CHUNK)
    sb_t = lambda c: jnp.transpose(c[lyr], (0, 2, 3, 1)).reshape(db, SB_WIDTH, past)
    new_t = lambda a: jnp.pad(jnp.swapaxes(smp(a), 1, 2), ((0, 0), (0, 0), (0, LANES - dseq)))
    os_s = _sb_decode(smp(qs_s), sb_t(cache_sb_k), sb_t(cache_sb_v), new_t(ksb_s), new_t(vsb_s))

    y_p = _out(x_main, od_p.reshape(nb * seq, DIFF_WIDTH), os_p.reshape(nb * seq, SB_WIDTH),
               gmix, w_gate, wdo, wso, wo, gffn, w1, w2, tm=ROW_TILE)
    y_s = _out(x_sample.reshape(db * dseq, D_MODEL), od_s.reshape(db * dseq, DIFF_WIDTH),
               os_s.reshape(db * dseq, SB_WIDTH), gmix, w_gate, wdo, wso, wo, gffn, w1, w2, tm=db * dseq)

    def diff_prompt_cache(head_major, small):
        meta = small[:N_META].reshape(N_META * N_HEADS, 2 * HEAD_DIM)
        full = _fill_lead_tokens(head_major, meta, nb=nb)
        return full.reshape(1, nb, seq + N_META, N_HEADS, 2 * HEAD_DIM)

    def sb_prompt_cache(main_t, small):
        meta_t = jnp.broadcast_to(small[:N_META].T[None], (nb, SB_WIDTH, N_META))
        full = jnp.concatenate([meta_t, main_t], axis=2).reshape(nb, N_HEADS, HEAD_DIM, seq + N_META)
        return jnp.transpose(full, (0, 3, 1, 2))[None]

    def sample_cache(small, dim):
        return small[N_META:].reshape(1, db, dseq, N_HEADS, dim)

    return (y_p.reshape(nb, seq, D_MODEL), y_s.reshape(db, dseq, D_MODEL),
            diff_prompt_cache(kd_hm, kd_s), diff_prompt_cache(vd_hm, vd_s),
            sb_prompt_cache(kst_m, ks_s), sb_prompt_cache(vstf_m, vs_s),
            sample_cache(kd_s, 2 * HEAD_DIM), sample_cache(vd_s, 2 * HEAD_DIM),
            sample_cache(ks_s, HEAD_DIM), sample_cache(vs_s, HEAD_DIM))
```

```python
import functools
import math

import jax
import jax.numpy as jnp
from jax import lax
from jax.experimental import pallas as pl
from jax.experimental.pallas import tpu as pltpu

F32 = jnp.float32
BF16 = jnp.bfloat16

D_MODEL = 1024
N_META = 16
CHUNK = 64
N_HEADS = 8
HEAD_DIM = 64
DIFF_WIDTH = N_HEADS * 2 * HEAD_DIM
SB_WIDTH = N_HEADS * HEAD_DIM
QKV_COLS = 3 * DIFF_WIDTH + 3 * SB_WIDTH
D_FF = 4 * D_MODEL
ROT_DIM = HEAD_DIM // 4
ROPE_THETA = 500000.0
EPS = 1e-6
NEG = -1e30
SKIP_BELOW = -160.0
LOG2E = math.log2(math.e)
Q_SCALE = HEAD_DIM ** -0.5 * LOG2E
ONES_ROWS = 16
LAM_INIT = 0.8 - 0.6 * math.exp(-0.3 * 0)

LANES = 128
MXU_TILE = 256
COL_BLOCK = 512
VMEM_LIMIT = 56 * 1024 * 1024

ROW_TILE = 512
DIFF_Q_TILE, DIFF_KEY_BLOCK = 2048, 512
SB_Q_TILE, SB_KEY_BLOCK = 2048, MXU_TILE
DECODE_CHUNK = 2048

_NT = (((1,), (1,)), ((), ()))


def _rms(x):
    return x * lax.rsqrt(jnp.mean(x * x, axis=-1, keepdims=True) + EPS)


def _const_spec(shape):
    return pl.BlockSpec(shape, lambda *_: (0,) * len(shape), pipeline_mode=pl.Buffered(1))


def _proj_kernel(x_ref, gmix_ref, w_ref, qg_ref, kg_ref, gmat_ref, cos_ref, sa_ref, sb_ref,
                 qd_ref, kd_ref, kdb_ref, vd_ref, vdb_ref, qs_ref, ks_ref, ksb_ref, vs_ref, vsb_ref,
                 *, prompt_layout):
    tm = x_ref.shape[0]
    heads_per_block = COL_BLOCK // LANES

    def store_diff(ref, j, y):
        if not prompt_layout:
            ref[:, j * COL_BLOCK:(j + 1) * COL_BLOCK] = y
            return
        for hh in range(heads_per_block):
            ref[pl.ds(j * heads_per_block + hh, tm, stride=N_HEADS), :] = y[:, hh * LANES:(hh + 1) * LANES]

    h = (_rms(x_ref[...]) * gmix_ref[...]).astype(BF16)
    cos = cos_ref[...]
    sa = sa_ref[...]
    sb = sb_ref[...]

    def col(j):
        return jnp.dot(h, w_ref[:, j * COL_BLOCK:(j + 1) * COL_BLOCK], preferred_element_type=F32)

    def normed_rot(y, g):
        sq = (y * y).astype(BF16)
        msq = jnp.concatenate(
            [jnp.dot(sq[:, c * MXU_TILE:(c + 1) * MXU_TILE], gmat_ref[...], preferred_element_type=F32)
             for c in range(COL_BLOCK // MXU_TILE)], axis=1)
        yn = y * lax.rsqrt(msq + EPS) * g
        parts = []
        for c in range(COL_BLOCK // LANES):
            t = yn[:, c * LANES:(c + 1) * LANES]
            parts.append(t * cos + pltpu.roll(t, LANES - ROT_DIM // 2, 1) * sa
                         + pltpu.roll(t, ROT_DIM // 2, 1) * sb)
        return jnp.concatenate(parts, axis=1)

    for j in range(2):
        sl = slice(j * COL_BLOCK, (j + 1) * COL_BLOCK)
        q = normed_rot(col(j), qg_ref[...])
        qd_ref[:, sl] = (q * Q_SCALE).astype(BF16)
        k = normed_rot(col(2 + j), kg_ref[...])
        store_diff(kd_ref, j, k)
        kdb_ref[:, sl] = k.astype(BF16)
        v = col(4 + j)
        store_diff(vd_ref, j, v)
        if prompt_layout:
            vdb_ref[sl, :] = v.T.astype(BF16)
        else:
            vdb_ref[:, sl] = v.astype(BF16)
    qs_ref[...] = (col(6) * Q_SCALE).astype(BF16)
    k = col(7)
    ksb_ref[...] = k.astype(BF16)
    v = col(8)
    if prompt_layout:
        ks_ref[...] = k.T
        vt = v.T
        vs_ref[...] = vt
        vsb_ref[...] = vt.astype(BF16)
    else:
        ks_ref[...] = k
        vs_ref[...] = v
        vsb_ref[...] = v.astype(BF16)


def _proj(x, gmix, w_qkv, qg, kg, gmat, cos, sa, sb, *, tm, prompt_layout, lead=0):
    rows = x.shape[0]
    seq = cos.shape[0]
    n_pos_tiles = seq // tm
    row = lambda w: pl.BlockSpec((tm, w), lambda i: (i, 0))
    tab = pl.BlockSpec((tm, LANES), lambda i: (i % n_pos_tiles, 0))
    wide = lambda dt: jax.ShapeDtypeStruct((rows, DIFF_WIDTH), dt)
    narrow = lambda dt: jax.ShapeDtypeStruct((rows, SB_WIDTH), dt)
    if prompt_layout:
        nb = rows // seq
        t_spec = lambda w: pl.BlockSpec((None, w, tm), lambda i: (i // n_pos_tiles, 0, i % n_pos_tiles))
        t_shape = lambda w, dt: jax.ShapeDtypeStruct((nb, w, seq), dt)
        hm_spec = pl.BlockSpec(
            (pl.Element(tm * N_HEADS), pl.Element(LANES)),
            lambda i: (((i // n_pos_tiles) * (seq + lead) + lead + (i % n_pos_tiles) * tm) * N_HEADS, 0))
        hm_shape = jax.ShapeDtypeStruct((nb * (seq + lead) * N_HEADS, LANES), F32)
        out_specs = [row(DIFF_WIDTH), hm_spec, row(DIFF_WIDTH), hm_spec, t_spec(DIFF_WIDTH),
                     row(SB_WIDTH), t_spec(SB_WIDTH), row(SB_WIDTH), t_spec(SB_WIDTH), t_spec(SB_WIDTH)]
        out_shape = [wide(BF16), hm_shape, wide(BF16), hm_shape, t_shape(DIFF_WIDTH, BF16),
                     narrow(BF16), t_shape(SB_WIDTH, F32), narrow(BF16), t_shape(SB_WIDTH, F32),
                     t_shape(SB_WIDTH, BF16)]
    else:
        out_specs = [row(DIFF_WIDTH)] * 5 + [row(SB_WIDTH)] * 5
        out_shape = [wide(BF16), wide(F32), wide(BF16), wide(F32), wide(BF16),
                     narrow(BF16), narrow(F32), narrow(BF16), narrow(F32), narrow(BF16)]
    return pl.pallas_call(
        functools.partial(_proj_kernel, prompt_layout=prompt_layout),
        grid=(rows // tm,),
        in_specs=[row(D_MODEL), _const_spec((1, D_MODEL)), _const_spec((D_MODEL, QKV_COLS)),
                  _const_spec((1, COL_BLOCK)), _const_spec((1, COL_BLOCK)),
                  _const_spec((MXU_TILE, MXU_TILE)), tab, tab, tab],
        out_specs=out_specs,
        out_shape=out_shape,
        compiler_params=pltpu.CompilerParams(dimension_semantics=("arbitrary",),
                                             vmem_limit_bytes=VMEM_LIMIT),
        name="proj",
    )(x, gmix, w_qkv, qg, kg, gmat, cos, sa, sb)


def _fill_lead_kernel(lead_ref, big_ref, out_ref):
    del big_ref
    out_ref[...] = lead_ref[...]


def _fill_lead_tokens(big, lead_rows, *, nb):
    n = lead_rows.shape[0]
    per_batch = big.shape[0] // nb
    return pl.pallas_call(
        _fill_lead_kernel,
        grid=(nb,),
        in_specs=[_const_spec((n, LANES)), pl.BlockSpec(memory_space=pl.ANY)],
        out_specs=pl.BlockSpec((pl.Element(n), pl.Element(LANES)), lambda b: (b * per_batch, 0)),
        out_shape=jax.ShapeDtypeStruct(big.shape, big.dtype),
        input_output_aliases={1: 0},
        compiler_params=pltpu.CompilerParams(dimension_semantics=("arbitrary",)),
        name="fill_lead_tokens",
    )(lead_rows, big)


def _split_halves(q):
    lane = lax.broadcasted_iota(jnp.int32, q.shape, 1)
    zero = jnp.zeros_like(q)
    return jnp.where(lane < HEAD_DIM, q, zero), jnp.where(lane >= HEAD_DIM, q, zero)


def _diff_kernel(lq1_ref, lk1_ref, lq2_ref, lk2_ref, subg_ref, q_ref, kp_ref, vp_ref, km_ref, vm_ref,
                 o_ref, m_ref, acc_ref, s_ref, *, tq, tk, tkp, pref_valid, n_diag):
    i = pl.program_id(2)
    n_full = i * n_diag
    qs = _split_halves(q_ref[0])
    m_ref[...] = jnp.full(m_ref.shape, NEG, F32)
    acc_ref[...] = jnp.zeros(acc_ref.shape, F32)

    def step(nxt, cur):
        if cur is not None:
            slot, vtblk, mask, qlo = cur
            n = vtblk.shape[1]
            vt_ones = jnp.concatenate([vtblk, jnp.ones((ONES_ROWS, n), BF16)], axis=0)
        for a in range(2):
            if cur is not None:
                s = s_ref[slot, a, :n, qlo:]
                if mask is not None:
                    s = jnp.where(mask[:, qlo:], s, NEG)
                m_prev = m_ref[a, :, qlo:]
                m_new = jnp.maximum(m_prev, jnp.max(s, axis=0, keepdims=True))
                alpha = jnp.exp2(m_prev - m_new)
                p = jnp.exp2(s - m_new)
            if nxt is not None:
                kblk, nslot, nqlo = nxt
                s_ref[nslot, a, :kblk.shape[0], nqlo:] = lax.dot_general(
                    kblk, qs[a][nqlo:], _NT, preferred_element_type=F32)
            if cur is not None:
                acc_ref[a, :, qlo:] = (alpha * acc_ref[a, :, qlo:]
                                       + jnp.dot(vt_ones, p.astype(BF16), preferred_element_type=F32))
                m_ref[a, :, qlo:] = m_new

    def main_k(g):
        return km_ref[0, pl.ds(pl.multiple_of(g * tk, tk), tk), :]

    def main_vt(g):
        return vm_ref[0, :, pl.ds(pl.multiple_of(g * tk, tk), tk)]

    step((kp_ref[0], 1, 0), None)
    step((main_k(0), 0, 0), (1, vp_ref[0], lax.broadcasted_iota(jnp.int32, (tkp, tq), 0) < pref_valid, 0))

    def full_body(t, carry):
        step((main_k(2 * t + 1), 1, 0), (0, main_vt(2 * t), None, 0))
        step((main_k(2 * t + 2), 0, 0), (1, main_vt(2 * t + 1), None, 0))
        return carry
    lax.fori_loop(0, i * (n_diag // 2), full_body, 0)

    key_idx = lax.broadcasted_iota(jnp.int32, (tk, tq), 0)
    q_chunk = lax.broadcasted_iota(jnp.int32, (tk, tq), 1) // CHUNK
    for d in range(n_diag):
        g = n_full + d
        nxt = (main_k(g + 1), (d + 1) % 2, (d + 1) * tk) if d + 1 < n_diag else None
        step(nxt, (d % 2, main_vt(g), (key_idx + d * tk) // CHUNK <= q_chunk, d * tk))

    lam = (jnp.exp(jnp.sum(lq1_ref[...] * lk1_ref[...], axis=1, keepdims=True))
           - jnp.exp(jnp.sum(lq2_ref[...] * lk2_ref[...], axis=1, keepdims=True)) + LAM_INIT)
    o = (acc_ref[0, :LANES] / acc_ref[0, LANES:LANES + 1]
         - lam * (acc_ref[1, :LANES] / acc_ref[1, LANES:LANES + 1]))
    o = o * lax.rsqrt(jnp.mean(o * o, axis=0, keepdims=True) + EPS) * subg_ref[...] * (1.0 - LAM_INIT)
    o_ref[0] = o.T.astype(BF16)


def _diff_attn(lams, subg_col, q, kp, vtp, km, vtm, *, tq, tk, pref_valid):
    nb, tq_total, _ = q.shape
    tkp = kp.shape[1]
    t_main = km.shape[1]
    n_diag = tq // tk
    assert n_diag % 2 == 0, "main blocks are consumed in pairs"
    kernel = functools.partial(_diff_kernel, tq=tq, tk=tk, tkp=tkp, pref_valid=pref_valid, n_diag=n_diag)
    small = _const_spec((1, HEAD_DIM))
    return pl.pallas_call(
        kernel,
        grid=(nb, N_HEADS, tq_total // tq),
        in_specs=[small, small, small, small, _const_spec((LANES, 1)),
                  pl.BlockSpec((1, tq, LANES), lambda b, h, i: (b, i, h)),
                  pl.BlockSpec((1, tkp, LANES), lambda b, h, i: (0, 0, h)),
                  pl.BlockSpec((1, LANES, tkp), lambda b, h, i: (0, h, 0)),
                  pl.BlockSpec((1, t_main, LANES), lambda b, h, i: (b, 0, h)),
                  pl.BlockSpec((1, LANES, t_main), lambda b, h, i: (b, h, 0))],
        out_specs=pl.BlockSpec((1, tq, LANES), lambda b, h, i: (b, i, h)),
        out_shape=jax.ShapeDtypeStruct((nb, tq_total, DIFF_WIDTH), BF16),
        scratch_shapes=[pltpu.VMEM((2, 1, tq), F32), pltpu.VMEM((2, LANES + ONES_ROWS, tq), F32),
                        pltpu.VMEM((2, 2, max(tk, tkp), tq), F32)],
        compiler_params=pltpu.CompilerParams(dimension_semantics=("arbitrary",) * 3,
                                             vmem_limit_bytes=VMEM_LIMIT),
        name="diff_attn",
    )(*lams, subg_col, q, kp, vtp, km, vtm)


def _suffix_matrix(n):
    r = lax.broadcasted_iota(jnp.int32, (n + ONES_ROWS, n), 0)
    c = lax.broadcasted_iota(jnp.int32, (n + ONES_ROWS, n), 1)
    return jnp.where((c >= r) | (r == n), 1.0, 0.0).astype(BF16)


def _sb_kernel(q_ref, kp_ref, vp_ref, km_ref, vm_ref, o_ref, c_ref, acc_ref, z_ref,
               *, tq, tk, tkp, pref_valid, n_diag):
    i = pl.program_id(2)
    n_full = i * n_diag
    qs = _split_halves(q_ref[0])
    c_ref[...] = jnp.zeros(c_ref.shape, F32)
    acc_ref[...] = jnp.zeros(acc_ref.shape, F32)

    def step(nxt, cur):
        if nxt is not None:
            kblk, nslot, nqlo = nxt
            for a in range(2):
                z_ref[nslot, a, :kblk.shape[0], nqlo:] = lax.dot_general(
                    kblk, qs[a][nqlo:], _NT, preferred_element_type=F32)
        if cur is None:
            return
        slot, vtblk, mask, sfx_mat, qlo, qhi = cur
        n = vtblk.shape[1]
        if mask is not None:
            mask = mask[:, qlo:qhi]
        for a in range(2):
            rows = slice(a * HEAD_DIM, (a + 1) * HEAD_DIM)
            u = z_ref[slot, a, :n, qlo:qhi]
            neg_part = jnp.minimum(u, 0.0)
            d = neg_part - u
            log1m = d - jnp.log(1.0 + jnp.exp2(neg_part + d)) * LOG2E
            if mask is not None:
                log1m = jnp.where(mask, log1m, 0.0)
            sfx = jnp.dot(sfx_mat, log1m.astype(BF16), preferred_element_type=F32)
            t = u + sfx[:n]
            if mask is not None:
                t = jnp.where(mask, t, NEG)
            pv = jnp.dot(vtblk[rows, :], jnp.exp2(t).astype(BF16), preferred_element_type=F32)
            acc_ref[rows, qlo:qhi] += pv * jnp.exp2(c_ref[a, :, qlo:qhi])
            c_ref[a, :, qlo:qhi] += sfx[n:n + 1]

    def main_k(g):
        return km_ref[0, pl.ds(pl.multiple_of(g * tk, tk), tk), :]

    def main_vt(g):
        return vm_ref[0, :, pl.ds(pl.multiple_of(g * tk, tk), tk)]

    sfx_main = _suffix_matrix(tk)
    sfx_pref = sfx_main if tkp == tk else _suffix_matrix(tkp)
    key = lax.broadcasted_iota(jnp.int32, (tk, tq), 0)
    qry = lax.broadcasted_iota(jnp.int32, (tk, tq), 1)

    def live(lo=0):
        return jnp.max(c_ref[:, :, lo:]) > SKIP_BELOW

    def fold(nxt, slot, vtblk, mask, sfx_mat, qlo, window):
        near_hi = min(qlo + window, tq)
        step(nxt, (slot, vtblk, mask, sfx_mat, qlo, near_hi))
        if near_hi < tq:
            @pl.when(live(near_hi))
            def _():
                step(None, (slot, vtblk, None, sfx_mat, near_hi, tq))

    step((kp_ref[0], 2, 0), None)
    step((main_k(n_full + n_diag - 1), 0, (n_diag - 1) * tk), None)
    for d in reversed(range(n_diag)):
        s = (n_diag - 1 - d) % 2
        if d > 0:
            nxt = (main_k(n_full + d - 1), 1 - s, (d - 1) * tk)
        else:
            nxt = (main_k(jnp.maximum(n_full - 1, 0)), 1 - s, 0)
        fold(nxt, s, main_vt(n_full + d), key + d * tk < qry, sfx_main, d * tk, tk)

    def full_body(t):
        g = n_full - 1 - 2 * t
        fold((main_k(g - 1), 1, 0), 0, main_vt(g), None, sfx_main, 0, tk)
        fold((main_k(jnp.maximum(g - 2, 0)), 0, 0), 1, main_vt(g - 1), None, sfx_main, 0, tk)
        return t + 1
    n_pairs = i * (n_diag // 2)
    lax.while_loop(lambda t: jnp.logical_and(t < n_pairs, live()), full_body, jnp.int32(0))

    @pl.when(live())
    def _():
        pmask = lax.broadcasted_iota(jnp.int32, (tkp, tq), 0) < pref_valid
        step(None, (2, vp_ref[0], pmask, sfx_pref, 0, tq))
    o_ref[0] = acc_ref[...].T.astype(BF16)


def _sb_attn(q, kp, vtp, km, vtm, *, tq, tk, pref_valid):
    nb, tq_total, _ = q.shape
    tkp = kp.shape[1]
    t_main = km.shape[1]
    n_diag = tq // tk
    assert n_diag % 2 == 0, "main blocks are consumed in pairs"
    kernel = functools.partial(_sb_kernel, tq=tq, tk=tk, tkp=tkp, pref_valid=pref_valid, n_diag=n_diag)
    return pl.pallas_call(
        kernel,
        grid=(nb, SB_WIDTH // LANES, tq_total // tq),
        in_specs=[pl.BlockSpec((1, tq, LANES), lambda b, h, i: (b, i, h)),
                  pl.BlockSpec((1, tkp, LANES), lambda b, h, i: (0, 0, h)),
                  pl.BlockSpec((1, LANES, tkp), lambda b, h, i: (0, h, 0)),
                  pl.BlockSpec((1, t_main, LANES), lambda b, h, i: (b, 0, h)),
                  pl.BlockSpec((1, LANES, t_main), lambda b, h, i: (b, h, 0))],
        out_specs=pl.BlockSpec((1, tq, LANES), lambda b, h, i: (b, i, h)),
        out_shape=jax.ShapeDtypeStruct((nb, tq_total, SB_WIDTH), BF16),
        scratch_shapes=[pltpu.VMEM((2, 1, tq), F32), pltpu.VMEM((LANES, tq), F32),
                        pltpu.VMEM((3, 2, max(tk, tkp), tq), F32)],
        compiler_params=pltpu.CompilerParams(dimension_semantics=("arbitrary",) * 3,
                                             vmem_limit_bytes=VMEM_LIMIT),
        name="sb_attn",
    )(q, kp, vtp, km, vtm)


def _diff_decode_kernel(lq1_ref, lk1_ref, lq2_ref, lk2_ref, subg_ref, q_ref, ck_ref, cv_ref, kn_ref, vn_ref,
                        o_ref, m_ref, l_ref, acc_ref, *, chunk):
    kc = pl.program_id(1)
    nq = q_ref.shape[0]

    @pl.when(kc == 0)
    def _():
        m_ref[...] = jnp.full(m_ref.shape, NEG, F32)
        l_ref[...] = jnp.zeros(l_ref.shape, F32)
        acc_ref[...] = jnp.zeros(acc_ref.shape, F32)

    def update(h, k, v):
        n = k.shape[0]
        qs = _split_halves(q_ref[:, h * LANES:(h + 1) * LANES])
        v_ones = jnp.concatenate([v, jnp.ones((n, LANES), BF16)], axis=1)
        ps, alphas = [], []
        for a in range(2):
            s = lax.dot_general(qs[a], k, _NT, preferred_element_type=F32)
            m_prev = m_ref[2 * h + a]
            m_new = jnp.maximum(m_prev, jnp.max(s, axis=1, keepdims=True))
            alphas.append(jnp.exp2(m_prev - m_new))
            ps.append(jnp.exp2(s - m_new).astype(BF16))
            m_ref[2 * h + a] = m_new
        pv = jnp.dot(jnp.concatenate(ps, axis=0), v_ones, preferred_element_type=F32)
        for a in range(2):
            part = pv[a * nq:(a + 1) * nq]
            acc_ref[2 * h + a] = alphas[a] * acc_ref[2 * h + a] + part[:, :LANES]
            l_ref[2 * h + a] = alphas[a] * l_ref[2 * h + a] + part[:, LANES:]

    for h in range(N_HEADS):
        update(h, ck_ref[pl.ds(h, chunk, stride=N_HEADS), :].astype(BF16),
               cv_ref[pl.ds(h, chunk, stride=N_HEADS), :].astype(BF16))

    @pl.when(kc == pl.num_programs(1) - 1)
    def _():
        lam = (jnp.exp(jnp.sum(lq1_ref[...] * lk1_ref[...], axis=1, keepdims=True))
               - jnp.exp(jnp.sum(lq2_ref[...] * lk2_ref[...], axis=1, keepdims=True)) + LAM_INIT)
        for h in range(N_HEADS):
            cols = slice(h * LANES, (h + 1) * LANES)
            update(h, kn_ref[:, cols], vn_ref[:, cols])
            o = acc_ref[2 * h] / l_ref[2 * h] - lam * (acc_ref[2 * h + 1] / l_ref[2 * h + 1])
            o_ref[:, cols] = (_rms(o) * subg_ref[...] * (1.0 - LAM_INIT)).astype(BF16)


def _diff_decode(lams, subg_row, q, ck, cv, kn, vn, *, chunk):
    nb, nq, _ = q.shape
    n_chunks = ck.shape[1] // (chunk * N_HEADS)
    small = _const_spec((1, HEAD_DIM))
    tok = pl.BlockSpec((None, nq, DIFF_WIDTH), lambda b, c: (b, 0, 0))
    cache = pl.BlockSpec((None, chunk * N_HEADS, LANES), lambda b, c: (b, c, 0))
    return pl.pallas_call(
        functools.partial(_diff_decode_kernel, chunk=chunk),
        grid=(nb, n_chunks),
        in_specs=[small, small, small, small, _const_spec((1, LANES)), tok, cache, cache, tok, tok],
        out_specs=tok,
        out_shape=jax.ShapeDtypeStruct((nb, nq, DIFF_WIDTH), BF16),
        scratch_shapes=[pltpu.VMEM((2 * N_HEADS, nq, 1), F32), pltpu.VMEM((2 * N_HEADS, nq, LANES), F32),
                        pltpu.VMEM((2 * N_HEADS, nq, LANES), F32)],
        compiler_params=pltpu.CompilerParams(dimension_semantics=("arbitrary",) * 2,
                                             vmem_limit_bytes=VMEM_LIMIT),
        name="diff_decode",
    )(*lams, subg_row, q, ck, cv, kn, vn)


SB_BLOCK = SB_KEY_BLOCK


def _sb_decode_kernel(q_ref, ckt_ref, cvt_ref, knt_ref, vnt_ref, o_ref):
    nq = q_ref.shape[0]
    past = ckt_ref.shape[1]
    n_blk = past // SB_BLOCK
    def suffix_cols(n):
        r = lax.broadcasted_iota(jnp.int32, (n, n + LANES), 0)
        c = lax.broadcasted_iota(jnp.int32, (n, n + LANES), 1)
        return jnp.where((r > c) | (c >= n), 1.0, 0.0).astype(BF16)
    sfx_blk = suffix_cols(SB_BLOCK)
    sfx_new = suffix_cols(LANES)
    row = lax.broadcasted_iota(jnp.int32, (nq, LANES), 0)
    lane = lax.broadcasted_iota(jnp.int32, (nq, LANES), 1)
    new_mask = lane < row

    def log1m_of(u):
        nu = -u
        return jnp.minimum(nu, 0.0) - jnp.log(1.0 + jnp.exp2(jnp.minimum(u, nu))) * LOG2E

    for j in range(ckt_ref.shape[0] // LANES):
        rows = slice(j * LANES, (j + 1) * LANES)
        kt = ckt_ref[rows, :].astype(BF16)
        vt = cvt_ref[rows, :].astype(BF16)
        qs = _split_halves(q_ref[:, rows])
        ws, wns = [], []
        for a in range(2):
            un = jnp.dot(qs[a], knt_ref[rows, :], preferred_element_type=F32)
            ln = jnp.where(new_mask, log1m_of(un), 0.0)
            sn = jnp.dot(ln.astype(BF16), sfx_new, preferred_element_type=F32)
            wns.append(jnp.exp2(jnp.where(new_mask, un + ln + sn[:, :LANES], NEG)).astype(BF16))
            carry = sn[:, LANES:]
            u = jnp.dot(qs[a], kt, preferred_element_type=F32)
            l1m = log1m_of(u)
            stacked = jnp.concatenate(
                [l1m[:, b * SB_BLOCK:(b + 1) * SB_BLOCK] for b in range(n_blk)], axis=0).astype(BF16)
            sfx = jnp.dot(stacked, sfx_blk, preferred_element_type=F32)
            ts = [None] * n_blk
            for b in reversed(range(n_blk)):
                cols = slice(b * SB_BLOCK, (b + 1) * SB_BLOCK)
                part = sfx[b * nq:(b + 1) * nq]
                c2 = jnp.concatenate([carry] * (SB_BLOCK // LANES), axis=1)
                ts[b] = u[:, cols] + l1m[:, cols] + part[:, :SB_BLOCK] + c2
                carry = carry + part[:, SB_BLOCK:]
            ws.append(jnp.exp2(jnp.concatenate(ts, axis=1)).astype(BF16))
        o = (lax.dot_general(jnp.concatenate(ws, axis=0), vt, _NT, preferred_element_type=F32)
             + lax.dot_general(jnp.concatenate(wns, axis=0), vnt_ref[rows, :], _NT,
                               preferred_element_type=F32))
        o_ref[:, rows] = jnp.where(lane < HEAD_DIM, o[:nq], o[nq:]).astype(BF16)


def _sb_decode(q, ckt, cvt, knt, vnt):
    nb, nq, _ = q.shape
    assert nq <= LANES
    past = ckt.shape[2]
    width = 2 * LANES
    tok = pl.BlockSpec((None, nq, width), lambda b, g: (b, 0, g))
    cache = pl.BlockSpec((None, width, past), lambda b, g: (b, g, 0))
    new = pl.BlockSpec((None, width, LANES), lambda b, g: (b, g, 0))
    return pl.pallas_call(
        _sb_decode_kernel,
        grid=(nb, SB_WIDTH // width),
        in_specs=[tok, cache, cache, new, new],
        out_specs=tok,
        out_shape=jax.ShapeDtypeStruct((nb, nq, SB_WIDTH), BF16),
        compiler_params=pltpu.CompilerParams(dimension_semantics=("arbitrary",) * 2,
                                             vmem_limit_bytes=VMEM_LIMIT),
        name="sb_decode",
    )(q, ckt, cvt, knt, vnt)


def _out_kernel(x_ref, od_ref, os_ref, gmix_ref, wg_ref, wdo_ref, wso_ref, wo_ref, gffn_ref,
                w1_ref, w2_ref, y_ref):
    x = x_ref[...]
    h = (_rms(x) * gmix_ref[...]).astype(BF16)
    gate = jax.nn.sigmoid(jnp.dot(h, wg_ref[...], preferred_element_type=F32))
    a = jnp.dot(od_ref[...], wdo_ref[...], preferred_element_type=F32)
    b = jnp.dot(os_ref[...], wso_ref[...], preferred_element_type=F32)
    merged = (gate[:, :D_MODEL] * a + gate[:, D_MODEL:] * b).astype(BF16)
    x1 = x + jnp.dot(merged, wo_ref[...], preferred_element_type=F32)
    h2 = (_rms(x1) * gffn_ref[...]).astype(BF16)
    y = x1
    for c in range(D_FF // D_MODEL):
        sl = slice(c * D_MODEL, (c + 1) * D_MODEL)
        f = jnp.maximum(jnp.dot(h2, w1_ref[:, sl], preferred_element_type=F32), 0.0)
        y = y + jnp.dot((f * f).astype(BF16), w2_ref[sl, :], preferred_element_type=F32)
    y_ref[...] = y


def _out(x, od, osb, gmix, wg, wdo, wso, wo, gffn, w1, w2, *, tm):
    rows = x.shape[0]
    row = lambda w: pl.BlockSpec((tm, w), lambda i: (i, 0))
    return pl.pallas_call(
        _out_kernel,
        grid=(rows // tm,),
        in_specs=[row(D_MODEL), row(DIFF_WIDTH), row(SB_WIDTH), _const_spec((1, D_MODEL)),
                  _const_spec((D_MODEL, 2 * D_MODEL)), _const_spec((DIFF_WIDTH, D_MODEL)),
                  _const_spec((SB_WIDTH, D_MODEL)), _const_spec((D_MODEL, D_MODEL)),
                  _const_spec((1, D_MODEL)), _const_spec((D_MODEL, D_FF)), _const_spec((D_FF, D_MODEL))],
        out_specs=row(D_MODEL),
        out_shape=jax.ShapeDtypeStruct((rows, D_MODEL), F32),
        compiler_params=pltpu.CompilerParams(dimension_semantics=("arbitrary",),
                                             vmem_limit_bytes=VMEM_LIMIT),
        name="out",
    )(x, od, osb, gmix, wg, wdo, wso, wo, gffn, w1, w2)


def _rope_tables(pos):
    half = ROT_DIM // 2
    d = jnp.arange(LANES, dtype=jnp.int32) % HEAD_DIM
    inv = ROPE_THETA ** (-(2 * (d % half)).astype(F32) / ROT_DIM)
    ang = pos.astype(F32)[:, None] * inv[None, :]
    cos, sin = jnp.cos(ang), jnp.sin(ang)
    cos_t = jnp.where(d < ROT_DIM, cos, 1.0)
    sa_t = jnp.where(d < half, -sin, 0.0)
    sb_t = jnp.where((d >= half) & (d < ROT_DIM), sin, 0.0)
    return cos_t, sa_t, sb_t


def kernel(x_prompt, x_sample, cache_diff_k, cache_diff_v, cache_sb_k, cache_sb_v, meta_tokens,
           g_mix, w_in, q_norm_g, k_norm_g, lam_q1, lam_k1, lam_q2, lam_k2, sub_g,
           w_diff_out, w_sb_out, w_out, g_ffn, w_ff1, w_ff2):
    nb, seq, _ = x_prompt.shape
    db, dseq, _ = x_sample.shape
    past = cache_diff_k.shape[2]
    lyr = 0

    w_qkv, w_gate = w_in[lyr, :, :QKV_COLS].astype(BF16), w_in[lyr, :, QKV_COLS:].astype(BF16)
    wdo, wso, wo = (w_diff_out[lyr].astype(BF16), w_sb_out[lyr].astype(BF16), w_out[lyr].astype(BF16))
    w1, w2 = w_ff1[lyr].astype(BF16), w_ff2[lyr].astype(BF16)
    gmix = g_mix[lyr].reshape(1, D_MODEL)
    gffn = g_ffn[lyr].reshape(1, D_MODEL)
    qg = jnp.tile(q_norm_g[lyr], COL_BLOCK // HEAD_DIM).reshape(1, COL_BLOCK)
    kg = jnp.tile(k_norm_g[lyr], COL_BLOCK // HEAD_DIM).reshape(1, COL_BLOCK)
    subg_col = sub_g[lyr].reshape(LANES, 1)
    lams = [t[lyr].reshape(1, HEAD_DIM) for t in (lam_q1, lam_k1, lam_q2, lam_k2)]
    grp = jnp.arange(MXU_TILE, dtype=jnp.int32) // HEAD_DIM
    gmat = jnp.where(grp[:, None] == grp[None, :], 1.0 / HEAD_DIM, 0.0).astype(BF16)

    main_pos = N_META + jnp.arange(seq, dtype=jnp.int32)
    small_pos = jnp.concatenate([jnp.arange(N_META, dtype=jnp.int32),
                                 jnp.tile(past + jnp.arange(dseq, dtype=jnp.int32), db)])
    x_main = x_prompt.reshape(nb * seq, D_MODEL)
    x_small = jnp.concatenate([meta_tokens.astype(F32), x_sample.reshape(db * dseq, D_MODEL)], axis=0)
    pm = _proj(x_main, gmix, w_qkv, qg, kg, gmat, *_rope_tables(main_pos), tm=ROW_TILE, prompt_layout=True,
               lead=N_META)
    ps = _proj(x_small, gmix, w_qkv, qg, kg, gmat, *_rope_tables(small_pos), tm=x_small.shape[0],
               prompt_layout=False)
    qd_m, kd_hm, kdb_m, vd_hm, vdt_m, qs_m, kst_m, ksb_m, vstf_m, vst_m = pm
    qd_s, kd_s, kdb_s, vd_s, vdb_s, qs_s, ks_s, ksb_s, vs_s, vsb_s = ps

    def bt(a, n, t):
        return a.reshape(n, t, a.shape[-1])

    def meta_keys(a):
        return jnp.pad(a[:N_META], ((0, LANES - N_META), (0, 0)))[None]

    def meta_vals_t(a):
        return jnp.swapaxes(meta_keys(a), 1, 2)

    od_p = _diff_attn(lams, subg_col, bt(qd_m, nb, seq), meta_keys(kdb_s), meta_vals_t(vdb_s),
                      bt(kdb_m, nb, seq), vdt_m,
                      tq=DIFF_Q_TILE, tk=DIFF_KEY_BLOCK, pref_valid=N_META)
    os_p = _sb_attn(bt(qs_m, nb, seq), meta_keys(ksb_s), meta_vals_t(vsb_s),
                    bt(ksb_m, nb, seq), vst_m,
                    tq=SB_Q_TILE, tk=SB_KEY_BLOCK, pref_valid=N_META)

    smp = lambda a: bt(a[N_META:], db, dseq)
    cdk = cache_diff_k[lyr].reshape(db, past * N_HEADS, LANES)
    cdv = cache_diff_v[lyr].reshape(db, past * N_HEADS, LANES)
    od_s = _diff_decode(lams, sub_g[lyr].reshape(1, LANES), smp(qd_s), cdk, cdv, smp(kdb_s), smp(vdb_s),
                        chunk=DECODE_CHUNK)
    sb_t = lambda c: jnp.transpose(c[lyr], (0, 2, 3, 1)).reshape(db, SB_WIDTH, past)
    new_t = lambda a: jnp.pad(jnp.swapaxes(smp(a), 1, 2), ((0, 0), (0, 0), (0, LANES - dseq)))
    os_s = _sb_decode(smp(qs_s), sb_t(cache_sb_k), sb_t(cache_sb_v), new_t(ksb_s), new_t(vsb_s))

    y_p = _out(x_main, od_p.reshape(nb * seq, DIFF_WIDTH), os_p.reshape(nb * seq, SB_WIDTH),
               gmix, w_gate, wdo, wso, wo, gffn, w1, w2, tm=ROW_TILE)
    y_s = _out(x_sample.reshape(db * dseq, D_MODEL), od_s.reshape(db * dseq, DIFF_WIDTH),
               os_s.reshape(db * dseq, SB_WIDTH), gmix, w_gate, wdo, wso, wo, gffn, w1, w2, tm=db * dseq)

    def diff_prompt_cache(head_major, small):
        meta = small[:N_META].reshape(N_META * N_HEADS, 2 * HEAD_DIM)
        full = _fill_lead_tokens(head_major, meta, nb=nb)
        return full.reshape(1, nb, seq + N_META, N_HEADS, 2 * HEAD_DIM)

    def sb_prompt_cache(main_t, small):
        meta_t = jnp.broadcast_to(small[:N_META].T[None], (nb, SB_WIDTH, N_META))
        full = jnp.concatenate([meta_t, main_t], axis=2).reshape(nb, N_HEADS, HEAD_DIM, seq + N_META)
        return jnp.transpose(full, (0, 3, 1, 2))[None]

    def sample_cache(small, dim):
        return small[N_META:].reshape(1, db, dseq, N_HEADS, dim)

    return (y_p.reshape(nb, seq, D_MODEL), y_s.reshape(db, dseq, D_MODEL),
            diff_prompt_cache(kd_hm, kd_s), diff_prompt_cache(vd_hm, vd_s),
            sb_prompt_cache(kst_m, ks_s), sb_prompt_cache(vstf_m, vs_s),
            sample_cache(kd_s, 2 * HEAD_DIM), sample_cache(vd_s, 2 * HEAD_DIM),
            sample_cache(ks_s, HEAD_DIM), sample_cache(vs_s, HEAD_DIM))
```

```python
import functools
import math

import jax
import jax.numpy as jnp
from jax import lax
from jax.experimental import pallas as pl
from jax.experimental.pallas import tpu as pltpu

F32 = jnp.float32
BF16 = jnp.bfloat16

D_MODEL = 1024
N_META = 16
CHUNK = 64
N_HEADS = 8
HEAD_DIM = 64
DIFF_WIDTH = N_HEADS * 2 * HEAD_DIM
SB_WIDTH = N_HEADS * HEAD_DIM
QKV_COLS = 3 * DIFF_WIDTH + 3 * SB_WIDTH
D_FF = 4 * D_MODEL
ROT_DIM = HEAD_DIM // 4
ROPE_THETA = 500000.0
EPS = 1e-6
NEG = -1e30
SKIP_BELOW = -160.0
LOG2E = math.log2(math.e)
Q_SCALE = HEAD_DIM ** -0.5 * LOG2E
ONES_ROWS = 16
LAM_INIT = 0.8 - 0.6 * math.exp(-0.3 * 0)

LANES = 128
MXU_TILE = 256
COL_BLOCK = 512
VMEM_LIMIT = 56 * 1024 * 1024

ROW_TILE = 512
DIFF_Q_TILE, DIFF_KEY_BLOCK = 2048, 512
SB_Q_TILE, SB_KEY_BLOCK = 4096, MXU_TILE
DECODE_CHUNK = 2048

_NT = (((1,), (1,)), ((), ()))


def _rms(x):
    return x * lax.rsqrt(jnp.mean(x * x, axis=-1, keepdims=True) + EPS)


def _const_spec(shape):
    return pl.BlockSpec(shape, lambda *_: (0,) * len(shape), pipeline_mode=pl.Buffered(1))


def _proj_kernel(x_ref, gmix_ref, w_ref, qg_ref, kg_ref, gmat_ref, cos_ref, sa_ref, sb_ref,
                 qd_ref, kd_ref, kdb_ref, vd_ref, vdb_ref, qs_ref, ks_ref, ksb_ref, vs_ref, vsb_ref,
                 *, prompt_layout):
    tm = x_ref.shape[0]
    heads_per_block = COL_BLOCK // LANES

    def store_diff(ref, j, y):
        if not prompt_layout:
            ref[:, j * COL_BLOCK:(j + 1) * COL_BLOCK] = y
            return
        for hh in range(heads_per_block):
            ref[pl.ds(j * heads_per_block + hh, tm, stride=N_HEADS), :] = y[:, hh * LANES:(hh + 1) * LANES]

    h = (_rms(x_ref[...]) * gmix_ref[...]).astype(BF16)
    cos = cos_ref[...]
    sa = sa_ref[...]
    sb = sb_ref[...]

    def col(j):
        return jnp.dot(h, w_ref[:, j * COL_BLOCK:(j + 1) * COL_BLOCK], preferred_element_type=F32)

    def normed_rot(y, g):
        sq = (y * y).astype(BF16)
        msq = jnp.concatenate(
            [jnp.dot(sq[:, c * MXU_TILE:(c + 1) * MXU_TILE], gmat_ref[...], preferred_element_type=F32)
             for c in range(COL_BLOCK // MXU_TILE)], axis=1)
        yn = y * lax.rsqrt(msq + EPS) * g
        parts = []
        for c in range(COL_BLOCK // LANES):
            t = yn[:, c * LANES:(c + 1) * LANES]
            parts.append(t * cos + pltpu.roll(t, LANES - ROT_DIM // 2, 1) * sa
                         + pltpu.roll(t, ROT_DIM // 2, 1) * sb)
        return jnp.concatenate(parts, axis=1)

    for j in range(2):
        sl = slice(j * COL_BLOCK, (j + 1) * COL_BLOCK)
        q = normed_rot(col(j), qg_ref[...])
        qd_ref[:, sl] = (q * Q_SCALE).astype(BF16)
        k = normed_rot(col(2 + j), kg_ref[...])
        store_diff(kd_ref, j, k)
        kdb_ref[:, sl] = k.astype(BF16)
        v = col(4 + j)
        store_diff(vd_ref, j, v)
        if prompt_layout:
            vdb_ref[sl, :] = v.T.astype(BF16)
        else:
            vdb_ref[:, sl] = v.astype(BF16)
    qs_ref[...] = (col(6) * Q_SCALE).astype(BF16)
    k = col(7)
    ksb_ref[...] = k.astype(BF16)
    v = col(8)
    if prompt_layout:
        ks_ref[...] = k.T
        vt = v.T
        vs_ref[...] = vt
        vsb_ref[...] = vt.astype(BF16)
    else:
        ks_ref[...] = k
        vs_ref[...] = v
        vsb_ref[...] = v.astype(BF16)


def _proj(x, gmix, w_qkv, qg, kg, gmat, cos, sa, sb, *, tm, prompt_layout, lead=0):
    rows = x.shape[0]
    seq = cos.shape[0]
    n_pos_tiles = seq // tm
    row = lambda w: pl.BlockSpec((tm, w), lambda i: (i, 0))
    tab = pl.BlockSpec((tm, LANES), lambda i: (i % n_pos_tiles, 0))
    wide = lambda dt: jax.ShapeDtypeStruct((rows, DIFF_WIDTH), dt)
    narrow = lambda dt: jax.ShapeDtypeStruct((rows, SB_WIDTH), dt)
    if prompt_layout:
        nb = rows // seq
        t_spec = lambda w: pl.BlockSpec((None, w, tm), lambda i: (i // n_pos_tiles, 0, i % n_pos_tiles))
        t_shape = lambda w, dt: jax.ShapeDtypeStruct((nb, w, seq), dt)
        hm_spec = pl.BlockSpec(
            (pl.Element(tm * N_HEADS), pl.Element(LANES)),
            lambda i: (((i // n_pos_tiles) * (seq + lead) + lead + (i % n_pos_tiles) * tm) * N_HEADS, 0))
        hm_shape = jax.ShapeDtypeStruct((nb * (seq + lead) * N_HEADS, LANES), F32)
        out_specs = [row(DIFF_WIDTH), hm_spec, row(DIFF_WIDTH), hm_spec, t_spec(DIFF_WIDTH),
                     row(SB_WIDTH), t_spec(SB_WIDTH), row(SB_WIDTH), t_spec(SB_WIDTH), t_spec(SB_WIDTH)]
        out_shape = [wide(BF16), hm_shape, wide(BF16), hm_shape, t_shape(DIFF_WIDTH, BF16),
                     narrow(BF16), t_shape(SB_WIDTH, F32), narrow(BF16), t_shape(SB_WIDTH, F32),
                     t_shape(SB_WIDTH, BF16)]
    else:
        out_specs = [row(DIFF_WIDTH)] * 5 + [row(SB_WIDTH)] * 5
        out_shape = [wide(BF16), wide(F32), wide(BF16), wide(F32), wide(BF16),
                     narrow(BF16), narrow(F32), narrow(BF16), narrow(F32), narrow(BF16)]
    return pl.pallas_call(
        functools.partial(_proj_kernel, prompt_layout=prompt_layout),
        grid=(rows // tm,),
        in_specs=[row(D_MODEL), _const_spec((1, D_MODEL)), _const_spec((D_MODEL, QKV_COLS)),
                  _const_spec((1, COL_BLOCK)), _const_spec((1, COL_BLOCK)),
                  _const_spec((MXU_TILE, MXU_TILE)), tab, tab, tab],
        out_specs=out_specs,
        out_shape=out_shape,
        compiler_params=pltpu.CompilerParams(dimension_semantics=("arbitrary",),
                                             vmem_limit_bytes=VMEM_LIMIT),
        name="proj",
    )(x, gmix, w_qkv, qg, kg, gmat, cos, sa, sb)


def _fill_lead_kernel(lead_ref, big_ref, out_ref):
    del big_ref
    out_ref[...] = lead_ref[...]


def _fill_lead_tokens(big, lead_rows, *, nb):
    n = lead_rows.shape[0]
    per_batch = big.shape[0] // nb
    return pl.pallas_call(
        _fill_lead_kernel,
        grid=(nb,),
        in_specs=[_const_spec((n, LANES)), pl.BlockSpec(memory_space=pl.ANY)],
        out_specs=pl.BlockSpec((pl.Element(n), pl.Element(LANES)), lambda b: (b * per_batch, 0)),
        out_shape=jax.ShapeDtypeStruct(big.shape, big.dtype),
        input_output_aliases={1: 0},
        compiler_params=pltpu.CompilerParams(dimension_semantics=("arbitrary",)),
        name="fill_lead_tokens",
    )(lead_rows, big)


def _split_halves(q):
    lane = lax.broadcasted_iota(jnp.int32, q.shape, 1)
    zero = jnp.zeros_like(q)
    return jnp.where(lane < HEAD_DIM, q, zero), jnp.where(lane >= HEAD_DIM, q, zero)


def _diff_kernel(lq1_ref, lk1_ref, lq2_ref, lk2_ref, subg_ref, q_ref, kp_ref, vp_ref, km_ref, vm_ref,
                 o_ref, m_ref, acc_ref, s_ref, *, tq, tk, tkp, pref_valid, n_diag):
    i = pl.program_id(2)
    n_full = i * n_diag
    qs = _split_halves(q_ref[0])
    m_ref[...] = jnp.full(m_ref.shape, NEG, F32)
    acc_ref[...] = jnp.zeros(acc_ref.shape, F32)

    def step(nxt, cur):
        if cur is not None:
            slot, vtblk, mask, qlo = cur
            n = vtblk.shape[1]
            vt_ones = jnp.concatenate([vtblk, jnp.ones((ONES_ROWS, n), BF16)], axis=0)
        for a in range(2):
            if cur is not None:
                s = s_ref[slot, a, :n, qlo:]
                if mask is not None:
                    s = jnp.where(mask[:, qlo:], s, NEG)
                m_prev = m_ref[a, :, qlo:]
                m_new = jnp.maximum(m_prev, jnp.max(s, axis=0, keepdims=True))
                alpha = jnp.exp2(m_prev - m_new)
                p = jnp.exp2(s - m_new)
            if nxt is not None:
                kblk, nslot, nqlo = nxt
                s_ref[nslot, a, :kblk.shape[0], nqlo:] = lax.dot_general(
                    kblk, qs[a][nqlo:], _NT, preferred_element_type=F32)
            if cur is not None:
                acc_ref[a, :, qlo:] = (alpha * acc_ref[a, :, qlo:]
                                       + jnp.dot(vt_ones, p.astype(BF16), preferred_element_type=F32))
                m_ref[a, :, qlo:] = m_new

    def main_k(g):
        return km_ref[0, pl.ds(pl.multiple_of(g * tk, tk), tk), :]

    def main_vt(g):
        return vm_ref[0, :, pl.ds(pl.multiple_of(g * tk, tk), tk)]

    step((kp_ref[0], 1, 0), None)
    step((main_k(0), 0, 0), (1, vp_ref[0], lax.broadcasted_iota(jnp.int32, (tkp, tq), 0) < pref_valid, 0))

    def full_body(t, carry):
        step((main_k(2 * t + 1), 1, 0), (0, main_vt(2 * t), None, 0))
        step((main_k(2 * t + 2), 0, 0), (1, main_vt(2 * t + 1), None, 0))
        return carry
    lax.fori_loop(0, i * (n_diag // 2), full_body, 0)

    key_idx = lax.broadcasted_iota(jnp.int32, (tk, tq), 0)
    q_chunk = lax.broadcasted_iota(jnp.int32, (tk, tq), 1) // CHUNK
    for d in range(n_diag):
        g = n_full + d
        nxt = (main_k(g + 1), (d + 1) % 2, (d + 1) * tk) if d + 1 < n_diag else None
        step(nxt, (d % 2, main_vt(g), (key_idx + d * tk) // CHUNK <= q_chunk, d * tk))

    lam = (jnp.exp(jnp.sum(lq1_ref[...] * lk1_ref[...], axis=1, keepdims=True))
           - jnp.exp(jnp.sum(lq2_ref[...] * lk2_ref[...], axis=1, keepdims=True)) + LAM_INIT)
    o = (acc_ref[0, :LANES] / acc_ref[0, LANES:LANES + 1]
         - lam * (acc_ref[1, :LANES] / acc_ref[1, LANES:LANES + 1]))
    o = o * lax.rsqrt(jnp.mean(o * o, axis=0, keepdims=True) + EPS) * subg_ref[...] * (1.0 - LAM_INIT)
    o_ref[0] = o.T.astype(BF16)


def _diff_attn(lams, subg_col, q, kp, vtp, km, vtm, *, tq, tk, pref_valid):
    nb, tq_total, _ = q.shape
    tkp = kp.shape[1]
    t_main = km.shape[1]
    n_diag = tq // tk
    assert n_diag % 2 == 0, "main blocks are consumed in pairs"
    kernel = functools.partial(_diff_kernel, tq=tq, tk=tk, tkp=tkp, pref_valid=pref_valid, n_diag=n_diag)
    small = _const_spec((1, HEAD_DIM))
    return pl.pallas_call(
        kernel,
        grid=(nb, N_HEADS, tq_total // tq),
        in_specs=[small, small, small, small, _const_spec((LANES, 1)),
                  pl.BlockSpec((1, tq, LANES), lambda b, h, i: (b, i, h)),
                  pl.BlockSpec((1, tkp, LANES), lambda b, h, i: (0, 0, h)),
                  pl.BlockSpec((1, LANES, tkp), lambda b, h, i: (0, h, 0)),
                  pl.BlockSpec((1, t_main, LANES), lambda b, h, i: (b, 0, h)),
                  pl.BlockSpec((1, LANES, t_main), lambda b, h, i: (b, h, 0))],
        out_specs=pl.BlockSpec((1, tq, LANES), lambda b, h, i: (b, i, h)),
        out_shape=jax.ShapeDtypeStruct((nb, tq_total, DIFF_WIDTH), BF16),
        scratch_shapes=[pltpu.VMEM((2, 1, tq), F32), pltpu.VMEM((2, LANES + ONES_ROWS, tq), F32),
                        pltpu.VMEM((2, 2, max(tk, tkp), tq), F32)],
        compiler_params=pltpu.CompilerParams(dimension_semantics=("arbitrary",) * 3,
                                             vmem_limit_bytes=VMEM_LIMIT),
        name="diff_attn",
    )(*lams, subg_col, q, kp, vtp, km, vtm)


def _suffix_matrix(n):
    r = lax.broadcasted_iota(jnp.int32, (n + ONES_ROWS, n), 0)
    c = lax.broadcasted_iota(jnp.int32, (n + ONES_ROWS, n), 1)
    return jnp.where((c >= r) | (r == n), 1.0, 0.0).astype(BF16)


def _sb_kernel(q_ref, kp_ref, vp_ref, km_ref, vm_ref, o_ref, c_ref, acc_ref, z_ref,
               *, tq, tk, tkp, pref_valid, n_diag):
    i = pl.program_id(2)
    n_full = i * n_diag
    qs = _split_halves(q_ref[0])
    c_ref[...] = jnp.zeros(c_ref.shape, F32)
    acc_ref[...] = jnp.zeros(acc_ref.shape, F32)

    def step(nxt, cur):
        if nxt is not None:
            kblk, nslot, nqlo = nxt
            for a in range(2):
                z_ref[nslot, a, :kblk.shape[0], nqlo:] = lax.dot_general(
                    kblk, qs[a][nqlo:], _NT, preferred_element_type=F32)
        if cur is None:
            return
        slot, vtblk, mask, sfx_mat, qlo, qhi = cur
        n = vtblk.shape[1]
        if mask is not None:
            mask = mask[:, qlo:qhi]
        for a in range(2):
            rows = slice(a * HEAD_DIM, (a + 1) * HEAD_DIM)
            u = z_ref[slot, a, :n, qlo:qhi]
            neg_part = jnp.minimum(u, 0.0)
            d = neg_part - u
            log1m = d - jnp.log(1.0 + jnp.exp2(neg_part + d)) * LOG2E
            if mask is not None:
                log1m = jnp.where(mask, log1m, 0.0)
            sfx = jnp.dot(sfx_mat, log1m.astype(BF16), preferred_element_type=F32)
            t = u + sfx[:n]
            if mask is not None:
                t = jnp.where(mask, t, NEG)
            pv = jnp.dot(vtblk[rows, :], jnp.exp2(t).astype(BF16), preferred_element_type=F32)
            acc_ref[rows, qlo:qhi] += pv * jnp.exp2(c_ref[a, :, qlo:qhi])
            c_ref[a, :, qlo:qhi] += sfx[n:n + 1]

    def main_k(g):
        return km_ref[0, pl.ds(pl.multiple_of(g * tk, tk), tk), :]

    def main_vt(g):
        return vm_ref[0, :, pl.ds(pl.multiple_of(g * tk, tk), tk)]

    sfx_main = _suffix_matrix(tk)
    sfx_pref = sfx_main if tkp == tk else _suffix_matrix(tkp)
    key = lax.broadcasted_iota(jnp.int32, (tk, tq), 0)
    qry = lax.broadcasted_iota(jnp.int32, (tk, tq), 1)

    def live(lo=0):
        return jnp.max(c_ref[:, :, lo:]) > SKIP_BELOW

    def fold(nxt, slot, vtblk, mask, sfx_mat, qlo, window):
        near_hi = min(qlo + window, tq)
        step(nxt, (slot, vtblk, mask, sfx_mat, qlo, near_hi))
        if near_hi < tq:
            @pl.when(live(near_hi))
            def _():
                step(None, (slot, vtblk, None, sfx_mat, near_hi, tq))

    step((kp_ref[0], 2, 0), None)
    step((main_k(n_full + n_diag - 1), 0, (n_diag - 1) * tk), None)
    for d in reversed(range(n_diag)):
        s = (n_diag - 1 - d) % 2
        if d > 0:
            nxt = (main_k(n_full + d - 1), 1 - s, (d - 1) * tk)
        else:
            nxt = (main_k(jnp.maximum(n_full - 1, 0)), 1 - s, 0)
        fold(nxt, s, main_vt(n_full + d), key + d * tk < qry, sfx_main, d * tk, 2 * tk)

    def full_body(t):
        g = n_full - 1 - 2 * t
        fold((main_k(g - 1), 1, 0), 0, main_vt(g), None, sfx_main, 0, tk)
        fold((main_k(jnp.maximum(g - 2, 0)), 0, 0), 1, main_vt(g - 1), None, sfx_main, 0, tk)
        return t + 1
    n_pairs = i * (n_diag // 2)
    lax.while_loop(lambda t: jnp.logical_and(t < n_pairs, live()), full_body, jnp.int32(0))

    @pl.when(live())
    def _():
        pmask = lax.broadcasted_iota(jnp.int32, (tkp, tq), 0) < pref_valid
        step(None, (2, vp_ref[0], pmask, sfx_pref, 0, tq))
    o_ref[0] = acc_ref[...].T.astype(BF16)


def _sb_attn(q, kp, vtp, km, vtm, *, tq, tk, pref_valid):
    nb, tq_total, _ = q.shape
    tkp = kp.shape[1]
    t_main = km.shape[1]
    n_diag = tq // tk
    assert n_diag % 2 == 0, "main blocks are consumed in pairs"
    kernel = functools.partial(_sb_kernel, tq=tq, tk=tk, tkp=tkp, pref_valid=pref_valid, n_diag=n_diag)
    return pl.pallas_call(
        kernel,
        grid=(nb, SB_WIDTH // LANES, tq_total // tq),
        in_specs=[pl.BlockSpec((1, tq, LANES), lambda b, h, i: (b, i, h)),
                  pl.BlockSpec((1, tkp, LANES), lambda b, h, i: (0, 0, h)),
                  pl.BlockSpec((1, LANES, tkp), lambda b, h, i: (0, h, 0)),
                  pl.BlockSpec((1, t_main, LANES), lambda b, h, i: (b, 0, h)),
                  pl.BlockSpec((1, LANES, t_main), lambda b, h, i: (b, h, 0))],
        out_specs=pl.BlockSpec((1, tq, LANES), lambda b, h, i: (b, i, h)),
        out_shape=jax.ShapeDtypeStruct((nb, tq_total, SB_WIDTH), BF16),
        scratch_shapes=[pltpu.VMEM((2, 1, tq), F32), pltpu.VMEM((LANES, tq), F32),
                        pltpu.VMEM((3, 2, max(tk, tkp), tq), F32)],
        compiler_params=pltpu.CompilerParams(dimension_semantics=("arbitrary",) * 3,
                                             vmem_limit_bytes=VMEM_LIMIT),
        name="sb_attn",
    )(q, kp, vtp, km, vtm)


def _diff_decode_kernel(lq1_ref, lk1_ref, lq2_ref, lk2_ref, subg_ref, q_ref, ck_ref, cv_ref, kn_ref, vn_ref,
                        o_ref, m_ref, l_ref, acc_ref, *, chunk):
    kc = pl.program_id(1)
    nq = q_ref.shape[0]

    @pl.when(kc == 0)
    def _():
        m_ref[...] = jnp.full(m_ref.shape, NEG, F32)
        l_ref[...] = jnp.zeros(l_ref.shape, F32)
        acc_ref[...] = jnp.zeros(acc_ref.shape, F32)

    def update(h, k, v):
        n = k.shape[0]
        qs = _split_halves(q_ref[:, h * LANES:(h + 1) * LANES])
        v_ones = jnp.concatenate([v, jnp.ones((n, LANES), BF16)], axis=1)
        ps, alphas = [], []
        for a in range(2):
            s = lax.dot_general(qs[a], k, _NT, preferred_element_type=F32)
            m_prev = m_ref[2 * h + a]
            m_new = jnp.maximum(m_prev, jnp.max(s, axis=1, keepdims=True))
            alphas.append(jnp.exp2(m_prev - m_new))
            ps.append(jnp.exp2(s - m_new).astype(BF16))
            m_ref[2 * h + a] = m_new
        pv = jnp.dot(jnp.concatenate(ps, axis=0), v_ones, preferred_element_type=F32)
        for a in range(2):
            part = pv[a * nq:(a + 1) * nq]
            acc_ref[2 * h + a] = alphas[a] * acc_ref[2 * h + a] + part[:, :LANES]
            l_ref[2 * h + a] = alphas[a] * l_ref[2 * h + a] + part[:, LANES:]

    for h in range(N_HEADS):
        update(h, ck_ref[pl.ds(h, chunk, stride=N_HEADS), :].astype(BF16),
               cv_ref[pl.ds(h, chunk, stride=N_HEADS), :].astype(BF16))

    @pl.when(kc == pl.num_programs(1) - 1)
    def _():
        lam = (jnp.exp(jnp.sum(lq1_ref[...] * lk1_ref[...], axis=1, keepdims=True))
               - jnp.exp(jnp.sum(lq2_ref[...] * lk2_ref[...], axis=1, keepdims=True)) + LAM_INIT)
        for h in range(N_HEADS):
            cols = slice(h * LANES, (h + 1) * LANES)
            update(h, kn_ref[:, cols], vn_ref[:, cols])
            o = acc_ref[2 * h] / l_ref[2 * h] - lam * (acc_ref[2 * h + 1] / l_ref[2 * h + 1])
            o_ref[:, cols] = (_rms(o) * subg_ref[...] * (1.0 - LAM_INIT)).astype(BF16)


def _diff_decode(lams, subg_row, q, ck, cv, kn, vn, *, chunk):
    nb, nq, _ = q.shape
    n_chunks = ck.shape[1] // (chunk * N_HEADS)
    small = _const_spec((1, HEAD_DIM))
    tok = pl.BlockSpec((None, nq, DIFF_WIDTH), lambda b, c: (b, 0, 0))
    cache = pl.BlockSpec((None, chunk * N_HEADS, LANES), lambda b, c: (b, c, 0))
    return pl.pallas_call(
        functools.partial(_diff_decode_kernel, chunk=chunk),
        grid=(nb, n_chunks),
        in_specs=[small, small, small, small, _const_spec((1, LANES)), tok, cache, cache, tok, tok],
        out_specs=tok,
        out_shape=jax.ShapeDtypeStruct((nb, nq, DIFF_WIDTH), BF16),
        scratch_shapes=[pltpu.VMEM((2 * N_HEADS, nq, 1), F32), pltpu.VMEM((2 * N_HEADS, nq, LANES), F32),
                        pltpu.VMEM((2 * N_HEADS, nq, LANES), F32)],
        compiler_params=pltpu.CompilerParams(dimension_semantics=("arbitrary",) * 2,
                                             vmem_limit_bytes=VMEM_LIMIT),
        name="diff_decode",
    )(*lams, subg_row, q, ck, cv, kn, vn)


SB_BLOCK = SB_KEY_BLOCK


def _sb_decode_kernel(q_ref, ckt_ref, cvt_ref, knt_ref, vnt_ref, o_ref):
    nq = q_ref.shape[0]
    past = ckt_ref.shape[1]
    n_blk = past // SB_BLOCK
    def suffix_cols(n):
        r = lax.broadcasted_iota(jnp.int32, (n, n + LANES), 0)
        c = lax.broadcasted_iota(jnp.int32, (n, n + LANES), 1)
        return jnp.where((r > c) | (c >= n), 1.0, 0.0).astype(BF16)
    sfx_blk = suffix_cols(SB_BLOCK)
    sfx_new = suffix_cols(LANES)
    row = lax.broadcasted_iota(jnp.int32, (nq, LANES), 0)
    lane = lax.broadcasted_iota(jnp.int32, (nq, LANES), 1)
    new_mask = lane < row

    def log1m_of(u):
        nu = -u
        return jnp.minimum(nu, 0.0) - jnp.log(1.0 + jnp.exp2(jnp.minimum(u, nu))) * LOG2E

    for j in range(ckt_ref.shape[0] // LANES):
        rows = slice(j * LANES, (j + 1) * LANES)
        kt = ckt_ref[rows, :].astype(BF16)
        vt = cvt_ref[rows, :].astype(BF16)
        qs = _split_halves(q_ref[:, rows])
        ws, wns = [], []
        for a in range(2):
            un = jnp.dot(qs[a], knt_ref[rows, :], preferred_element_type=F32)
            ln = jnp.where(new_mask, log1m_of(un), 0.0)
            sn = jnp.dot(ln.astype(BF16), sfx_new, preferred_element_type=F32)
            wns.append(jnp.exp2(jnp.where(new_mask, un + ln + sn[:, :LANES], NEG)).astype(BF16))
            carry = sn[:, LANES:]
            u = jnp.dot(qs[a], kt, preferred_element_type=F32)
            l1m = log1m_of(u)
            stacked = jnp.concatenate(
                [l1m[:, b * SB_BLOCK:(b + 1) * SB_BLOCK] for b in range(n_blk)], axis=0).astype(BF16)
            sfx = jnp.dot(stacked, sfx_blk, preferred_element_type=F32)
            ts = [None] * n_blk
            for b in reversed(range(n_blk)):
                cols = slice(b * SB_BLOCK, (b + 1) * SB_BLOCK)
                part = sfx[b * nq:(b + 1) * nq]
                c2 = jnp.concatenate([carry] * (SB_BLOCK // LANES), axis=1)
                ts[b] = u[:, cols] + l1m[:, cols] + part[:, :SB_BLOCK] + c2
                carry = carry + part[:, SB_BLOCK:]
            ws.append(jnp.exp2(jnp.concatenate(ts, axis=1)).astype(BF16))
        o = (lax.dot_general(jnp.concatenate(ws, axis=0), vt, _NT, preferred_element_type=F32)
             + lax.dot_general(jnp.concatenate(wns, axis=0), vnt_ref[rows, :], _NT,
                               preferred_element_type=F32))
        o_ref[:, rows] = jnp.where(lane < HEAD_DIM, o[:nq], o[nq:]).astype(BF16)


def _sb_decode(q, ckt, cvt, knt, vnt):
    nb, nq, _ = q.shape
    assert nq <= LANES
    past = ckt.shape[2]
    width = 2 * LANES
    tok = pl.BlockSpec((None, nq, width), lambda b, g: (b, 0, g))
    cache = pl.BlockSpec((None, width, past), lambda b, g: (b, g, 0))
    new = pl.BlockSpec((None, width, LANES), lambda b, g: (b, g, 0))
    return pl.pallas_call(
        _sb_decode_kernel,
        grid=(nb, SB_WIDTH // width),
        in_specs=[tok, cache, cache, new, new],
        out_specs=tok,
        out_shape=jax.ShapeDtypeStruct((nb, nq, SB_WIDTH), BF16),
        compiler_params=pltpu.CompilerParams(dimension_semantics=("arbitrary",) * 2,
                                             vmem_limit_bytes=VMEM_LIMIT),
        name="sb_decode",
    )(q, ckt, cvt, knt, vnt)


def _out_kernel(x_ref, od_ref, os_ref, gmix_ref, wg_ref, wdo_ref, wso_ref, wo_ref, gffn_ref,
                w1_ref, w2_ref, y_ref):
    x = x_ref[...]
    h = (_rms(x) * gmix_ref[...]).astype(BF16)
    gate = jax.nn.sigmoid(jnp.dot(h, wg_ref[...], preferred_element_type=F32))
    a = jnp.dot(od_ref[...], wdo_ref[...], preferred_element_type=F32)
    b = jnp.dot(os_ref[...], wso_ref[...], preferred_element_type=F32)
    merged = (gate[:, :D_MODEL] * a + gate[:, D_MODEL:] * b).astype(BF16)
    x1 = x + jnp.dot(merged, wo_ref[...], preferred_element_type=F32)
    h2 = (_rms(x1) * gffn_ref[...]).astype(BF16)
    y = x1
    for c in range(D_FF // D_MODEL):
        sl = slice(c * D_MODEL, (c + 1) * D_MODEL)
        f = jnp.maximum(jnp.dot(h2, w1_ref[:, sl], preferred_element_type=F32), 0.0)
        y = y + jnp.dot((f * f).astype(BF16), w2_ref[sl, :], preferred_element_type=F32)
    y_ref[...] = y


def _out(x, od, osb, gmix, wg, wdo, wso, wo, gffn, w1, w2, *, tm):
    rows = x.shape[0]
    row = lambda w: pl.BlockSpec((tm, w), lambda i: (i, 0))
    return pl.pallas_call(
        _out_kernel,
        grid=(rows // tm,),
        in_specs=[row(D_MODEL), row(DIFF_WIDTH), row(SB_WIDTH), _const_spec((1, D_MODEL)),
                  _const_spec((D_MODEL, 2 * D_MODEL)), _const_spec((DIFF_WIDTH, D_MODEL)),
                  _const_spec((SB_WIDTH, D_MODEL)), _const_spec((D_MODEL, D_MODEL)),
                  _const_spec((1, D_MODEL)), _const_spec((D_MODEL, D_FF)), _const_spec((D_FF, D_MODEL))],
        out_specs=row(D_MODEL),
        out_shape=jax.ShapeDtypeStruct((rows, D_MODEL), F32),
        compiler_params=pltpu.CompilerParams(dimension_semantics=("arbitrary",),
                                             vmem_limit_bytes=VMEM_LIMIT),
        name="out",
    )(x, od, osb, gmix, wg, wdo, wso, wo, gffn, w1, w2)


def _rope_tables(pos):
    half = ROT_DIM // 2
    d = jnp.arange(LANES, dtype=jnp.int32) % HEAD_DIM
    inv = ROPE_THETA ** (-(2 * (d % half)).astype(F32) / ROT_DIM)
    ang = pos.astype(F32)[:, None] * inv[None, :]
    cos, sin = jnp.cos(ang), jnp.sin(ang)
    cos_t = jnp.where(d < ROT_DIM, cos, 1.0)
    sa_t = jnp.where(d < half, -sin, 0.0)
    sb_t = jnp.where((d >= half) & (d < ROT_DIM), sin, 0.0)
    return cos_t, sa_t, sb_t


def kernel(x_prompt, x_sample, cache_diff_k, cache_diff_v, cache_sb_k, cache_sb_v, meta_tokens,
           g_mix, w_in, q_norm_g, k_norm_g, lam_q1, lam_k1, lam_q2, lam_k2, sub_g,
           w_diff_out, w_sb_out, w_out, g_ffn, w_ff1, w_ff2):
    nb, seq, _ = x_prompt.shape
    db, dseq, _ = x_sample.shape
    past = cache_diff_k.shape[2]
    lyr = 0

    w_qkv, w_gate = w_in[lyr, :, :QKV_COLS].astype(BF16), w_in[lyr, :, QKV_COLS:].astype(BF16)
    wdo, wso, wo = (w_diff_out[lyr].astype(BF16), w_sb_out[lyr].astype(BF16), w_out[lyr].astype(BF16))
    w1, w2 = w_ff1[lyr].astype(BF16), w_ff2[lyr].astype(BF16)
    gmix = g_mix[lyr].reshape(1, D_MODEL)
    gffn = g_ffn[lyr].reshape(1, D_MODEL)
    qg = jnp.tile(q_norm_g[lyr], COL_BLOCK // HEAD_DIM).reshape(1, COL_BLOCK)
    kg = jnp.tile(k_norm_g[lyr], COL_BLOCK // HEAD_DIM).reshape(1, COL_BLOCK)
    subg_col = sub_g[lyr].reshape(LANES, 1)
    lams = [t[lyr].reshape(1, HEAD_DIM) for t in (lam_q1, lam_k1, lam_q2, lam_k2)]
    grp = jnp.arange(MXU_TILE, dtype=jnp.int32) // HEAD_DIM
    gmat = jnp.where(grp[:, None] == grp[None, :], 1.0 / HEAD_DIM, 0.0).astype(BF16)

    main_pos = N_META + jnp.arange(seq, dtype=jnp.int32)
    small_pos = jnp.concatenate([jnp.arange(N_META, dtype=jnp.int32),
                                 jnp.tile(past + jnp.arange(dseq, dtype=jnp.int32), db)])
    x_main = x_prompt.reshape(nb * seq, D_MODEL)
    x_small = jnp.concatenate([meta_tokens.astype(F32), x_sample.reshape(db * dseq, D_MODEL)], axis=0)
    pm = _proj(x_main, gmix, w_qkv, qg, kg, gmat, *_rope_tables(main_pos), tm=ROW_TILE, prompt_layout=True,
               lead=N_META)
    ps = _proj(x_small, gmix, w_qkv, qg, kg, gmat, *_rope_tables(small_pos), tm=x_small.shape[0],
               prompt_layout=False)
    qd_m, kd_hm, kdb_m, vd_hm, vdt_m, qs_m, kst_m, ksb_m, vstf_m, vst_m = pm
    qd_s, kd_s, kdb_s, vd_s, vdb_s, qs_s, ks_s, ksb_s, vs_s, vsb_s = ps

    def bt(a, n, t):
        return a.reshape(n, t, a.shape[-1])

    def meta_keys(a):
        return jnp.pad(a[:N_META], ((0, LANES - N_META), (0, 0)))[None]

    def meta_vals_t(a):
        return jnp.swapaxes(meta_keys(a), 1, 2)

    od_p = _diff_attn(lams, subg_col, bt(qd_m, nb, seq), meta_keys(kdb_s), meta_vals_t(vdb_s),
                      bt(kdb_m, nb, seq), vdt_m,
                      tq=DIFF_Q_TILE, tk=DIFF_KEY_BLOCK, pref_valid=N_META)
    os_p = _sb_attn(bt(qs_m, nb, seq), meta_keys(ksb_s), meta_vals_t(vsb_s),
                    bt(ksb_m, nb, seq), vst_m,
                    tq=SB_Q_TILE, tk=SB_KEY_BLOCK, pref_valid=N_META)

    smp = lambda a: bt(a[N_META:], db, dseq)
    cdk = cache_diff_k[lyr].reshape(db, past * N_HEADS, LANES)
    cdv = cache_diff_v[lyr].reshape(db, past * N_HEADS, LANES)
    od_s = _diff_decode(lams, sub_g[lyr].reshape(1, LANES), smp(qd_s), cdk, cdv, smp(kdb_s), smp(vdb_s),
                        chunk=DECODE_CHUNK)
    sb_t = lambda c: jnp.transpose(c[lyr], (0, 2, 3, 1)).reshape(db, SB_WIDTH, past)
    new_t = lambda a: jnp.pad(jnp.swapaxes(smp(a), 1, 2), ((0, 0), (0, 0), (0, LANES - dseq)))
    os_s = _sb_decode(smp(qs_s), sb_t(cache_sb_k), sb_t(cache_sb_v), new_t(ksb_s), new_t(vsb_s))

    y_p = _out(x_main, od_p.reshape(nb * seq, DIFF_WIDTH), os_p.reshape(nb * seq, SB_WIDTH),
               gmix, w_gate, wdo, wso, wo, gffn, w1, w2, tm=ROW_TILE)
    y_s = _out(x_sample.reshape(db * dseq, D_MODEL), od_s.reshape(db * dseq, DIFF_WIDTH),
               os_s.reshape(db * dseq, SB_WIDTH), gmix, w_gate, wdo, wso, wo, gffn, w1, w2, tm=db * dseq)

    def diff_prompt_cache(head_major, small):
        meta = small[:N_META].reshape(N_META * N_HEADS, 2 * HEAD_DIM)
        full = _fill_lead_tokens(head_major, meta, nb=nb)
        return full.reshape(1, nb, seq + N_META, N_HEADS, 2 * HEAD_DIM)

    def sb_prompt_cache(main_t, small):
        meta_t = jnp.broadcast_to(small[:N_META].T[None], (nb, SB_WIDTH, N_META))
        full = jnp.concatenate([meta_t, main_t], axis=2).reshape(nb, N_HEADS, HEAD_DIM, seq + N_META)
        return jnp.transpose(full, (0, 3, 1, 2))[None]

    def sample_cache(small, dim):
        return small[N_META:].reshape(1, db, dseq, N_HEADS, dim)

    return (y_p.reshape(nb, seq, D_MODEL), y_s.reshape(db, dseq, D_MODEL),
            diff_prompt_cache(kd_hm, kd_s), diff_prompt_cache(vd_hm, vd_s),
            sb_prompt_cache(kst_m, ks_s), sb_prompt_cache(vstf_m, vs_s),
            sample_cache(kd_s, 2 * HEAD_DIM), sample_cache(vd_s, 2 * HEAD_DIM),
            sample_cache(ks_s, HEAD_DIM), sample_cache(vs_s, HEAD_DIM))
```

```python
import functools
import math

import jax
import jax.numpy as jnp
from jax import lax
from jax.experimental import pallas as pl
from jax.experimental.pallas import tpu as pltpu

F32 = jnp.float32
BF16 = jnp.bfloat16

D_MODEL = 1024
N_META = 16
CHUNK = 64
N_HEADS = 8
HEAD_DIM = 64
DIFF_WIDTH = N_HEADS * 2 * HEAD_DIM
SB_WIDTH = N_HEADS * HEAD_DIM
QKV_COLS = 3 * DIFF_WIDTH + 3 * SB_WIDTH
D_FF = 4 * D_MODEL
ROT_DIM = HEAD_DIM // 4
ROPE_THETA = 500000.0
EPS = 1e-6
NEG = -1e30
SKIP_BELOW = -160.0
LOG2E = math.log2(math.e)
Q_SCALE = HEAD_DIM ** -0.5 * LOG2E
ONES_ROWS = 16
LAM_INIT = 0.8 - 0.6 * math.exp(-0.3 * 0)

LANES = 128
MXU_TILE = 256
COL_BLOCK = 512
VMEM_LIMIT = 56 * 1024 * 1024

ROW_TILE = 512
DIFF_Q_TILE, DIFF_KEY_BLOCK = 2048, 512
SB_Q_TILE, SB_KEY_BLOCK = 2048, MXU_TILE
DECODE_CHUNK = 2048

_NT = (((1,), (1,)), ((), ()))


def _rms(x):
    return x * lax.rsqrt(jnp.mean(x * x, axis=-1, keepdims=True) + EPS)


def _const_spec(shape):
    return pl.BlockSpec(shape, lambda *_: (0,) * len(shape), pipeline_mode=pl.Buffered(1))


def _proj_kernel(x_ref, gmix_ref, w_ref, qg_ref, kg_ref, gmat_ref, cos_ref, sa_ref, sb_ref,
                 qd_ref, kd_ref, kdb_ref, vd_ref, vdb_ref, qs_ref, ks_ref, ksb_ref, vs_ref, vsb_ref,
                 *, prompt_layout):
    tm = x_ref.shape[0]
    heads_per_block = COL_BLOCK // LANES

    def store_diff(ref, j, y):
        if not prompt_layout:
            ref[:, j * COL_BLOCK:(j + 1) * COL_BLOCK] = y
            return
        for hh in range(heads_per_block):
            ref[pl.ds(j * heads_per_block + hh, tm, stride=N_HEADS), :] = y[:, hh * LANES:(hh + 1) * LANES]

    h = (_rms(x_ref[...]) * gmix_ref[...]).astype(BF16)
    cos = cos_ref[...]
    sa = sa_ref[...]
    sb = sb_ref[...]

    def col(j):
        return jnp.dot(h, w_ref[:, j * COL_BLOCK:(j + 1) * COL_BLOCK], preferred_element_type=F32)

    def normed_rot(y, g):
        sq = (y * y).astype(BF16)
        msq = jnp.concatenate(
            [jnp.dot(sq[:, c * MXU_TILE:(c + 1) * MXU_TILE], gmat_ref[...], preferred_element_type=F32)
             for c in range(COL_BLOCK // MXU_TILE)], axis=1)
        yn = y * lax.rsqrt(msq + EPS) * g
        parts = []
        for c in range(COL_BLOCK // LANES):
            t = yn[:, c * LANES:(c + 1) * LANES]
            parts.append(t * cos + pltpu.roll(t, LANES - ROT_DIM // 2, 1) * sa
                         + pltpu.roll(t, ROT_DIM // 2, 1) * sb)
        return jnp.concatenate(parts, axis=1)

    for j in range(2):
        sl = slice(j * COL_BLOCK, (j + 1) * COL_BLOCK)
        q = normed_rot(col(j), qg_ref[...])
        qd_ref[:, sl] = (q * Q_SCALE).astype(BF16)
        k = normed_rot(col(2 + j), kg_ref[...])
        store_diff(kd_ref, j, k)
        kdb_ref[:, sl] = k.astype(BF16)
        v = col(4 + j)
        store_diff(vd_ref, j, v)
        if prompt_layout:
            vdb_ref[sl, :] = v.T.astype(BF16)
        else:
            vdb_ref[:, sl] = v.astype(BF16)
    qs_ref[...] = (col(6) * Q_SCALE).astype(BF16)
    k = col(7)
    ksb_ref[...] = k.astype(BF16)
    v = col(8)
    if prompt_layout:
        ks_ref[...] = k.T
        vt = v.T
        vs_ref[...] = vt
        vsb_ref[...] = vt.astype(BF16)
    else:
        ks_ref[...] = k
        vs_ref[...] = v
        vsb_ref[...] = v.astype(BF16)


def _proj(x, gmix, w_qkv, qg, kg, gmat, cos, sa, sb, *, tm, prompt_layout, lead=0):
    rows = x.shape[0]
    seq = cos.shape[0]
    n_pos_tiles = seq // tm
    row = lambda w: pl.BlockSpec((tm, w), lambda i: (i, 0))
    tab = pl.BlockSpec((tm, LANES), lambda i: (i % n_pos_tiles, 0))
    wide = lambda dt: jax.ShapeDtypeStruct((rows, DIFF_WIDTH), dt)
    narrow = lambda dt: jax.ShapeDtypeStruct((rows, SB_WIDTH), dt)
    if prompt_layout:
        nb = rows // seq
        t_spec = lambda w: pl.BlockSpec((None, w, tm), lambda i: (i // n_pos_tiles, 0, i % n_pos_tiles))
        t_shape = lambda w, dt: jax.ShapeDtypeStruct((nb, w, seq), dt)
        hm_spec = pl.BlockSpec(
            (pl.Element(tm * N_HEADS), pl.Element(LANES)),
            lambda i: (((i // n_pos_tiles) * (seq + lead) + lead + (i % n_pos_tiles) * tm) * N_HEADS, 0))
        hm_shape = jax.ShapeDtypeStruct((nb * (seq + lead) * N_HEADS, LANES), F32)
        out_specs = [row(DIFF_WIDTH), hm_spec, row(DIFF_WIDTH), hm_spec, t_spec(DIFF_WIDTH),
                     row(SB_WIDTH), t_spec(SB_WIDTH), row(SB_WIDTH), t_spec(SB_WIDTH), t_spec(SB_WIDTH)]
        out_shape = [wide(BF16), hm_shape, wide(BF16), hm_shape, t_shape(DIFF_WIDTH, BF16),
                     narrow(BF16), t_shape(SB_WIDTH, F32), narrow(BF16), t_shape(SB_WIDTH, F32),
                     t_shape(SB_WIDTH, BF16)]
    else:
        out_specs = [row(DIFF_WIDTH)] * 5 + [row(SB_WIDTH)] * 5
        out_shape = [wide(BF16), wide(F32), wide(BF16), wide(F32), wide(BF16),
                     narrow(BF16), narrow(F32), narrow(BF16), narrow(F32), narrow(BF16)]
    return pl.pallas_call(
        functools.partial(_proj_kernel, prompt_layout=prompt_layout),
        grid=(rows // tm,),
        in_specs=[row(D_MODEL), _const_spec((1, D_MODEL)), _const_spec((D_MODEL, QKV_COLS)),
                  _const_spec((1, COL_BLOCK)), _const_spec((1, COL_BLOCK)),
                  _const_spec((MXU_TILE, MXU_TILE)), tab, tab, tab],
        out_specs=out_specs,
        out_shape=out_shape,
        compiler_params=pltpu.CompilerParams(dimension_semantics=("arbitrary",),
                                             vmem_limit_bytes=VMEM_LIMIT),
        name="proj",
    )(x, gmix, w_qkv, qg, kg, gmat, cos, sa, sb)


def _fill_lead_kernel(lead_ref, big_ref, out_ref):
    del big_ref
    out_ref[...] = lead_ref[...]


def _fill_lead_tokens(big, lead_rows, *, nb):
    n = lead_rows.shape[0]
    per_batch = big.shape[0] // nb
    return pl.pallas_call(
        _fill_lead_kernel,
        grid=(nb,),
        in_specs=[_const_spec((n, LANES)), pl.BlockSpec(memory_space=pl.ANY)],
        out_specs=pl.BlockSpec((pl.Element(n), pl.Element(LANES)), lambda b: (b * per_batch, 0)),
        out_shape=jax.ShapeDtypeStruct(big.shape, big.dtype),
        input_output_aliases={1: 0},
        compiler_params=pltpu.CompilerParams(dimension_semantics=("arbitrary",)),
        name="fill_lead_tokens",
    )(lead_rows, big)


def _split_halves(q):
    lane = lax.broadcasted_iota(jnp.int32, q.shape, 1)
    zero = jnp.zeros_like(q)
    return jnp.where(lane < HEAD_DIM, q, zero), jnp.where(lane >= HEAD_DIM, q, zero)


def _diff_kernel(lq1_ref, lk1_ref, lq2_ref, lk2_ref, subg_ref, q_ref, kp_ref, vp_ref, km_ref, vm_ref,
                 o_ref, m_ref, acc_ref, s_ref, *, tq, tk, tkp, pref_valid, n_diag):
    i = pl.program_id(2)
    n_full = i * n_diag
    qs = _split_halves(q_ref[0])
    m_ref[...] = jnp.full(m_ref.shape, NEG, F32)
    acc_ref[...] = jnp.zeros(acc_ref.shape, F32)

    def step(nxt, cur):
        if cur is not None:
            slot, vtblk, mask, qlo = cur
            n = vtblk.shape[1]
            vt_ones = jnp.concatenate([vtblk, jnp.ones((ONES_ROWS, n), BF16)], axis=0)
        for a in range(2):
            if cur is not None:
                s = s_ref[slot, a, :n, qlo:]
                if mask is not None:
                    s = jnp.where(mask[:, qlo:], s, NEG)
                m_prev = m_ref[a, :, qlo:]
                m_new = jnp.maximum(m_prev, jnp.max(s, axis=0, keepdims=True))
                alpha = jnp.exp2(m_prev - m_new)
                p = jnp.exp2(s - m_new)
            if nxt is not None:
                kblk, nslot, nqlo = nxt
                s_ref[nslot, a, :kblk.shape[0], nqlo:] = lax.dot_general(
                    kblk, qs[a][nqlo:], _NT, preferred_element_type=F32)
            if cur is not None:
                acc_ref[a, :, qlo:] = (alpha * acc_ref[a, :, qlo:]
                                       + jnp.dot(vt_ones, p.astype(BF16), preferred_element_type=F32))
                m_ref[a, :, qlo:] = m_new

    def main_k(g):
        return km_ref[0, pl.ds(pl.multiple_of(g * tk, tk), tk), :]

    def main_vt(g):
        return vm_ref[0, :, pl.ds(pl.multiple_of(g * tk, tk), tk)]

    step((kp_ref[0], 1, 0), None)
    step((main_k(0), 0, 0), (1, vp_ref[0], lax.broadcasted_iota(jnp.int32, (tkp, tq), 0) < pref_valid, 0))

    def full_body(t, carry):
        step((main_k(2 * t + 1), 1, 0), (0, main_vt(2 * t), None, 0))
        step((main_k(2 * t + 2), 0, 0), (1, main_vt(2 * t + 1), None, 0))
        return carry
    lax.fori_loop(0, i * (n_diag // 2), full_body, 0)

    key_idx = lax.broadcasted_iota(jnp.int32, (tk, tq), 0)
    q_chunk = lax.broadcasted_iota(jnp.int32, (tk, tq), 1) // CHUNK
    for d in range(n_diag):
        g = n_full + d
        nxt = (main_k(g + 1), (d + 1) % 2, (d + 1) * tk) if d + 1 < n_diag else None
        step(nxt, (d % 2, main_vt(g), (key_idx + d * tk) // CHUNK <= q_chunk, d * tk))

    lam = (jnp.exp(jnp.sum(lq1_ref[...] * lk1_ref[...], axis=1, keepdims=True))
           - jnp.exp(jnp.sum(lq2_ref[...] * lk2_ref[...], axis=1, keepdims=True)) + LAM_INIT)
    o = (acc_ref[0, :LANES] / acc_ref[0, LANES:LANES + 1]
         - lam * (acc_ref[1, :LANES] / acc_ref[1, LANES:LANES + 1]))
    o = o * lax.rsqrt(jnp.mean(o * o, axis=0, keepdims=True) + EPS) * subg_ref[...] * (1.0 - LAM_INIT)
    o_ref[0] = o.T.astype(BF16)


def _diff_attn(lams, subg_col, q, kp, vtp, km, vtm, *, tq, tk, pref_valid):
    nb, tq_total, _ = q.shape
    tkp = kp.shape[1]
    t_main = km.shape[1]
    n_diag = tq // tk
    assert n_diag % 2 == 0, "main blocks are consumed in pairs"
    kernel = functools.partial(_diff_kernel, tq=tq, tk=tk, tkp=tkp, pref_valid=pref_valid, n_diag=n_diag)
    small = _const_spec((1, HEAD_DIM))
    return pl.pallas_call(
        kernel,
        grid=(nb, N_HEADS, tq_total // tq),
        in_specs=[small, small, small, small, _const_spec((LANES, 1)),
                  pl.BlockSpec((1, tq, LANES), lambda b, h, i: (b, i, h)),
                  pl.BlockSpec((1, tkp, LANES), lambda b, h, i: (0, 0, h)),
                  pl.BlockSpec((1, LANES, tkp), lambda b, h, i: (0, h, 0)),
                  pl.BlockSpec((1, t_main, LANES), lambda b, h, i: (b, 0, h)),
                  pl.BlockSpec((1, LANES, t_main), lambda b, h, i: (b, h, 0))],
        out_specs=pl.BlockSpec((1, tq, LANES), lambda b, h, i: (b, i, h)),
        out_shape=jax.ShapeDtypeStruct((nb, tq_total, DIFF_WIDTH), BF16),
        scratch_shapes=[pltpu.VMEM((2, 1, tq), F32), pltpu.VMEM((2, LANES + ONES_ROWS, tq), F32),
                        pltpu.VMEM((2, 2, max(tk, tkp), tq), F32)],
        compiler_params=pltpu.CompilerParams(dimension_semantics=("arbitrary",) * 3,
                                             vmem_limit_bytes=VMEM_LIMIT),
        name="diff_attn",
    )(*lams, subg_col, q, kp, vtp, km, vtm)


def _suffix_matrix(n):
    r = lax.broadcasted_iota(jnp.int32, (n + ONES_ROWS, n), 0)
    c = lax.broadcasted_iota(jnp.int32, (n + ONES_ROWS, n), 1)
    return jnp.where((c >= r) | (r == n), 1.0, 0.0).astype(BF16)


def _sb_kernel(q_ref, kp_ref, vp_ref, km_ref, vm_ref, o_ref, c_ref, acc_ref, z_ref,
               *, tq, tk, tkp, pref_valid, n_diag):
    i = pl.program_id(2)
    n_full = i * n_diag
    qs = _split_halves(q_ref[0])
    c_ref[...] = jnp.zeros(c_ref.shape, F32)
    acc_ref[...] = jnp.zeros(acc_ref.shape, F32)

    def step(nxt, cur):
        if nxt is not None:
            kblk, nslot, nqlo, nqhi = nxt
            for a in range(2):
                z_ref[nslot, a, :kblk.shape[0], nqlo:nqhi] = lax.dot_general(
                    kblk, qs[a][nqlo:nqhi], _NT, preferred_element_type=F32)
        if cur is None:
            return
        slot, vtblk, mask, sfx_mat, qlo, qhi = cur
        n = vtblk.shape[1]
        if mask is not None:
            mask = mask[:, qlo:qhi]
        for a in range(2):
            rows = slice(a * HEAD_DIM, (a + 1) * HEAD_DIM)
            u = z_ref[slot, a, :n, qlo:qhi]
            neg_part = jnp.minimum(u, 0.0)
            d = neg_part - u
            log1m = d - jnp.log(1.0 + jnp.exp2(neg_part + d)) * LOG2E
            if mask is not None:
                log1m = jnp.where(mask, log1m, 0.0)
            sfx = jnp.dot(sfx_mat, log1m.astype(BF16), preferred_element_type=F32)
            t = u + sfx[:n]
            if mask is not None:
                t = jnp.where(mask, t, NEG)
            pv = jnp.dot(vtblk[rows, :], jnp.exp2(t).astype(BF16), preferred_element_type=F32)
            acc_ref[rows, qlo:qhi] += pv * jnp.exp2(c_ref[a, :, qlo:qhi])
            c_ref[a, :, qlo:qhi] += sfx[n:n + 1]

    def main_k(g):
        return km_ref[0, pl.ds(pl.multiple_of(g * tk, tk), tk), :]

    def main_vt(g):
        return vm_ref[0, :, pl.ds(pl.multiple_of(g * tk, tk), tk)]

    sfx_main = _suffix_matrix(tk)
    sfx_pref = sfx_main if tkp == tk else _suffix_matrix(tkp)
    key = lax.broadcasted_iota(jnp.int32, (tk, tq), 0)
    qry = lax.broadcasted_iota(jnp.int32, (tk, tq), 1)

    def live(lo=0):
        return jnp.max(c_ref[:, :, lo:]) > SKIP_BELOW

    def fold(nxt, slot, kblk, vtblk, mask, sfx_mat, qlo, window):
        near_hi = min(qlo + window, tq)
        step(nxt, (slot, vtblk, mask, sfx_mat, qlo, near_hi))
        if near_hi < tq:
            @pl.when(live(near_hi))
            def _():
                step((kblk, slot, near_hi, tq), None)
                step(None, (slot, vtblk, None, sfx_mat, near_hi, tq))

    def ahead(g, slot, qlo, window):
        return main_k(g), slot, qlo, min(qlo + window, tq)

    step((kp_ref[0], 2, 0, tq), None)
    step(ahead(n_full + n_diag - 1, 0, (n_diag - 1) * tk, 2 * tk), None)
    for d in reversed(range(n_diag)):
        s = (n_diag - 1 - d) % 2
        if d > 0:
            nxt = ahead(n_full + d - 1, 1 - s, (d - 1) * tk, 2 * tk)
        else:
            nxt = ahead(jnp.maximum(n_full - 1, 0), 1 - s, 0, tk)
        fold(nxt, s, main_k(n_full + d), main_vt(n_full + d), key + d * tk < qry, sfx_main, d * tk, 2 * tk)

    def full_body(t):
        g = n_full - 1 - 2 * t
        fold(ahead(g - 1, 1, 0, tk), 0, main_k(g), main_vt(g), None, sfx_main, 0, tk)
        fold(ahead(jnp.maximum(g - 2, 0), 0, 0, tk), 1, main_k(g - 1), main_vt(g - 1), None, sfx_main, 0, tk)
        return t + 1
    n_pairs = i * (n_diag // 2)
    lax.while_loop(lambda t: jnp.logical_and(t < n_pairs, live()), full_body, jnp.int32(0))

    @pl.when(live())
    def _():
        pmask = lax.broadcasted_iota(jnp.int32, (tkp, tq), 0) < pref_valid
        step(None, (2, vp_ref[0], pmask, sfx_pref, 0, tq))
    o_ref[0] = acc_ref[...].T.astype(BF16)


def _sb_attn(q, kp, vtp, km, vtm, *, tq, tk, pref_valid):
    nb, tq_total, _ = q.shape
    tkp = kp.shape[1]
    t_main = km.shape[1]
    n_diag = tq // tk
    assert n_diag % 2 == 0, "main blocks are consumed in pairs"
    kernel = functools.partial(_sb_kernel, tq=tq, tk=tk, tkp=tkp, pref_valid=pref_valid, n_diag=n_diag)
    return pl.pallas_call(
        kernel,
        grid=(nb, SB_WIDTH // LANES, tq_total // tq),
        in_specs=[pl.BlockSpec((1, tq, LANES), lambda b, h, i: (b, i, h)),
                  pl.BlockSpec((1, tkp, LANES), lambda b, h, i: (0, 0, h)),
                  pl.BlockSpec((1, LANES, tkp), lambda b, h, i: (0, h, 0)),
                  pl.BlockSpec((1, t_main, LANES), lambda b, h, i: (b, 0, h)),
                  pl.BlockSpec((1, LANES, t_main), lambda b, h, i: (b, h, 0))],
        out_specs=pl.BlockSpec((1, tq, LANES), lambda b, h, i: (b, i, h)),
        out_shape=jax.ShapeDtypeStruct((nb, tq_total, SB_WIDTH), BF16),
        scratch_shapes=[pltpu.VMEM((2, 1, tq), F32), pltpu.VMEM((LANES, tq), F32),
                        pltpu.VMEM((3, 2, max(tk, tkp), tq), F32)],
        compiler_params=pltpu.CompilerParams(dimension_semantics=("arbitrary",) * 3,
                                             vmem_limit_bytes=VMEM_LIMIT),
        name="sb_attn",
    )(q, kp, vtp, km, vtm)


def _diff_decode_kernel(lq1_ref, lk1_ref, lq2_ref, lk2_ref, subg_ref, q_ref, ck_ref, cv_ref, kn_ref, vn_ref,
                        o_ref, m_ref, l_ref, acc_ref, *, chunk):
    kc = pl.program_id(1)
    nq = q_ref.shape[0]

    @pl.when(kc == 0)
    def _():
        m_ref[...] = jnp.full(m_ref.shape, NEG, F32)
        l_ref[...] = jnp.zeros(l_ref.shape, F32)
        acc_ref[...] = jnp.zeros(acc_ref.shape, F32)

    def update(h, k, v):
        n = k.shape[0]
        qs = _split_halves(q_ref[:, h * LANES:(h + 1) * LANES])
        v_ones = jnp.concatenate([v, jnp.ones((n, LANES), BF16)], axis=1)
        ps, alphas = [], []
        for a in range(2):
            s = lax.dot_general(qs[a], k, _NT, preferred_element_type=F32)
            m_prev = m_ref[2 * h + a]
            m_new = jnp.maximum(m_prev, jnp.max(s, axis=1, keepdims=True))
            alphas.append(jnp.exp2(m_prev - m_new))
            ps.append(jnp.exp2(s - m_new).astype(BF16))
            m_ref[2 * h + a] = m_new
        pv = jnp.dot(jnp.concatenate(ps, axis=0), v_ones, preferred_element_type=F32)
        for a in range(2):
            part = pv[a * nq:(a + 1) * nq]
            acc_ref[2 * h + a] = alphas[a] * acc_ref[2 * h + a] + part[:, :LANES]
            l_ref[2 * h + a] = alphas[a] * l_ref[2 * h + a] + part[:, LANES:]

    for h in range(N_HEADS):
        update(h, ck_ref[pl.ds(h, chunk, stride=N_HEADS), :].astype(BF16),
               cv_ref[pl.ds(h, chunk, stride=N_HEADS), :].astype(BF16))

    @pl.when(kc == pl.num_programs(1) - 1)
    def _():
        lam = (jnp.exp(jnp.sum(lq1_ref[...] * lk1_ref[...], axis=1, keepdims=True))
               - jnp.exp(jnp.sum(lq2_ref[...] * lk2_ref[...], axis=1, keepdims=True)) + LAM_INIT)
        for h in range(N_HEADS):
            cols = slice(h * LANES, (h + 1) * LANES)
            update(h, kn_ref[:, cols], vn_ref[:, cols])
            o = acc_ref[2 * h] / l_ref[2 * h] - lam * (acc_ref[2 * h + 1] / l_ref[2 * h + 1])
            o_ref[:, cols] = (_rms(o) * subg_ref[...] * (1.0 - LAM_INIT)).astype(BF16)


def _diff_decode(lams, subg_row, q, ck, cv, kn, vn, *, chunk):
    nb, nq, _ = q.shape
    n_chunks = ck.shape[1] // (chunk * N_HEADS)
    small = _const_spec((1, HEAD_DIM))
    tok = pl.BlockSpec((None, nq, DIFF_WIDTH), lambda b, c: (b, 0, 0))
    cache = pl.BlockSpec((None, chunk * N_HEADS, LANES), lambda b, c: (b, c, 0))
    return pl.pallas_call(
        functools.partial(_diff_decode_kernel, chunk=chunk),
        grid=(nb, n_chunks),
        in_specs=[small, small, small, small, _const_spec((1, LANES)), tok, cache, cache, tok, tok],
        out_specs=tok,
        out_shape=jax.ShapeDtypeStruct((nb, nq, DIFF_WIDTH), BF16),
        scratch_shapes=[pltpu.VMEM((2 * N_HEADS, nq, 1), F32), pltpu.VMEM((2 * N_HEADS, nq, LANES), F32),
                        pltpu.VMEM((2 * N_HEADS, nq, LANES), F32)],
        compiler_params=pltpu.CompilerParams(dimension_semantics=("arbitrary",) * 2,
                                             vmem_limit_bytes=VMEM_LIMIT),
        name="diff_decode",
    )(*lams, subg_row, q, ck, cv, kn, vn)


SB_BLOCK = SB_KEY_BLOCK


def _sb_decode_kernel(q_ref, ckt_ref, cvt_ref, knt_ref, vnt_ref, o_ref):
    nq = q_ref.shape[0]
    past = ckt_ref.shape[1]
    n_blk = past // SB_BLOCK
    def suffix_cols(n):
        r = lax.broadcasted_iota(jnp.int32, (n, n + LANES), 0)
        c = lax.broadcasted_iota(jnp.int32, (n, n + LANES), 1)
        return jnp.where((r > c) | (c >= n), 1.0, 0.0).astype(BF16)
    sfx_blk = suffix_cols(SB_BLOCK)
    sfx_new = suffix_cols(LANES)
    row = lax.broadcasted_iota(jnp.int32, (nq, LANES), 0)
    lane = lax.broadcasted_iota(jnp.int32, (nq, LANES), 1)
    new_mask = lane < row

    def log1m_of(u):
        nu = -u
        return jnp.minimum(nu, 0.0) - jnp.log(1.0 + jnp.exp2(jnp.minimum(u, nu))) * LOG2E

    for j in range(ckt_ref.shape[0] // LANES):
        rows = slice(j * LANES, (j + 1) * LANES)
        kt = ckt_ref[rows, :].astype(BF16)
        vt = cvt_ref[rows, :].astype(BF16)
        qs = _split_halves(q_ref[:, rows])
        ws, wns = [], []
        for a in range(2):
            un = jnp.dot(qs[a], knt_ref[rows, :], preferred_element_type=F32)
            ln = jnp.where(new_mask, log1m_of(un), 0.0)
            sn = jnp.dot(ln.astype(BF16), sfx_new, preferred_element_type=F32)
            wns.append(jnp.exp2(jnp.where(new_mask, un + ln + sn[:, :LANES], NEG)).astype(BF16))
            carry = sn[:, LANES:]
            u = jnp.dot(qs[a], kt, preferred_element_type=F32)
            l1m = log1m_of(u)
            stacked = jnp.concatenate(
                [l1m[:, b * SB_BLOCK:(b + 1) * SB_BLOCK] for b in range(n_blk)], axis=0).astype(BF16)
            sfx = jnp.dot(stacked, sfx_blk, preferred_element_type=F32)
            ts = [None] * n_blk
            for b in reversed(range(n_blk)):
                cols = slice(b * SB_BLOCK, (b + 1) * SB_BLOCK)
                part = sfx[b * nq:(b + 1) * nq]
                c2 = jnp.concatenate([carry] * (SB_BLOCK // LANES), axis=1)
                ts[b] = u[:, cols] + l1m[:, cols] + part[:, :SB_BLOCK] + c2
                carry = carry + part[:, SB_BLOCK:]
            ws.append(jnp.exp2(jnp.concatenate(ts, axis=1)).astype(BF16))
        o = (lax.dot_general(jnp.concatenate(ws, axis=0), vt, _NT, preferred_element_type=F32)
             + lax.dot_general(jnp.concatenate(wns, axis=0), vnt_ref[rows, :], _NT,
                               preferred_element_type=F32))
        o_ref[:, rows] = jnp.where(lane < HEAD_DIM, o[:nq], o[nq:]).astype(BF16)


def _sb_decode(q, ckt, cvt, knt, vnt):
    nb, nq, _ = q.shape
    assert nq <= LANES
    past = ckt.shape[2]
    width = 2 * LANES
    tok = pl.BlockSpec((None, nq, width), lambda b, g: (b, 0, g))
    cache = pl.BlockSpec((None, width, past), lambda b, g: (b, g, 0))
    new = pl.BlockSpec((None, width, LANES), lambda b, g: (b, g, 0))
    return pl.pallas_call(
        _sb_decode_kernel,
        grid=(nb, SB_WIDTH // width),
        in_specs=[tok, cache, cache, new, new],
        out_specs=tok,
        out_shape=jax.ShapeDtypeStruct((nb, nq, SB_WIDTH), BF16),
        compiler_params=pltpu.CompilerParams(dimension_semantics=("arbitrary",) * 2,
                                             vmem_limit_bytes=VMEM_LIMIT),
        name="sb_decode",
    )(q, ckt, cvt, knt, vnt)


def _out_kernel(x_ref, od_ref, os_ref, gmix_ref, wg_ref, wdo_ref, wso_ref, wo_ref, gffn_ref,
                w1_ref, w2_ref, y_ref):
    x = x_ref[...]
    h = (_rms(x) * gmix_ref[...]).astype(BF16)
    gate = jax.nn.sigmoid(jnp.dot(h, wg_ref[...], preferred_element_type=F32))
    a = jnp.dot(od_ref[...], wdo_ref[...], preferred_element_type=F32)
    b = jnp.dot(os_ref[...], wso_ref[...], preferred_element_type=F32)
    merged = (gate[:, :D_MODEL] * a + gate[:, D_MODEL:] * b).astype(BF16)
    x1 = x + jnp.dot(merged, wo_ref[...], preferred_element_type=F32)
    h2 = (_rms(x1) * gffn_ref[...]).astype(BF16)
    y = x1
    for c in range(D_FF // D_MODEL):
        sl = slice(c * D_MODEL, (c + 1) * D_MODEL)
        f = jnp.maximum(jnp.dot(h2, w1_ref[:, sl], preferred_element_type=F32), 0.0)
        y = y + jnp.dot((f * f).astype(BF16), w2_ref[sl, :], preferred_element_type=F32)
    y_ref[...] = y


def _out(x, od, osb, gmix, wg, wdo, wso, wo, gffn, w1, w2, *, tm):
    rows = x.shape[0]
    row = lambda w: pl.BlockSpec((tm, w), lambda i: (i, 0))
    return pl.pallas_call(
        _out_kernel,
        grid=(rows // tm,),
        in_specs=[row(D_MODEL), row(DIFF_WIDTH), row(SB_WIDTH), _const_spec((1, D_MODEL)),
                  _const_spec((D_MODEL, 2 * D_MODEL)), _const_spec((DIFF_WIDTH, D_MODEL)),
                  _const_spec((SB_WIDTH, D_MODEL)), _const_spec((D_MODEL, D_MODEL)),
                  _const_spec((1, D_MODEL)), _const_spec((D_MODEL, D_FF)), _const_spec((D_FF, D_MODEL))],
        out_specs=row(D_MODEL),
        out_shape=jax.ShapeDtypeStruct((rows, D_MODEL), F32),
        compiler_params=pltpu.CompilerParams(dimension_semantics=("arbitrary",),
                                             vmem_limit_bytes=VMEM_LIMIT),
        name="out",
    )(x, od, osb, gmix, wg, wdo, wso, wo, gffn, w1, w2)


def _rope_tables(pos):
    half = ROT_DIM // 2
    d = jnp.arange(LANES, dtype=jnp.int32) % HEAD_DIM
    inv = ROPE_THETA ** (-(2 * (d % half)).astype(F32) / ROT_DIM)
    ang = pos.astype(F32)[:, None] * inv[None, :]
    cos, sin = jnp.cos(ang), jnp.sin(ang)
    cos_t = jnp.where(d < ROT_DIM, cos, 1.0)
    sa_t = jnp.where(d < half, -sin, 0.0)
    sb_t = jnp.where((d >= half) & (d < ROT_DIM), sin, 0.0)
    return cos_t, sa_t, sb_t


def kernel(x_prompt, x_sample, cache_diff_k, cache_diff_v, cache_sb_k, cache_sb_v, meta_tokens,
           g_mix, w_in, q_norm_g, k_norm_g, lam_q1, lam_k1, lam_q2, lam_k2, sub_g,
           w_diff_out, w_sb_out, w_out, g_ffn, w_ff1, w_ff2):
    nb, seq, _ = x_prompt.shape
    db, dseq, _ = x_sample.shape
    past = cache_diff_k.shape[2]
    lyr = 0

    w_qkv, w_gate = w_in[lyr, :, :QKV_COLS].astype(BF16), w_in[lyr, :, QKV_COLS:].astype(BF16)
    wdo, wso, wo = (w_diff_out[lyr].astype(BF16), w_sb_out[lyr].astype(BF16), w_out[lyr].astype(BF16))
    w1, w2 = w_ff1[lyr].astype(BF16), w_ff2[lyr].astype(BF16)
    gmix = g_mix[lyr].reshape(1, D_MODEL)
    gffn = g_ffn[lyr].reshape(1, D_MODEL)
    qg = jnp.tile(q_norm_g[lyr], COL_BLOCK // HEAD_DIM).reshape(1, COL_BLOCK)
    kg = jnp.tile(k_norm_g[lyr], COL_BLOCK // HEAD_DIM).reshape(1, COL_BLOCK)
    subg_col = sub_g[lyr].reshape(LANES, 1)
    lams = [t[lyr].reshape(1, HEAD_DIM) for t in (lam_q1, lam_k1, lam_q2, lam_k2)]
    grp = jnp.arange(MXU_TILE, dtype=jnp.int32) // HEAD_DIM
    gmat = jnp.where(grp[:, None] == grp[None, :], 1.0 / HEAD_DIM, 0.0).astype(BF16)

    main_pos = N_META + jnp.arange(seq, dtype=jnp.int32)
    small_pos = jnp.concatenate([jnp.arange(N_META, dtype=jnp.int32),
                                 jnp.tile(past + jnp.arange(dseq, dtype=jnp.int32), db)])
    x_main = x_prompt.reshape(nb * seq, D_MODEL)
    x_small = jnp.concatenate([meta_tokens.astype(F32), x_sample.reshape(db * dseq, D_MODEL)], axis=0)
    pm = _proj(x_main, gmix, w_qkv, qg, kg, gmat, *_rope_tables(main_pos), tm=ROW_TILE, prompt_layout=True,
               lead=N_META)
    ps = _proj(x_small, gmix, w_qkv, qg, kg, gmat, *_rope_tables(small_pos), tm=x_small.shape[0],
               prompt_layout=False)
    qd_m, kd_hm, kdb_m, vd_hm, vdt_m, qs_m, kst_m, ksb_m, vstf_m, vst_m = pm
    qd_s, kd_s, kdb_s, vd_s, vdb_s, qs_s, ks_s, ksb_s, vs_s, vsb_s = ps

    def bt(a, n, t):
        return a.reshape(n, t, a.shape[-1])

    def meta_keys(a):
        return jnp.pad(a[:N_META], ((0, LANES - N_META), (0, 0)))[None]

    def meta_vals_t(a):
        return jnp.swapaxes(meta_keys(a), 1, 2)

    od_p = _diff_attn(lams, subg_col, bt(qd_m, nb, seq), meta_keys(kdb_s), meta_vals_t(vdb_s),
                      bt(kdb_m, nb, seq), vdt_m,
                      tq=DIFF_Q_TILE, tk=DIFF_KEY_BLOCK, pref_valid=N_META)
    os_p = _sb_attn(bt(qs_m, nb, seq), meta_keys(ksb_s), meta_vals_t(vsb_s),
                    bt(ksb_m, nb, seq), vst_m,
                    tq=SB_Q_TILE, tk=SB_KEY_BLOCK, pref_valid=N_META)

    smp = lambda a: bt(a[N_META:], db, dseq)
    cdk = cache_diff_k[lyr].reshape(db, past * N_HEADS, LANES)
    cdv = cache_diff_v[lyr].reshape(db, past * N_HEADS, LANES)
    od_s = _diff_decode(lams, sub_g[lyr].reshape(1, LANES), smp(qd_s), cdk, cdv, smp(kdb_s), smp(vdb_s),
                        chunk=DECODE_CHUNK)
    sb_t = lambda c: jnp.transpose(c[lyr], (0, 2, 3, 1)).reshape(db, SB_WIDTH, past)
    new_t = lambda a: jnp.pad(jnp.swapaxes(smp(a), 1, 2), ((0, 0), (0, 0), (0, LANES - dseq)))
    os_s = _sb_decode(smp(qs_s), sb_t(cache_sb_k), sb_t(cache_sb_v), new_t(ksb_s), new_t(vsb_s))

    y_p = _out(x_main, od_p.reshape(nb * seq, DIFF_WIDTH), os_p.reshape(nb * seq, SB_WIDTH),
               gmix, w_gate, wdo, wso, wo, gffn, w1, w2, tm=ROW_TILE)
    y_s = _out(x_sample.reshape(db * dseq, D_MODEL), od_s.reshape(db * dseq, DIFF_WIDTH),
               os_s.reshape(db * dseq, SB_WIDTH), gmix, w_gate, wdo, wso, wo, gffn, w1, w2, tm=db * dseq)

    def diff_prompt_cache(head_major, small):
        meta = small[:N_META].reshape(N_META * N_HEADS, 2 * HEAD_DIM)
        full = _fill_lead_tokens(head_major, meta, nb=nb)
        return full.reshape(1, nb, seq + N_META, N_HEADS, 2 * HEAD_DIM)

    def sb_prompt_cache(main_t, small):
        meta_t = jnp.broadcast_to(small[:N_META].T[None], (nb, SB_WIDTH, N_META))
        full = jnp.concatenate([meta_t, main_t], axis=2).reshape(nb, N_HEADS, HEAD_DIM, seq + N_META)
        return jnp.transpose(full, (0, 3, 1, 2))[None]

    def sample_cache(small, dim):
        return small[N_META:].reshape(1, db, dseq, N_HEADS, dim)

    return (y_p.reshape(nb, seq, D_MODEL), y_s.reshape(db, dseq, D_MODEL),
            diff_prompt_cache(kd_hm, kd_s), diff_prompt_cache(vd_hm, vd_s),
            sb_prompt_cache(kst_m, ks_s), sb_prompt_cache(vstf_m, vs_s),
            sample_cache(kd_s, 2 * HEAD_DIM), sample_cache(vd_s, 2 * HEAD_DIM),
            sample_cache(ks_s, HEAD_DIM), sample_cache(vs_s, HEAD_DIM))
```
